```python
import math
import jax, jax.numpy as jnp
from jax import lax
import numpy as np

D_MODEL = 2048
BATCH = 1
SEQ = 8192
DEPTH = 2

N_BRANCH = 4
BR_WIDTH = 512
LRU_BLOCKS = 8
LRU_BLOCK = BR_WIDTH // LRU_BLOCKS
LRU_CONV = 4
LRU_C = 8.0
ATT_GROUPS = ((128, 1), (512, 4), (2048, 16))
ATT_HEADS_PER_GROUP = 4
ATT_HEAD_DIM = BR_WIDTH // ATT_HEADS_PER_GROUP
ATT_HEADS = len(ATT_GROUPS) * ATT_HEADS_PER_GROUP
ATT_QKV = ATT_HEADS * ATT_HEAD_DIM
ATT_SPAN = 128
N_BUCKETS = 32
MAX_DISTANCE = 2048
NEG_INF = -1e30
RWKV_HEAD = 64
RWKV_HEADS = BR_WIDTH // RWKV_HEAD
DECAY_RANK = 64
ICLR_RANK = 64
RWKV_GN_EPS = 64e-5
RWKV_SHIFT_W = 3 * BR_WIDTH + DECAY_RANK + ICLR_RANK
CONF_KERNEL = 31
LN_EPS = 1e-5
ALPHA = (2.0 * DEPTH) ** 0.25
BETA = (8.0 * DEPTH) ** -0.25

IN_SPLITS = (
    BR_WIDTH, BR_WIDTH,
    ATT_QKV, ATT_QKV, ATT_QKV, BR_WIDTH,
    BR_WIDTH, BR_WIDTH, BR_WIDTH, DECAY_RANK, ICLR_RANK, BR_WIDTH,
    BR_WIDTH, BR_WIDTH, BR_WIDTH,
    N_BRANCH * D_MODEL,
)
D_IN = sum(IN_SPLITS)

kernel_name = 'hybrid_rglru_dilattn_rwkv7_conformer_deepnorm'


def _layernorm(x, g, b, eps=LN_EPS):
    xf = x.astype(jnp.float32)
    mu = xf.mean(-1, keepdims=True)
    var = jnp.square(xf - mu).mean(-1, keepdims=True)
    return ((xf - mu) * lax.rsqrt(var + eps) * g + b).astype(x.dtype)


def _causal_depthwise_conv(x, w, b):
    width, ch = w.shape
    y = lax.conv_general_dilated(x, w[:, None, :], window_strides=(1,), padding=[(width - 1, 0)],
                                 dimension_numbers=('NWC', 'WIO', 'NWC'), feature_group_count=ch)
    return y + b


def _linear_scan(a, b):
    def op(c1, c2):
        a1, b1 = c1
        a2, b2 = c2
        return a1 * a2, a2 * b1 + b2
    _, h = lax.associative_scan(op, (a, b), axis=1)
    return h


def _rwkv7_scan(r, w, k, v, a, b):
    bsz, _, nh, n = r.shape
    def step(state, inp):
        r_t, w_t, k_t, v_t, a_t, b_t = inp
        sa = jnp.einsum('bhij,bhj->bhi', state, a_t)
        state = state * w_t[:, :, None, :] + sa[..., :, None] * b_t[:, :, None, :] + v_t[..., :, None] * k_t[:, :, None, :]
        return state, jnp.einsum('bhij,bhj->bhi', state, r_t)
    xs = tuple(jnp.moveaxis(t, 1, 0) for t in (r, w, k, v, a, b))
    s0 = jnp.zeros((bsz, nh, n, n), jnp.float32)
    _, ys = lax.scan(step, s0, xs)
    return jnp.moveaxis(ys, 0, 1)


def _t5_bucket(dist):
    max_exact = N_BUCKETS // 2
    large = max_exact + (np.log(np.maximum(dist, 1) / max_exact) / math.log(MAX_DISTANCE / max_exact)
                         * (N_BUCKETS - max_exact)).astype(np.int32)
    large = np.minimum(large, N_BUCKETS - 1)
    return np.where(dist < max_exact, dist, large).astype(np.int32)


def _group_bias(table, dilation):
    qi = np.arange(ATT_SPAN)[:, None]
    kj = np.arange(2 * ATT_SPAN)[None, :]
    dist = qi + ATT_SPAN - kj
    valid = (dist >= 0) & (dist <= ATT_SPAN)
    bucket = _t5_bucket(np.clip(dist, 0, ATT_SPAN) * dilation)
    bias = jnp.where(valid[..., None], table[bucket].astype(jnp.float32), NEG_INF)
    return jnp.transpose(bias, (0, 2, 1))


def _dilated_window_attention(q, k, v, bias, dilation):
    bsz, s, h, dh = q.shape
    m_rows = s // dilation
    nb = -(-m_rows // ATT_SPAN)
    mp = nb * ATT_SPAN

    def to_res(t):
        t = jnp.transpose(t.reshape(bsz, m_rows, dilation, h, dh), (0, 2, 1, 3, 4))
        t = jnp.pad(t, ((0, 0), (0, 0), (0, mp - m_rows), (0, 0), (0, 0)))
        return t.reshape(bsz, dilation, nb, ATT_SPAN, h, dh)

    def with_prev(t):
        prev = jnp.pad(t, ((0, 0), (0, 0), (1, 0), (0, 0), (0, 0), (0, 0)))[:, :, :-1]
        return jnp.concatenate([prev, t], axis=3)

    qb = to_res(q).astype(jnp.float32)
    kw = with_prev(to_res(k)).astype(jnp.float32)
    vw = with_prev(to_res(v)).astype(jnp.float32)
    logits = jnp.einsum('brnqhd,brnkhd->brnqhk', qb, kw) * (dh ** -0.5) + bias
    first = (np.arange(nb)[:, None] == 0) & (np.arange(2 * ATT_SPAN)[None, :] < ATT_SPAN)
    logits = jnp.where(first[:, None, None, :], NEG_INF, logits)
    m = logits.max(-1)
    p = jnp.exp(logits - m[..., None])
    den = p.sum(-1)
    o = jnp.einsum('brnqhk,brnkhd->brnqhd', p, vw) / den[..., None]

    def from_res(t):
        t = t.reshape((bsz, dilation, mp) + t.shape[4:])[:, :, :m_rows]
        t = jnp.moveaxis(t, 1, 2)
        return t.reshape((bsz, s) + t.shape[3:])

    return from_res(o), from_res(m), from_res(den)


def _hybrid_layer(x, rel_bias, w_in, b_in, lru_conv_w, lru_conv_b, lru_gate_a_w, lru_gate_a_b,
                  lru_gate_x_w, lru_gate_x_b, lru_lambda, rwkv_mu, rwkv_w0, rwkv_w_up, rwkv_a0, rwkv_a_up,
                  rwkv_k_k, rwkv_k_a, rwkv_r_k, rwkv_gn_g, rwkv_gn_b, conf_dw_w, conf_dw_b, conf_ln_g,
                  conf_ln_b, w_br, w_out, ln_g, ln_b):
    bsz, s, _ = x.shape
    dt = x.dtype
    f32 = jnp.float32
    split_points = np.cumsum(IN_SPLITS)[:-1].tolist()
    h = jnp.einsum('bsd,de->bse', x, w_in) + b_in
    (a_x, a_gate, q, k, v, b_gate, c_r, c_k, c_v, c_wd, c_ad, c_gate,
     d_val, d_glu, d_gate, merge_logits) = jnp.split(h, split_points, axis=-1)

    u = _causal_depthwise_conv(a_x, lru_conv_w, lru_conv_b)
    ub = u.reshape(bsz, s, LRU_BLOCKS, LRU_BLOCK)
    gate_r = jax.nn.sigmoid(jnp.einsum('bsgi,gij->bsgj', ub, lru_gate_a_w).reshape(bsz, s, BR_WIDTH) + lru_gate_a_b)
    gate_i = jax.nn.sigmoid(jnp.einsum('bsgi,gij->bsgj', ub, lru_gate_x_w).reshape(bsz, s, BR_WIDTH) + lru_gate_x_b)
    log_a = -LRU_C * gate_r.astype(f32) * jax.nn.softplus(-lru_lambda.astype(f32))
    a_t = jnp.exp(log_a)
    mult = jnp.sqrt(-jnp.expm1(2.0 * log_a))
    y_a = _linear_scan(a_t, mult * (gate_i * u).astype(f32)).astype(dt)

    qh = q.reshape(bsz, s, ATT_HEADS, ATT_HEAD_DIM)
    kh = k.reshape(bsz, s, ATT_HEADS, ATT_HEAD_DIM)
    vh = v.reshape(bsz, s, ATT_HEADS, ATT_HEAD_DIM)
    outs, maxes, dens = [], [], []
    for g, (window, dil) in enumerate(ATT_GROUPS):
        hs = slice(g * ATT_HEADS_PER_GROUP, (g + 1) * ATT_HEADS_PER_GROUP)
        bias = _group_bias(rel_bias[:, hs], dil)
        o_g, m_g, s_g = _dilated_window_attention(qh[:, :, hs], kh[:, :, hs], vh[:, :, hs], bias, dil)
        outs.append(o_g)
        maxes.append(m_g)
        dens.append(s_g)
    m_all = jnp.stack(maxes)
    wts = jnp.exp(m_all - m_all.max(0)) * jnp.stack(dens)
    y_b = (wts[..., None] * jnp.stack(outs)).sum(0) / wts.sum(0)[..., None]
    y_b = y_b.reshape(bsz, s, BR_WIDTH).astype(dt)

    c_in = jnp.concatenate([c_r, c_k, c_v, c_wd, c_ad], axis=-1)
    c_prev = jnp.pad(c_in, ((0, 0), (1, 0), (0, 0)))[:, :-1]
    c_in = c_in + rwkv_mu * (c_prev - c_in)
    r, kx, vv, wd, ad = jnp.split(c_in, [BR_WIDTH, 2 * BR_WIDTH, 3 * BR_WIDTH, 3 * BR_WIDTH + DECAY_RANK], axis=-1)
    w_log = -jax.nn.softplus(-(rwkv_w0 + jnp.tanh(wd) @ rwkv_w_up).astype(f32)) - 0.5
    decay = jnp.exp(-jnp.exp(w_log))
    a_icl = jax.nn.sigmoid((rwkv_a0 + ad @ rwkv_a_up).astype(f32))

    def heads(t):
        return t.reshape(bsz, s, RWKV_HEADS, RWKV_HEAD)

    kxf = kx.astype(f32)
    kk = heads(kxf * rwkv_k_k)
    kk = kk / jnp.maximum(jnp.sqrt(jnp.sum(kk * kk, -1, keepdims=True)), 1e-12)
    kc = heads(kxf * (1.0 + (a_icl - 1.0) * rwkv_k_a))
    rh = heads(r.astype(f32))
    vf = heads(vv.astype(f32))
    wy = _rwkv7_scan(rh, heads(decay), kc, vf, -kk, kk * heads(a_icl))
    mu = wy.mean(-1, keepdims=True)
    var = jnp.square(wy - mu).mean(-1, keepdims=True)
    wy = ((wy - mu) * lax.rsqrt(var + RWKV_GN_EPS)).reshape(bsz, s, BR_WIDTH) * rwkv_gn_g + rwkv_gn_b
    bonus = jnp.sum(rh * kc * rwkv_r_k, -1, keepdims=True) * vf
    y_c = (wy + bonus.reshape(bsz, s, BR_WIDTH)).astype(dt)

    cu = d_val * jax.nn.sigmoid(d_glu)
    cu = _causal_depthwise_conv(cu, conf_dw_w, conf_dw_b)
    y_d = jax.nn.silu(_layernorm(cu, conf_ln_g, conf_ln_b))

    merge_g = jax.nn.sigmoid(merge_logits).reshape(bsz, s, N_BRANCH, D_MODEL)
    ys = (y_a, y_b, y_c, y_d)
    gates = (a_gate, b_gate, c_gate, d_gate)
    mixed = jnp.zeros_like(x)
    for n in range(N_BRANCH):
        mixed = mixed + merge_g[:, :, n] * jnp.einsum('bsc,cd->bsd', ys[n] * jax.nn.silu(gates[n]), w_br[n])
    out = jnp.einsum('bsd,de->bse', mixed, w_out)
    return _layernorm(ALPHA * x + out, ln_g, ln_b)


def setup_inputs(seed: int = 0) -> dict:
    key = jax.random.key(seed)
    ks = jax.random.split(key, 32)
    f32 = jnp.float32

    def nrm(k, shape, scale):
        return jax.random.normal(k, shape, f32) * scale

    def unif(k, shape, lo, hi):
        return jax.random.uniform(k, shape, f32, lo, hi)

    a_target = unif(ks[10], (DEPTH, BR_WIDTH), 0.9, 0.999)
    s_lam = a_target ** (1.0 / LRU_C)
    return {
        'x': nrm(ks[0], (BATCH, SEQ, D_MODEL), 1.0),
        'att_rel_bias': nrm(ks[1], (N_BUCKETS, ATT_HEADS), 0.1),
        'w_in': nrm(ks[2], (DEPTH, D_MODEL, D_IN), D_MODEL ** -0.5),
        'b_in': nrm(ks[3], (DEPTH, D_IN), 0.02),
        'lru_conv_w': nrm(ks[4], (DEPTH, LRU_CONV, BR_WIDTH), LRU_CONV ** -0.5),
        'lru_conv_b': nrm(ks[5], (DEPTH, BR_WIDTH), 0.02),
        'lru_gate_a_w': nrm(ks[6], (DEPTH, LRU_BLOCKS, LRU_BLOCK, LRU_BLOCK), LRU_BLOCK ** -0.5),
        'lru_gate_a_b': nrm(ks[7], (DEPTH, BR_WIDTH), 0.02),
        'lru_gate_x_w': nrm(ks[8], (DEPTH, LRU_BLOCKS, LRU_BLOCK, LRU_BLOCK), LRU_BLOCK ** -0.5),
        'lru_gate_x_b': nrm(ks[9], (DEPTH, BR_WIDTH), 0.02),
        'lru_lambda': jnp.log(s_lam) - jnp.log1p(-s_lam),
        'rwkv_mu': unif(ks[11], (DEPTH, RWKV_SHIFT_W), 0.0, 1.0),
        'rwkv_w0': jnp.linspace(-6.5, -1.5, BR_WIDTH, dtype=f32)[None, :] + nrm(ks[12], (DEPTH, BR_WIDTH), 0.1),
        'rwkv_w_up': nrm(ks[13], (DEPTH, DECAY_RANK, BR_WIDTH), 0.5 * DECAY_RANK ** -0.5),
        'rwkv_a0': nrm(ks[14], (DEPTH, BR_WIDTH), 0.1),
        'rwkv_a_up': nrm(ks[15], (DEPTH, ICLR_RANK, BR_WIDTH), 0.5 * ICLR_RANK ** -0.5),
        'rwkv_k_k': 0.85 + nrm(ks[16], (DEPTH, BR_WIDTH), 0.02),
        'rwkv_k_a': 1.0 + nrm(ks[17], (DEPTH, BR_WIDTH), 0.02),
        'rwkv_r_k': nrm(ks[18], (DEPTH, RWKV_HEADS, RWKV_HEAD), 0.1),
        'rwkv_gn_g': 1.0 + nrm(ks[19], (DEPTH, BR_WIDTH), 0.02),
        'rwkv_gn_b': nrm(ks[20], (DEPTH, BR_WIDTH), 0.02),
        'conf_dw_w': nrm(ks[21], (DEPTH, CONF_KERNEL, BR_WIDTH), CONF_KERNEL ** -0.5),
        'conf_dw_b': nrm(ks[22], (DEPTH, BR_WIDTH), 0.02),
        'conf_ln_g': 1.0 + nrm(ks[23], (DEPTH, BR_WIDTH), 0.02),
        'conf_ln_b': nrm(ks[24], (DEPTH, BR_WIDTH), 0.02),
        'w_br': nrm(ks[25], (DEPTH, N_BRANCH, BR_WIDTH, D_MODEL), BR_WIDTH ** -0.5),
        'w_out': nrm(ks[26], (DEPTH, D_MODEL, D_MODEL), BETA * D_MODEL ** -0.5),
        'ln_g': 1.0 + nrm(ks[27], (DEPTH, D_MODEL), 0.02),
        'ln_b': nrm(ks[28], (DEPTH, D_MODEL), 0.02),
    }


def reference(x, att_rel_bias, w_in, b_in, lru_conv_w, lru_conv_b, lru_gate_a_w, lru_gate_a_b,
              lru_gate_x_w, lru_gate_x_b, lru_lambda, rwkv_mu, rwkv_w0, rwkv_w_up, rwkv_a0, rwkv_a_up,
              rwkv_k_k, rwkv_k_a, rwkv_r_k, rwkv_gn_g, rwkv_gn_b, conf_dw_w, conf_dw_b, conf_ln_g,
              conf_ln_b, w_br, w_out, ln_g, ln_b):
    for l in range(DEPTH):
        x = _hybrid_layer(x, att_rel_bias, w_in[l], b_in[l], lru_conv_w[l], lru_conv_b[l], lru_gate_a_w[l],
                          lru_gate_a_b[l], lru_gate_x_w[l], lru_gate_x_b[l], lru_lambda[l], rwkv_mu[l],
                          rwkv_w0[l], rwkv_w_up[l], rwkv_a0[l], rwkv_a_up[l], rwkv_k_k[l], rwkv_k_a[l],
                          rwkv_r_k[l], rwkv_gn_g[l], rwkv_gn_b[l], conf_dw_w[l], conf_dw_b[l], conf_ln_g[l],
                          conf_ln_b[l], w_br[l], w_out[l], ln_g[l], ln_b[l])
    return x
```

```python
import functools
import math

import numpy as np
import jax
import jax.numpy as jnp
from jax import lax
from jax.experimental import pallas as pl
from jax.experimental.pallas import tpu as pltpu

D_MODEL = 2048
DEPTH = 2
N_BRANCH = 4
BR_WIDTH = 512
LRU_BLOCKS = 8
LRU_BLOCK = BR_WIDTH // LRU_BLOCKS
LRU_CONV = 4
LRU_C = 8.0
ATT_GROUPS = ((128, 1), (512, 4), (2048, 16))
ATT_HEADS_PER_GROUP = 4
ATT_HEAD_DIM = BR_WIDTH // ATT_HEADS_PER_GROUP
ATT_HEADS = len(ATT_GROUPS) * ATT_HEADS_PER_GROUP
ATT_QKV = ATT_HEADS * ATT_HEAD_DIM
ATT_SPAN = 128
N_BUCKETS = 32
MAX_DISTANCE = 2048
NEG_INF = -1e30
RWKV_HEAD = 64
RWKV_HEADS = BR_WIDTH // RWKV_HEAD
DECAY_RANK = 64
ICLR_RANK = 64
RWKV_GN_EPS = 64e-5
CONF_KERNEL = 31
LN_EPS = 1e-5
ALPHA = (2.0 * DEPTH) ** 0.25

LANES = 128
SUBLANES = 8
MIB = 1024 * 1024

COL_A_X, COL_A_GATE = 0, 1
COL_Q, COL_K, COL_V, COL_B_GATE = 2, 5, 8, 11
COL_D_VAL, COL_D_GLU, COL_D_GATE = 12, 13, 14
COL_C_R, COL_C_K, COL_C_V, COL_C_LORA, COL_C_GATE = 15, 16, 17, 18, 19
H_BLOCKS = 20
H_WIDTH = H_BLOCKS * BR_WIDTH
BRANCH_IN = 2 * BR_WIDTH + 3 * ATT_QKV + BR_WIDTH + (4 * BR_WIDTH + DECAY_RANK + ICLR_RANK) + 3 * BR_WIDTH

RWKV_CHUNK = 64
PAIR = 2 * RWKV_HEAD
N_PAIRS = BR_WIDTH // PAIR

F32 = jnp.float32
BF16 = jnp.bfloat16
HIGHEST = lax.Precision.HIGHEST


def _params(semantics, vmem_mib):
    return pltpu.CompilerParams(dimension_semantics=semantics, vmem_limit_bytes=vmem_mib * MIB)


def _bdot(a, b):
    return jnp.dot(a.astype(BF16), b.astype(BF16), preferred_element_type=F32)


def _bdot_nt(a, b):
    return lax.dot_general(a.astype(BF16), b.astype(BF16), (((1,), (1,)), ((), ())), preferred_element_type=F32)


def _bdot_tn(a, b):
    return lax.dot_general(a.astype(BF16), b.astype(BF16), (((0,), (0,)), ((), ())), preferred_element_type=F32)


def _fdot(a, b):
    return jnp.dot(a, b, preferred_element_type=F32, precision=HIGHEST)


def _fdot_nt(a, b):
    return lax.dot_general(a, b, (((1,), (1,)), ((), ())), preferred_element_type=F32, precision=HIGHEST)


def _softplus(z):
    return jnp.maximum(z, 0.0) + jnp.log1p(jnp.exp(-jnp.abs(z)))


def _expm1_nonpos(z):
    u = jnp.exp(z)
    safe = jnp.where(u == 1.0, 0.5, u)
    return jnp.where(u == 1.0, z, jnp.where(u == 0.0, -1.0, (safe - 1.0) * z / jnp.log(safe)))


def _silu(z):
    return z * jax.nn.sigmoid(z)


def _in_proj_kernel(x_ref, w_ref, b_ref, h_ref, xb_ref):
    @pl.when(pl.program_id(1) == 0)
    def _():
        xb_ref[...] = x_ref[...].astype(BF16)

    h_ref[...] = jnp.dot(xb_ref[...], w_ref[...], preferred_element_type=F32) + b_ref[...]


def _in_proj(x, w, b, tm=1024, tn=1024):
    s, k = x.shape
    n = w.shape[1]
    return pl.pallas_call(
        _in_proj_kernel,
        grid=(s // tm, n // tn),
        in_specs=[
            pl.BlockSpec((tm, k), lambda i, j: (i, 0)),
            pl.BlockSpec((k, tn), lambda i, j: (0, j)),
            pl.BlockSpec((1, tn), lambda i, j: (0, j)),
        ],
        out_specs=[
            pl.BlockSpec((tm, tn), lambda i, j: (i, j)),
            pl.BlockSpec((tm, k), lambda i, j: (i, 0)),
        ],
        out_shape=[jax.ShapeDtypeStruct((s, n), F32), jax.ShapeDtypeStruct((s, k), BF16)],
        compiler_params=_params(("parallel", "arbitrary"), 48),
        name="in_proj",
    )(x, w, b)


def _lru_kernel(ax_ref, ag_ref, cw_ref, cb_ref, wa_ref, ba_ref, wx_ref, bx_ref, lam_ref, o_ref, ebuf, hc):
    t = ax_ref.shape[0]
    halo = SUBLANES

    @pl.when(pl.program_id(0) == 0)
    def _():
        ebuf[0:halo, :] = jnp.zeros((halo, BR_WIDTH), F32)
        hc[...] = jnp.zeros_like(hc)

    x = ax_ref[...]
    ebuf[halo:halo + t, :] = x
    u = cb_ref[...] + jnp.zeros((t, BR_WIDTH), F32)
    for j in range(LRU_CONV):
        u = u + cw_ref[j:j + 1, :] * ebuf[pl.ds(halo - (LRU_CONV - 1) + j, t), :]
    ebuf[0:halo, :] = x[t - halo:t, :]

    gate_r = jax.nn.sigmoid(_bdot(u, wa_ref[...]) + ba_ref[...])
    gate_i = jax.nn.sigmoid(_bdot(u, wx_ref[...]) + bx_ref[...])
    log_a = -LRU_C * gate_r * _softplus(-lam_ref[...])
    a = jnp.exp(log_a)
    b = jnp.sqrt(-_expm1_nonpos(2.0 * log_a)) * (gate_i * u)

    row = lax.broadcasted_iota(jnp.int32, (t, BR_WIDTH), 0)
    shift = 1
    while shift < t:
        valid = row >= shift
        b = jnp.where(valid, a * pltpu.roll(b, shift, 0), 0.0) + b
        a = jnp.where(valid, a * pltpu.roll(a, shift, 0), a)
        shift *= 2
    h = a * hc[0:1, :] + b
    hc[0:1, :] = h[t - 1:t, :]
    o_ref[...] = (h * _silu(ag_ref[...])).astype(BF16)


def _lru(h, cw, cb, wa, ba, wx, bx, lam, tile=256):
    s = h.shape[0]
    row = lambda c: pl.BlockSpec((tile, BR_WIDTH), lambda i: (i, c))
    full = lambda shape: pl.BlockSpec(shape, lambda i: (0,) * len(shape))
    return pl.pallas_call(
        _lru_kernel,
        grid=(s // tile,),
        in_specs=[row(COL_A_X), row(COL_A_GATE), full((LRU_CONV, BR_WIDTH)), full((1, BR_WIDTH)),
                  full((BR_WIDTH, BR_WIDTH)), full((1, BR_WIDTH)), full((BR_WIDTH, BR_WIDTH)), full((1, BR_WIDTH)),
                  full((1, BR_WIDTH))],
        out_specs=pl.BlockSpec((tile, BR_WIDTH), lambda i: (i, 0)),
        out_shape=jax.ShapeDtypeStruct((s, BR_WIDTH), BF16),
        scratch_shapes=[pltpu.VMEM((tile + SUBLANES, BR_WIDTH), F32), pltpu.VMEM((SUBLANES, BR_WIDTH), F32)],
        compiler_params=_params(("arbitrary",), 32),
        name="rglru",
    )(h, h, cw, cb, wa, ba, wx, bx, lam)


def _conf_kernel(val_ref, glu_ref, gate_ref, w_ref, b_ref, g_ref, beta_ref, o_ref, ebuf):
    t = val_ref.shape[0]
    halo = 32

    @pl.when(pl.program_id(0) == 0)
    def _():
        ebuf[0:halo, :] = jnp.zeros((halo, BR_WIDTH), F32)

    cu = val_ref[...] * jax.nn.sigmoid(glu_ref[...])
    ebuf[halo:halo + t, :] = cu
    acc = b_ref[...] + jnp.zeros((t, BR_WIDTH), F32)
    for j in range(CONF_KERNEL):
        acc = acc + w_ref[j:j + 1, :] * ebuf[pl.ds(halo - (CONF_KERNEL - 1) + j, t), :]
    ebuf[0:halo, :] = cu[t - halo:t, :]

    mu = jnp.mean(acc, axis=-1, keepdims=True)
    var = jnp.mean(jnp.square(acc - mu), axis=-1, keepdims=True)
    ln = (acc - mu) * lax.rsqrt(var + LN_EPS) * g_ref[...] + beta_ref[...]
    o_ref[...] = (_silu(ln) * _silu(gate_ref[...])).astype(BF16)


def _conformer(h, w, b, g, beta, tile=256):
    s = h.shape[0]
    row = lambda c: pl.BlockSpec((tile, BR_WIDTH), lambda i: (i, c))
    full = lambda shape: pl.BlockSpec(shape, lambda i: (0,) * len(shape))
    return pl.pallas_call(
        _conf_kernel,
        grid=(s // tile,),
        in_specs=[row(COL_D_VAL), row(COL_D_GLU), row(COL_D_GATE), full((CONF_KERNEL, BR_WIDTH)),
                  full((1, BR_WIDTH)), full((1, BR_WIDTH)), full((1, BR_WIDTH))],
        out_specs=pl.BlockSpec((tile, BR_WIDTH), lambda i: (i, 0)),
        out_shape=jax.ShapeDtypeStruct((s, BR_WIDTH), BF16),
        scratch_shapes=[pltpu.VMEM((tile + 32, BR_WIDTH), F32)],
        compiler_params=_params(("arbitrary",), 32),
        name="conformer",
    )(h, h, h, w, b, g, beta)


def _attn_kernel(q_ref, kc_ref, kp_ref, vc_ref, vp_ref, bias_ref, o_ref, st_ref):
    first = pl.program_id(1) == 0
    scale = ATT_HEAD_DIM ** -0.5
    lane = lax.broadcasted_iota(jnp.int32, (ATT_SPAN, LANES), 1)
    stats = jnp.zeros((ATT_SPAN, LANES), F32)
    for hh in range(ATT_HEADS_PER_GROUP):
        sl = slice(hh * ATT_HEAD_DIM, (hh + 1) * ATT_HEAD_DIM)
        q = q_ref[:, sl]
        lp = _bdot_nt(q, kp_ref[:, sl]) * scale + bias_ref[hh, :, 0:ATT_SPAN]
        lc = _bdot_nt(q, kc_ref[:, sl]) * scale + bias_ref[hh, :, ATT_SPAN:2 * ATT_SPAN]
        lp = jnp.where(first, NEG_INF, lp)
        m = jnp.maximum(jnp.max(lp, axis=-1, keepdims=True), jnp.max(lc, axis=-1, keepdims=True))
        pp = jnp.exp(lp - m)
        pc = jnp.exp(lc - m)
        den = jnp.sum(pp, axis=-1, keepdims=True) + jnp.sum(pc, axis=-1, keepdims=True)
        o_ref[:, sl] = (_bdot(pp, vp_ref[:, sl]) + _bdot(pc, vc_ref[:, sl])) / den
        stats = jnp.where(lane == hh, m, stats)
        stats = jnp.where(lane == ATT_HEADS_PER_GROUP + hh, den, stats)
    st_ref[...] = stats


def _attention_group(h, bias, group, dilation):
    s = h.shape[0]
    rows = s // dilation
    nb = rows // ATT_SPAN
    hv = h.reshape(rows, dilation * H_WIDTH)

    def spec(col, prev):
        if prev:
            return pl.BlockSpec((ATT_SPAN, BR_WIDTH), lambda r, n: (jnp.maximum(n - 1, 0), r * H_BLOCKS + col + group))
        return pl.BlockSpec((ATT_SPAN, BR_WIDTH), lambda r, n: (n, r * H_BLOCKS + col + group))

    o, st = pl.pallas_call(
        _attn_kernel,
        grid=(dilation, nb),
        in_specs=[spec(COL_Q, False), spec(COL_K, False), spec(COL_K, True), spec(COL_V, False), spec(COL_V, True),
                  pl.BlockSpec((ATT_HEADS_PER_GROUP, ATT_SPAN, 2 * ATT_SPAN), lambda r, n: (0, 0, 0))],
        out_specs=[pl.BlockSpec((ATT_SPAN, BR_WIDTH), lambda r, n: (n, r)),
                   pl.BlockSpec((ATT_SPAN, LANES), lambda r, n: (n, r))],
        out_shape=[jax.ShapeDtypeStruct((rows, dilation * BR_WIDTH), F32),
                   jax.ShapeDtypeStruct((rows, dilation * LANES), F32)],
        compiler_params=_params(("parallel", "arbitrary"), 32),
        name=f"dil_attn_d{dilation}",
    )(hv, hv, hv, hv, hv, bias)
    return o.reshape(s, BR_WIDTH), st.reshape(s, LANES)


def _attn_merge_kernel(o0, o1, o2, s0, s1, s2, gate_ref, y_ref):
    outs = (o0, o1, o2)
    stats = (s0[...], s1[...], s2[...])
    for hh in range(ATT_HEADS_PER_GROUP):
        sl = slice(hh * ATT_HEAD_DIM, (hh + 1) * ATT_HEAD_DIM)
        ms = [st[:, hh:hh + 1] for st in stats]
        dens = [st[:, ATT_HEADS_PER_GROUP + hh:ATT_HEADS_PER_GROUP + hh + 1] for st in stats]
        m_all = jnp.maximum(jnp.maximum(ms[0], ms[1]), ms[2])
        wts = [jnp.exp(m - m_all) * d for m, d in zip(ms, dens)]
        num = wts[0] * outs[0][:, sl] + wts[1] * outs[1][:, sl] + wts[2] * outs[2][:, sl]
        y = num / (wts[0] + wts[1] + wts[2])
        y_ref[:, sl] = (y * _silu(gate_ref[:, sl])).astype(BF16)


def _attn_merge(outs, stats, h, tile=512):
    s = h.shape[0]
    o_spec = pl.BlockSpec((tile, BR_WIDTH), lambda i: (i, 0))
    s_spec = pl.BlockSpec((tile, LANES), lambda i: (i, 0))
    return pl.pallas_call(
        _attn_merge_kernel,
        grid=(s // tile,),
        in_specs=[o_spec] * 3 + [s_spec] * 3 + [pl.BlockSpec((tile, BR_WIDTH), lambda i: (i, COL_B_GATE))],
        out_specs=o_spec,
        out_shape=jax.ShapeDtypeStruct((s, BR_WIDTH), BF16),
        compiler_params=_params(("parallel",), 32),
        name="attn_merge",
    )(*outs, *stats, h)


def _t5_bucket(dist):
    max_exact = N_BUCKETS // 2
    large = max_exact + (np.log(np.maximum(dist, 1) / max_exact) / math.log(MAX_DISTANCE / max_exact)
                         * (N_BUCKETS - max_exact)).astype(np.int32)
    large = np.minimum(large, N_BUCKETS - 1)
    return np.where(dist < max_exact, dist, large).astype(np.int32)


def _group_bias(table, dilation):
    qi = np.arange(ATT_SPAN)[:, None]
    kj = np.arange(2 * ATT_SPAN)[None, :]
    dist = qi + ATT_SPAN - kj
    valid = (dist >= 0) & (dist <= ATT_SPAN)
    bucket = _t5_bucket(np.clip(dist, 0, ATT_SPAN) * dilation)
    bias = jnp.where(valid[..., None], table[bucket].astype(F32), NEG_INF)
    return jnp.transpose(bias, (2, 0, 1))


def _rwkv_prep_kernel(r_ref, k_ref, v_ref, lora_ref, mu_r, mu_k, mu_v, mu_l, w0_ref, wup_ref, a0_ref, aup_ref,
                      kk_ref, ka_ref, rk_ref, seg_ref,
                      lw_o, r_o, k_o, v_o, a_o, b_o, bonus_o, carry, carry_l):
    t = r_ref.shape[0]

    @pl.when(pl.program_id(0) == 0)
    def _():
        carry[...] = jnp.zeros_like(carry)
        carry_l[...] = jnp.zeros_like(carry_l)

    def shift_mix(x, mu, prev_row):
        row = lax.broadcasted_iota(jnp.int32, x.shape, 0)
        x_prev = jnp.where(row == 0, prev_row, pltpu.roll(x, 1, 0))
        return x + mu * (x_prev - x)

    r_in, k_in, v_in, l_in = r_ref[...], k_ref[...], v_ref[...], lora_ref[...]
    r = shift_mix(r_in, mu_r[...], carry[0:1, :])
    kx = shift_mix(k_in, mu_k[...], carry[1:2, :])
    vv = shift_mix(v_in, mu_v[...], carry[2:3, :])
    lo = shift_mix(l_in, mu_l[...], carry_l[0:1, :])
    carry[0:1, :] = r_in[t - 1:t, :]
    carry[1:2, :] = k_in[t - 1:t, :]
    carry[2:3, :] = v_in[t - 1:t, :]
    carry_l[0:1, :] = l_in[t - 1:t, :]

    w_log = -_softplus(-(w0_ref[...] + _bdot(jnp.tanh(lo), wup_ref[...]))) - 0.5
    lw_o[...] = -jnp.exp(w_log)
    a_icl = jax.nn.sigmoid(a0_ref[...] + _bdot(lo, aup_ref[...]))

    seg = seg_ref[...]
    kk = kx * kk_ref[...]
    kk = kk / jnp.maximum(jnp.sqrt(_fdot(kk * kk, seg)), 1e-12)
    kc = kx * (1.0 + (a_icl - 1.0) * ka_ref[...])
    r_o[...] = r
    k_o[...] = kc
    v_o[...] = vv
    a_o[...] = -kk
    b_o[...] = kk * a_icl
    bonus_o[...] = _fdot(r * kc * rk_ref[...], seg) * vv


def _rwkv_prep(h, mu_r, mu_k, mu_v, mu_l, w0, wup, a0, aup, k_k, k_a, r_k, seg, tile=256):
    s = h.shape[0]
    row = lambda c: pl.BlockSpec((tile, BR_WIDTH), lambda i: (i, c))
    full = lambda shape: pl.BlockSpec(shape, lambda i: (0,) * len(shape))
    vec = full((1, BR_WIDTH))
    out = pl.BlockSpec((tile, BR_WIDTH), lambda i: (i, 0))
    return pl.pallas_call(
        _rwkv_prep_kernel,
        grid=(s // tile,),
        in_specs=[row(COL_C_R), row(COL_C_K), row(COL_C_V),
                  pl.BlockSpec((tile, LANES), lambda i: (i, COL_C_LORA * (BR_WIDTH // LANES))),
                  vec, vec, vec, full((1, LANES)), vec, full((LANES, BR_WIDTH)), vec, full((LANES, BR_WIDTH)),
                  vec, vec, vec, full((BR_WIDTH, BR_WIDTH))],
        out_specs=[out] * 7,
        out_shape=[jax.ShapeDtypeStruct((s, BR_WIDTH), F32)] * 7,
        scratch_shapes=[pltpu.VMEM((SUBLANES, BR_WIDTH), F32), pltpu.VMEM((SUBLANES, LANES), F32)],
        compiler_params=_params(("arbitrary",), 32),
        name="rwkv_prep",
    )(h, h, h, h, mu_r, mu_k, mu_v, mu_l, w0, wup, a0, aup, k_k, k_a, r_k, seg)


def _stack_heads(x):
    lane = lax.broadcasted_iota(jnp.int32, x.shape, 1)
    return jnp.concatenate([jnp.where(lane < RWKV_HEAD, x, 0.0), jnp.where(lane >= RWKV_HEAD, x, 0.0)], axis=0)


def _unit_lower_inverse(a_strict):
    n = a_strict.shape[0]
    ri = lax.broadcasted_iota(jnp.int32, (n, n), 0)
    ci = lax.broadcasted_iota(jnp.int32, (n, n), 1)

    def same_block(bits):
        return (ri >> bits) == (ci >> bits)

    n16 = jnp.where(same_block(4), a_strict, 0.0)
    x = jnp.where(ri == ci, 1.0, 0.0) + n16
    pw = n16
    for _ in range(3):
        pw = _bdot(pw, pw)
        x = x + _bdot(x, pw)
    for bits in (5, 6):
        e = jnp.where(same_block(bits) & jnp.logical_not(same_block(bits - 1)), a_strict, 0.0)
        x = x + _bdot(_bdot(x, e), x)
    return x


def _rwkv_chunk_kernel(lw_ref, r_ref, k_ref, v_ref, a_ref, b_ref, q_o, yc_o, g_o, z_o):
    c = RWKV_CHUNK
    n = 2 * c
    ti = lax.broadcasted_iota(jnp.int32, (c, c), 0)
    si = lax.broadcasted_iota(jnp.int32, (c, c), 1)
    lower_ones = jnp.where(si <= ti, 1.0, 0.0)
    ri = lax.broadcasted_iota(jnp.int32, (n, n), 0)
    ci = lax.broadcasted_iota(jnp.int32, (n, n), 1)
    same_head = (ri >> 6) == (ci >> 6)
    strict = same_head & (ri > ci)
    incl = same_head & (ri >= ci)
    eye = ri == ci

    for p in range(N_PAIRS):
        sl = slice(p * PAIR, (p + 1) * PAIR)
        lw, r, k, v, a, b = (ref[:, sl] for ref in (lw_ref, r_ref, k_ref, v_ref, a_ref, b_ref))
        cs = _fdot(lower_ones, lw)
        c_end = cs[c - 1:c, :]
        g_in = jnp.exp(cs)
        g_inv = jnp.exp(-cs)
        g_rem = jnp.exp(c_end - cs)
        r_t = _stack_heads(r * g_in)
        a_t = _stack_heads(a * jnp.exp(cs - lw))
        b_t = _stack_heads(b * g_inv)
        k_t = _stack_heads(k * g_inv)
        b_h = _stack_heads(b * g_rem)
        k_h = _stack_heads(k * g_rem)
        v_s = _stack_heads(v)

        aa = _bdot_nt(jnp.concatenate([a_t, r_t], axis=0), jnp.concatenate([b_t, k_t], axis=0))
        a_ab = jnp.where(strict, aa[0:n, 0:n], 0.0)
        a_ak = jnp.where(strict, aa[0:n, n:2 * n], 0.0)
        a_rb = jnp.where(incl, aa[n:2 * n, 0:n], 0.0)
        a_rk = jnp.where(incl, aa[n:2 * n, n:2 * n], 0.0)

        minv = _unit_lower_inverse(a_ab)
        w = _bdot(minv, a_t)
        uv = _bdot(minv, _bdot(a_ak, v_s))
        q_o[:, sl] = r_t + _bdot(a_rb, w)
        yc_o[:, sl] = _bdot(a_rb, uv) + _bdot(a_rk, v_s)
        g_o[:, sl] = jnp.where(eye, jnp.exp(c_end), 0.0) + _bdot_tn(w, b_h)
        z_o[:, sl] = _bdot_tn(jnp.concatenate([uv, v_s], axis=0), jnp.concatenate([b_h, k_h], axis=0))


def _rwkv_chunk(lw, r, k, v, a, b):
    s = lw.shape[0]
    nc = s // RWKV_CHUNK
    inp = pl.BlockSpec((RWKV_CHUNK, BR_WIDTH), lambda i: (i, 0))
    out = pl.BlockSpec((2 * RWKV_CHUNK, BR_WIDTH), lambda i: (i, 0))
    return pl.pallas_call(
        _rwkv_chunk_kernel,
        grid=(nc,),
        in_specs=[inp] * 6,
        out_specs=[out] * 4,
        out_shape=[jax.ShapeDtypeStruct((2 * s, BR_WIDTH), F32)] * 4,
        compiler_params=_params(("parallel",), 32),
        name="rwkv_chunk",
    )(lw, r, k, v, a, b)


def _rwkv_scan_kernel(q_ref, yc_ref, g_ref, z_ref, bonus_ref, gate_ref, gn_g, gn_b, seg_ref, o_ref, state, ybuf):
    c = RWKV_CHUNK
    n = 2 * c
    chunks = q_ref.shape[0] // n

    @pl.when(pl.program_id(0) == 0)
    def _():
        state[...] = jnp.zeros_like(state)

    for ch in range(chunks):
        rows = slice(ch * n, (ch + 1) * n)
        for p in range(N_PAIRS):
            sl = slice(p * PAIR, (p + 1) * PAIR)
            st = state[:, sl]
            y_st = _fdot_nt(q_ref[rows, sl], st) + yc_ref[rows, sl]
            ybuf[ch * c:(ch + 1) * c, sl] = y_st[0:c, :] + y_st[c:n, :]
            state[:, sl] = _fdot(st, g_ref[rows, sl]) + z_ref[rows, sl]

    wy = ybuf[...]
    seg = seg_ref[...]
    inv_n = 1.0 / RWKV_HEAD
    mu = _fdot(wy, seg) * inv_n
    d = wy - mu
    var = _fdot(d * d, seg) * inv_n
    wy = d * lax.rsqrt(var + RWKV_GN_EPS) * gn_g[...] + gn_b[...]
    o_ref[...] = ((wy + bonus_ref[...]) * _silu(gate_ref[...])).astype(BF16)


def _rwkv_scan(q, yc, g, z, bonus, h, gn_g, gn_b, seg, chunks=4):
    s = bonus.shape[0]
    tile = chunks * RWKV_CHUNK
    big = pl.BlockSpec((2 * tile, BR_WIDTH), lambda i: (i, 0))
    row = pl.BlockSpec((tile, BR_WIDTH), lambda i: (i, 0))
    full = lambda shape: pl.BlockSpec(shape, lambda i: (0,) * len(shape))
    return pl.pallas_call(
        _rwkv_scan_kernel,
        grid=(s // tile,),
        in_specs=[big] * 4 + [row, pl.BlockSpec((tile, BR_WIDTH), lambda i: (i, COL_C_GATE)),
                              full((1, BR_WIDTH)), full((1, BR_WIDTH)), full((BR_WIDTH, BR_WIDTH))],
        out_specs=row,
        out_shape=jax.ShapeDtypeStruct((s, BR_WIDTH), BF16),
        scratch_shapes=[pltpu.VMEM((PAIR, BR_WIDTH), F32), pltpu.VMEM((tile, BR_WIDTH), F32)],
        compiler_params=_params(("arbitrary",), 32),
        name="rwkv_scan",
    )(q, yc, g, z, bonus, h, gn_g, gn_b, seg)


def _mix_kernel(xb_ref, yg_ref, wm_ref, bm_ref, wbr_ref, o_ref, acc):
    nbr = pl.program_id(2)
    gate = jax.nn.sigmoid(jnp.dot(xb_ref[...], wm_ref[...], preferred_element_type=F32) + bm_ref[...])
    val = gate * jnp.dot(yg_ref[...], wbr_ref[...], preferred_element_type=F32)

    @pl.when(nbr == 0)
    def _():
        acc[...] = val

    @pl.when(nbr > 0)
    def _():
        acc[...] += val

    @pl.when(nbr == N_BRANCH - 1)
    def _():
        o_ref[...] = acc[...].astype(BF16)


def _mix(xb, yg, wm, bm, wbr, tm=512, tn=1024):
    s = xb.shape[0]
    return pl.pallas_call(
        _mix_kernel,
        grid=(s // tm, D_MODEL // tn, N_BRANCH),
        in_specs=[
            pl.BlockSpec((tm, D_MODEL), lambda i, j, n: (i, 0)),
            pl.BlockSpec((None, tm, BR_WIDTH), lambda i, j, n: (n, i, 0)),
            pl.BlockSpec((None, D_MODEL, tn), lambda i, j, n: (n, 0, j)),
            pl.BlockSpec((None, 1, tn), lambda i, j, n: (n, 0, j)),
            pl.BlockSpec((None, BR_WIDTH, tn), lambda i, j, n: (n, 0, j)),
        ],
        out_specs=pl.BlockSpec((tm, tn), lambda i, j, n: (i, j)),
        out_shape=jax.ShapeDtypeStruct((s, D_MODEL), BF16),
        scratch_shapes=[pltpu.VMEM((tm, tn), F32)],
        compiler_params=_params(("parallel", "arbitrary", "arbitrary"), 40),
        name="branch_mix",
    )(xb, yg, wm, bm, wbr)


def _out_kernel(mixed_ref, x_ref, w_ref, g_ref, b_ref, o_ref):
    y = ALPHA * x_ref[...] + jnp.dot(mixed_ref[...], w_ref[...], preferred_element_type=F32)
    mu = jnp.mean(y, axis=-1, keepdims=True)
    var = jnp.mean(jnp.square(y - mu), axis=-1, keepdims=True)
    o_ref[...] = (y - mu) * lax.rsqrt(var + LN_EPS) * g_ref[...] + b_ref[...]


def _out_proj(mixed, x, w, g, b, tm=512):
    s = x.shape[0]
    row = pl.BlockSpec((tm, D_MODEL), lambda i: (i, 0))
    vec = pl.BlockSpec((1, D_MODEL), lambda i: (0, 0))
    return pl.pallas_call(
        _out_kernel,
        grid=(s // tm,),
        in_specs=[row, row, pl.BlockSpec((D_MODEL, D_MODEL), lambda i: (0, 0)), vec, vec],
        out_specs=row,
        out_shape=jax.ShapeDtypeStruct((s, D_MODEL), F32),
        compiler_params=_params(("parallel",), 48),
        name="out_proj_ln",
    )(mixed, x, w, g, b)


def _block_diag(w):
    blocks, n, _ = w.shape
    eye = jnp.eye(blocks, dtype=w.dtype)
    return (eye[:, None, :, None] * w[:, :, None, :]).reshape(blocks * n, blocks * n)


def _branch_weights(w_in, b_in):
    a0 = 0
    c0 = 2 * BR_WIDTH + 3 * ATT_QKV + BR_WIDTH
    lora0 = c0 + 3 * BR_WIDTH
    cg0 = lora0 + DECAY_RANK + ICLR_RANK
    d0 = cg0 + BR_WIDTH
    end = d0 + 3 * BR_WIDTH
    pad = BR_WIDTH - (DECAY_RANK + ICLR_RANK)

    def build(m):
        z = jnp.zeros(m.shape[:-1] + (pad,), m.dtype)
        return jnp.concatenate([m[..., a0:c0], m[..., d0:end], m[..., c0:lora0], m[..., lora0:cg0], z,
                                m[..., cg0:d0]], axis=-1)

    return build(w_in).astype(BF16), build(b_in)[None, :]


def _layer(x, rel_bias, w_in, b_in, lru_conv_w, lru_conv_b, lru_gate_a_w, lru_gate_a_b, lru_gate_x_w, lru_gate_x_b,
           lru_lambda, rwkv_mu, rwkv_w0, rwkv_w_up, rwkv_a0, rwkv_a_up, rwkv_k_k, rwkv_k_a, rwkv_r_k, rwkv_gn_g,
           rwkv_gn_b, conf_dw_w, conf_dw_b, conf_ln_g, conf_ln_b, w_br, w_out, ln_g, ln_b):
    vec = lambda t: t.reshape(1, -1)
    w_h, b_h = _branch_weights(w_in[:, :BRANCH_IN], b_in[:BRANCH_IN])
    h, xb = _in_proj(x, w_h, b_h)

    yg_a = _lru(h, lru_conv_w, vec(lru_conv_b), _block_diag(lru_gate_a_w).astype(BF16), vec(lru_gate_a_b),
                _block_diag(lru_gate_x_w).astype(BF16), vec(lru_gate_x_b), vec(lru_lambda))

    outs, stats = [], []
    for g, (_, dil) in enumerate(ATT_GROUPS):
        hs = slice(g * ATT_HEADS_PER_GROUP, (g + 1) * ATT_HEADS_PER_GROUP)
        o_g, st_g = _attention_group(h, _group_bias(rel_bias[:, hs], dil), g, dil)
        outs.append(o_g)
        stats.append(st_g)
    yg_b = _attn_merge(outs, stats, h)

    mu = rwkv_mu
    zpad = jnp.zeros((DECAY_RANK, BR_WIDTH), F32)
    wup = jnp.concatenate([rwkv_w_up, zpad], axis=0).astype(BF16)
    aup = jnp.concatenate([zpad, rwkv_a_up], axis=0).astype(BF16)
    seg = _block_diag(jnp.ones((RWKV_HEADS, RWKV_HEAD, RWKV_HEAD), F32))
    lw, r, kc, vv, a, b, bonus = _rwkv_prep(
        h, vec(mu[:BR_WIDTH]), vec(mu[BR_WIDTH:2 * BR_WIDTH]), vec(mu[2 * BR_WIDTH:3 * BR_WIDTH]),
        vec(mu[3 * BR_WIDTH:]), vec(rwkv_w0), wup, vec(rwkv_a0), aup, vec(rwkv_k_k), vec(rwkv_k_a),
        vec(rwkv_r_k), seg)
    q, yc, g_mat, z = _rwkv_chunk(lw, r, kc, vv, a, b)
    yg_c = _rwkv_scan(q, yc, g_mat, z, bonus, h, vec(rwkv_gn_g), vec(rwkv_gn_b), seg)

    yg_d = _conformer(h, conf_dw_w, vec(conf_dw_b), vec(conf_ln_g), vec(conf_ln_b))

    yg = jnp.stack([yg_a, yg_b, yg_c, yg_d])
    wm = jnp.transpose(w_in[:, BRANCH_IN:].reshape(D_MODEL, N_BRANCH, D_MODEL), (1, 0, 2)).astype(BF16)
    bm = b_in[BRANCH_IN:].reshape(N_BRANCH, 1, D_MODEL)
    mixed = _mix(xb, yg, wm, bm, w_br.astype(BF16))
    return _out_proj(mixed, x, w_out.astype(BF16), vec(ln_g), vec(ln_b))


def kernel(x, att_rel_bias, w_in, b_in, lru_conv_w, lru_conv_b, lru_gate_a_w, lru_gate_a_b, lru_gate_x_w, lru_gate_x_b, lru_lambda, rwkv_mu, rwkv_w0, rwkv_w_up, rwkv_a0, rwkv_a_up, rwkv_k_k, rwkv_k_a, rwkv_r_k, rwkv_gn_g, rwkv_gn_b, conf_dw_w, conf_dw_b, conf_ln_g, conf_ln_b, w_br, w_out, ln_g, ln_b):
    bsz, s, d = x.shape
    assert bsz == 1 and d == D_MODEL and s % (16 * ATT_SPAN) == 0
    per_layer = (w_in, b_in, lru_conv_w, lru_conv_b, lru_gate_a_w, lru_gate_a_b, lru_gate_x_w, lru_gate_x_b,
                 lru_lambda, rwkv_mu, rwkv_w0, rwkv_w_up, rwkv_a0, rwkv_a_up, rwkv_k_k, rwkv_k_a, rwkv_r_k,
                 rwkv_gn_g, rwkv_gn_b, conf_dw_w, conf_dw_b, conf_ln_g, conf_ln_b, w_br, w_out, ln_g, ln_b)
    y = x.reshape(s, d)
    for l in range(DEPTH):
        y = _layer(y, att_rel_bias, *(t[l] for t in per_layer))
    return y.reshape(bsz, s, d)
```

```python
import functools
import math

import numpy as np
import jax
import jax.numpy as jnp
from jax import lax
from jax.experimental import pallas as pl
from jax.experimental.pallas import tpu as pltpu

D_MODEL = 2048
DEPTH = 2
N_BRANCH = 4
BR_WIDTH = 512
LRU_BLOCKS = 8
LRU_BLOCK = BR_WIDTH // LRU_BLOCKS
LRU_CONV = 4
LRU_C = 8.0
ATT_GROUPS = ((128, 1), (512, 4), (2048, 16))
ATT_HEADS_PER_GROUP = 4
ATT_HEAD_DIM = BR_WIDTH // ATT_HEADS_PER_GROUP
ATT_HEADS = len(ATT_GROUPS) * ATT_HEADS_PER_GROUP
ATT_QKV = ATT_HEADS * ATT_HEAD_DIM
ATT_SPAN = 128
N_BUCKETS = 32
MAX_DISTANCE = 2048
NEG_INF = -1e30
RWKV_HEAD = 64
RWKV_HEADS = BR_WIDTH // RWKV_HEAD
DECAY_RANK = 64
ICLR_RANK = 64
RWKV_GN_EPS = 64e-5
CONF_KERNEL = 31
LN_EPS = 1e-5
ALPHA = (2.0 * DEPTH) ** 0.25

LANES = 128
SUBLANES = 8
MIB = 1024 * 1024

COL_A_X, COL_A_GATE = 0, 1
COL_Q, COL_K, COL_V, COL_B_GATE = 2, 5, 8, 11
COL_D_VAL, COL_D_GLU, COL_D_GATE = 12, 13, 14
COL_C_R, COL_C_K, COL_C_V, COL_C_LORA, COL_C_GATE = 15, 16, 17, 18, 19
H_BLOCKS = 20
H_WIDTH = H_BLOCKS * BR_WIDTH
BRANCH_IN = 2 * BR_WIDTH + 3 * ATT_QKV + BR_WIDTH + (4 * BR_WIDTH + DECAY_RANK + ICLR_RANK) + 3 * BR_WIDTH

RWKV_CHUNK = 64
PAIR = 2 * RWKV_HEAD
N_PAIRS = BR_WIDTH // PAIR

F32 = jnp.float32
BF16 = jnp.bfloat16
HIGHEST = lax.Precision.HIGHEST


def _params(semantics, vmem_mib):
    return pltpu.CompilerParams(dimension_semantics=semantics, vmem_limit_bytes=vmem_mib * MIB)


def _bdot(a, b):
    return jnp.dot(a.astype(BF16), b.astype(BF16), preferred_element_type=F32)


def _bdot_nt(a, b):
    return lax.dot_general(a.astype(BF16), b.astype(BF16), (((1,), (1,)), ((), ())), preferred_element_type=F32)


def _bdot_tn(a, b):
    return lax.dot_general(a.astype(BF16), b.astype(BF16), (((0,), (0,)), ((), ())), preferred_element_type=F32)


def _fdot(a, b):
    return jnp.dot(a, b, preferred_element_type=F32, precision=HIGHEST)


def _fdot_nt(a, b):
    return lax.dot_general(a, b, (((1,), (1,)), ((), ())), preferred_element_type=F32, precision=HIGHEST)


def _softplus(z):
    return jnp.maximum(z, 0.0) + jnp.log1p(jnp.exp(-jnp.abs(z)))


def _expm1_nonpos(z):
    u = jnp.exp(z)
    safe = jnp.where(u == 1.0, 0.5, u)
    return jnp.where(u == 1.0, z, jnp.where(u == 0.0, -1.0, (safe - 1.0) * z / jnp.log(safe)))


def _silu(z):
    return z * jax.nn.sigmoid(z)


def _in_proj_kernel(x_ref, w_ref, b_ref, h_ref, xb_ref):
    @pl.when(pl.program_id(1) == 0)
    def _():
        xb_ref[...] = x_ref[...].astype(BF16)

    h_ref[...] = jnp.dot(xb_ref[...], w_ref[...], preferred_element_type=F32) + b_ref[...]


def _in_proj(x, w, b, tm=1024, tn=1024):
    s, k = x.shape
    n = w.shape[1]
    return pl.pallas_call(
        _in_proj_kernel,
        grid=(s // tm, n // tn),
        in_specs=[
            pl.BlockSpec((tm, k), lambda i, j: (i, 0)),
            pl.BlockSpec((k, tn), lambda i, j: (0, j)),
            pl.BlockSpec((1, tn), lambda i, j: (0, j)),
        ],
        out_specs=[
            pl.BlockSpec((tm, tn), lambda i, j: (i, j)),
            pl.BlockSpec((tm, k), lambda i, j: (i, 0)),
        ],
        out_shape=[jax.ShapeDtypeStruct((s, n), F32), jax.ShapeDtypeStruct((s, k), BF16)],
        compiler_params=_params(("parallel", "arbitrary"), 48),
        name="in_proj",
    )(x, w, b)


def _lru_kernel(ax_ref, ag_ref, cw_ref, cb_ref, wa_ref, ba_ref, wx_ref, bx_ref, lam_ref, o_ref, ebuf, hc):
    t = ax_ref.shape[0]
    halo = SUBLANES

    @pl.when(pl.program_id(0) == 0)
    def _():
        ebuf[0:halo, :] = jnp.zeros((halo, BR_WIDTH), F32)
        hc[...] = jnp.zeros_like(hc)

    x = ax_ref[...]
    ebuf[halo:halo + t, :] = x
    u = cb_ref[...] + jnp.zeros((t, BR_WIDTH), F32)
    for j in range(LRU_CONV):
        u = u + cw_ref[j:j + 1, :] * ebuf[pl.ds(halo - (LRU_CONV - 1) + j, t), :]
    ebuf[0:halo, :] = x[t - halo:t, :]

    gate_r = jax.nn.sigmoid(_bdot(u, wa_ref[...]) + ba_ref[...])
    gate_i = jax.nn.sigmoid(_bdot(u, wx_ref[...]) + bx_ref[...])
    log_a = -LRU_C * gate_r * _softplus(-lam_ref[...])
    a = jnp.exp(log_a)
    b = jnp.sqrt(-_expm1_nonpos(2.0 * log_a)) * (gate_i * u)

    row = lax.broadcasted_iota(jnp.int32, (t, BR_WIDTH), 0)
    shift = 1
    while shift < t:
        valid = row >= shift
        b = jnp.where(valid, a * pltpu.roll(b, shift, 0), 0.0) + b
        a = jnp.where(valid, a * pltpu.roll(a, shift, 0), a)
        shift *= 2
    h = a * hc[0:1, :] + b
    hc[0:1, :] = h[t - 1:t, :]
    o_ref[...] = (h * _silu(ag_ref[...])).astype(BF16)


def _lru(h, cw, cb, wa, ba, wx, bx, lam, tile=256):
    s = h.shape[0]
    row = lambda c: pl.BlockSpec((tile, BR_WIDTH), lambda i: (i, c))
    full = lambda shape: pl.BlockSpec(shape, lambda i: (0,) * len(shape))
    return pl.pallas_call(
        _lru_kernel,
        grid=(s // tile,),
        in_specs=[row(COL_A_X), row(COL_A_GATE), full((LRU_CONV, BR_WIDTH)), full((1, BR_WIDTH)),
                  full((BR_WIDTH, BR_WIDTH)), full((1, BR_WIDTH)), full((BR_WIDTH, BR_WIDTH)), full((1, BR_WIDTH)),
                  full((1, BR_WIDTH))],
        out_specs=pl.BlockSpec((tile, BR_WIDTH), lambda i: (i, 0)),
        out_shape=jax.ShapeDtypeStruct((s, BR_WIDTH), BF16),
        scratch_shapes=[pltpu.VMEM((tile + SUBLANES, BR_WIDTH), F32), pltpu.VMEM((SUBLANES, BR_WIDTH), F32)],
        compiler_params=_params(("arbitrary",), 32),
        name="rglru",
    )(h, h, cw, cb, wa, ba, wx, bx, lam)


def _conf_kernel(val_ref, glu_ref, gate_ref, w_ref, b_ref, g_ref, beta_ref, o_ref, ebuf):
    t = val_ref.shape[0]
    halo = 32

    @pl.when(pl.program_id(0) == 0)
    def _():
        ebuf[0:halo, :] = jnp.zeros((halo, BR_WIDTH), F32)

    cu = val_ref[...] * jax.nn.sigmoid(glu_ref[...])
    ebuf[halo:halo + t, :] = cu
    acc = b_ref[...] + jnp.zeros((t, BR_WIDTH), F32)
    for j in range(CONF_KERNEL):
        acc = acc + w_ref[j:j + 1, :] * ebuf[pl.ds(halo - (CONF_KERNEL - 1) + j, t), :]
    ebuf[0:halo, :] = cu[t - halo:t, :]

    mu = jnp.mean(acc, axis=-1, keepdims=True)
    var = jnp.mean(jnp.square(acc - mu), axis=-1, keepdims=True)
    ln = (acc - mu) * lax.rsqrt(var + LN_EPS) * g_ref[...] + beta_ref[...]
    o_ref[...] = (_silu(ln) * _silu(gate_ref[...])).astype(BF16)


def _conformer(h, w, b, g, beta, tile=256):
    s = h.shape[0]
    row = lambda c: pl.BlockSpec((tile, BR_WIDTH), lambda i: (i, c))
    full = lambda shape: pl.BlockSpec(shape, lambda i: (0,) * len(shape))
    return pl.pallas_call(
        _conf_kernel,
        grid=(s // tile,),
        in_specs=[row(COL_D_VAL), row(COL_D_GLU), row(COL_D_GATE), full((CONF_KERNEL, BR_WIDTH)),
                  full((1, BR_WIDTH)), full((1, BR_WIDTH)), full((1, BR_WIDTH))],
        out_specs=pl.BlockSpec((tile, BR_WIDTH), lambda i: (i, 0)),
        out_shape=jax.ShapeDtypeStruct((s, BR_WIDTH), BF16),
        scratch_shapes=[pltpu.VMEM((tile + 32, BR_WIDTH), F32)],
        compiler_params=_params(("arbitrary",), 32),
        name="conformer",
    )(h, h, h, w, b, g, beta)


def _t5_bucket(dist):
    max_exact = N_BUCKETS // 2
    large = max_exact + (np.log(np.maximum(dist, 1) / max_exact) / math.log(MAX_DISTANCE / max_exact)
                         * (N_BUCKETS - max_exact)).astype(np.int32)
    large = np.minimum(large, N_BUCKETS - 1)
    return np.where(dist < max_exact, dist, large).astype(np.int32)


def _bucket_index():
    qi = np.arange(ATT_SPAN)[:, None]
    kj = np.arange(2 * ATT_SPAN)[None, :]
    dist = qi + ATT_SPAN - kj
    valid = (dist >= 0) & (dist <= ATT_SPAN)
    per_group = [np.where(valid, _t5_bucket(np.clip(dist, 0, ATT_SPAN) * dil), -1) for _, dil in ATT_GROUPS]
    return np.stack(per_group).astype(np.int32)


def _bias_kernel(table_ref, bucket_ref, o_ref):
    head = pl.program_id(0)
    bucket = bucket_ref[...]
    acc = jnp.full(bucket.shape, NEG_INF, F32)
    for bkt in range(N_BUCKETS):
        acc = jnp.where(bucket == bkt, table_ref[bkt, head], acc)
    o_ref[...] = acc


def _attn_bias(table):
    blk = (None, ATT_SPAN, 2 * ATT_SPAN)
    return pl.pallas_call(
        _bias_kernel,
        grid=(ATT_HEADS,),
        in_specs=[pl.BlockSpec(memory_space=pltpu.SMEM),
                  pl.BlockSpec(blk, lambda hd: (hd // ATT_HEADS_PER_GROUP, 0, 0))],
        out_specs=pl.BlockSpec(blk, lambda hd: (hd, 0, 0)),
        out_shape=jax.ShapeDtypeStruct((ATT_HEADS, ATT_SPAN, 2 * ATT_SPAN), F32),
        compiler_params=_params(("parallel",), 32),
        name="attn_bias",
    )(table, jnp.asarray(_bucket_index()))


def _attn_kernel(q_ref, kc_ref, kp_ref, vc_ref, vp_ref, bias_ref, o_ref, st_ref, *, dilation):
    first = pl.program_id(0) == 0
    hh = pl.program_id(1)
    scale = ATT_HEAD_DIM ** -0.5
    lane = lax.broadcasted_iota(jnp.int32, (ATT_SPAN, LANES), 1)

    @pl.when(hh == 0)
    def _():
        st_ref[...] = jnp.zeros_like(st_ref)

    bias_p = bias_ref[:, 0:ATT_SPAN]
    bias_c = bias_ref[:, ATT_SPAN:2 * ATT_SPAN]
    for r in range(dilation):
        rows = pl.ds(r, ATT_SPAN, stride=dilation) if dilation > 1 else slice(None)
        q = q_ref[rows, :]
        lp = _bdot_nt(q, kp_ref[rows, :]) * scale + bias_p
        lc = _bdot_nt(q, kc_ref[rows, :]) * scale + bias_c
        lp = jnp.where(first, NEG_INF, lp)
        m = jnp.maximum(jnp.max(lp, axis=-1, keepdims=True), jnp.max(lc, axis=-1, keepdims=True))
        pp = jnp.exp(lp - m)
        pc = jnp.exp(lc - m)
        den = jnp.sum(pp, axis=-1, keepdims=True) + jnp.sum(pc, axis=-1, keepdims=True)
        o_ref[rows, :] = (_bdot(pp, vp_ref[rows, :]) + _bdot(pc, vc_ref[rows, :])) / den
        st = jnp.where(lane == hh, m, st_ref[rows, :])
        st_ref[rows, :] = jnp.where(lane == ATT_HEADS_PER_GROUP + hh, den, st)


def _attention_group(h, bias, group, dilation):
    s = h.shape[0]
    blk = ATT_SPAN * dilation
    heads = ATT_HEADS_PER_GROUP

    def spec(col, prev):
        base = (col + group) * heads
        if prev:
            return pl.BlockSpec((blk, ATT_HEAD_DIM), lambda n, hd: (jnp.maximum(n - 1, 0), base + hd))
        return pl.BlockSpec((blk, ATT_HEAD_DIM), lambda n, hd: (n, base + hd))

    return pl.pallas_call(
        functools.partial(_attn_kernel, dilation=dilation),
        grid=(s // blk, heads),
        in_specs=[spec(COL_Q, False), spec(COL_K, False), spec(COL_K, True), spec(COL_V, False), spec(COL_V, True),
                  pl.BlockSpec((None, ATT_SPAN, 2 * ATT_SPAN), lambda n, hd: (group * heads + hd, 0, 0))],
        out_specs=[pl.BlockSpec((blk, ATT_HEAD_DIM), lambda n, hd: (n, hd)),
                   pl.BlockSpec((blk, LANES), lambda n, hd: (n, 0))],
        out_shape=[jax.ShapeDtypeStruct((s, BR_WIDTH), F32), jax.ShapeDtypeStruct((s, LANES), F32)],
        compiler_params=_params(("parallel", "arbitrary"), 32),
        name=f"dil_attn_d{dilation}",
    )(h, h, h, h, h, bias)


def _attn_merge_kernel(o0, o1, o2, s0, s1, s2, gate_ref, y_ref):
    outs = (o0, o1, o2)
    stats = (s0[...], s1[...], s2[...])
    for hh in range(ATT_HEADS_PER_GROUP):
        sl = slice(hh * ATT_HEAD_DIM, (hh + 1) * ATT_HEAD_DIM)
        ms = [st[:, hh:hh + 1] for st in stats]
        dens = [st[:, ATT_HEADS_PER_GROUP + hh:ATT_HEADS_PER_GROUP + hh + 1] for st in stats]
        m_all = jnp.maximum(jnp.maximum(ms[0], ms[1]), ms[2])
        wts = [jnp.exp(m - m_all) * d for m, d in zip(ms, dens)]
        num = wts[0] * outs[0][:, sl] + wts[1] * outs[1][:, sl] + wts[2] * outs[2][:, sl]
        y = num / (wts[0] + wts[1] + wts[2])
        y_ref[:, sl] = (y * _silu(gate_ref[:, sl])).astype(BF16)


def _attn_merge(outs, stats, h, tile=512):
    s = h.shape[0]
    o_spec = pl.BlockSpec((tile, BR_WIDTH), lambda i: (i, 0))
    s_spec = pl.BlockSpec((tile, LANES), lambda i: (i, 0))
    return pl.pallas_call(
        _attn_merge_kernel,
        grid=(s // tile,),
        in_specs=[o_spec] * 3 + [s_spec] * 3 + [pl.BlockSpec((tile, BR_WIDTH), lambda i: (i, COL_B_GATE))],
        out_specs=o_spec,
        out_shape=jax.ShapeDtypeStruct((s, BR_WIDTH), BF16),
        compiler_params=_params(("parallel",), 32),
        name="attn_merge",
    )(*outs, *stats, h)


def _rwkv_prep_kernel(r_ref, k_ref, v_ref, lora_ref, mu_r, mu_k, mu_v, mu_l, w0_ref, wup_ref, a0_ref, aup_ref,
                      kk_ref, ka_ref, rk_ref, seg_ref,
                      lw_o, r_o, k_o, v_o, a_o, b_o, bonus_o, carry, carry_l):
    t = r_ref.shape[0]

    @pl.when(pl.program_id(0) == 0)
    def _():
        carry[...] = jnp.zeros_like(carry)
        carry_l[...] = jnp.zeros_like(carry_l)

    def shift_mix(x, mu, prev_row):
        row = lax.broadcasted_iota(jnp.int32, x.shape, 0)
        x_prev = jnp.where(row == 0, prev_row, pltpu.roll(x, 1, 0))
        return x + mu * (x_prev - x)

    r_in, k_in, v_in, l_in = r_ref[...], k_ref[...], v_ref[...], lora_ref[...]
    r = shift_mix(r_in, mu_r[...], carry[0:1, :])
    kx = shift_mix(k_in, mu_k[...], carry[1:2, :])
    vv = shift_mix(v_in, mu_v[...], carry[2:3, :])
    lo = shift_mix(l_in, mu_l[...], carry_l[0:1, :])
    carry[0:1, :] = r_in[t - 1:t, :]
    carry[1:2, :] = k_in[t - 1:t, :]
    carry[2:3, :] = v_in[t - 1:t, :]
    carry_l[0:1, :] = l_in[t - 1:t, :]

    w_log = -_softplus(-(w0_ref[...] + _bdot(jnp.tanh(lo), wup_ref[...]))) - 0.5
    lw_o[...] = -jnp.exp(w_log)
    a_icl = jax.nn.sigmoid(a0_ref[...] + _bdot(lo, aup_ref[...]))

    seg = seg_ref[...]
    kk = kx * kk_ref[...]
    kk = kk / jnp.maximum(jnp.sqrt(_fdot(kk * kk, seg)), 1e-12)
    kc = kx * (1.0 + (a_icl - 1.0) * ka_ref[...])
    r_o[...] = r
    k_o[...] = kc
    v_o[...] = vv
    a_o[...] = -kk
    b_o[...] = kk * a_icl
    bonus_o[...] = _fdot(r * kc * rk_ref[...], seg) * vv


def _rwkv_prep(h, mu_r, mu_k, mu_v, mu_l, w0, wup, a0, aup, k_k, k_a, r_k, seg, tile=256):
    s = h.shape[0]
    row = lambda c: pl.BlockSpec((tile, BR_WIDTH), lambda i: (i, c))
    full = lambda shape: pl.BlockSpec(shape, lambda i: (0,) * len(shape))
    vec = full((1, BR_WIDTH))
    out = pl.BlockSpec((tile, BR_WIDTH), lambda i: (i, 0))
    return pl.pallas_call(
        _rwkv_prep_kernel,
        grid=(s // tile,),
        in_specs=[row(COL_C_R), row(COL_C_K), row(COL_C_V),
                  pl.BlockSpec((tile, LANES), lambda i: (i, COL_C_LORA * (BR_WIDTH // LANES))),
                  vec, vec, vec, full((1, LANES)), vec, full((LANES, BR_WIDTH)), vec, full((LANES, BR_WIDTH)),
                  vec, vec, vec, full((BR_WIDTH, BR_WIDTH))],
        out_specs=[out] * 7,
        out_shape=[jax.ShapeDtypeStruct((s, BR_WIDTH), F32)] * 7,
        scratch_shapes=[pltpu.VMEM((SUBLANES, BR_WIDTH), F32), pltpu.VMEM((SUBLANES, LANES), F32)],
        compiler_params=_params(("arbitrary",), 32),
        name="rwkv_prep",
    )(h, h, h, h, mu_r, mu_k, mu_v, mu_l, w0, wup, a0, aup, k_k, k_a, r_k, seg)


def _stack_heads(x):
    lane = lax.broadcasted_iota(jnp.int32, x.shape, 1)
    return jnp.concatenate([jnp.where(lane < RWKV_HEAD, x, 0.0), jnp.where(lane >= RWKV_HEAD, x, 0.0)], axis=0)


def _unit_lower_inverse(a_strict):
    n = a_strict.shape[0]
    ri = lax.broadcasted_iota(jnp.int32, (n, n), 0)
    ci = lax.broadcasted_iota(jnp.int32, (n, n), 1)

    def same_block(bits):
        return (ri >> bits) == (ci >> bits)

    n16 = jnp.where(same_block(4), a_strict, 0.0)
    x = jnp.where(ri == ci, 1.0, 0.0) + n16
    pw = n16
    for _ in range(3):
        pw = _bdot(pw, pw)
        x = x + _bdot(x, pw)
    for bits in (5, 6):
        e = jnp.where(same_block(bits) & jnp.logical_not(same_block(bits - 1)), a_strict, 0.0)
        x = x + _bdot(_bdot(x, e), x)
    return x


def _rwkv_chunk_kernel(lw_ref, r_ref, k_ref, v_ref, a_ref, b_ref, q_o, yc_o, g_o, z_o):
    c = RWKV_CHUNK
    n = 2 * c
    ti = lax.broadcasted_iota(jnp.int32, (c, c), 0)
    si = lax.broadcasted_iota(jnp.int32, (c, c), 1)
    lower_ones = jnp.where(si <= ti, 1.0, 0.0)
    ri = lax.broadcasted_iota(jnp.int32, (n, n), 0)
    ci = lax.broadcasted_iota(jnp.int32, (n, n), 1)
    same_head = (ri >> 6) == (ci >> 6)
    strict = same_head & (ri > ci)
    incl = same_head & (ri >= ci)
    eye = ri == ci

    for p in range(N_PAIRS):
        sl = slice(p * PAIR, (p + 1) * PAIR)
        lw, r, k, v, a, b = (ref[:, sl] for ref in (lw_ref, r_ref, k_ref, v_ref, a_ref, b_ref))
        cs = _fdot(lower_ones, lw)
        c_end = cs[c - 1:c, :]
        g_in = jnp.exp(cs)
        g_inv = jnp.exp(-cs)
        g_rem = jnp.exp(c_end - cs)
        r_t = _stack_heads(r * g_in)
        a_t = _stack_heads(a * jnp.exp(cs - lw))
        b_t = _stack_heads(b * g_inv)
        k_t = _stack_heads(k * g_inv)
        b_h = _stack_heads(b * g_rem)
        k_h = _stack_heads(k * g_rem)
        v_s = _stack_heads(v)

        aa = _bdot_nt(jnp.concatenate([a_t, r_t], axis=0), jnp.concatenate([b_t, k_t], axis=0))
        a_ab = jnp.where(strict, aa[0:n, 0:n], 0.0)
        a_ak = jnp.where(strict, aa[0:n, n:2 * n], 0.0)
        a_rb = jnp.where(incl, aa[n:2 * n, 0:n], 0.0)
        a_rk = jnp.where(incl, aa[n:2 * n, n:2 * n], 0.0)

        minv = _unit_lower_inverse(a_ab)
        w = _bdot(minv, a_t)
        uv = _bdot(minv, _bdot(a_ak, v_s))
        q_o[:, sl] = r_t + _bdot(a_rb, w)
        yc_o[:, sl] = _bdot(a_rb, uv) + _bdot(a_rk, v_s)
        g_o[:, sl] = jnp.where(eye, jnp.exp(c_end), 0.0) + _bdot_tn(w, b_h)
        z_o[:, sl] = _bdot_tn(jnp.concatenate([uv, v_s], axis=0), jnp.concatenate([b_h, k_h], axis=0))


def _rwkv_chunk(lw, r, k, v, a, b):
    s = lw.shape[0]
    nc = s // RWKV_CHUNK
    inp = pl.BlockSpec((RWKV_CHUNK, BR_WIDTH), lambda i: (i, 0))
    out = pl.BlockSpec((2 * RWKV_CHUNK, BR_WIDTH), lambda i: (i, 0))
    return pl.pallas_call(
        _rwkv_chunk_kernel,
        grid=(nc,),
        in_specs=[inp] * 6,
        out_specs=[out] * 4,
        out_shape=[jax.ShapeDtypeStruct((2 * s, BR_WIDTH), F32)] * 4,
        compiler_params=_params(("parallel",), 32),
        name="rwkv_chunk",
    )(lw, r, k, v, a, b)


def _rwkv_scan_kernel(q_ref, yc_ref, g_ref, z_ref, bonus_ref, gate_ref, gn_g, gn_b, seg_ref, o_ref, state, ybuf):
    c = RWKV_CHUNK
    n = 2 * c
    chunks = q_ref.shape[0] // n

    @pl.when(pl.program_id(0) == 0)
    def _():
        state[...] = jnp.zeros_like(state)

    for ch in range(chunks):
        rows = slice(ch * n, (ch + 1) * n)
        for p in range(N_PAIRS):
            sl = slice(p * PAIR, (p + 1) * PAIR)
            st = state[:, sl]
            y_st = _fdot_nt(q_ref[rows, sl], st) + yc_ref[rows, sl]
            ybuf[ch * c:(ch + 1) * c, sl] = y_st[0:c, :] + y_st[c:n, :]
            state[:, sl] = _fdot(st, g_ref[rows, sl]) + z_ref[rows, sl]

    wy = ybuf[...]
    seg = seg_ref[...]
    inv_n = 1.0 / RWKV_HEAD
    mu = _fdot(wy, seg) * inv_n
    d = wy - mu
    var = _fdot(d * d, seg) * inv_n
    wy = d * lax.rsqrt(var + RWKV_GN_EPS) * gn_g[...] + gn_b[...]
    o_ref[...] = ((wy + bonus_ref[...]) * _silu(gate_ref[...])).astype(BF16)


def _rwkv_scan(q, yc, g, z, bonus, h, gn_g, gn_b, seg, chunks=4):
    s = bonus.shape[0]
    tile = chunks * RWKV_CHUNK
    big = pl.BlockSpec((2 * tile, BR_WIDTH), lambda i: (i, 0))
    row = pl.BlockSpec((tile, BR_WIDTH), lambda i: (i, 0))
    full = lambda shape: pl.BlockSpec(shape, lambda i: (0,) * len(shape))
    return pl.pallas_call(
        _rwkv_scan_kernel,
        grid=(s // tile,),
        in_specs=[big] * 4 + [row, pl.BlockSpec((tile, BR_WIDTH), lambda i: (i, COL_C_GATE)),
                              full((1, BR_WIDTH)), full((1, BR_WIDTH)), full((BR_WIDTH, BR_WIDTH))],
        out_specs=row,
        out_shape=jax.ShapeDtypeStruct((s, BR_WIDTH), BF16),
        scratch_shapes=[pltpu.VMEM((PAIR, BR_WIDTH), F32), pltpu.VMEM((tile, BR_WIDTH), F32)],
        compiler_params=_params(("arbitrary",), 32),
        name="rwkv_scan",
    )(q, yc, g, z, bonus, h, gn_g, gn_b, seg)


def _mix_kernel(xb_ref, *refs):
    ygs, wms, bms, wbrs = (refs[k * N_BRANCH:(k + 1) * N_BRANCH] for k in range(4))
    o_ref = refs[4 * N_BRANCH]
    xb = xb_ref[...]
    acc = None
    for n in range(N_BRANCH):
        gate = jax.nn.sigmoid(jnp.dot(xb, wms[n][...], preferred_element_type=F32) + bms[n][...])
        val = gate * jnp.dot(ygs[n][...], wbrs[n][...], preferred_element_type=F32)
        acc = val if acc is None else acc + val
    o_ref[...] = acc.astype(BF16)


def _mix(xb, ygs, wm, bm, wbr, tm=1024, tn=512):
    s = xb.shape[0]
    nj = D_MODEL // tn
    per_branch = lambda make: [make(n) for n in range(N_BRANCH)]
    return pl.pallas_call(
        _mix_kernel,
        grid=(s // tm, nj),
        in_specs=[pl.BlockSpec((tm, D_MODEL), lambda i, j: (i, 0))]
        + per_branch(lambda n: pl.BlockSpec((tm, BR_WIDTH), lambda i, j: (i, 0)))
        + per_branch(lambda n: pl.BlockSpec((D_MODEL, tn), lambda i, j: (0, n * nj + j)))
        + per_branch(lambda n: pl.BlockSpec((1, tn), lambda i, j: (0, n * nj + j)))
        + per_branch(lambda n: pl.BlockSpec((None, BR_WIDTH, tn), lambda i, j: (n, 0, j))),
        out_specs=pl.BlockSpec((tm, tn), lambda i, j: (i, j)),
        out_shape=jax.ShapeDtypeStruct((s, D_MODEL), BF16),
        compiler_params=_params(("parallel", "arbitrary"), 48),
        name="branch_mix",
    )(xb, *ygs, *([wm] * N_BRANCH), *([bm] * N_BRANCH), *([wbr] * N_BRANCH))


def _out_kernel(mixed_ref, x_ref, w_ref, g_ref, b_ref, o_ref):
    y = ALPHA * x_ref[...] + jnp.dot(mixed_ref[...], w_ref[...], preferred_element_type=F32)
    mu = jnp.mean(y, axis=-1, keepdims=True)
    var = jnp.mean(jnp.square(y - mu), axis=-1, keepdims=True)
    o_ref[...] = (y - mu) * lax.rsqrt(var + LN_EPS) * g_ref[...] + b_ref[...]


def _out_proj(mixed, x, w, g, b, tm=512):
    s = x.shape[0]
    row = pl.BlockSpec((tm, D_MODEL), lambda i: (i, 0))
    vec = pl.BlockSpec((1, D_MODEL), lambda i: (0, 0))
    return pl.pallas_call(
        _out_kernel,
        grid=(s // tm,),
        in_specs=[row, row, pl.BlockSpec((D_MODEL, D_MODEL), lambda i: (0, 0)), vec, vec],
        out_specs=row,
        out_shape=jax.ShapeDtypeStruct((s, D_MODEL), F32),
        compiler_params=_params(("parallel",), 48),
        name="out_proj_ln",
    )(mixed, x, w, g, b)


def _block_diag(w):
    blocks, n, _ = w.shape
    eye = jnp.eye(blocks, dtype=w.dtype)
    return (eye[:, None, :, None] * w[:, :, None, :]).reshape(blocks * n, blocks * n)


def _branch_weights(w_in, b_in):
    a0 = 0
    c0 = 2 * BR_WIDTH + 3 * ATT_QKV + BR_WIDTH
    lora0 = c0 + 3 * BR_WIDTH
    cg0 = lora0 + DECAY_RANK + ICLR_RANK
    d0 = cg0 + BR_WIDTH
    end = d0 + 3 * BR_WIDTH
    pad = BR_WIDTH - (DECAY_RANK + ICLR_RANK)

    def build(m):
        z = jnp.zeros(m.shape[:-1] + (pad,), m.dtype)
        return jnp.concatenate([m[..., a0:c0], m[..., d0:end], m[..., c0:lora0], m[..., lora0:cg0], z,
                                m[..., cg0:d0]], axis=-1)

    return build(w_in).astype(BF16), build(b_in)[None, :]


def _layer(x, att_bias, w_in, b_in, lru_conv_w, lru_conv_b, lru_gate_a_w, lru_gate_a_b, lru_gate_x_w, lru_gate_x_b,
           lru_lambda, rwkv_mu, rwkv_w0, rwkv_w_up, rwkv_a0, rwkv_a_up, rwkv_k_k, rwkv_k_a, rwkv_r_k, rwkv_gn_g,
           rwkv_gn_b, conf_dw_w, conf_dw_b, conf_ln_g, conf_ln_b, w_br, w_out, ln_g, ln_b):
    vec = lambda t: t.reshape(1, -1)
    w_h, b_h = _branch_weights(w_in[:, :BRANCH_IN], b_in[:BRANCH_IN])
    h, xb = _in_proj(x, w_h, b_h)

    yg_a = _lru(h, lru_conv_w, vec(lru_conv_b), _block_diag(lru_gate_a_w).astype(BF16), vec(lru_gate_a_b),
                _block_diag(lru_gate_x_w).astype(BF16), vec(lru_gate_x_b), vec(lru_lambda))

    outs, stats = [], []
    for g, (_, dil) in enumerate(ATT_GROUPS):
        o_g, st_g = _attention_group(h, att_bias, g, dil)
        outs.append(o_g)
        stats.append(st_g)
    yg_b = _attn_merge(outs, stats, h)

    mu = rwkv_mu
    zpad = jnp.zeros((DECAY_RANK, BR_WIDTH), F32)
    wup = jnp.concatenate([rwkv_w_up, zpad], axis=0).astype(BF16)
    aup = jnp.concatenate([zpad, rwkv_a_up], axis=0).astype(BF16)
    seg = _block_diag(jnp.ones((RWKV_HEADS, RWKV_HEAD, RWKV_HEAD), F32))
    lw, r, kc, vv, a, b, bonus = _rwkv_prep(
        h, vec(mu[:BR_WIDTH]), vec(mu[BR_WIDTH:2 * BR_WIDTH]), vec(mu[2 * BR_WIDTH:3 * BR_WIDTH]),
        vec(mu[3 * BR_WIDTH:]), vec(rwkv_w0), wup, vec(rwkv_a0), aup, vec(rwkv_k_k), vec(rwkv_k_a),
        vec(rwkv_r_k), seg)
    q, yc, g_mat, z = _rwkv_chunk(lw, r, kc, vv, a, b)
    yg_c = _rwkv_scan(q, yc, g_mat, z, bonus, h, vec(rwkv_gn_g), vec(rwkv_gn_b), seg)

    yg_d = _conformer(h, conf_dw_w, vec(conf_dw_b), vec(conf_ln_g), vec(conf_ln_b))

    mixed = _mix(xb, (yg_a, yg_b, yg_c, yg_d), w_in[:, BRANCH_IN:].astype(BF16), vec(b_in[BRANCH_IN:]),
                 w_br.astype(BF16))
    return _out_proj(mixed, x, w_out.astype(BF16), vec(ln_g), vec(ln_b))


def kernel(x, att_rel_bias, w_in, b_in, lru_conv_w, lru_conv_b, lru_gate_a_w, lru_gate_a_b, lru_gate_x_w, lru_gate_x_b, lru_lambda, rwkv_mu, rwkv_w0, rwkv_w_up, rwkv_a0, rwkv_a_up, rwkv_k_k, rwkv_k_a, rwkv_r_k, rwkv_gn_g, rwkv_gn_b, conf_dw_w, conf_dw_b, conf_ln_g, conf_ln_b, w_br, w_out, ln_g, ln_b):
    bsz, s, d = x.shape
    assert bsz == 1 and d == D_MODEL and s % (16 * ATT_SPAN) == 0
    per_layer = (w_in, b_in, lru_conv_w, lru_conv_b, lru_gate_a_w, lru_gate_a_b, lru_gate_x_w, lru_gate_x_b,
                 lru_lambda, rwkv_mu, rwkv_w0, rwkv_w_up, rwkv_a0, rwkv_a_up, rwkv_k_k, rwkv_k_a, rwkv_r_k,
                 rwkv_gn_g, rwkv_gn_b, conf_dw_w, conf_dw_b, conf_ln_g, conf_ln_b, w_br, w_out, ln_g, ln_b)
    y = x.reshape(s, d)
    att_bias = _attn_bias(att_rel_bias)
    for l in range(DEPTH):
        y = _layer(y, att_bias, *(t[l] for t in per_layer))
    return y.reshape(bsz, s, d)
```

```python
import functools
import math

import numpy as np
import jax
import jax.numpy as jnp
from jax import lax
from jax.experimental import pallas as pl
from jax.experimental.pallas import tpu as pltpu

D_MODEL = 2048
DEPTH = 2
N_BRANCH = 4
BR_WIDTH = 512
LRU_BLOCKS = 8
LRU_BLOCK = BR_WIDTH // LRU_BLOCKS
LRU_CONV = 4
LRU_C = 8.0
ATT_GROUPS = ((128, 1), (512, 4), (2048, 16))
ATT_HEADS_PER_GROUP = 4
ATT_HEAD_DIM = BR_WIDTH // ATT_HEADS_PER_GROUP
ATT_HEADS = len(ATT_GROUPS) * ATT_HEADS_PER_GROUP
ATT_QKV = ATT_HEADS * ATT_HEAD_DIM
ATT_SPAN = 128
N_BUCKETS = 32
MAX_DISTANCE = 2048
NEG_INF = -1e30
RWKV_HEAD = 64
RWKV_HEADS = BR_WIDTH // RWKV_HEAD
DECAY_RANK = 64
ICLR_RANK = 64
RWKV_GN_EPS = 64e-5
CONF_KERNEL = 31
LN_EPS = 1e-5
ALPHA = (2.0 * DEPTH) ** 0.25

LANES = 128
SUBLANES = 8
MIB = 1024 * 1024

COL_A_X, COL_A_GATE = 0, 1
COL_Q, COL_K, COL_V, COL_B_GATE = 2, 5, 8, 11
COL_D_VAL, COL_D_GLU, COL_D_GATE = 12, 13, 14
COL_C_R, COL_C_K, COL_C_V, COL_C_LORA, COL_C_GATE = 15, 16, 17, 18, 19
H_BLOCKS = 20
H_WIDTH = H_BLOCKS * BR_WIDTH
BRANCH_IN = 2 * BR_WIDTH + 3 * ATT_QKV + BR_WIDTH + (4 * BR_WIDTH + DECAY_RANK + ICLR_RANK) + 3 * BR_WIDTH

RWKV_CHUNK = 64
PAIR = 2 * RWKV_HEAD
N_PAIRS = BR_WIDTH // PAIR

F32 = jnp.float32
BF16 = jnp.bfloat16


def _params(semantics, vmem_mib):
    return pltpu.CompilerParams(dimension_semantics=semantics, vmem_limit_bytes=vmem_mib * MIB)


def _bdot(a, b):
    return jnp.dot(a.astype(BF16), b.astype(BF16), preferred_element_type=F32)


def _bdot_nt(a, b):
    return lax.dot_general(a.astype(BF16), b.astype(BF16), (((1,), (1,)), ((), ())), preferred_element_type=F32)


def _bdot_tn(a, b):
    return lax.dot_general(a.astype(BF16), b.astype(BF16), (((0,), (0,)), ((), ())), preferred_element_type=F32)


NN_DIMS = (((1,), (0,)), ((), ()))
NT_DIMS = (((1,), (1,)), ((), ()))


def _bf16_parts(x, parts):
    out = []
    for _ in range(parts):
        hi = x.astype(BF16)
        out.append(hi)
        x = x - hi.astype(F32)
    return out


def _split_dot(a, b, a_parts, b_parts, dims=NN_DIMS):
    acc = None
    b_terms = _bf16_parts(b, b_parts)
    for i, ai in enumerate(_bf16_parts(a, a_parts)):
        for j, bj in enumerate(b_terms):
            if i + j < max(a_parts, b_parts):
                term = lax.dot_general(ai, bj, dims, preferred_element_type=F32)
                acc = term if acc is None else acc + term
    return acc


def _softplus(z):
    return jnp.maximum(z, 0.0) + jnp.log1p(jnp.exp(-jnp.abs(z)))


def _expm1_nonpos(z):
    u = jnp.exp(z)
    safe = jnp.where(u == 1.0, 0.5, u)
    return jnp.where(u == 1.0, z, jnp.where(u == 0.0, -1.0, (safe - 1.0) * z / jnp.log(safe)))


def _silu(z):
    return z * jax.nn.sigmoid(z)


def _in_proj_kernel(x_ref, w_ref, b_ref, h_ref, xb_ref):
    @pl.when(pl.program_id(1) == 0)
    def _():
        xb_ref[...] = x_ref[...].astype(BF16)

    h_ref[...] = jnp.dot(xb_ref[...], w_ref[...], preferred_element_type=F32) + b_ref[...]


def _in_proj(x, w, b, tm=1024, tn=1024):
    s, k = x.shape
    n = w.shape[1]
    return pl.pallas_call(
        _in_proj_kernel,
        grid=(s // tm, n // tn),
        in_specs=[
            pl.BlockSpec((tm, k), lambda i, j: (i, 0)),
            pl.BlockSpec((k, tn), lambda i, j: (0, j)),
            pl.BlockSpec((1, tn), lambda i, j: (0, j)),
        ],
        out_specs=[
            pl.BlockSpec((tm, tn), lambda i, j: (i, j)),
            pl.BlockSpec((tm, k), lambda i, j: (i, 0)),
        ],
        out_shape=[jax.ShapeDtypeStruct((s, n), F32), jax.ShapeDtypeStruct((s, k), BF16)],
        compiler_params=_params(("parallel", "arbitrary"), 48),
        name="in_proj",
    )(x, w, b)


def _lru_kernel(ax_ref, ag_ref, cw_ref, cb_ref, wa_ref, ba_ref, wx_ref, bx_ref, lam_ref, o_ref, ebuf, hc):
    t = ax_ref.shape[0]
    halo = SUBLANES

    @pl.when(pl.program_id(0) == 0)
    def _():
        ebuf[0:halo, :] = jnp.zeros((halo, BR_WIDTH), F32)
        hc[...] = jnp.zeros_like(hc)

    x = ax_ref[...]
    ebuf[halo:halo + t, :] = x
    u = cb_ref[...] + jnp.zeros((t, BR_WIDTH), F32)
    for j in range(LRU_CONV):
        u = u + cw_ref[j:j + 1, :] * ebuf[pl.ds(halo - (LRU_CONV - 1) + j, t), :]
    ebuf[0:halo, :] = x[t - halo:t, :]

    gate_r = jax.nn.sigmoid(_bdot(u, wa_ref[...]) + ba_ref[...])
    gate_i = jax.nn.sigmoid(_bdot(u, wx_ref[...]) + bx_ref[...])
    log_a = -LRU_C * gate_r * _softplus(-lam_ref[...])
    a = jnp.exp(log_a)
    b = jnp.sqrt(-_expm1_nonpos(2.0 * log_a)) * (gate_i * u)

    row = lax.broadcasted_iota(jnp.int32, (t, BR_WIDTH), 0)
    shift = 1
    while shift < t:
        valid = row >= shift
        b = jnp.where(valid, a * pltpu.roll(b, shift, 0), 0.0) + b
        a = jnp.where(valid, a * pltpu.roll(a, shift, 0), a)
        shift *= 2
    h = a * hc[0:1, :] + b
    hc[0:1, :] = h[t - 1:t, :]
    o_ref[...] = (h * _silu(ag_ref[...])).astype(BF16)


def _lru(h, cw, cb, wa, ba, wx, bx, lam, tile=256):
    s = h.shape[0]
    row = lambda c: pl.BlockSpec((tile, BR_WIDTH), lambda i: (i, c))
    full = lambda shape: pl.BlockSpec(shape, lambda i: (0,) * len(shape))
    return pl.pallas_call(
        _lru_kernel,
        grid=(s // tile,),
        in_specs=[row(COL_A_X), row(COL_A_GATE), full((LRU_CONV, BR_WIDTH)), full((1, BR_WIDTH)),
                  full((BR_WIDTH, BR_WIDTH)), full((1, BR_WIDTH)), full((BR_WIDTH, BR_WIDTH)), full((1, BR_WIDTH)),
                  full((1, BR_WIDTH))],
        out_specs=pl.BlockSpec((tile, BR_WIDTH), lambda i: (i, 0)),
        out_shape=jax.ShapeDtypeStruct((s, BR_WIDTH), BF16),
        scratch_shapes=[pltpu.VMEM((tile + SUBLANES, BR_WIDTH), F32), pltpu.VMEM((SUBLANES, BR_WIDTH), F32)],
        compiler_params=_params(("arbitrary",), 32),
        name="rglru",
    )(h, h, cw, cb, wa, ba, wx, bx, lam)


def _conf_kernel(val_ref, glu_ref, gate_ref, w_ref, b_ref, g_ref, beta_ref, o_ref, ebuf):
    t = val_ref.shape[0]
    halo = 32

    @pl.when(pl.program_id(0) == 0)
    def _():
        ebuf[0:halo, :] = jnp.zeros((halo, BR_WIDTH), F32)

    cu = val_ref[...] * jax.nn.sigmoid(glu_ref[...])
    ebuf[halo:halo + t, :] = cu
    acc = b_ref[...] + jnp.zeros((t, BR_WIDTH), F32)
    for j in range(CONF_KERNEL):
        acc = acc + w_ref[j:j + 1, :] * ebuf[pl.ds(halo - (CONF_KERNEL - 1) + j, t), :]
    ebuf[0:halo, :] = cu[t - halo:t, :]

    mu = jnp.mean(acc, axis=-1, keepdims=True)
    var = jnp.mean(jnp.square(acc - mu), axis=-1, keepdims=True)
    ln = (acc - mu) * lax.rsqrt(var + LN_EPS) * g_ref[...] + beta_ref[...]
    o_ref[...] = (_silu(ln) * _silu(gate_ref[...])).astype(BF16)


def _conformer(h, w, b, g, beta, tile=256):
    s = h.shape[0]
    row = lambda c: pl.BlockSpec((tile, BR_WIDTH), lambda i: (i, c))
    full = lambda shape: pl.BlockSpec(shape, lambda i: (0,) * len(shape))
    return pl.pallas_call(
        _conf_kernel,
        grid=(s // tile,),
        in_specs=[row(COL_D_VAL), row(COL_D_GLU), row(COL_D_GATE), full((CONF_KERNEL, BR_WIDTH)),
                  full((1, BR_WIDTH)), full((1, BR_WIDTH)), full((1, BR_WIDTH))],
        out_specs=pl.BlockSpec((tile, BR_WIDTH), lambda i: (i, 0)),
        out_shape=jax.ShapeDtypeStruct((s, BR_WIDTH), BF16),
        scratch_shapes=[pltpu.VMEM((tile + 32, BR_WIDTH), F32)],
        compiler_params=_params(("arbitrary",), 32),
        name="conformer",
    )(h, h, h, w, b, g, beta)


def _t5_bucket(dist):
    max_exact = N_BUCKETS // 2
    large = max_exact + (np.log(np.maximum(dist, 1) / max_exact) / math.log(MAX_DISTANCE / max_exact)
                         * (N_BUCKETS - max_exact)).astype(np.int32)
    large = np.minimum(large, N_BUCKETS - 1)
    return np.where(dist < max_exact, dist, large).astype(np.int32)


def _bucket_index():
    qi = np.arange(ATT_SPAN)[:, None]
    kj = np.arange(2 * ATT_SPAN)[None, :]
    dist = qi + ATT_SPAN - kj
    valid = (dist >= 0) & (dist <= ATT_SPAN)
    per_group = [np.where(valid, _t5_bucket(np.clip(dist, 0, ATT_SPAN) * dil), -1) for _, dil in ATT_GROUPS]
    return np.stack(per_group).astype(np.int32)


def _bias_kernel(table_ref, bucket_ref, o_ref):
    head = pl.program_id(0)
    bucket = bucket_ref[...]
    acc = jnp.full(bucket.shape, NEG_INF, F32)
    for bkt in range(N_BUCKETS):
        acc = jnp.where(bucket == bkt, table_ref[bkt, head], acc)
    o_ref[...] = acc


def _attn_bias(table):
    blk = (None, ATT_SPAN, 2 * ATT_SPAN)
    return pl.pallas_call(
        _bias_kernel,
        grid=(ATT_HEADS,),
        in_specs=[pl.BlockSpec(memory_space=pltpu.SMEM),
                  pl.BlockSpec(blk, lambda hd: (hd // ATT_HEADS_PER_GROUP, 0, 0))],
        out_specs=pl.BlockSpec(blk, lambda hd: (hd, 0, 0)),
        out_shape=jax.ShapeDtypeStruct((ATT_HEADS, ATT_SPAN, 2 * ATT_SPAN), F32),
        compiler_params=_params(("parallel",), 32),
        name="attn_bias",
    )(table, jnp.asarray(_bucket_index()))


ATT_DIRECT_STRIDE = 4


def _residue_reader(ref, slab, dilation):
    if dilation == 1:
        return lambda b, r: ref[b * ATT_SPAN:(b + 1) * ATT_SPAN, :]
    if dilation <= ATT_DIRECT_STRIDE:
        return lambda b, r: ref[pl.ds(b * ATT_SPAN * dilation + r, ATT_SPAN, stride=dilation), :]
    inner, outer = ATT_DIRECT_STRIDE, dilation // ATT_DIRECT_STRIDE
    per = ref.shape[0] // inner
    for r0 in range(inner):
        slab[r0] = ref[pl.ds(r0, per, stride=inner), :]
    return lambda b, r: slab[r % inner, pl.ds(b * ATT_SPAN * outer + r // inner, ATT_SPAN, stride=outer), :]


def _residue_writer(ref, slab, dilation):
    if dilation == 1:
        def write(b, r, val):
            ref[b * ATT_SPAN:(b + 1) * ATT_SPAN, :] = val
        return write, lambda: None
    if dilation <= ATT_DIRECT_STRIDE:
        def write(b, r, val):
            ref[pl.ds(b * ATT_SPAN * dilation + r, ATT_SPAN, stride=dilation), :] = val
        return write, lambda: None
    inner, outer = ATT_DIRECT_STRIDE, dilation // ATT_DIRECT_STRIDE
    per = ref.shape[0] // inner

    def write(b, r, val):
        slab[r % inner, pl.ds(b * ATT_SPAN * outer + r // inner, ATT_SPAN, stride=outer), :] = val

    def flush():
        for r0 in range(inner):
            ref[pl.ds(r0, per, stride=inner), :] = slab[r0]

    return write, flush


def _attn_kernel(q_ref, kc_ref, kp_ref, vc_ref, vp_ref, bias_ref, o_ref, st_ref, st_acc, *slabs, dilation, blocks):
    first = pl.program_id(0) == 0
    hh = pl.program_id(1)
    scale = ATT_HEAD_DIM ** -0.5
    lane = lax.broadcasted_iota(jnp.int32, (ATT_SPAN, LANES), 1)
    slabs = slabs if slabs else (None,) * 7

    @pl.when(hh == 0)
    def _():
        st_acc[...] = jnp.zeros_like(st_acc)

    read_q, read_kc, read_kp, read_vc, read_vp = (
        _residue_reader(ref, slab, dilation) for ref, slab in zip((q_ref, kc_ref, kp_ref, vc_ref, vp_ref), slabs[:5]))
    write_o, flush_o = _residue_writer(o_ref, slabs[5], dilation)
    keys, values = {}, {}
    for r in range(dilation):
        keys[-1, r] = read_kp(0, r).astype(BF16)
        values[-1, r] = read_vp(0, r).astype(BF16)
        for b in range(blocks):
            keys[b, r] = read_kc(b, r).astype(BF16)
            values[b, r] = read_vc(b, r).astype(BF16)

    bias_p = bias_ref[:, 0:ATT_SPAN]
    bias_c = bias_ref[:, ATT_SPAN:2 * ATT_SPAN]
    units = [(b, r) for b in range(blocks) for r in range(dilation)]
    qs = [read_q(b, r).astype(BF16) for b, r in units]
    lps = [_bdot_nt(q, keys[b - 1, r]) * scale + bias_p for q, (b, r) in zip(qs, units)]
    lps = [jnp.where(first, NEG_INF, lp) if b == 0 else lp for lp, (b, r) in zip(lps, units)]
    lcs = [_bdot_nt(q, keys[b, r]) * scale + bias_c for q, (b, r) in zip(qs, units)]
    ms = [jnp.max(jnp.maximum(lp, lc), axis=-1, keepdims=True) for lp, lc in zip(lps, lcs)]
    pps = [jnp.exp(lp - m) for lp, m in zip(lps, ms)]
    pcs = [jnp.exp(lc - m) for lc, m in zip(lcs, ms)]
    dens = [jnp.sum(pp + pc, axis=-1, keepdims=True) for pp, pc in zip(pps, pcs)]
    for (b, r), pp, pc, m, den in zip(units, pps, pcs, ms, dens):
        write_o(b, r, (_bdot(pp, values[b - 1, r]) + _bdot(pc, values[b, r])) / den)
        tile = pl.ds((b * dilation + r) * ATT_SPAN, ATT_SPAN)
        st = jnp.where(lane == hh, m, st_acc[tile, :])
        st_acc[tile, :] = jnp.where(lane == ATT_HEADS_PER_GROUP + hh, den, st)
    flush_o()

    @pl.when(hh == ATT_HEADS_PER_GROUP - 1)
    def _():
        write_st, flush_st = _residue_writer(st_ref, slabs[6], dilation)
        for b in range(blocks):
            for r in range(dilation):
                write_st(b, r, st_acc[pl.ds((b * dilation + r) * ATT_SPAN, ATT_SPAN), :])
        flush_st()


def _attention_group(h, bias, group, dilation, rows_per_step=1024):
    s = h.shape[0]
    blk = ATT_SPAN * dilation
    blocks = max(1, rows_per_step // blk)
    heads = ATT_HEADS_PER_GROUP
    rows = blk * blocks

    def spec(col, prev):
        base = (col + group) * heads
        if prev:
            return pl.BlockSpec((blk, ATT_HEAD_DIM), lambda n, hd: (jnp.maximum(n * blocks - 1, 0), base + hd))
        return pl.BlockSpec((rows, ATT_HEAD_DIM), lambda n, hd: (n, base + hd))

    scratch = [pltpu.VMEM((rows, LANES), F32)]
    if dilation > ATT_DIRECT_STRIDE:
        slab = lambda nrows: pltpu.VMEM((ATT_DIRECT_STRIDE, nrows // ATT_DIRECT_STRIDE, LANES), F32)
        scratch += [slab(rows), slab(rows), slab(blk), slab(rows), slab(blk), slab(rows), slab(rows)]
    return pl.pallas_call(
        functools.partial(_attn_kernel, dilation=dilation, blocks=blocks),
        grid=(s // rows, heads),
        in_specs=[spec(COL_Q, False), spec(COL_K, False), spec(COL_K, True), spec(COL_V, False), spec(COL_V, True),
                  pl.BlockSpec((None, ATT_SPAN, 2 * ATT_SPAN), lambda n, hd: (group * heads + hd, 0, 0))],
        out_specs=[pl.BlockSpec((rows, ATT_HEAD_DIM), lambda n, hd: (n, hd)),
                   pl.BlockSpec((rows, LANES), lambda n, hd: (n, 0))],
        out_shape=[jax.ShapeDtypeStruct((s, BR_WIDTH), F32), jax.ShapeDtypeStruct((s, LANES), F32)],
        scratch_shapes=scratch,
        compiler_params=_params(("parallel", "arbitrary"), 32),
        name=f"dil_attn_d{dilation}",
    )(h, h, h, h, h, bias)


def _attn_merge_kernel(o0, o1, o2, s0, s1, s2, gate_ref, y_ref):
    outs = (o0, o1, o2)
    stats = (s0[...], s1[...], s2[...])
    for hh in range(ATT_HEADS_PER_GROUP):
        sl = slice(hh * ATT_HEAD_DIM, (hh + 1) * ATT_HEAD_DIM)
        ms = [st[:, hh:hh + 1] for st in stats]
        dens = [st[:, ATT_HEADS_PER_GROUP + hh:ATT_HEADS_PER_GROUP + hh + 1] for st in stats]
        m_all = jnp.maximum(jnp.maximum(ms[0], ms[1]), ms[2])
        wts = [jnp.exp(m - m_all) * d for m, d in zip(ms, dens)]
        num = wts[0] * outs[0][:, sl] + wts[1] * outs[1][:, sl] + wts[2] * outs[2][:, sl]
        y = num / (wts[0] + wts[1] + wts[2])
        y_ref[:, sl] = (y * _silu(gate_ref[:, sl])).astype(BF16)


def _attn_merge(outs, stats, h, tile=512):
    s = h.shape[0]
    o_spec = pl.BlockSpec((tile, BR_WIDTH), lambda i: (i, 0))
    s_spec = pl.BlockSpec((tile, LANES), lambda i: (i, 0))
    return pl.pallas_call(
        _attn_merge_kernel,
        grid=(s // tile,),
        in_specs=[o_spec] * 3 + [s_spec] * 3 + [pl.BlockSpec((tile, BR_WIDTH), lambda i: (i, COL_B_GATE))],
        out_specs=o_spec,
        out_shape=jax.ShapeDtypeStruct((s, BR_WIDTH), BF16),
        compiler_params=_params(("parallel",), 32),
        name="attn_merge",
    )(*outs, *stats, h)


def _rwkv_prep_kernel(r_ref, k_ref, v_ref, lora_ref, mu_r, mu_k, mu_v, mu_l, w0_ref, wup_ref, a0_ref, aup_ref,
                      kk_ref, ka_ref, rk_ref, seg_ref,
                      lw_o, r_o, k_o, v_o, a_o, b_o, bonus_o, carry, carry_l):
    t = r_ref.shape[0]

    @pl.when(pl.program_id(0) == 0)
    def _():
        carry[...] = jnp.zeros_like(carry)
        carry_l[...] = jnp.zeros_like(carry_l)

    def shift_mix(x, mu, prev_row):
        row = lax.broadcasted_iota(jnp.int32, x.shape, 0)
        x_prev = jnp.where(row == 0, prev_row, pltpu.roll(x, 1, 0))
        return x + mu * (x_prev - x)

    r_in, k_in, v_in, l_in = r_ref[...], k_ref[...], v_ref[...], lora_ref[...]
    r = shift_mix(r_in, mu_r[...], carry[0:1, :])
    kx = shift_mix(k_in, mu_k[...], carry[1:2, :])
    vv = shift_mix(v_in, mu_v[...], carry[2:3, :])
    lo = shift_mix(l_in, mu_l[...], carry_l[0:1, :])
    carry[0:1, :] = r_in[t - 1:t, :]
    carry[1:2, :] = k_in[t - 1:t, :]
    carry[2:3, :] = v_in[t - 1:t, :]
    carry_l[0:1, :] = l_in[t - 1:t, :]

    w_log = -_softplus(-(w0_ref[...] + _bdot(jnp.tanh(lo), wup_ref[...]))) - 0.5
    lw_o[...] = -jnp.exp(w_log)
    a_icl = jax.nn.sigmoid(a0_ref[...] + _bdot(lo, aup_ref[...]))

    seg = seg_ref[...]
    kk = kx * kk_ref[...]
    kk = kk / jnp.maximum(jnp.sqrt(_split_dot(kk * kk, seg, 2, 1)), 1e-12)
    kc = kx * (1.0 + (a_icl - 1.0) * ka_ref[...])
    r_o[...] = r
    k_o[...] = kc
    v_o[...] = vv
    a_o[...] = -kk
    b_o[...] = kk * a_icl
    bonus_o[...] = _split_dot(r * kc * rk_ref[...], seg, 2, 1) * vv


def _rwkv_prep(h, mu_r, mu_k, mu_v, mu_l, w0, wup, a0, aup, k_k, k_a, r_k, seg, tile=256):
    s = h.shape[0]
    row = lambda c: pl.BlockSpec((tile, BR_WIDTH), lambda i: (i, c))
    full = lambda shape: pl.BlockSpec(shape, lambda i: (0,) * len(shape))
    vec = full((1, BR_WIDTH))
    out = pl.BlockSpec((tile, BR_WIDTH), lambda i: (i, 0))
    return pl.pallas_call(
        _rwkv_prep_kernel,
        grid=(s // tile,),
        in_specs=[row(COL_C_R), row(COL_C_K), row(COL_C_V),
                  pl.BlockSpec((tile, LANES), lambda i: (i, COL_C_LORA * (BR_WIDTH // LANES))),
                  vec, vec, vec, full((1, LANES)), vec, full((LANES, BR_WIDTH)), vec, full((LANES, BR_WIDTH)),
                  vec, vec, vec, full((BR_WIDTH, BR_WIDTH))],
        out_specs=[out] * 7,
        out_shape=[jax.ShapeDtypeStruct((s, BR_WIDTH), F32)] * 7,
        scratch_shapes=[pltpu.VMEM((SUBLANES, BR_WIDTH), F32), pltpu.VMEM((SUBLANES, LANES), F32)],
        compiler_params=_params(("arbitrary",), 32),
        name="rwkv_prep",
    )(h, h, h, h, mu_r, mu_k, mu_v, mu_l, w0, wup, a0, aup, k_k, k_a, r_k, seg)


def _stack_heads(x):
    lane = lax.broadcasted_iota(jnp.int32, x.shape, 1)
    return jnp.concatenate([jnp.where(lane < RWKV_HEAD, x, 0.0), jnp.where(lane >= RWKV_HEAD, x, 0.0)], axis=0)


def _unit_lower_inverse(a_strict):
    n = a_strict[0].shape[0]
    ri = lax.broadcasted_iota(jnp.int32, (n, n), 0)
    ci = lax.broadcasted_iota(jnp.int32, (n, n), 1)

    def same_block(bits):
        return (ri >> bits) == (ci >> bits)

    pw = [jnp.where(same_block(4), a, 0.0) for a in a_strict]
    x = [jnp.where(ri == ci, 1.0, 0.0) + p for p in pw]
    for _ in range(3):
        pw = [_bdot(p, p) for p in pw]
        x = [xi + _bdot(xi, p) for xi, p in zip(x, pw)]
    for bits in (5, 6):
        join = same_block(bits) & jnp.logical_not(same_block(bits - 1))
        xe = [_bdot(xi, jnp.where(join, a, 0.0)) for xi, a in zip(x, a_strict)]
        x = [xi + _bdot(t, xi) for xi, t in zip(x, xe)]
    return x


def _rwkv_chunk_kernel(lw_ref, r_ref, k_ref, v_ref, a_ref, b_ref, q_o, yc_o, g_o, z_o):
    c = RWKV_CHUNK
    n = 2 * c
    ti = lax.broadcasted_iota(jnp.int32, (c, c), 0)
    si = lax.broadcasted_iota(jnp.int32, (c, c), 1)
    lower_ones = jnp.where(si <= ti, 1.0, 0.0)
    ri = lax.broadcasted_iota(jnp.int32, (n, n), 0)
    ci = lax.broadcasted_iota(jnp.int32, (n, n), 1)
    same_head = (ri >> 6) == (ci >> 6)
    strict = same_head & (ri > ci)
    incl = same_head & (ri >= ci)
    eye = ri == ci

    units = [(ch, p) for ch in range(lw_ref.shape[0] // c) for p in range(N_PAIRS)]
    each = lambda f, *cols: [f(*args) for args in zip(*cols)]

    def load(ref):
        return [ref[ch * c:(ch + 1) * c, p * PAIR:(p + 1) * PAIR] for ch, p in units]

    lw, r, k, v, a, b = (load(ref) for ref in (lw_ref, r_ref, k_ref, v_ref, a_ref, b_ref))
    cs = each(lambda x: _split_dot(lower_ones, x, 1, 3), lw)
    c_end = each(lambda x: x[c - 1:c, :], cs)
    r_t = each(lambda x, y: _stack_heads(x * jnp.exp(y)), r, cs)
    a_t = each(lambda x, y, z: _stack_heads(x * jnp.exp(y - z)), a, cs, lw)
    b_t = each(lambda x, y: _stack_heads(x * jnp.exp(-y)), b, cs)
    k_t = each(lambda x, y: _stack_heads(x * jnp.exp(-y)), k, cs)
    b_h = each(lambda x, y, e: _stack_heads(x * jnp.exp(e - y)), b, cs, c_end)
    k_h = each(lambda x, y, e: _stack_heads(x * jnp.exp(e - y)), k, cs, c_end)
    v_s = each(_stack_heads, v)

    aa = each(lambda at, rt, bt, kt: _bdot_nt(jnp.concatenate([at, rt], axis=0), jnp.concatenate([bt, kt], axis=0)),
              a_t, r_t, b_t, k_t)
    a_ab = each(lambda x: jnp.where(strict, x[0:n, 0:n], 0.0), aa)
    a_ak = each(lambda x: jnp.where(strict, x[0:n, n:2 * n], 0.0), aa)
    a_rb = each(lambda x: jnp.where(incl, x[n:2 * n, 0:n], 0.0), aa)
    a_rk = each(lambda x: jnp.where(incl, x[n:2 * n, n:2 * n], 0.0), aa)

    minv = _unit_lower_inverse(a_ab)
    w = each(_bdot, minv, a_t)
    uv = each(_bdot, minv, each(_bdot, a_ak, v_s))
    q = each(lambda rt, x, y: rt + _bdot(x, y), r_t, a_rb, w)
    yc = each(lambda x, y, z, t: _bdot(x, y) + _bdot(z, t), a_rb, uv, a_rk, v_s)
    g = each(lambda e, x, y: jnp.where(eye, jnp.exp(e), 0.0) + _bdot_tn(x, y), c_end, w, b_h)
    z = each(lambda u_, v_, bh, kh: _bdot_tn(jnp.concatenate([u_, v_], axis=0), jnp.concatenate([bh, kh], axis=0)),
             uv, v_s, b_h, k_h)
    for (ch, p), q_u, yc_u, g_u, z_u in zip(units, q, yc, g, z):
        rows, sl = slice(ch * n, (ch + 1) * n), slice(p * PAIR, (p + 1) * PAIR)
        q_o[rows, sl] = q_u
        yc_o[rows, sl] = yc_u
        g_o[rows, sl] = g_u
        z_o[rows, sl] = z_u


def _rwkv_chunk(lw, r, k, v, a, b, chunks=4):
    s = lw.shape[0]
    nc = s // (chunks * RWKV_CHUNK)
    inp = pl.BlockSpec((chunks * RWKV_CHUNK, BR_WIDTH), lambda i: (i, 0))
    out = pl.BlockSpec((chunks * 2 * RWKV_CHUNK, BR_WIDTH), lambda i: (i, 0))
    return pl.pallas_call(
        _rwkv_chunk_kernel,
        grid=(nc,),
        in_specs=[inp] * 6,
        out_specs=[out] * 4,
        out_shape=[jax.ShapeDtypeStruct((2 * s, BR_WIDTH), F32)] * 4,
        compiler_params=_params(("parallel",), 32),
        name="rwkv_chunk",
    )(lw, r, k, v, a, b)


def _rwkv_scan_kernel(q_ref, yc_ref, g_ref, z_ref, bonus_ref, gate_ref, gn_g, gn_b, seg_ref, o_ref, state, ybuf):
    c = RWKV_CHUNK
    n = 2 * c
    chunks = q_ref.shape[0] // n

    @pl.when(pl.program_id(0) == 0)
    def _():
        state[...] = jnp.zeros_like(state)

    pairs = [slice(p * PAIR, (p + 1) * PAIR) for p in range(N_PAIRS)]
    sts = [state[:, sl] for sl in pairs]
    starts = []
    for ch in range(chunks):
        rows = slice(ch * n, (ch + 1) * n)
        starts.append(sts)
        sts = [_split_dot(st, g_ref[rows, sl], 2, 2) + z_ref[rows, sl] for st, sl in zip(sts, pairs)]
    for st, sl in zip(sts, pairs):
        state[:, sl] = st
    for ch in range(chunks):
        rows = slice(ch * n, (ch + 1) * n)
        for st, sl in zip(starts[ch], pairs):
            y_st = _split_dot(q_ref[rows, sl], st, 2, 2, NT_DIMS) + yc_ref[rows, sl]
            ybuf[ch * c:(ch + 1) * c, sl] = y_st[0:c, :] + y_st[c:n, :]

    wy = ybuf[...]
    seg = seg_ref[...]
    inv_n = 1.0 / RWKV_HEAD
    mu = _split_dot(wy, seg, 2, 1) * inv_n
    d = wy - mu
    var = _split_dot(d * d, seg, 2, 1) * inv_n
    wy = d * lax.rsqrt(var + RWKV_GN_EPS) * gn_g[...] + gn_b[...]
    o_ref[...] = ((wy + bonus_ref[...]) * _silu(gate_ref[...])).astype(BF16)


def _rwkv_scan(q, yc, g, z, bonus, h, gn_g, gn_b, seg, chunks=4):
    s = bonus.shape[0]
    tile = chunks * RWKV_CHUNK
    big = pl.BlockSpec((2 * tile, BR_WIDTH), lambda i: (i, 0))
    row = pl.BlockSpec((tile, BR_WIDTH), lambda i: (i, 0))
    full = lambda shape: pl.BlockSpec(shape, lambda i: (0,) * len(shape))
    return pl.pallas_call(
        _rwkv_scan_kernel,
        grid=(s // tile,),
        in_specs=[big] * 4 + [row, pl.BlockSpec((tile, BR_WIDTH), lambda i: (i, COL_C_GATE)),
                              full((1, BR_WIDTH)), full((1, BR_WIDTH)), full((BR_WIDTH, BR_WIDTH))],
        out_specs=row,
        out_shape=jax.ShapeDtypeStruct((s, BR_WIDTH), BF16),
        scratch_shapes=[pltpu.VMEM((PAIR, BR_WIDTH), F32), pltpu.VMEM((tile, BR_WIDTH), F32)],
        compiler_params=_params(("arbitrary",), 32),
        name="rwkv_scan",
    )(q, yc, g, z, bonus, h, gn_g, gn_b, seg)


def _mix_kernel(xb_ref, *refs):
    ygs, wms, bms, wbrs = (refs[k * N_BRANCH:(k + 1) * N_BRANCH] for k in range(4))
    o_ref = refs[4 * N_BRANCH]
    xb = xb_ref[...]
    acc = None
    for n in range(N_BRANCH):
        gate = jax.nn.sigmoid(jnp.dot(xb, wms[n][...], preferred_element_type=F32) + bms[n][...])
        val = gate * jnp.dot(ygs[n][...], wbrs[n][...], preferred_element_type=F32)
        acc = val if acc is None else acc + val
    o_ref[...] = acc.astype(BF16)


def _mix(xb, ygs, wm, bm, wbr, tm=1024, tn=512):
    s = xb.shape[0]
    nj = D_MODEL // tn
    per_branch = lambda make: [make(n) for n in range(N_BRANCH)]
    return pl.pallas_call(
        _mix_kernel,
        grid=(s // tm, nj),
        in_specs=[pl.BlockSpec((tm, D_MODEL), lambda i, j: (i, 0))]
        + per_branch(lambda n: pl.BlockSpec((tm, BR_WIDTH), lambda i, j: (i, 0)))
        + per_branch(lambda n: pl.BlockSpec((D_MODEL, tn), lambda i, j: (0, n * nj + j)))
        + per_branch(lambda n: pl.BlockSpec((1, tn), lambda i, j: (0, n * nj + j)))
        + per_branch(lambda n: pl.BlockSpec((None, BR_WIDTH, tn), lambda i, j: (n, 0, j))),
        out_specs=pl.BlockSpec((tm, tn), lambda i, j: (i, j)),
        out_shape=jax.ShapeDtypeStruct((s, D_MODEL), BF16),
        compiler_params=_params(("parallel", "arbitrary"), 48),
        name="branch_mix",
    )(xb, *ygs, *([wm] * N_BRANCH), *([bm] * N_BRANCH), *([wbr] * N_BRANCH))


def _out_kernel(mixed_ref, x_ref, w_ref, g_ref, b_ref, o_ref):
    y = ALPHA * x_ref[...] + jnp.dot(mixed_ref[...], w_ref[...], preferred_element_type=F32)
    mu = jnp.mean(y, axis=-1, keepdims=True)
    var = jnp.mean(jnp.square(y - mu), axis=-1, keepdims=True)
    o_ref[...] = (y - mu) * lax.rsqrt(var + LN_EPS) * g_ref[...] + b_ref[...]


def _out_proj(mixed, x, w, g, b, tm=512):
    s = x.shape[0]
    row = pl.BlockSpec((tm, D_MODEL), lambda i: (i, 0))
    vec = pl.BlockSpec((1, D_MODEL), lambda i: (0, 0))
    return pl.pallas_call(
        _out_kernel,
        grid=(s // tm,),
        in_specs=[row, row, pl.BlockSpec((D_MODEL, D_MODEL), lambda i: (0, 0)), vec, vec],
        out_specs=row,
        out_shape=jax.ShapeDtypeStruct((s, D_MODEL), F32),
        compiler_params=_params(("parallel",), 48),
        name="out_proj_ln",
    )(mixed, x, w, g, b)


def _block_diag(w):
    blocks, n, _ = w.shape
    eye = jnp.eye(blocks, dtype=w.dtype)
    return (eye[:, None, :, None] * w[:, :, None, :]).reshape(blocks * n, blocks * n)


def _branch_weights(w_in, b_in):
    a0 = 0
    c0 = 2 * BR_WIDTH + 3 * ATT_QKV + BR_WIDTH
    lora0 = c0 + 3 * BR_WIDTH
    cg0 = lora0 + DECAY_RANK + ICLR_RANK
    d0 = cg0 + BR_WIDTH
    end = d0 + 3 * BR_WIDTH
    pad = BR_WIDTH - (DECAY_RANK + ICLR_RANK)

    def build(m):
        z = jnp.zeros(m.shape[:-1] + (pad,), m.dtype)
        return jnp.concatenate([m[..., a0:c0], m[..., d0:end], m[..., c0:lora0], m[..., lora0:cg0], z,
                                m[..., cg0:d0]], axis=-1)

    return build(w_in).astype(BF16), build(b_in)[None, :]


def _layer(x, att_bias, w_in, b_in, lru_conv_w, lru_conv_b, lru_gate_a_w, lru_gate_a_b, lru_gate_x_w, lru_gate_x_b,
           lru_lambda, rwkv_mu, rwkv_w0, rwkv_w_up, rwkv_a0, rwkv_a_up, rwkv_k_k, rwkv_k_a, rwkv_r_k, rwkv_gn_g,
           rwkv_gn_b, conf_dw_w, conf_dw_b, conf_ln_g, conf_ln_b, w_br, w_out, ln_g, ln_b):
    vec = lambda t: t.reshape(1, -1)
    w_h, b_h = _branch_weights(w_in[:, :BRANCH_IN], b_in[:BRANCH_IN])
    h, xb = _in_proj(x, w_h, b_h)

    yg_a = _lru(h, lru_conv_w, vec(lru_conv_b), _block_diag(lru_gate_a_w).astype(BF16), vec(lru_gate_a_b),
                _block_diag(lru_gate_x_w).astype(BF16), vec(lru_gate_x_b), vec(lru_lambda))

    outs, stats = [], []
    for g, (_, dil) in enumerate(ATT_GROUPS):
        o_g, st_g = _attention_group(h, att_bias, g, dil)
        outs.append(o_g)
        stats.append(st_g)
    yg_b = _attn_merge(outs, stats, h)

    mu = rwkv_mu
    zpad = jnp.zeros((DECAY_RANK, BR_WIDTH), F32)
    wup = jnp.concatenate([rwkv_w_up, zpad], axis=0).astype(BF16)
    aup = jnp.concatenate([zpad, rwkv_a_up], axis=0).astype(BF16)
    seg = _block_diag(jnp.ones((RWKV_HEADS, RWKV_HEAD, RWKV_HEAD), BF16))
    lw, r, kc, vv, a, b, bonus = _rwkv_prep(
        h, vec(mu[:BR_WIDTH]), vec(mu[BR_WIDTH:2 * BR_WIDTH]), vec(mu[2 * BR_WIDTH:3 * BR_WIDTH]),
        vec(mu[3 * BR_WIDTH:]), vec(rwkv_w0), wup, vec(rwkv_a0), aup, vec(rwkv_k_k), vec(rwkv_k_a),
        vec(rwkv_r_k), seg)
    q, yc, g_mat, z = _rwkv_chunk(lw, r, kc, vv, a, b)
    yg_c = _rwkv_scan(q, yc, g_mat, z, bonus, h, vec(rwkv_gn_g), vec(rwkv_gn_b), seg)

    yg_d = _conformer(h, conf_dw_w, vec(conf_dw_b), vec(conf_ln_g), vec(conf_ln_b))

    mixed = _mix(xb, (yg_a, yg_b, yg_c, yg_d), w_in[:, BRANCH_IN:].astype(BF16), vec(b_in[BRANCH_IN:]),
                 w_br.astype(BF16))
    return _out_proj(mixed, x, w_out.astype(BF16), vec(ln_g), vec(ln_b))


def kernel(x, att_rel_bias, w_in, b_in, lru_conv_w, lru_conv_b, lru_gate_a_w, lru_gate_a_b, lru_gate_x_w, lru_gate_x_b, lru_lambda, rwkv_mu, rwkv_w0, rwkv_w_up, rwkv_a0, rwkv_a_up, rwkv_k_k, rwkv_k_a, rwkv_r_k, rwkv_gn_g, rwkv_gn_b, conf_dw_w, conf_dw_b, conf_ln_g, conf_ln_b, w_br, w_out, ln_g, ln_b):
    bsz, s, d = x.shape
    assert bsz == 1 and d == D_MODEL and s % (16 * ATT_SPAN) == 0
    per_layer = (w_in, b_in, lru_conv_w, lru_conv_b, lru_gate_a_w, lru_gate_a_b, lru_gate_x_w, lru_gate_x_b,
                 lru_lambda, rwkv_mu, rwkv_w0, rwkv_w_up, rwkv_a0, rwkv_a_up, rwkv_k_k, rwkv_k_a, rwkv_r_k,
                 rwkv_gn_g, rwkv_gn_b, conf_dw_w, conf_dw_b, conf_ln_g, conf_ln_b, w_br, w_out, ln_g, ln_b)
    y = x.reshape(s, d)
    att_bias = _attn_bias(att_rel_bias)
    for l in range(DEPTH):
        y = _layer(y, att_bias, *(t[l] for t in per_layer))
    return y.reshape(bsz, s, d)
```

```python
import functools
import math

import numpy as np
import jax
import jax.numpy as jnp
from jax import lax
from jax.experimental import pallas as pl
from jax.experimental.pallas import tpu as pltpu

D_MODEL = 2048
DEPTH = 2
N_BRANCH = 4
BR_WIDTH = 512
LRU_BLOCKS = 8
LRU_BLOCK = BR_WIDTH // LRU_BLOCKS
LRU_CONV = 4
LRU_C = 8.0
ATT_GROUPS = ((128, 1), (512, 4), (2048, 16))
ATT_HEADS_PER_GROUP = 4
ATT_HEAD_DIM = BR_WIDTH // ATT_HEADS_PER_GROUP
ATT_HEADS = len(ATT_GROUPS) * ATT_HEADS_PER_GROUP
ATT_QKV = ATT_HEADS * ATT_HEAD_DIM
ATT_SPAN = 128
N_BUCKETS = 32
MAX_DISTANCE = 2048
NEG_INF = -1e30
RWKV_HEAD = 64
RWKV_HEADS = BR_WIDTH // RWKV_HEAD
DECAY_RANK = 64
ICLR_RANK = 64
RWKV_GN_EPS = 64e-5
CONF_KERNEL = 31
LN_EPS = 1e-5
ALPHA = (2.0 * DEPTH) ** 0.25

LANES = 128
SUBLANES = 8
MIB = 1024 * 1024

BRANCH_IN = 2 * BR_WIDTH + 3 * ATT_QKV + BR_WIDTH + (4 * BR_WIDTH + DECAY_RANK + ICLR_RANK) + 3 * BR_WIDTH
C_GATE_START = BRANCH_IN - 4 * BR_WIDTH
H_SPLIT = 16
H_BLOCKS = 20
H_WIDTH = H_BLOCKS * BR_WIDTH
COL_A_X, COL_A_GATE = 0, 1
COL_Q, COL_K, COL_V, COL_B_GATE = 2, 5, 8, 11
COL_C_R, COL_C_K, COL_C_V, COL_C_LORA = 12, 13, 14, 15
COL_C_GATE, COL_D_VAL, COL_D_GLU, COL_D_GATE = 16, 17, 18, 19

CONF_HALO = 32
RWKV_CHUNK = 64
PAIR = 2 * RWKV_HEAD
N_PAIRS = BR_WIDTH // PAIR

F32 = jnp.float32
BF16 = jnp.bfloat16


def _params(semantics, vmem_mib):
    return pltpu.CompilerParams(dimension_semantics=semantics, vmem_limit_bytes=vmem_mib * MIB)


def _bdot(a, b):
    return jnp.dot(a.astype(BF16), b.astype(BF16), preferred_element_type=F32)


def _bdot_nt(a, b):
    return lax.dot_general(a.astype(BF16), b.astype(BF16), (((1,), (1,)), ((), ())), preferred_element_type=F32)


def _bdot_tn(a, b):
    return lax.dot_general(a.astype(BF16), b.astype(BF16), (((0,), (0,)), ((), ())), preferred_element_type=F32)


NN_DIMS = (((1,), (0,)), ((), ()))
NT_DIMS = (((1,), (1,)), ((), ()))


def _bf16_parts(x, parts):
    out = []
    for _ in range(parts):
        hi = x.astype(BF16)
        out.append(hi)
        x = x - hi.astype(F32)
    return out


def _split_dot(a, b, a_parts, b_parts, dims=NN_DIMS):
    acc = None
    b_terms = _bf16_parts(b, b_parts)
    for i, ai in enumerate(_bf16_parts(a, a_parts)):
        for j, bj in enumerate(b_terms):
            if i + j < max(a_parts, b_parts):
                term = lax.dot_general(ai, bj, dims, preferred_element_type=F32)
                acc = term if acc is None else acc + term
    return acc


def _softplus(z):
    return jnp.maximum(z, 0.0) + jnp.log1p(jnp.exp(-jnp.abs(z)))


def _expm1_nonpos(z):
    u = jnp.exp(z)
    safe = jnp.where(u == 1.0, 0.5, u)
    return jnp.where(u == 1.0, z, jnp.where(u == 0.0, -1.0, (safe - 1.0) * z / jnp.log(safe)))


def _silu(z):
    return z * jax.nn.sigmoid(z)


def _in_proj_kernel(x_ref, w_ref, b_ref, h_ref, xb_ref):
    @pl.when(pl.program_id(1) == 0)
    def _():
        xb_ref[...] = x_ref[...].astype(BF16)

    h_ref[...] = jnp.dot(xb_ref[...], w_ref[...], preferred_element_type=F32) + b_ref[...]


def _h_source_column(block):
    return block * BR_WIDTH if block < H_SPLIT else C_GATE_START + (block - H_SPLIT) * BR_WIDTH


def _in_proj(x, w_all, layer, b, tm=1024, tn=1024):
    s, k = x.shape
    assert (H_SPLIT * BR_WIDTH) % tn == 0 and tn % BR_WIDTH == 0
    per_tile = tn // BR_WIDTH
    starts = np.array([_h_source_column(j * per_tile) // LANES for j in range(H_WIDTH // tn)], np.int32)
    return pl.pallas_call(
        lambda starts_ref, *refs: _in_proj_kernel(*refs),
        grid_spec=pltpu.PrefetchScalarGridSpec(
            num_scalar_prefetch=1,
            grid=(s // tm, H_WIDTH // tn),
            in_specs=[
                pl.BlockSpec((tm, k), lambda i, j, st: (i, 0)),
                pl.BlockSpec((pl.Squeezed(), pl.Element(k), pl.Element(tn)),
                             lambda i, j, st: (layer, 0, st[j] * LANES)),
                pl.BlockSpec((1, tn), lambda i, j, st: (0, j)),
            ],
            out_specs=[
                pl.BlockSpec((tm, tn), lambda i, j, st: (i, j)),
                pl.BlockSpec((tm, k), lambda i, j, st: (i, 0)),
            ],
        ),
        out_shape=[jax.ShapeDtypeStruct((s, H_WIDTH), F32), jax.ShapeDtypeStruct((s, k), BF16)],
        compiler_params=_params(("parallel", "arbitrary"), 48),
        name="in_proj",
    )(jnp.asarray(starts), x, w_all, b)


def _lru_kernel(ax_ref, ag_ref, cw_ref, cb_ref, wa_ref, ba_ref, wx_ref, bx_ref, lam_ref, o_ref, ebuf, hc):
    t = ax_ref.shape[0]
    halo = SUBLANES

    @pl.when(pl.program_id(0) == 0)
    def _():
        ebuf[0:halo, :] = jnp.zeros((halo, BR_WIDTH), F32)
        hc[...] = jnp.zeros_like(hc)

    x = ax_ref[...]
    ebuf[halo:halo + t, :] = x
    u = cb_ref[...] + jnp.zeros((t, BR_WIDTH), F32)
    for j in range(LRU_CONV):
        u = u + cw_ref[j:j + 1, :] * ebuf[pl.ds(halo - (LRU_CONV - 1) + j, t), :]
    ebuf[0:halo, :] = x[t - halo:t, :]

    gate_r = jax.nn.sigmoid(_bdot(u, wa_ref[...]) + ba_ref[...])
    gate_i = jax.nn.sigmoid(_bdot(u, wx_ref[...]) + bx_ref[...])
    log_a = -LRU_C * gate_r * _softplus(-lam_ref[...])
    a = jnp.exp(log_a)
    b = jnp.sqrt(-_expm1_nonpos(2.0 * log_a)) * (gate_i * u)

    row = lax.broadcasted_iota(jnp.int32, (t, BR_WIDTH), 0)
    shift = 1
    while shift < t:
        valid = row >= shift
        b = jnp.where(valid, a * pltpu.roll(b, shift, 0), 0.0) + b
        a = jnp.where(valid, a * pltpu.roll(a, shift, 0), a)
        shift *= 2
    h = a * hc[0:1, :] + b
    hc[0:1, :] = h[t - 1:t, :]
    o_ref[...] = (h * _silu(ag_ref[...])).astype(BF16)


def _lru(h, cw, cb, wa, ba, wx, bx, lam, tile=256):
    s = h.shape[0]
    row = lambda c: pl.BlockSpec((tile, BR_WIDTH), lambda i: (i, c))
    full = lambda shape: pl.BlockSpec(shape, lambda i: (0,) * len(shape))
    return pl.pallas_call(
        _lru_kernel,
        grid=(s // tile,),
        in_specs=[row(COL_A_X), row(COL_A_GATE), full((LRU_CONV, BR_WIDTH)), full((1, BR_WIDTH)),
                  full((BR_WIDTH, BR_WIDTH)), full((1, BR_WIDTH)), full((BR_WIDTH, BR_WIDTH)), full((1, BR_WIDTH)),
                  full((1, BR_WIDTH))],
        out_specs=pl.BlockSpec((tile, BR_WIDTH), lambda i: (i, 0)),
        out_shape=jax.ShapeDtypeStruct((s, BR_WIDTH), BF16),
        scratch_shapes=[pltpu.VMEM((tile + SUBLANES, BR_WIDTH), F32), pltpu.VMEM((SUBLANES, BR_WIDTH), F32)],
        compiler_params=_params(("arbitrary",), 32),
        name="rglru",
    )(h, h, cw, cb, wa, ba, wx, bx, lam)


def _conf_kernel(val_ref, glu_ref, gate_ref, w_ref, b_ref, g_ref, beta_ref, o_ref, ebuf, shifted):
    t = val_ref.shape[0]
    halo = CONF_HALO

    @pl.when(pl.program_id(0) == 0)
    def _():
        ebuf[0:halo, :] = jnp.zeros((halo, BR_WIDTH), F32)

    cu = val_ref[...] * jax.nn.sigmoid(glu_ref[...])
    ebuf[halo:halo + t, :] = cu
    for b in range(SUBLANES):
        span = t + (CONF_KERNEL - 1 - b) // SUBLANES * SUBLANES
        shifted[b, 0:span, :] = ebuf[pl.ds(halo - (CONF_KERNEL - 1) + b, span), :]
    acc = b_ref[...] + jnp.zeros((t, BR_WIDTH), F32)
    for j in range(CONF_KERNEL):
        b = j % SUBLANES
        acc = acc + w_ref[j:j + 1, :] * shifted[b, j - b:j - b + t, :]
    ebuf[0:halo, :] = cu[t - halo:t, :]

    mu = jnp.mean(acc, axis=-1, keepdims=True)
    var = jnp.mean(jnp.square(acc - mu), axis=-1, keepdims=True)
    ln = (acc - mu) * lax.rsqrt(var + LN_EPS) * g_ref[...] + beta_ref[...]
    o_ref[...] = (_silu(ln) * _silu(gate_ref[...])).astype(BF16)


def _conformer(h, w, b, g, beta, tile=256):
    s = h.shape[0]
    row = lambda c: pl.BlockSpec((tile, BR_WIDTH), lambda i: (i, c))
    full = lambda shape: pl.BlockSpec(shape, lambda i: (0,) * len(shape))
    return pl.pallas_call(
        _conf_kernel,
        grid=(s // tile,),
        in_specs=[row(COL_D_VAL), row(COL_D_GLU), row(COL_D_GATE), full((CONF_KERNEL, BR_WIDTH)),
                  full((1, BR_WIDTH)), full((1, BR_WIDTH)), full((1, BR_WIDTH))],
        out_specs=pl.BlockSpec((tile, BR_WIDTH), lambda i: (i, 0)),
        out_shape=jax.ShapeDtypeStruct((s, BR_WIDTH), BF16),
        scratch_shapes=[pltpu.VMEM((tile + CONF_HALO, BR_WIDTH), F32),
                        pltpu.VMEM((SUBLANES, tile + CONF_HALO - SUBLANES, BR_WIDTH), F32)],
        compiler_params=_params(("arbitrary",), 32),
        name="conformer",
    )(h, h, h, w, b, g, beta)


def _t5_bucket(dist):
    max_exact = N_BUCKETS // 2
    large = max_exact + (np.log(np.maximum(dist, 1) / max_exact) / math.log(MAX_DISTANCE / max_exact)
                         * (N_BUCKETS - max_exact)).astype(np.int32)
    large = np.minimum(large, N_BUCKETS - 1)
    return np.where(dist < max_exact, dist, large).astype(np.int32)


def _bucket_index():
    qi = np.arange(ATT_SPAN)[:, None]
    kj = np.arange(2 * ATT_SPAN)[None, :]
    dist = qi + ATT_SPAN - kj
    valid = (dist >= 0) & (dist <= ATT_SPAN)
    per_group = [np.where(valid, _t5_bucket(np.clip(dist, 0, ATT_SPAN) * dil), -1) for _, dil in ATT_GROUPS]
    return np.stack(per_group).astype(np.int32)


def _bias_kernel(table_ref, bucket_ref, o_ref):
    head = pl.program_id(0)
    bucket = bucket_ref[...]
    acc = jnp.full(bucket.shape, NEG_INF, F32)
    for bkt in range(N_BUCKETS):
        acc = jnp.where(bucket == bkt, table_ref[bkt, head], acc)
    o_ref[...] = acc


def _attn_bias(table):
    blk = (None, ATT_SPAN, 2 * ATT_SPAN)
    return pl.pallas_call(
        _bias_kernel,
        grid=(ATT_HEADS,),
        in_specs=[pl.BlockSpec(memory_space=pltpu.SMEM),
                  pl.BlockSpec(blk, lambda hd: (hd // ATT_HEADS_PER_GROUP, 0, 0))],
        out_specs=pl.BlockSpec(blk, lambda hd: (hd, 0, 0)),
        out_shape=jax.ShapeDtypeStruct((ATT_HEADS, ATT_SPAN, 2 * ATT_SPAN), F32),
        compiler_params=_params(("parallel",), 32),
        name="attn_bias",
    )(table, jnp.asarray(_bucket_index()))


ATT_DIRECT_STRIDE = 4


def _residue_reader(ref, slab, dilation):
    if dilation == 1:
        return lambda b, r: ref[b * ATT_SPAN:(b + 1) * ATT_SPAN, :]
    if dilation <= ATT_DIRECT_STRIDE:
        return lambda b, r: ref[pl.ds(b * ATT_SPAN * dilation + r, ATT_SPAN, stride=dilation), :]
    inner, outer = ATT_DIRECT_STRIDE, dilation // ATT_DIRECT_STRIDE
    per = ref.shape[0] // inner
    for r0 in range(inner):
        slab[r0] = ref[pl.ds(r0, per, stride=inner), :]
    return lambda b, r: slab[r % inner, pl.ds(b * ATT_SPAN * outer + r // inner, ATT_SPAN, stride=outer), :]


def _residue_writer(ref, slab, dilation):
    if dilation == 1:
        def write(b, r, val):
            ref[b * ATT_SPAN:(b + 1) * ATT_SPAN, :] = val
        return write, lambda: None
    if dilation <= ATT_DIRECT_STRIDE:
        def write(b, r, val):
            ref[pl.ds(b * ATT_SPAN * dilation + r, ATT_SPAN, stride=dilation), :] = val
        return write, lambda: None
    inner, outer = ATT_DIRECT_STRIDE, dilation // ATT_DIRECT_STRIDE
    per = ref.shape[0] // inner

    def write(b, r, val):
        slab[r % inner, pl.ds(b * ATT_SPAN * outer + r // inner, ATT_SPAN, stride=outer), :] = val

    def flush():
        for r0 in range(inner):
            ref[pl.ds(r0, per, stride=inner), :] = slab[r0]

    return write, flush


def _attn_kernel(q_ref, kc_ref, kp_ref, vc_ref, vp_ref, bias_ref, o_ref, st_ref, st_acc, *slabs, dilation, blocks):
    first = pl.program_id(0) == 0
    hh = pl.program_id(1)
    scale = ATT_HEAD_DIM ** -0.5
    lane = lax.broadcasted_iota(jnp.int32, (ATT_SPAN, LANES), 1)
    slabs = slabs if slabs else (None,) * 7

    @pl.when(hh == 0)
    def _():
        st_acc[...] = jnp.zeros_like(st_acc)

    read_q, read_kc, read_kp, read_vc, read_vp = (
        _residue_reader(ref, slab, dilation) for ref, slab in zip((q_ref, kc_ref, kp_ref, vc_ref, vp_ref), slabs[:5]))
    write_o, flush_o = _residue_writer(o_ref, slabs[5], dilation)
    keys, values = {}, {}
    for r in range(dilation):
        keys[-1, r] = read_kp(0, r).astype(BF16)
        values[-1, r] = read_vp(0, r).astype(BF16)
        for b in range(blocks):
            keys[b, r] = read_kc(b, r).astype(BF16)
            values[b, r] = read_vc(b, r).astype(BF16)

    bias_p = bias_ref[:, 0:ATT_SPAN]
    bias_c = bias_ref[:, ATT_SPAN:2 * ATT_SPAN]
    units = [(b, r) for b in range(blocks) for r in range(dilation)]
    qs = [read_q(b, r).astype(BF16) for b, r in units]
    lps = [_bdot_nt(q, keys[b - 1, r]) * scale + bias_p for q, (b, r) in zip(qs, units)]
    lps = [jnp.where(first, NEG_INF, lp) if b == 0 else lp for lp, (b, r) in zip(lps, units)]
    lcs = [_bdot_nt(q, keys[b, r]) * scale + bias_c for q, (b, r) in zip(qs, units)]
    ms = [jnp.max(jnp.maximum(lp, lc), axis=-1, keepdims=True) for lp, lc in zip(lps, lcs)]
    pps = [jnp.exp(lp - m) for lp, m in zip(lps, ms)]
    pcs = [jnp.exp(lc - m) for lc, m in zip(lcs, ms)]
    dens = [jnp.sum(pp + pc, axis=-1, keepdims=True) for pp, pc in zip(pps, pcs)]
    for (b, r), pp, pc, m, den in zip(units, pps, pcs, ms, dens):
        write_o(b, r, (_bdot(pp, values[b - 1, r]) + _bdot(pc, values[b, r])) / den)
        tile = pl.ds((b * dilation + r) * ATT_SPAN, ATT_SPAN)
        st = jnp.where(lane == hh, m, st_acc[tile, :])
        st_acc[tile, :] = jnp.where(lane == ATT_HEADS_PER_GROUP + hh, den, st)
    flush_o()

    @pl.when(hh == ATT_HEADS_PER_GROUP - 1)
    def _():
        write_st, flush_st = _residue_writer(st_ref, slabs[6], dilation)
        for b in range(blocks):
            for r in range(dilation):
                write_st(b, r, st_acc[pl.ds((b * dilation + r) * ATT_SPAN, ATT_SPAN), :])
        flush_st()


def _attention_group(h, bias, group, dilation, rows_per_step=1024):
    s = h.shape[0]
    blk = ATT_SPAN * dilation
    blocks = max(1, rows_per_step // blk)
    heads = ATT_HEADS_PER_GROUP
    rows = blk * blocks

    def spec(col, prev):
        base = (col + group) * heads
        if prev:
            return pl.BlockSpec((blk, ATT_HEAD_DIM), lambda n, hd: (jnp.maximum(n * blocks - 1, 0), base + hd))
        return pl.BlockSpec((rows, ATT_HEAD_DIM), lambda n, hd: (n, base + hd))

    scratch = [pltpu.VMEM((rows, LANES), F32)]
    if dilation > ATT_DIRECT_STRIDE:
        slab = lambda nrows: pltpu.VMEM((ATT_DIRECT_STRIDE, nrows // ATT_DIRECT_STRIDE, LANES), F32)
        scratch += [slab(rows), slab(rows), slab(blk), slab(rows), slab(blk), slab(rows), slab(rows)]
    return pl.pallas_call(
        functools.partial(_attn_kernel, dilation=dilation, blocks=blocks),
        grid=(s // rows, heads),
        in_specs=[spec(COL_Q, False), spec(COL_K, False), spec(COL_K, True), spec(COL_V, False), spec(COL_V, True),
                  pl.BlockSpec((None, ATT_SPAN, 2 * ATT_SPAN), lambda n, hd: (group * heads + hd, 0, 0))],
        out_specs=[pl.BlockSpec((rows, ATT_HEAD_DIM), lambda n, hd: (n, hd)),
                   pl.BlockSpec((rows, LANES), lambda n, hd: (n, 0))],
        out_shape=[jax.ShapeDtypeStruct((s, BR_WIDTH), F32), jax.ShapeDtypeStruct((s, LANES), F32)],
        scratch_shapes=scratch,
        compiler_params=_params(("parallel", "arbitrary"), 32),
        name=f"dil_attn_d{dilation}",
    )(h, h, h, h, h, bias)


def _attn_merge_kernel(o0, o1, o2, s0, s1, s2, gate_ref, y_ref):
    outs = (o0, o1, o2)
    stats = (s0[...], s1[...], s2[...])
    for hh in range(ATT_HEADS_PER_GROUP):
        sl = slice(hh * ATT_HEAD_DIM, (hh + 1) * ATT_HEAD_DIM)
        ms = [st[:, hh:hh + 1] for st in stats]
        dens = [st[:, ATT_HEADS_PER_GROUP + hh:ATT_HEADS_PER_GROUP + hh + 1] for st in stats]
        m_all = jnp.maximum(jnp.maximum(ms[0], ms[1]), ms[2])
        wts = [jnp.exp(m - m_all) * d for m, d in zip(ms, dens)]
        num = wts[0] * outs[0][:, sl] + wts[1] * outs[1][:, sl] + wts[2] * outs[2][:, sl]
        y = num / (wts[0] + wts[1] + wts[2])
        y_ref[:, sl] = (y * _silu(gate_ref[:, sl])).astype(BF16)


def _attn_merge(outs, stats, h, tile=512):
    s = h.shape[0]
    o_spec = pl.BlockSpec((tile, BR_WIDTH), lambda i: (i, 0))
    s_spec = pl.BlockSpec((tile, LANES), lambda i: (i, 0))
    return pl.pallas_call(
        _attn_merge_kernel,
        grid=(s // tile,),
        in_specs=[o_spec] * 3 + [s_spec] * 3 + [pl.BlockSpec((tile, BR_WIDTH), lambda i: (i, COL_B_GATE))],
        out_specs=o_spec,
        out_shape=jax.ShapeDtypeStruct((s, BR_WIDTH), BF16),
        compiler_params=_params(("parallel",), 32),
        name="attn_merge",
    )(*outs, *stats, h)


def _head_sums(x):
    ri = lax.broadcasted_iota(jnp.int32, (PAIR, PAIR), 0)
    ci = lax.broadcasted_iota(jnp.int32, (PAIR, PAIR), 1)
    same_head = jnp.where((ri < RWKV_HEAD) == (ci < RWKV_HEAD), 1.0, 0.0).astype(BF16)
    return jnp.concatenate([_split_dot(x[:, p * PAIR:(p + 1) * PAIR], same_head, 2, 1) for p in range(N_PAIRS)], axis=1)


def _rwkv_prep_kernel(r_ref, k_ref, v_ref, lora_ref, mu_r, mu_k, mu_v, mu_l, w0_ref, wup_ref, a0_ref, aup_ref,
                      kk_ref, ka_ref, rk_ref,
                      lw_o, r_o, k_o, v_o, a_o, b_o, bonus_o, carry, carry_l):
    t = r_ref.shape[0]

    @pl.when(pl.program_id(0) == 0)
    def _():
        carry[...] = jnp.zeros_like(carry)
        carry_l[...] = jnp.zeros_like(carry_l)

    def shift_mix(x, mu, prev_row):
        row = lax.broadcasted_iota(jnp.int32, x.shape, 0)
        x_prev = jnp.where(row == 0, prev_row, pltpu.roll(x, 1, 0))
        return x + mu * (x_prev - x)

    r_in, k_in, v_in, l_in = r_ref[...], k_ref[...], v_ref[...], lora_ref[...]
    r = shift_mix(r_in, mu_r[...], carry[0:1, :])
    kx = shift_mix(k_in, mu_k[...], carry[1:2, :])
    vv = shift_mix(v_in, mu_v[...], carry[2:3, :])
    lo = shift_mix(l_in, mu_l[...], carry_l[0:1, :])
    carry[0:1, :] = r_in[t - 1:t, :]
    carry[1:2, :] = k_in[t - 1:t, :]
    carry[2:3, :] = v_in[t - 1:t, :]
    carry_l[0:1, :] = l_in[t - 1:t, :]

    w_log = -_softplus(-(w0_ref[...] + _bdot(jnp.tanh(lo), wup_ref[...]))) - 0.5
    lw_o[...] = -jnp.exp(w_log)
    a_icl = jax.nn.sigmoid(a0_ref[...] + _bdot(lo, aup_ref[...]))

    kk = kx * kk_ref[...]
    kk = kk / jnp.maximum(jnp.sqrt(_head_sums(kk * kk)), 1e-12)
    kc = kx * (1.0 + (a_icl - 1.0) * ka_ref[...])
    r_o[...] = r
    k_o[...] = kc
    v_o[...] = vv
    a_o[...] = -kk
    b_o[...] = kk * a_icl
    bonus_o[...] = _head_sums(r * kc * rk_ref[...]) * vv


def _rwkv_prep(h, mu_r, mu_k, mu_v, mu_l, w0, wup, a0, aup, k_k, k_a, r_k, tile=256):
    s = h.shape[0]
    row = lambda c: pl.BlockSpec((tile, BR_WIDTH), lambda i: (i, c))
    full = lambda shape: pl.BlockSpec(shape, lambda i: (0,) * len(shape))
    vec = full((1, BR_WIDTH))
    out = pl.BlockSpec((tile, BR_WIDTH), lambda i: (i, 0))
    return pl.pallas_call(
        _rwkv_prep_kernel,
        grid=(s // tile,),
        in_specs=[row(COL_C_R), row(COL_C_K), row(COL_C_V),
                  pl.BlockSpec((tile, LANES), lambda i: (i, COL_C_LORA * (BR_WIDTH // LANES))),
                  vec, vec, vec, full((1, LANES)), vec, full((LANES, BR_WIDTH)), vec, full((LANES, BR_WIDTH)),
                  vec, vec, vec],
        out_specs=[out] * 7,
        out_shape=[jax.ShapeDtypeStruct((s, BR_WIDTH), F32)] * 7,
        scratch_shapes=[pltpu.VMEM((SUBLANES, BR_WIDTH), F32), pltpu.VMEM((SUBLANES, LANES), F32)],
        compiler_params=_params(("arbitrary",), 32),
        name="rwkv_prep",
    )(h, h, h, h, mu_r, mu_k, mu_v, mu_l, w0, wup, a0, aup, k_k, k_a, r_k)


def _stack_heads(x):
    lane = lax.broadcasted_iota(jnp.int32, x.shape, 1)
    return jnp.concatenate([jnp.where(lane < RWKV_HEAD, x, 0.0), jnp.where(lane >= RWKV_HEAD, x, 0.0)], axis=0)


def _unit_lower_inverse(a_strict):
    n = a_strict[0].shape[0]
    ri = lax.broadcasted_iota(jnp.int32, (n, n), 0)
    ci = lax.broadcasted_iota(jnp.int32, (n, n), 1)

    def same_block(bits):
        return (ri >> bits) == (ci >> bits)

    pw = [jnp.where(same_block(4), a, 0.0) for a in a_strict]
    x = [jnp.where(ri == ci, 1.0, 0.0) + p for p in pw]
    for _ in range(3):
        pw = [_bdot(p, p) for p in pw]
        x = [xi + _bdot(xi, p) for xi, p in zip(x, pw)]
    for bits in (5, 6):
        join = same_block(bits) & jnp.logical_not(same_block(bits - 1))
        xe = [_bdot(xi, jnp.where(join, a, 0.0)) for xi, a in zip(x, a_strict)]
        x = [xi + _bdot(t, xi) for xi, t in zip(x, xe)]
    return x


def _rwkv_chunk_kernel(lw_ref, r_ref, k_ref, v_ref, a_ref, b_ref, q_o, yc_o, g_o, z_o):
    c = RWKV_CHUNK
    n = 2 * c
    ti = lax.broadcasted_iota(jnp.int32, (c, c), 0)
    si = lax.broadcasted_iota(jnp.int32, (c, c), 1)
    lower_ones = jnp.where(si <= ti, 1.0, 0.0)
    ri = lax.broadcasted_iota(jnp.int32, (n, n), 0)
    ci = lax.broadcasted_iota(jnp.int32, (n, n), 1)
    same_head = (ri >> 6) == (ci >> 6)
    strict = same_head & (ri > ci)
    incl = same_head & (ri >= ci)
    eye = ri == ci

    units = [(ch, p) for ch in range(lw_ref.shape[0] // c) for p in range(N_PAIRS)]
    each = lambda f, *cols: [f(*args) for args in zip(*cols)]

    def load(ref):
        return [ref[ch * c:(ch + 1) * c, p * PAIR:(p + 1) * PAIR] for ch, p in units]

    lw, r, k, v, a, b = (load(ref) for ref in (lw_ref, r_ref, k_ref, v_ref, a_ref, b_ref))
    cs = each(lambda x: _split_dot(lower_ones, x, 1, 3), lw)
    c_end = each(lambda x: x[c - 1:c, :], cs)
    r_t = each(lambda x, y: _stack_heads(x * jnp.exp(y)), r, cs)
    a_t = each(lambda x, y, z: _stack_heads(x * jnp.exp(y - z)), a, cs, lw)
    b_t = each(lambda x, y: _stack_heads(x * jnp.exp(-y)), b, cs)
    k_t = each(lambda x, y: _stack_heads(x * jnp.exp(-y)), k, cs)
    b_h = each(lambda x, y, e: _stack_heads(x * jnp.exp(e - y)), b, cs, c_end)
    k_h = each(lambda x, y, e: _stack_heads(x * jnp.exp(e - y)), k, cs, c_end)
    v_s = each(_stack_heads, v)

    aa = each(lambda at, rt, bt, kt: _bdot_nt(jnp.concatenate([at, rt], axis=0), jnp.concatenate([bt, kt], axis=0)),
              a_t, r_t, b_t, k_t)
    a_ab = each(lambda x: jnp.where(strict, x[0:n, 0:n], 0.0), aa)
    a_ak = each(lambda x: jnp.where(strict, x[0:n, n:2 * n], 0.0), aa)
    a_rb = each(lambda x: jnp.where(incl, x[n:2 * n, 0:n], 0.0), aa)
    a_rk = each(lambda x: jnp.where(incl, x[n:2 * n, n:2 * n], 0.0), aa)

    minv = _unit_lower_inverse(a_ab)
    w = each(_bdot, minv, a_t)
    uv = each(_bdot, minv, each(_bdot, a_ak, v_s))
    q = each(lambda rt, x, y: rt + _bdot(x, y), r_t, a_rb, w)
    yc = each(lambda x, y, z, t: _bdot(x, y) + _bdot(z, t), a_rb, uv, a_rk, v_s)
    g = each(lambda e, x, y: jnp.where(eye, jnp.exp(e), 0.0) + _bdot_tn(x, y), c_end, w, b_h)
    z = each(lambda u_, v_, bh, kh: _bdot_tn(jnp.concatenate([u_, v_], axis=0), jnp.concatenate([bh, kh], axis=0)),
             uv, v_s, b_h, k_h)
    for (ch, p), q_u, yc_u, g_u, z_u in zip(units, q, yc, g, z):
        rows, sl = slice(ch * n, (ch + 1) * n), slice(p * PAIR, (p + 1) * PAIR)
        q_o[rows, sl] = q_u
        yc_o[rows, sl] = yc_u
        g_o[rows, sl] = g_u
        z_o[rows, sl] = z_u


def _rwkv_chunk(lw, r, k, v, a, b, chunks=4):
    s = lw.shape[0]
    nc = s // (chunks * RWKV_CHUNK)
    inp = pl.BlockSpec((chunks * RWKV_CHUNK, BR_WIDTH), lambda i: (i, 0))
    out = pl.BlockSpec((chunks * 2 * RWKV_CHUNK, BR_WIDTH), lambda i: (i, 0))
    return pl.pallas_call(
        _rwkv_chunk_kernel,
        grid=(nc,),
        in_specs=[inp] * 6,
        out_specs=[out] * 4,
        out_shape=[jax.ShapeDtypeStruct((2 * s, BR_WIDTH), F32)] * 4,
        compiler_params=_params(("parallel",), 32),
        name="rwkv_chunk",
    )(lw, r, k, v, a, b)


def _rwkv_scan_kernel(q_ref, yc_ref, g_ref, z_ref, bonus_ref, gate_ref, gn_g, gn_b, o_ref, state, ybuf):
    c = RWKV_CHUNK
    n = 2 * c
    chunks = q_ref.shape[0] // n

    @pl.when(pl.program_id(0) == 0)
    def _():
        state[...] = jnp.zeros_like(state)

    pairs = [slice(p * PAIR, (p + 1) * PAIR) for p in range(N_PAIRS)]
    sts = [state[:, sl] for sl in pairs]
    starts = []
    for ch in range(chunks):
        rows = slice(ch * n, (ch + 1) * n)
        starts.append(sts)
        sts = [_split_dot(st, g_ref[rows, sl], 2, 2) + z_ref[rows, sl] for st, sl in zip(sts, pairs)]
    for st, sl in zip(sts, pairs):
        state[:, sl] = st
    for ch in range(chunks):
        rows = slice(ch * n, (ch + 1) * n)
        for st, sl in zip(starts[ch], pairs):
            y_st = _split_dot(q_ref[rows, sl], st, 2, 2, NT_DIMS) + yc_ref[rows, sl]
            ybuf[ch * c:(ch + 1) * c, sl] = y_st[0:c, :] + y_st[c:n, :]

    wy = ybuf[...]
    inv_n = 1.0 / RWKV_HEAD
    mu = _head_sums(wy) * inv_n
    d = wy - mu
    var = _head_sums(d * d) * inv_n
    wy = d * lax.rsqrt(var + RWKV_GN_EPS) * gn_g[...] + gn_b[...]
    o_ref[...] = ((wy + bonus_ref[...]) * _silu(gate_ref[...])).astype(BF16)


def _rwkv_scan(q, yc, g, z, bonus, h, gn_g, gn_b, chunks=4):
    s = bonus.shape[0]
    tile = chunks * RWKV_CHUNK
    big = pl.BlockSpec((2 * tile, BR_WIDTH), lambda i: (i, 0))
    row = pl.BlockSpec((tile, BR_WIDTH), lambda i: (i, 0))
    full = lambda shape: pl.BlockSpec(shape, lambda i: (0,) * len(shape))
    return pl.pallas_call(
        _rwkv_scan_kernel,
        grid=(s // tile,),
        in_specs=[big] * 4 + [row, pl.BlockSpec((tile, BR_WIDTH), lambda i: (i, COL_C_GATE)),
                              full((1, BR_WIDTH)), full((1, BR_WIDTH))],
        out_specs=row,
        out_shape=jax.ShapeDtypeStruct((s, BR_WIDTH), BF16),
        scratch_shapes=[pltpu.VMEM((PAIR, BR_WIDTH), F32), pltpu.VMEM((tile, BR_WIDTH), F32)],
        compiler_params=_params(("arbitrary",), 32),
        name="rwkv_scan",
    )(q, yc, g, z, bonus, h, gn_g, gn_b)


def _mix_kernel(xb_ref, *refs):
    ygs, wms, bms, wbrs = (refs[k * N_BRANCH:(k + 1) * N_BRANCH] for k in range(4))
    o_ref = refs[4 * N_BRANCH]
    xb = xb_ref[...]
    acc = None
    for n in range(N_BRANCH):
        gate = jax.nn.sigmoid(jnp.dot(xb, wms[n][...], preferred_element_type=F32) + bms[n][...])
        val = gate * jnp.dot(ygs[n][...], wbrs[n][...], preferred_element_type=F32)
        acc = val if acc is None else acc + val
    o_ref[...] = acc.astype(BF16)


def _mix(xb, ygs, w_all, layer, bm, wbr, tm=1024, tn=512):
    s = xb.shape[0]
    nj = D_MODEL // tn
    per_branch = lambda make: [make(n) for n in range(N_BRANCH)]
    return pl.pallas_call(
        _mix_kernel,
        grid=(s // tm, nj),
        in_specs=[pl.BlockSpec((tm, D_MODEL), lambda i, j: (i, 0))]
        + per_branch(lambda n: pl.BlockSpec((tm, BR_WIDTH), lambda i, j: (i, 0)))
        + per_branch(lambda n: pl.BlockSpec((pl.Squeezed(), pl.Element(D_MODEL), pl.Element(tn)),
                                            lambda i, j: (layer, 0, ((BRANCH_IN + n * D_MODEL) // LANES
                                                                     + j * (tn // LANES)) * LANES)))
        + per_branch(lambda n: pl.BlockSpec((1, tn), lambda i, j: (0, n * nj + j)))
        + per_branch(lambda n: pl.BlockSpec((None, BR_WIDTH, tn), lambda i, j: (n, 0, j))),
        out_specs=pl.BlockSpec((tm, tn), lambda i, j: (i, j)),
        out_shape=jax.ShapeDtypeStruct((s, D_MODEL), BF16),
        compiler_params=_params(("parallel", "arbitrary"), 48),
        name="branch_mix",
    )(xb, *ygs, *([w_all] * N_BRANCH), *([bm] * N_BRANCH), *([wbr] * N_BRANCH))


def _out_kernel(mixed_ref, x_ref, w_ref, g_ref, b_ref, o_ref):
    y = ALPHA * x_ref[...] + jnp.dot(mixed_ref[...], w_ref[...], preferred_element_type=F32)
    mu = jnp.mean(y, axis=-1, keepdims=True)
    var = jnp.mean(jnp.square(y - mu), axis=-1, keepdims=True)
    o_ref[...] = (y - mu) * lax.rsqrt(var + LN_EPS) * g_ref[...] + b_ref[...]


def _out_proj(mixed, x, w, g, b, tm=512):
    s = x.shape[0]
    row = pl.BlockSpec((tm, D_MODEL), lambda i: (i, 0))
    vec = pl.BlockSpec((1, D_MODEL), lambda i: (0, 0))
    return pl.pallas_call(
        _out_kernel,
        grid=(s // tm,),
        in_specs=[row, row, pl.BlockSpec((D_MODEL, D_MODEL), lambda i: (0, 0)), vec, vec],
        out_specs=row,
        out_shape=jax.ShapeDtypeStruct((s, D_MODEL), F32),
        compiler_params=_params(("parallel",), 48),
        name="out_proj_ln",
    )(mixed, x, w, g, b)


def _block_diag(w):
    blocks, n, _ = w.shape
    eye = jnp.eye(blocks, dtype=w.dtype)
    return (eye[:, None, :, None] * w[:, :, None, :]).reshape(blocks * n, blocks * n)


def _layer(x, att_bias, w_in_bf16, layer, b_in, lru_conv_w, lru_conv_b, lru_gate_a_w, lru_gate_a_b, lru_gate_x_w, lru_gate_x_b,
           lru_lambda, rwkv_mu, rwkv_w0, rwkv_w_up, rwkv_a0, rwkv_a_up, rwkv_k_k, rwkv_k_a, rwkv_r_k, rwkv_gn_g,
           rwkv_gn_b, conf_dw_w, conf_dw_b, conf_ln_g, conf_ln_b, w_br, w_out, ln_g, ln_b):
    vec = lambda t: t.reshape(1, -1)
    b_h = jnp.concatenate([b_in[:H_SPLIT * BR_WIDTH], b_in[C_GATE_START:BRANCH_IN]])
    h, xb = _in_proj(x, w_in_bf16, layer, vec(b_h))

    yg_a = _lru(h, lru_conv_w, vec(lru_conv_b), _block_diag(lru_gate_a_w).astype(BF16), vec(lru_gate_a_b),
                _block_diag(lru_gate_x_w).astype(BF16), vec(lru_gate_x_b), vec(lru_lambda))

    outs, stats = [], []
    for g, (_, dil) in enumerate(ATT_GROUPS):
        o_g, st_g = _attention_group(h, att_bias, g, dil)
        outs.append(o_g)
        stats.append(st_g)
    yg_b = _attn_merge(outs, stats, h)

    mu = rwkv_mu
    zpad = jnp.zeros((DECAY_RANK, BR_WIDTH), F32)
    wup = jnp.concatenate([rwkv_w_up, zpad], axis=0).astype(BF16)
    aup = jnp.concatenate([zpad, rwkv_a_up], axis=0).astype(BF16)
    lw, r, kc, vv, a, b, bonus = _rwkv_prep(
        h, vec(mu[:BR_WIDTH]), vec(mu[BR_WIDTH:2 * BR_WIDTH]), vec(mu[2 * BR_WIDTH:3 * BR_WIDTH]),
        vec(mu[3 * BR_WIDTH:]), vec(rwkv_w0), wup, vec(rwkv_a0), aup, vec(rwkv_k_k), vec(rwkv_k_a),
        vec(rwkv_r_k))
    q, yc, g_mat, z = _rwkv_chunk(lw, r, kc, vv, a, b)
    yg_c = _rwkv_scan(q, yc, g_mat, z, bonus, h, vec(rwkv_gn_g), vec(rwkv_gn_b))

    yg_d = _conformer(h, conf_dw_w, vec(conf_dw_b), vec(conf_ln_g), vec(conf_ln_b))

    mixed = _mix(xb, (yg_a, yg_b, yg_c, yg_d), w_in_bf16, layer, vec(b_in[BRANCH_IN:]), w_br.astype(BF16))
    return _out_proj(mixed, x, w_out.astype(BF16), vec(ln_g), vec(ln_b))


def kernel(x, att_rel_bias, w_in, b_in, lru_conv_w, lru_conv_b, lru_gate_a_w, lru_gate_a_b, lru_gate_x_w, lru_gate_x_b, lru_lambda, rwkv_mu, rwkv_w0, rwkv_w_up, rwkv_a0, rwkv_a_up, rwkv_k_k, rwkv_k_a, rwkv_r_k, rwkv_gn_g, rwkv_gn_b, conf_dw_w, conf_dw_b, conf_ln_g, conf_ln_b, w_br, w_out, ln_g, ln_b):
    bsz, s, d = x.shape
    assert bsz == 1 and d == D_MODEL and s % (16 * ATT_SPAN) == 0
    per_layer = (b_in, lru_conv_w, lru_conv_b, lru_gate_a_w, lru_gate_a_b, lru_gate_x_w, lru_gate_x_b,
                 lru_lambda, rwkv_mu, rwkv_w0, rwkv_w_up, rwkv_a0, rwkv_a_up, rwkv_k_k, rwkv_k_a, rwkv_r_k,
                 rwkv_gn_g, rwkv_gn_b, conf_dw_w, conf_dw_b, conf_ln_g, conf_ln_b, w_br, w_out, ln_g, ln_b)
    y = x.reshape(s, d)
    att_bias = _attn_bias(att_rel_bias)
    w_in_bf16 = w_in.astype(BF16)
    for l in range(DEPTH):
        y = _layer(y, att_bias, w_in_bf16, l, *(t[l] for t in per_layer))
    return y.reshape(bsz, s, d)
```

```python
import functools
import math

import numpy as np
import jax
import jax.numpy as jnp
from jax import lax
from jax.experimental import pallas as pl
from jax.experimental.pallas import tpu as pltpu

D_MODEL = 2048
DEPTH = 2
N_BRANCH = 4
BR_WIDTH = 512
LRU_BLOCKS = 8
LRU_BLOCK = BR_WIDTH // LRU_BLOCKS
LRU_CONV = 4
LRU_C = 8.0
ATT_GROUPS = ((128, 1), (512, 4), (2048, 16))
ATT_HEADS_PER_GROUP = 4
ATT_HEAD_DIM = BR_WIDTH // ATT_HEADS_PER_GROUP
ATT_HEADS = len(ATT_GROUPS) * ATT_HEADS_PER_GROUP
ATT_QKV = ATT_HEADS * ATT_HEAD_DIM
ATT_SPAN = 128
N_BUCKETS = 32
MAX_DISTANCE = 2048
NEG_INF = -1e30
RWKV_HEAD = 64
RWKV_HEADS = BR_WIDTH // RWKV_HEAD
DECAY_RANK = 64
ICLR_RANK = 64
RWKV_GN_EPS = 64e-5
CONF_KERNEL = 31
LN_EPS = 1e-5
ALPHA = (2.0 * DEPTH) ** 0.25

LANES = 128
SUBLANES = 8
MIB = 1024 * 1024

BRANCH_IN = 2 * BR_WIDTH + 3 * ATT_QKV + BR_WIDTH + (4 * BR_WIDTH + DECAY_RANK + ICLR_RANK) + 3 * BR_WIDTH
C_GATE_START = BRANCH_IN - 4 * BR_WIDTH
H_SPLIT = 16
H_BLOCKS = 20
H_WIDTH = H_BLOCKS * BR_WIDTH
COL_A_X, COL_A_GATE = 0, 1
COL_Q, COL_K, COL_V, COL_B_GATE = 2, 5, 8, 11
COL_C_R, COL_C_K, COL_C_V, COL_C_LORA = 12, 13, 14, 15
COL_C_GATE, COL_D_VAL, COL_D_GLU, COL_D_GATE = 16, 17, 18, 19

CONF_HALO = 32
RWKV_CHUNK = 64
PAIR = 2 * RWKV_HEAD
N_PAIRS = BR_WIDTH // PAIR

F32 = jnp.float32
BF16 = jnp.bfloat16


def _params(semantics, vmem_mib):
    return pltpu.CompilerParams(dimension_semantics=semantics, vmem_limit_bytes=vmem_mib * MIB)


def _bdot(a, b):
    return jnp.dot(a.astype(BF16), b.astype(BF16), preferred_element_type=F32)


def _bdot_nt(a, b):
    return lax.dot_general(a.astype(BF16), b.astype(BF16), (((1,), (1,)), ((), ())), preferred_element_type=F32)


def _bdot_tn(a, b):
    return lax.dot_general(a.astype(BF16), b.astype(BF16), (((0,), (0,)), ((), ())), preferred_element_type=F32)


NN_DIMS = (((1,), (0,)), ((), ()))
NT_DIMS = (((1,), (1,)), ((), ()))


def _bf16_parts(x, parts):
    out = []
    for _ in range(parts):
        hi = x.astype(BF16)
        out.append(hi)
        x = x - hi.astype(F32)
    return out


def _split_dot(a, b, a_parts, b_parts, dims=NN_DIMS):
    acc = None
    b_terms = _bf16_parts(b, b_parts)
    for i, ai in enumerate(_bf16_parts(a, a_parts)):
        for j, bj in enumerate(b_terms):
            if i + j < max(a_parts, b_parts):
                term = lax.dot_general(ai, bj, dims, preferred_element_type=F32)
                acc = term if acc is None else acc + term
    return acc


def _softplus(z):
    return jnp.maximum(z, 0.0) + jnp.log1p(jnp.exp(-jnp.abs(z)))


def _expm1_nonpos(z):
    u = jnp.exp(z)
    safe = jnp.where(u == 1.0, 0.5, u)
    return jnp.where(u == 1.0, z, jnp.where(u == 0.0, -1.0, (safe - 1.0) * z / jnp.log(safe)))


def _silu(z):
    return z * jax.nn.sigmoid(z)


def _in_proj_kernel(x_ref, w_ref, b_ref, h_ref, xb_ref):
    @pl.when(pl.program_id(1) == 0)
    def _():
        xb_ref[...] = x_ref[...].astype(BF16)

    h_ref[...] = jnp.dot(xb_ref[...], w_ref[...], preferred_element_type=F32) + b_ref[...]


def _h_source_column(block):
    return block * BR_WIDTH if block < H_SPLIT else C_GATE_START + (block - H_SPLIT) * BR_WIDTH


def _in_proj(x, w_all, layer, b, tm=1024, tn=1024):
    s, k = x.shape
    assert (H_SPLIT * BR_WIDTH) % tn == 0 and tn % BR_WIDTH == 0
    per_tile = tn // BR_WIDTH
    starts = np.array([_h_source_column(j * per_tile) // LANES for j in range(H_WIDTH // tn)], np.int32)
    return pl.pallas_call(
        lambda starts_ref, *refs: _in_proj_kernel(*refs),
        grid_spec=pltpu.PrefetchScalarGridSpec(
            num_scalar_prefetch=1,
            grid=(s // tm, H_WIDTH // tn),
            in_specs=[
                pl.BlockSpec((tm, k), lambda i, j, st: (i, 0)),
                pl.BlockSpec((pl.Squeezed(), pl.Element(k), pl.Element(tn)),
                             lambda i, j, st: (layer, 0, st[j] * LANES)),
                pl.BlockSpec((1, tn), lambda i, j, st: (0, j)),
            ],
            out_specs=[
                pl.BlockSpec((tm, tn), lambda i, j, st: (i, j)),
                pl.BlockSpec((tm, k), lambda i, j, st: (i, 0)),
            ],
        ),
        out_shape=[jax.ShapeDtypeStruct((s, H_WIDTH), F32), jax.ShapeDtypeStruct((s, k), BF16)],
        compiler_params=_params(("parallel", "arbitrary"), 48),
        name="in_proj",
    )(jnp.asarray(starts), x, w_all, b)


def _lru_kernel(ax_ref, ag_ref, cw_ref, cb_ref, wa_ref, ba_ref, wx_ref, bx_ref, lam_ref, o_ref, ebuf, hc):
    t = ax_ref.shape[0]
    halo = SUBLANES

    @pl.when(pl.program_id(0) == 0)
    def _():
        ebuf[0:halo, :] = jnp.zeros((halo, BR_WIDTH), F32)
        hc[...] = jnp.zeros_like(hc)

    x = ax_ref[...]
    ebuf[halo:halo + t, :] = x
    u = cb_ref[...] + jnp.zeros((t, BR_WIDTH), F32)
    for j in range(LRU_CONV):
        u = u + cw_ref[j:j + 1, :] * ebuf[pl.ds(halo - (LRU_CONV - 1) + j, t), :]
    ebuf[0:halo, :] = x[t - halo:t, :]

    gate_r = jax.nn.sigmoid(_bdot(u, wa_ref[...]) + ba_ref[...])
    gate_i = jax.nn.sigmoid(_bdot(u, wx_ref[...]) + bx_ref[...])
    log_a = -LRU_C * gate_r * _softplus(-lam_ref[...])
    a = jnp.exp(log_a)
    b = jnp.sqrt(-_expm1_nonpos(2.0 * log_a)) * (gate_i * u)

    row = lax.broadcasted_iota(jnp.int32, (t, BR_WIDTH), 0)
    shift = 1
    while shift < t:
        valid = row >= shift
        b = jnp.where(valid, a * pltpu.roll(b, shift, 0), 0.0) + b
        a = jnp.where(valid, a * pltpu.roll(a, shift, 0), a)
        shift *= 2
    h = a * hc[0:1, :] + b
    hc[0:1, :] = h[t - 1:t, :]
    o_ref[...] = (h * _silu(ag_ref[...])).astype(BF16)


def _lru(h, cw, cb, wa, ba, wx, bx, lam, tile=256):
    s = h.shape[0]
    row = lambda c: pl.BlockSpec((tile, BR_WIDTH), lambda i: (i, c))
    full = lambda shape: pl.BlockSpec(shape, lambda i: (0,) * len(shape))
    return pl.pallas_call(
        _lru_kernel,
        grid=(s // tile,),
        in_specs=[row(COL_A_X), row(COL_A_GATE), full((LRU_CONV, BR_WIDTH)), full((1, BR_WIDTH)),
                  full((BR_WIDTH, BR_WIDTH)), full((1, BR_WIDTH)), full((BR_WIDTH, BR_WIDTH)), full((1, BR_WIDTH)),
                  full((1, BR_WIDTH))],
        out_specs=pl.BlockSpec((tile, BR_WIDTH), lambda i: (i, 0)),
        out_shape=jax.ShapeDtypeStruct((s, BR_WIDTH), BF16),
        scratch_shapes=[pltpu.VMEM((tile + SUBLANES, BR_WIDTH), F32), pltpu.VMEM((SUBLANES, BR_WIDTH), F32)],
        compiler_params=_params(("arbitrary",), 32),
        name="rglru",
    )(h, h, cw, cb, wa, ba, wx, bx, lam)


def _conf_kernel(val_ref, glu_ref, gate_ref, w_ref, b_ref, g_ref, beta_ref, o_ref, ebuf, shifted):
    t = val_ref.shape[0]
    halo = CONF_HALO

    @pl.when(pl.program_id(0) == 0)
    def _():
        ebuf[0:halo, :] = jnp.zeros((halo, BR_WIDTH), F32)

    cu = val_ref[...] * jax.nn.sigmoid(glu_ref[...])
    ebuf[halo:halo + t, :] = cu
    for b in range(SUBLANES):
        span = t + (CONF_KERNEL - 1 - b) // SUBLANES * SUBLANES
        shifted[b, 0:span, :] = ebuf[pl.ds(halo - (CONF_KERNEL - 1) + b, span), :]
    acc = b_ref[...] + jnp.zeros((t, BR_WIDTH), F32)
    for j in range(CONF_KERNEL):
        b = j % SUBLANES
        acc = acc + w_ref[j:j + 1, :] * shifted[b, j - b:j - b + t, :]
    ebuf[0:halo, :] = cu[t - halo:t, :]

    mu = jnp.mean(acc, axis=-1, keepdims=True)
    var = jnp.mean(jnp.square(acc - mu), axis=-1, keepdims=True)
    ln = (acc - mu) * lax.rsqrt(var + LN_EPS) * g_ref[...] + beta_ref[...]
    o_ref[...] = (_silu(ln) * _silu(gate_ref[...])).astype(BF16)


def _conformer(h, w, b, g, beta, tile=256):
    s = h.shape[0]
    row = lambda c: pl.BlockSpec((tile, BR_WIDTH), lambda i: (i, c))
    full = lambda shape: pl.BlockSpec(shape, lambda i: (0,) * len(shape))
    return pl.pallas_call(
        _conf_kernel,
        grid=(s // tile,),
        in_specs=[row(COL_D_VAL), row(COL_D_GLU), row(COL_D_GATE), full((CONF_KERNEL, BR_WIDTH)),
                  full((1, BR_WIDTH)), full((1, BR_WIDTH)), full((1, BR_WIDTH))],
        out_specs=pl.BlockSpec((tile, BR_WIDTH), lambda i: (i, 0)),
        out_shape=jax.ShapeDtypeStruct((s, BR_WIDTH), BF16),
        scratch_shapes=[pltpu.VMEM((tile + CONF_HALO, BR_WIDTH), F32),
                        pltpu.VMEM((SUBLANES, tile + CONF_HALO - SUBLANES, BR_WIDTH), F32)],
        compiler_params=_params(("arbitrary",), 32),
        name="conformer",
    )(h, h, h, w, b, g, beta)


def _t5_bucket(dist):
    max_exact = N_BUCKETS // 2
    large = max_exact + (np.log(np.maximum(dist, 1) / max_exact) / math.log(MAX_DISTANCE / max_exact)
                         * (N_BUCKETS - max_exact)).astype(np.int32)
    large = np.minimum(large, N_BUCKETS - 1)
    return np.where(dist < max_exact, dist, large).astype(np.int32)


def _bucket_index():
    qi = np.arange(ATT_SPAN)[:, None]
    kj = np.arange(2 * ATT_SPAN)[None, :]
    dist = qi + ATT_SPAN - kj
    valid = (dist >= 0) & (dist <= ATT_SPAN)
    per_group = [np.where(valid, _t5_bucket(np.clip(dist, 0, ATT_SPAN) * dil), -1) for _, dil in ATT_GROUPS]
    return np.stack(per_group).astype(np.int32)


def _bias_kernel(table_ref, bucket_ref, o_ref):
    head = pl.program_id(0)
    bucket = bucket_ref[...]
    acc = jnp.full(bucket.shape, NEG_INF, F32)
    for bkt in range(N_BUCKETS):
        acc = jnp.where(bucket == bkt, table_ref[bkt, head], acc)
    o_ref[...] = acc


def _attn_bias(table):
    blk = (None, ATT_SPAN, 2 * ATT_SPAN)
    return pl.pallas_call(
        _bias_kernel,
        grid=(ATT_HEADS,),
        in_specs=[pl.BlockSpec(memory_space=pltpu.SMEM),
                  pl.BlockSpec(blk, lambda hd: (hd // ATT_HEADS_PER_GROUP, 0, 0))],
        out_specs=pl.BlockSpec(blk, lambda hd: (hd, 0, 0)),
        out_shape=jax.ShapeDtypeStruct((ATT_HEADS, ATT_SPAN, 2 * ATT_SPAN), F32),
        compiler_params=_params(("parallel",), 32),
        name="attn_bias",
    )(table, jnp.asarray(_bucket_index()))


ATT_DIRECT_STRIDE = 4


def _residue_reader(ref, slab, dilation):
    if dilation == 1:
        return lambda b, r: ref[b * ATT_SPAN:(b + 1) * ATT_SPAN, :]
    if dilation <= ATT_DIRECT_STRIDE:
        return lambda b, r: ref[pl.ds(b * ATT_SPAN * dilation + r, ATT_SPAN, stride=dilation), :]
    inner, outer = ATT_DIRECT_STRIDE, dilation // ATT_DIRECT_STRIDE
    per = ref.shape[0] // inner
    for r0 in range(inner):
        slab[r0] = ref[pl.ds(r0, per, stride=inner), :]
    return lambda b, r: slab[r % inner, pl.ds(b * ATT_SPAN * outer + r // inner, ATT_SPAN, stride=outer), :]


def _residue_writer(ref, slab, dilation):
    if dilation == 1:
        def write(b, r, val):
            ref[b * ATT_SPAN:(b + 1) * ATT_SPAN, :] = val
        return write, lambda: None
    if dilation <= ATT_DIRECT_STRIDE:
        def write(b, r, val):
            ref[pl.ds(b * ATT_SPAN * dilation + r, ATT_SPAN, stride=dilation), :] = val
        return write, lambda: None
    inner, outer = ATT_DIRECT_STRIDE, dilation // ATT_DIRECT_STRIDE
    per = ref.shape[0] // inner

    def write(b, r, val):
        slab[r % inner, pl.ds(b * ATT_SPAN * outer + r // inner, ATT_SPAN, stride=outer), :] = val

    def flush():
        for r0 in range(inner):
            ref[pl.ds(r0, per, stride=inner), :] = slab[r0]

    return write, flush


def _attn_kernel(q_ref, kc_ref, kp_ref, vc_ref, vp_ref, bias_ref, o_ref, st_ref, st_acc, *slabs, dilation, blocks):
    first = pl.program_id(0) == 0
    hh = pl.program_id(1)
    scale = ATT_HEAD_DIM ** -0.5
    lane = lax.broadcasted_iota(jnp.int32, (ATT_SPAN, LANES), 1)
    slabs = slabs if slabs else (None,) * 7

    @pl.when(hh == 0)
    def _():
        st_acc[...] = jnp.zeros_like(st_acc)

    read_q, read_kc, read_kp, read_vc, read_vp = (
        _residue_reader(ref, slab, dilation) for ref, slab in zip((q_ref, kc_ref, kp_ref, vc_ref, vp_ref), slabs[:5]))
    write_o, flush_o = _residue_writer(o_ref, slabs[5], dilation)
    keys, values = {}, {}
    for r in range(dilation):
        keys[-1, r] = read_kp(0, r).astype(BF16)
        values[-1, r] = read_vp(0, r).astype(BF16)
        for b in range(blocks):
            keys[b, r] = read_kc(b, r).astype(BF16)
            values[b, r] = read_vc(b, r).astype(BF16)

    bias_p = bias_ref[:, 0:ATT_SPAN]
    bias_c = bias_ref[:, ATT_SPAN:2 * ATT_SPAN]
    units = [(b, r) for b in range(blocks) for r in range(dilation)]
    qs = [read_q(b, r).astype(BF16) for b, r in units]
    lps = [_bdot_nt(q, keys[b - 1, r]) * scale + bias_p for q, (b, r) in zip(qs, units)]
    lps = [jnp.where(first, NEG_INF, lp) if b == 0 else lp for lp, (b, r) in zip(lps, units)]
    lcs = [_bdot_nt(q, keys[b, r]) * scale + bias_c for q, (b, r) in zip(qs, units)]
    ms = [jnp.max(jnp.maximum(lp, lc), axis=-1, keepdims=True) for lp, lc in zip(lps, lcs)]
    pps = [jnp.exp(lp - m) for lp, m in zip(lps, ms)]
    pcs = [jnp.exp(lc - m) for lc, m in zip(lcs, ms)]
    dens = [jnp.sum(pp + pc, axis=-1, keepdims=True) for pp, pc in zip(pps, pcs)]
    for (b, r), pp, pc, m, den in zip(units, pps, pcs, ms, dens):
        write_o(b, r, (_bdot(pp, values[b - 1, r]) + _bdot(pc, values[b, r])) / den)
        tile = pl.ds((b * dilation + r) * ATT_SPAN, ATT_SPAN)
        st = jnp.where(lane == hh, m, st_acc[tile, :])
        st_acc[tile, :] = jnp.where(lane == ATT_HEADS_PER_GROUP + hh, den, st)
    flush_o()

    @pl.when(hh == ATT_HEADS_PER_GROUP - 1)
    def _():
        write_st, flush_st = _residue_writer(st_ref, slabs[6], dilation)
        for b in range(blocks):
            for r in range(dilation):
                write_st(b, r, st_acc[pl.ds((b * dilation + r) * ATT_SPAN, ATT_SPAN), :])
        flush_st()


def _attention_group(h, bias, group, dilation, rows_per_step=1024):
    s = h.shape[0]
    blk = ATT_SPAN * dilation
    blocks = max(1, rows_per_step // blk)
    heads = ATT_HEADS_PER_GROUP
    rows = blk * blocks

    def spec(col, prev):
        base = (col + group) * heads
        if prev:
            return pl.BlockSpec((blk, ATT_HEAD_DIM), lambda n, hd: (jnp.maximum(n * blocks - 1, 0), base + hd))
        return pl.BlockSpec((rows, ATT_HEAD_DIM), lambda n, hd: (n, base + hd))

    scratch = [pltpu.VMEM((rows, LANES), F32)]
    if dilation > ATT_DIRECT_STRIDE:
        slab = lambda nrows: pltpu.VMEM((ATT_DIRECT_STRIDE, nrows // ATT_DIRECT_STRIDE, LANES), F32)
        scratch += [slab(rows), slab(rows), slab(blk), slab(rows), slab(blk), slab(rows), slab(rows)]
    return pl.pallas_call(
        functools.partial(_attn_kernel, dilation=dilation, blocks=blocks),
        grid=(s // rows, heads),
        in_specs=[spec(COL_Q, False), spec(COL_K, False), spec(COL_K, True), spec(COL_V, False), spec(COL_V, True),
                  pl.BlockSpec((None, ATT_SPAN, 2 * ATT_SPAN), lambda n, hd: (group * heads + hd, 0, 0))],
        out_specs=[pl.BlockSpec((rows, ATT_HEAD_DIM), lambda n, hd: (n, hd)),
                   pl.BlockSpec((rows, LANES), lambda n, hd: (n, 0))],
        out_shape=[jax.ShapeDtypeStruct((s, BR_WIDTH), F32), jax.ShapeDtypeStruct((s, LANES), F32)],
        scratch_shapes=scratch,
        compiler_params=_params(("parallel", "arbitrary"), 32),
        name=f"dil_attn_d{dilation}",
    )(h, h, h, h, h, bias)


def _attn_merge_kernel(o0, o1, o2, s0, s1, s2, gate_ref, y_ref):
    outs = (o0, o1, o2)
    stats = (s0[...], s1[...], s2[...])
    for hh in range(ATT_HEADS_PER_GROUP):
        sl = slice(hh * ATT_HEAD_DIM, (hh + 1) * ATT_HEAD_DIM)
        ms = [st[:, hh:hh + 1] for st in stats]
        dens = [st[:, ATT_HEADS_PER_GROUP + hh:ATT_HEADS_PER_GROUP + hh + 1] for st in stats]
        m_all = jnp.maximum(jnp.maximum(ms[0], ms[1]), ms[2])
        wts = [jnp.exp(m - m_all) * d for m, d in zip(ms, dens)]
        num = wts[0] * outs[0][:, sl] + wts[1] * outs[1][:, sl] + wts[2] * outs[2][:, sl]
        y = num / (wts[0] + wts[1] + wts[2])
        y_ref[:, sl] = (y * _silu(gate_ref[:, sl])).astype(BF16)


def _attn_merge(outs, stats, h, tile=512):
    s = h.shape[0]
    o_spec = pl.BlockSpec((tile, BR_WIDTH), lambda i: (i, 0))
    s_spec = pl.BlockSpec((tile, LANES), lambda i: (i, 0))
    return pl.pallas_call(
        _attn_merge_kernel,
        grid=(s // tile,),
        in_specs=[o_spec] * 3 + [s_spec] * 3 + [pl.BlockSpec((tile, BR_WIDTH), lambda i: (i, COL_B_GATE))],
        out_specs=o_spec,
        out_shape=jax.ShapeDtypeStruct((s, BR_WIDTH), BF16),
        compiler_params=_params(("parallel",), 32),
        name="attn_merge",
    )(*outs, *stats, h)


def _head_sums(x):
    ri = lax.broadcasted_iota(jnp.int32, (PAIR, PAIR), 0)
    ci = lax.broadcasted_iota(jnp.int32, (PAIR, PAIR), 1)
    same_head = jnp.where((ri < RWKV_HEAD) == (ci < RWKV_HEAD), 1.0, 0.0).astype(BF16)
    return jnp.concatenate([_split_dot(x[:, p * PAIR:(p + 1) * PAIR], same_head, 2, 1) for p in range(N_PAIRS)], axis=1)


def _rwkv_prepare(r_ref, k_ref, v_ref, lora_ref, mu_r, mu_k, mu_v, mu_l, w0_ref, wup_ref, a0_ref, aup_ref,
                  kk_ref, ka_ref, rk_ref, carry, carry_l):
    t = r_ref.shape[0]

    def shift_mix(x, mu, prev_row):
        row = lax.broadcasted_iota(jnp.int32, x.shape, 0)
        x_prev = jnp.where(row == 0, prev_row, pltpu.roll(x, 1, 0))
        return x + mu * (x_prev - x)

    r_in, k_in, v_in, l_in = r_ref[...], k_ref[...], v_ref[...], lora_ref[...]
    r = shift_mix(r_in, mu_r[...], carry[0:1, :])
    kx = shift_mix(k_in, mu_k[...], carry[1:2, :])
    vv = shift_mix(v_in, mu_v[...], carry[2:3, :])
    lo = shift_mix(l_in, mu_l[...], carry_l[0:1, :])
    carry[0:1, :] = r_in[t - 1:t, :]
    carry[1:2, :] = k_in[t - 1:t, :]
    carry[2:3, :] = v_in[t - 1:t, :]
    carry_l[0:1, :] = l_in[t - 1:t, :]

    w_log = -_softplus(-(w0_ref[...] + _bdot(jnp.tanh(lo), wup_ref[...]))) - 0.5
    log_decay = -jnp.exp(w_log)
    a_icl = jax.nn.sigmoid(a0_ref[...] + _bdot(lo, aup_ref[...]))

    kk = kx * kk_ref[...]
    kk = kk / jnp.maximum(jnp.sqrt(_head_sums(kk * kk)), 1e-12)
    kc = kx * (1.0 + (a_icl - 1.0) * ka_ref[...])
    bonus = _head_sums(r * kc * rk_ref[...]) * vv
    return log_decay, r, kc, vv, -kk, kk * a_icl, bonus


def _stack_heads(x):
    lane = lax.broadcasted_iota(jnp.int32, x.shape, 1)
    return jnp.concatenate([jnp.where(lane < RWKV_HEAD, x, 0.0), jnp.where(lane >= RWKV_HEAD, x, 0.0)], axis=0)


def _unit_lower_inverse(a_strict):
    n = a_strict[0].shape[0]
    ri = lax.broadcasted_iota(jnp.int32, (n, n), 0)
    ci = lax.broadcasted_iota(jnp.int32, (n, n), 1)

    def same_block(bits):
        return (ri >> bits) == (ci >> bits)

    pw = [jnp.where(same_block(4), a, 0.0) for a in a_strict]
    x = [jnp.where(ri == ci, 1.0, 0.0) + p for p in pw]
    for _ in range(3):
        pw = [_bdot(p, p) for p in pw]
        x = [xi + _bdot(xi, p) for xi, p in zip(x, pw)]
    for bits in (5, 6):
        join = same_block(bits) & jnp.logical_not(same_block(bits - 1))
        xe = [_bdot(xi, jnp.where(join, a, 0.0)) for xi, a in zip(x, a_strict)]
        x = [xi + _bdot(t, xi) for xi, t in zip(x, xe)]
    return x


def _rwkv_chunk_transforms(lw_all, r_all, k_all, v_all, a_all, b_all):
    c = RWKV_CHUNK
    n = 2 * c
    ti = lax.broadcasted_iota(jnp.int32, (c, c), 0)
    si = lax.broadcasted_iota(jnp.int32, (c, c), 1)
    lower_ones = jnp.where(si <= ti, 1.0, 0.0)
    ri = lax.broadcasted_iota(jnp.int32, (n, n), 0)
    ci = lax.broadcasted_iota(jnp.int32, (n, n), 1)
    same_head = (ri >> 6) == (ci >> 6)
    strict = same_head & (ri > ci)
    incl = same_head & (ri >= ci)
    eye = ri == ci

    units = [(ch, p) for ch in range(lw_all.shape[0] // c) for p in range(N_PAIRS)]
    each = lambda f, *cols: [f(*args) for args in zip(*cols)]

    def split(x):
        return [x[ch * c:(ch + 1) * c, p * PAIR:(p + 1) * PAIR] for ch, p in units]

    lw, r, k, v, a, b = (split(x) for x in (lw_all, r_all, k_all, v_all, a_all, b_all))
    cs = each(lambda x: _split_dot(lower_ones, x, 1, 3), lw)
    c_end = each(lambda x: x[c - 1:c, :], cs)
    r_t = each(lambda x, y: _stack_heads(x * jnp.exp(y)), r, cs)
    a_t = each(lambda x, y, z: _stack_heads(x * jnp.exp(y - z)), a, cs, lw)
    b_t = each(lambda x, y: _stack_heads(x * jnp.exp(-y)), b, cs)
    k_t = each(lambda x, y: _stack_heads(x * jnp.exp(-y)), k, cs)
    b_h = each(lambda x, y, e: _stack_heads(x * jnp.exp(e - y)), b, cs, c_end)
    k_h = each(lambda x, y, e: _stack_heads(x * jnp.exp(e - y)), k, cs, c_end)
    v_s = each(_stack_heads, v)

    aa = each(lambda at, rt, bt, kt: _bdot_nt(jnp.concatenate([at, rt], axis=0), jnp.concatenate([bt, kt], axis=0)),
              a_t, r_t, b_t, k_t)
    a_ab = each(lambda x: jnp.where(strict, x[0:n, 0:n], 0.0), aa)
    a_ak = each(lambda x: jnp.where(strict, x[0:n, n:2 * n], 0.0), aa)
    a_rb = each(lambda x: jnp.where(incl, x[n:2 * n, 0:n], 0.0), aa)
    a_rk = each(lambda x: jnp.where(incl, x[n:2 * n, n:2 * n], 0.0), aa)

    minv = _unit_lower_inverse(a_ab)
    w = each(_bdot, minv, a_t)
    uv = each(_bdot, minv, each(_bdot, a_ak, v_s))
    q = each(lambda rt, x, y: rt + _bdot(x, y), r_t, a_rb, w)
    yc = each(lambda x, y, z, t: _bdot(x, y) + _bdot(z, t), a_rb, uv, a_rk, v_s)
    g = each(lambda e, x, y: jnp.where(eye, jnp.exp(e), 0.0) + _bdot_tn(x, y), c_end, w, b_h)
    z = each(lambda u_, v_, bh, kh: _bdot_tn(jnp.concatenate([u_, v_], axis=0), jnp.concatenate([bh, kh], axis=0)),
             uv, v_s, b_h, k_h)
    return {unit: terms for unit, *terms in zip(units, q, yc, g, z)}


def _rwkv_kernel(r_ref, k_ref, v_ref, lora_ref, gate_ref, mu_r, mu_k, mu_v, mu_l, w0_ref, wup_ref, a0_ref, aup_ref,
                 kk_ref, ka_ref, rk_ref, gn_g, gn_b, o_ref, carry, carry_l, state, ybuf):
    c = RWKV_CHUNK
    chunks = r_ref.shape[0] // c

    @pl.when(pl.program_id(0) == 0)
    def _():
        carry[...] = jnp.zeros_like(carry)
        carry_l[...] = jnp.zeros_like(carry_l)
        state[...] = jnp.zeros_like(state)

    *scan_inputs, bonus = _rwkv_prepare(r_ref, k_ref, v_ref, lora_ref, mu_r, mu_k, mu_v, mu_l, w0_ref, wup_ref,
                                        a0_ref, aup_ref, kk_ref, ka_ref, rk_ref, carry, carry_l)
    terms = _rwkv_chunk_transforms(*scan_inputs)

    pairs = range(N_PAIRS)
    sts = [state[:, p * PAIR:(p + 1) * PAIR] for p in pairs]
    starts = []
    for ch in range(chunks):
        starts.append(sts)
        sts = [_split_dot(sts[p], terms[ch, p][2], 2, 2) + terms[ch, p][3] for p in pairs]
    for p in pairs:
        state[:, p * PAIR:(p + 1) * PAIR] = sts[p]
    for ch in range(chunks):
        for p in pairs:
            q, yc = terms[ch, p][0], terms[ch, p][1]
            y_st = _split_dot(q, starts[ch][p], 2, 2, NT_DIMS) + yc
            ybuf[ch * c:(ch + 1) * c, p * PAIR:(p + 1) * PAIR] = y_st[0:c, :] + y_st[c:2 * c, :]

    wy = ybuf[...]
    inv_n = 1.0 / RWKV_HEAD
    mu = _head_sums(wy) * inv_n
    d = wy - mu
    var = _head_sums(d * d) * inv_n
    wy = d * lax.rsqrt(var + RWKV_GN_EPS) * gn_g[...] + gn_b[...]
    o_ref[...] = ((wy + bonus) * _silu(gate_ref[...])).astype(BF16)


def _rwkv(h, mu_r, mu_k, mu_v, mu_l, w0, wup, a0, aup, k_k, k_a, r_k, gn_g, gn_b, chunks=4):
    s = h.shape[0]
    tile = chunks * RWKV_CHUNK
    row = lambda c: pl.BlockSpec((tile, BR_WIDTH), lambda i: (i, c))
    full = lambda shape: pl.BlockSpec(shape, lambda i: (0,) * len(shape))
    vec = full((1, BR_WIDTH))
    return pl.pallas_call(
        _rwkv_kernel,
        grid=(s // tile,),
        in_specs=[row(COL_C_R), row(COL_C_K), row(COL_C_V),
                  pl.BlockSpec((tile, LANES), lambda i: (i, COL_C_LORA * (BR_WIDTH // LANES))), row(COL_C_GATE),
                  vec, vec, vec, full((1, LANES)), vec, full((LANES, BR_WIDTH)), vec, full((LANES, BR_WIDTH)),
                  vec, vec, vec, vec, vec],
        out_specs=pl.BlockSpec((tile, BR_WIDTH), lambda i: (i, 0)),
        out_shape=jax.ShapeDtypeStruct((s, BR_WIDTH), BF16),
        scratch_shapes=[pltpu.VMEM((SUBLANES, BR_WIDTH), F32), pltpu.VMEM((SUBLANES, LANES), F32),
                        pltpu.VMEM((PAIR, BR_WIDTH), F32), pltpu.VMEM((tile, BR_WIDTH), F32)],
        compiler_params=_params(("arbitrary",), 32),
        name="rwkv7",
    )(h, h, h, h, h, mu_r, mu_k, mu_v, mu_l, w0, wup, a0, aup, k_k, k_a, r_k, gn_g, gn_b)


def _mix_kernel(xb_ref, *refs):
    ygs, wms, bms, wbrs = (refs[k * N_BRANCH:(k + 1) * N_BRANCH] for k in range(4))
    o_ref = refs[4 * N_BRANCH]
    xb = xb_ref[...]
    acc = None
    for n in range(N_BRANCH):
        gate = jax.nn.sigmoid(jnp.dot(xb, wms[n][...], preferred_element_type=F32) + bms[n][...])
        val = gate * jnp.dot(ygs[n][...], wbrs[n][...], preferred_element_type=F32)
        acc = val if acc is None else acc + val
    o_ref[...] = acc.astype(BF16)


def _mix(xb, ygs, w_all, layer, bm, wbr, tm=1024, tn=512):
    s = xb.shape[0]
    nj = D_MODEL // tn
    per_branch = lambda make: [make(n) for n in range(N_BRANCH)]
    return pl.pallas_call(
        _mix_kernel,
        grid=(s // tm, nj),
        in_specs=[pl.BlockSpec((tm, D_MODEL), lambda i, j: (i, 0))]
        + per_branch(lambda n: pl.BlockSpec((tm, BR_WIDTH), lambda i, j: (i, 0)))
        + per_branch(lambda n: pl.BlockSpec((pl.Squeezed(), pl.Element(D_MODEL), pl.Element(tn)),
                                            lambda i, j: (layer, 0, ((BRANCH_IN + n * D_MODEL) // LANES
                                                                     + j * (tn // LANES)) * LANES)))
        + per_branch(lambda n: pl.BlockSpec((1, tn), lambda i, j: (0, n * nj + j)))
        + per_branch(lambda n: pl.BlockSpec((None, BR_WIDTH, tn), lambda i, j: (n, 0, j))),
        out_specs=pl.BlockSpec((tm, tn), lambda i, j: (i, j)),
        out_shape=jax.ShapeDtypeStruct((s, D_MODEL), BF16),
        compiler_params=_params(("parallel", "arbitrary"), 48),
        name="branch_mix",
    )(xb, *ygs, *([w_all] * N_BRANCH), *([bm] * N_BRANCH), *([wbr] * N_BRANCH))


def _out_kernel(mixed_ref, x_ref, w_ref, g_ref, b_ref, o_ref):
    y = ALPHA * x_ref[...] + jnp.dot(mixed_ref[...], w_ref[...], preferred_element_type=F32)
    mu = jnp.mean(y, axis=-1, keepdims=True)
    var = jnp.mean(jnp.square(y - mu), axis=-1, keepdims=True)
    o_ref[...] = (y - mu) * lax.rsqrt(var + LN_EPS) * g_ref[...] + b_ref[...]


def _out_proj(mixed, x, w, g, b, tm=512):
    s = x.shape[0]
    row = pl.BlockSpec((tm, D_MODEL), lambda i: (i, 0))
    vec = pl.BlockSpec((1, D_MODEL), lambda i: (0, 0))
    return pl.pallas_call(
        _out_kernel,
        grid=(s // tm,),
        in_specs=[row, row, pl.BlockSpec((D_MODEL, D_MODEL), lambda i: (0, 0)), vec, vec],
        out_specs=row,
        out_shape=jax.ShapeDtypeStruct((s, D_MODEL), F32),
        compiler_params=_params(("parallel",), 48),
        name="out_proj_ln",
    )(mixed, x, w, g, b)


def _block_diag(w):
    blocks, n, _ = w.shape
    eye = jnp.eye(blocks, dtype=w.dtype)
    return (eye[:, None, :, None] * w[:, :, None, :]).reshape(blocks * n, blocks * n)


def _layer(x, att_bias, w_in_bf16, layer, b_in, lru_conv_w, lru_conv_b, lru_gate_a_w, lru_gate_a_b, lru_gate_x_w, lru_gate_x_b,
           lru_lambda, rwkv_mu, rwkv_w0, rwkv_w_up, rwkv_a0, rwkv_a_up, rwkv_k_k, rwkv_k_a, rwkv_r_k, rwkv_gn_g,
           rwkv_gn_b, conf_dw_w, conf_dw_b, conf_ln_g, conf_ln_b, w_br, w_out, ln_g, ln_b):
    vec = lambda t: t.reshape(1, -1)
    b_h = jnp.concatenate([b_in[:H_SPLIT * BR_WIDTH], b_in[C_GATE_START:BRANCH_IN]])
    h, xb = _in_proj(x, w_in_bf16, layer, vec(b_h))

    yg_a = _lru(h, lru_conv_w, vec(lru_conv_b), _block_diag(lru_gate_a_w).astype(BF16), vec(lru_gate_a_b),
                _block_diag(lru_gate_x_w).astype(BF16), vec(lru_gate_x_b), vec(lru_lambda))

    outs, stats = [], []
    for g, (_, dil) in enumerate(ATT_GROUPS):
        o_g, st_g = _attention_group(h, att_bias, g, dil)
        outs.append(o_g)
        stats.append(st_g)
    yg_b = _attn_merge(outs, stats, h)

    mu = rwkv_mu
    zpad = jnp.zeros((DECAY_RANK, BR_WIDTH), F32)
    wup = jnp.concatenate([rwkv_w_up, zpad], axis=0).astype(BF16)
    aup = jnp.concatenate([zpad, rwkv_a_up], axis=0).astype(BF16)
    yg_c = _rwkv(h, vec(mu[:BR_WIDTH]), vec(mu[BR_WIDTH:2 * BR_WIDTH]), vec(mu[2 * BR_WIDTH:3 * BR_WIDTH]),
                 vec(mu[3 * BR_WIDTH:]), vec(rwkv_w0), wup, vec(rwkv_a0), aup, vec(rwkv_k_k), vec(rwkv_k_a),
                 vec(rwkv_r_k), vec(rwkv_gn_g), vec(rwkv_gn_b))

    yg_d = _conformer(h, conf_dw_w, vec(conf_dw_b), vec(conf_ln_g), vec(conf_ln_b))

    mixed = _mix(xb, (yg_a, yg_b, yg_c, yg_d), w_in_bf16, layer, vec(b_in[BRANCH_IN:]), w_br.astype(BF16))
    return _out_proj(mixed, x, w_out.astype(BF16), vec(ln_g), vec(ln_b))


def kernel(x, att_rel_bias, w_in, b_in, lru_conv_w, lru_conv_b, lru_gate_a_w, lru_gate_a_b, lru_gate_x_w, lru_gate_x_b, lru_lambda, rwkv_mu, rwkv_w0, rwkv_w_up, rwkv_a0, rwkv_a_up, rwkv_k_k, rwkv_k_a, rwkv_r_k, rwkv_gn_g, rwkv_gn_b, conf_dw_w, conf_dw_b, conf_ln_g, conf_ln_b, w_br, w_out, ln_g, ln_b):
    bsz, s, d = x.shape
    assert bsz == 1 and d == D_MODEL and s % (16 * ATT_SPAN) == 0
    per_layer = (b_in, lru_conv_w, lru_conv_b, lru_gate_a_w, lru_gate_a_b, lru_gate_x_w, lru_gate_x_b,
                 lru_lambda, rwkv_mu, rwkv_w0, rwkv_w_up, rwkv_a0, rwkv_a_up, rwkv_k_k, rwkv_k_a, rwkv_r_k,
                 rwkv_gn_g, rwkv_gn_b, conf_dw_w, conf_dw_b, conf_ln_g, conf_ln_b, w_br, w_out, ln_g, ln_b)
    y = x.reshape(s, d)
    att_bias = _attn_bias(att_rel_bias)
    w_in_bf16 = w_in.astype(BF16)
    for l in range(DEPTH):
        y = _layer(y, att_bias, w_in_bf16, l, *(t[l] for t in per_layer))
    return y.reshape(bsz, s, d)
```

```python
import functools
import math

import numpy as np
import jax
import jax.numpy as jnp
from jax import lax
from jax.experimental import pallas as pl
from jax.experimental.pallas import tpu as pltpu

D_MODEL = 2048
DEPTH = 2
N_BRANCH = 4
BR_WIDTH = 512
LRU_BLOCKS = 8
LRU_BLOCK = BR_WIDTH // LRU_BLOCKS
LRU_CONV = 4
LRU_C = 8.0
ATT_GROUPS = ((128, 1), (512, 4), (2048, 16))
ATT_HEADS_PER_GROUP = 4
ATT_HEAD_DIM = BR_WIDTH // ATT_HEADS_PER_GROUP
ATT_HEADS = len(ATT_GROUPS) * ATT_HEADS_PER_GROUP
ATT_QKV = ATT_HEADS * ATT_HEAD_DIM
ATT_SPAN = 128
N_BUCKETS = 32
MAX_DISTANCE = 2048
NEG_INF = -1e30
RWKV_HEAD = 64
RWKV_HEADS = BR_WIDTH // RWKV_HEAD
DECAY_RANK = 64
ICLR_RANK = 64
RWKV_GN_EPS = 64e-5
CONF_KERNEL = 31
LN_EPS = 1e-5
ALPHA = (2.0 * DEPTH) ** 0.25

LANES = 128
SUBLANES = 8
MIB = 1024 * 1024

BRANCH_IN = 2 * BR_WIDTH + 3 * ATT_QKV + BR_WIDTH + (4 * BR_WIDTH + DECAY_RANK + ICLR_RANK) + 3 * BR_WIDTH
C_GATE_START = BRANCH_IN - 4 * BR_WIDTH
H_SPLIT = 16
H_BLOCKS = 20
H_WIDTH = H_BLOCKS * BR_WIDTH
COL_A_X, COL_A_GATE = 0, 1
COL_Q, COL_K, COL_V, COL_B_GATE = 2, 5, 8, 11
COL_C_R, COL_C_K, COL_C_V, COL_C_LORA = 12, 13, 14, 15
COL_C_GATE, COL_D_VAL, COL_D_GLU, COL_D_GATE = 16, 17, 18, 19

CONF_HALO = 32
RWKV_CHUNK = 64
PAIR = 2 * RWKV_HEAD
N_PAIRS = BR_WIDTH // PAIR

F32 = jnp.float32
BF16 = jnp.bfloat16


def _params(semantics, vmem_mib):
    return pltpu.CompilerParams(dimension_semantics=semantics, vmem_limit_bytes=vmem_mib * MIB)


def _bdot(a, b):
    return jnp.dot(a.astype(BF16), b.astype(BF16), preferred_element_type=F32)


def _bdot_nt(a, b):
    return lax.dot_general(a.astype(BF16), b.astype(BF16), (((1,), (1,)), ((), ())), preferred_element_type=F32)


def _bdot_tn(a, b):
    return lax.dot_general(a.astype(BF16), b.astype(BF16), (((0,), (0,)), ((), ())), preferred_element_type=F32)


NN_DIMS = (((1,), (0,)), ((), ()))
NT_DIMS = (((1,), (1,)), ((), ()))


def _bf16_parts(x, parts):
    out = []
    for _ in range(parts):
        hi = x.astype(BF16)
        out.append(hi)
        x = x - hi.astype(F32)
    return out


def _split_dot(a, b, a_parts, b_parts, dims=NN_DIMS):
    acc = None
    b_terms = _bf16_parts(b, b_parts)
    for i, ai in enumerate(_bf16_parts(a, a_parts)):
        for j, bj in enumerate(b_terms):
            if i + j < max(a_parts, b_parts):
                term = lax.dot_general(ai, bj, dims, preferred_element_type=F32)
                acc = term if acc is None else acc + term
    return acc


def _softplus(z):
    return jnp.maximum(z, 0.0) + jnp.log1p(jnp.exp(-jnp.abs(z)))


def _expm1_nonpos(z):
    u = jnp.exp(z)
    safe = jnp.where(u == 1.0, 0.5, u)
    return jnp.where(u == 1.0, z, jnp.where(u == 0.0, -1.0, (safe - 1.0) * z / jnp.log(safe)))


def _silu(z):
    return z * jax.nn.sigmoid(z)


def _in_proj_kernel(x_ref, w_ref, b_ref, h_ref, xb_ref):
    @pl.when(pl.program_id(1) == 0)
    def _():
        xb_ref[...] = x_ref[...].astype(BF16)

    h_ref[...] = jnp.dot(xb_ref[...], w_ref[...], preferred_element_type=F32) + b_ref[...]


def _h_source_column(block):
    return block * BR_WIDTH if block < H_SPLIT else C_GATE_START + (block - H_SPLIT) * BR_WIDTH


def _in_proj(x, w_all, layer, b, tm=1024, tn=1024):
    s, k = x.shape
    assert (H_SPLIT * BR_WIDTH) % tn == 0 and tn % BR_WIDTH == 0
    per_tile = tn // BR_WIDTH
    starts = np.array([_h_source_column(j * per_tile) // LANES for j in range(H_WIDTH // tn)], np.int32)
    return pl.pallas_call(
        lambda starts_ref, *refs: _in_proj_kernel(*refs),
        grid_spec=pltpu.PrefetchScalarGridSpec(
            num_scalar_prefetch=1,
            grid=(s // tm, H_WIDTH // tn),
            in_specs=[
                pl.BlockSpec((tm, k), lambda i, j, st: (i, 0)),
                pl.BlockSpec((pl.Squeezed(), pl.Element(k), pl.Element(tn)),
                             lambda i, j, st: (layer, 0, st[j] * LANES)),
                pl.BlockSpec((1, tn), lambda i, j, st: (0, j)),
            ],
            out_specs=[
                pl.BlockSpec((tm, tn), lambda i, j, st: (i, j)),
                pl.BlockSpec((tm, k), lambda i, j, st: (i, 0)),
            ],
        ),
        out_shape=[jax.ShapeDtypeStruct((s, H_WIDTH), F32), jax.ShapeDtypeStruct((s, k), BF16)],
        compiler_params=_params(("parallel", "arbitrary"), 48),
        name="in_proj",
    )(jnp.asarray(starts), x, w_all, b)


LRU_IN, CONF_IN, RWKV_IN = 9, 7, 18
LRU_SCRATCH, CONF_SCRATCH, RWKV_SCRATCH = 2, 2, 4


def _lru_init(ebuf, hc):
    ebuf[0:SUBLANES, :] = jnp.zeros((SUBLANES, BR_WIDTH), F32)
    hc[...] = jnp.zeros_like(hc)


def _lru_body(ax_ref, ag_ref, cw_ref, cb_ref, wa_ref, ba_ref, wx_ref, bx_ref, lam_ref, o_ref, ebuf, hc):
    t = ax_ref.shape[0]
    halo = SUBLANES
    x = ax_ref[...]
    ebuf[halo:halo + t, :] = x
    u = cb_ref[...] + jnp.zeros((t, BR_WIDTH), F32)
    for j in range(LRU_CONV):
        u = u + cw_ref[j:j + 1, :] * ebuf[pl.ds(halo - (LRU_CONV - 1) + j, t), :]
    ebuf[0:halo, :] = x[t - halo:t, :]

    gate_r = jax.nn.sigmoid(_bdot(u, wa_ref[...]) + ba_ref[...])
    gate_i = jax.nn.sigmoid(_bdot(u, wx_ref[...]) + bx_ref[...])
    log_a = -LRU_C * gate_r * _softplus(-lam_ref[...])
    a = jnp.exp(log_a)
    b = jnp.sqrt(-_expm1_nonpos(2.0 * log_a)) * (gate_i * u)

    row = lax.broadcasted_iota(jnp.int32, (t, BR_WIDTH), 0)
    shift = 1
    while shift < t:
        valid = row >= shift
        b = jnp.where(valid, a * pltpu.roll(b, shift, 0), 0.0) + b
        a = jnp.where(valid, a * pltpu.roll(a, shift, 0), a)
        shift *= 2
    h = a * hc[0:1, :] + b
    hc[0:1, :] = h[t - 1:t, :]
    o_ref[...] = (h * _silu(ag_ref[...])).astype(BF16)


def _row_block(tile, col):
    return pl.BlockSpec((tile, BR_WIDTH), lambda i: (i, col))


def _resident(shape):
    return pl.BlockSpec(shape, lambda i: (0,) * len(shape))


def _lru_specs(tile):
    vec, mat = _resident((1, BR_WIDTH)), _resident((BR_WIDTH, BR_WIDTH))
    in_specs = [_row_block(tile, COL_A_X), _row_block(tile, COL_A_GATE), _resident((LRU_CONV, BR_WIDTH)), vec,
                mat, vec, mat, vec, vec]
    return in_specs, [pltpu.VMEM((tile + SUBLANES, BR_WIDTH), F32), pltpu.VMEM((SUBLANES, BR_WIDTH), F32)]


def _conf_init(ebuf, shifted):
    ebuf[0:CONF_HALO, :] = jnp.zeros((CONF_HALO, BR_WIDTH), F32)


def _conf_body(val_ref, glu_ref, gate_ref, w_ref, b_ref, g_ref, beta_ref, o_ref, ebuf, shifted):
    t = val_ref.shape[0]
    halo = CONF_HALO
    cu = val_ref[...] * jax.nn.sigmoid(glu_ref[...])
    ebuf[halo:halo + t, :] = cu
    for b in range(SUBLANES):
        span = t + (CONF_KERNEL - 1 - b) // SUBLANES * SUBLANES
        shifted[b, 0:span, :] = ebuf[pl.ds(halo - (CONF_KERNEL - 1) + b, span), :]
    acc = b_ref[...] + jnp.zeros((t, BR_WIDTH), F32)
    for j in range(CONF_KERNEL):
        b = j % SUBLANES
        acc = acc + w_ref[j:j + 1, :] * shifted[b, j - b:j - b + t, :]
    ebuf[0:halo, :] = cu[t - halo:t, :]

    mu = jnp.mean(acc, axis=-1, keepdims=True)
    var = jnp.mean(jnp.square(acc - mu), axis=-1, keepdims=True)
    ln = (acc - mu) * lax.rsqrt(var + LN_EPS) * g_ref[...] + beta_ref[...]
    o_ref[...] = (_silu(ln) * _silu(gate_ref[...])).astype(BF16)


def _conf_specs(tile):
    vec = _resident((1, BR_WIDTH))
    in_specs = [_row_block(tile, COL_D_VAL), _row_block(tile, COL_D_GLU), _row_block(tile, COL_D_GATE),
                _resident((CONF_KERNEL, BR_WIDTH)), vec, vec, vec]
    return in_specs, [pltpu.VMEM((tile + CONF_HALO, BR_WIDTH), F32),
                      pltpu.VMEM((SUBLANES, tile + CONF_HALO - SUBLANES, BR_WIDTH), F32)]


def _t5_bucket(dist):
    max_exact = N_BUCKETS // 2
    large = max_exact + (np.log(np.maximum(dist, 1) / max_exact) / math.log(MAX_DISTANCE / max_exact)
                         * (N_BUCKETS - max_exact)).astype(np.int32)
    large = np.minimum(large, N_BUCKETS - 1)
    return np.where(dist < max_exact, dist, large).astype(np.int32)


def _bucket_index():
    qi = np.arange(ATT_SPAN)[:, None]
    kj = np.arange(2 * ATT_SPAN)[None, :]
    dist = qi + ATT_SPAN - kj
    valid = (dist >= 0) & (dist <= ATT_SPAN)
    per_group = [np.where(valid, _t5_bucket(np.clip(dist, 0, ATT_SPAN) * dil), -1) for _, dil in ATT_GROUPS]
    return np.stack(per_group).astype(np.int32)


def _bias_kernel(table_ref, bucket_ref, o_ref):
    head = pl.program_id(0)
    bucket = bucket_ref[...]
    acc = jnp.full(bucket.shape, NEG_INF, F32)
    for bkt in range(N_BUCKETS):
        acc = jnp.where(bucket == bkt, table_ref[bkt, head], acc)
    o_ref[...] = acc


def _attn_bias(table):
    blk = (None, ATT_SPAN, 2 * ATT_SPAN)
    return pl.pallas_call(
        _bias_kernel,
        grid=(ATT_HEADS,),
        in_specs=[pl.BlockSpec(memory_space=pltpu.SMEM),
                  pl.BlockSpec(blk, lambda hd: (hd // ATT_HEADS_PER_GROUP, 0, 0))],
        out_specs=pl.BlockSpec(blk, lambda hd: (hd, 0, 0)),
        out_shape=jax.ShapeDtypeStruct((ATT_HEADS, ATT_SPAN, 2 * ATT_SPAN), F32),
        compiler_params=_params(("parallel",), 32),
        name="attn_bias",
    )(table, jnp.asarray(_bucket_index()))


ATT_DIRECT_STRIDE = 4


def _residue_reader(ref, slab, dilation):
    if dilation == 1:
        return lambda b, r: ref[b * ATT_SPAN:(b + 1) * ATT_SPAN, :]
    if dilation <= ATT_DIRECT_STRIDE:
        return lambda b, r: ref[pl.ds(b * ATT_SPAN * dilation + r, ATT_SPAN, stride=dilation), :]
    inner, outer = ATT_DIRECT_STRIDE, dilation // ATT_DIRECT_STRIDE
    per = ref.shape[0] // inner
    for r0 in range(inner):
        slab[r0] = ref[pl.ds(r0, per, stride=inner), :]
    return lambda b, r: slab[r % inner, pl.ds(b * ATT_SPAN * outer + r // inner, ATT_SPAN, stride=outer), :]


def _residue_writer(ref, slab, dilation):
    if dilation == 1:
        def write(b, r, val):
            ref[b * ATT_SPAN:(b + 1) * ATT_SPAN, :] = val
        return write, lambda: None
    if dilation <= ATT_DIRECT_STRIDE:
        def write(b, r, val):
            ref[pl.ds(b * ATT_SPAN * dilation + r, ATT_SPAN, stride=dilation), :] = val
        return write, lambda: None
    inner, outer = ATT_DIRECT_STRIDE, dilation // ATT_DIRECT_STRIDE
    per = ref.shape[0] // inner

    def write(b, r, val):
        slab[r % inner, pl.ds(b * ATT_SPAN * outer + r // inner, ATT_SPAN, stride=outer), :] = val

    def flush():
        for r0 in range(inner):
            ref[pl.ds(r0, per, stride=inner), :] = slab[r0]

    return write, flush


def _attn_kernel(q_ref, kc_ref, kp_ref, vc_ref, vp_ref, bias_ref, o_ref, st_ref, st_acc, *slabs, dilation, blocks):
    first = pl.program_id(0) == 0
    hh = pl.program_id(1)
    scale = ATT_HEAD_DIM ** -0.5
    lane = lax.broadcasted_iota(jnp.int32, (ATT_SPAN, LANES), 1)
    slabs = slabs if slabs else (None,) * 7

    @pl.when(hh == 0)
    def _():
        st_acc[...] = jnp.zeros_like(st_acc)

    read_q, read_kc, read_kp, read_vc, read_vp = (
        _residue_reader(ref, slab, dilation) for ref, slab in zip((q_ref, kc_ref, kp_ref, vc_ref, vp_ref), slabs[:5]))
    write_o, flush_o = _residue_writer(o_ref, slabs[5], dilation)
    keys, values = {}, {}
    for r in range(dilation):
        keys[-1, r] = read_kp(0, r).astype(BF16)
        values[-1, r] = read_vp(0, r).astype(BF16)
        for b in range(blocks):
            keys[b, r] = read_kc(b, r).astype(BF16)
            values[b, r] = read_vc(b, r).astype(BF16)

    bias_p = bias_ref[:, 0:ATT_SPAN]
    bias_c = bias_ref[:, ATT_SPAN:2 * ATT_SPAN]
    units = [(b, r) for b in range(blocks) for r in range(dilation)]
    qs = [read_q(b, r).astype(BF16) for b, r in units]
    lps = [_bdot_nt(q, keys[b - 1, r]) * scale + bias_p for q, (b, r) in zip(qs, units)]
    lps = [jnp.where(first, NEG_INF, lp) if b == 0 else lp for lp, (b, r) in zip(lps, units)]
    lcs = [_bdot_nt(q, keys[b, r]) * scale + bias_c for q, (b, r) in zip(qs, units)]
    ms = [jnp.max(jnp.maximum(lp, lc), axis=-1, keepdims=True) for lp, lc in zip(lps, lcs)]
    pps = [jnp.exp(lp - m) for lp, m in zip(lps, ms)]
    pcs = [jnp.exp(lc - m) for lc, m in zip(lcs, ms)]
    dens = [jnp.sum(pp + pc, axis=-1, keepdims=True) for pp, pc in zip(pps, pcs)]
    for (b, r), pp, pc, m, den in zip(units, pps, pcs, ms, dens):
        write_o(b, r, (_bdot(pp, values[b - 1, r]) + _bdot(pc, values[b, r])) / den)
        tile = pl.ds((b * dilation + r) * ATT_SPAN, ATT_SPAN)
        st = jnp.where(lane == hh, m, st_acc[tile, :])
        st_acc[tile, :] = jnp.where(lane == ATT_HEADS_PER_GROUP + hh, den, st)
    flush_o()

    @pl.when(hh == ATT_HEADS_PER_GROUP - 1)
    def _():
        write_st, flush_st = _residue_writer(st_ref, slabs[6], dilation)
        for b in range(blocks):
            for r in range(dilation):
                write_st(b, r, st_acc[pl.ds((b * dilation + r) * ATT_SPAN, ATT_SPAN), :])
        flush_st()


def _attention_group(h, bias, group, dilation, rows_per_step=1024):
    s = h.shape[0]
    blk = ATT_SPAN * dilation
    blocks = max(1, rows_per_step // blk)
    heads = ATT_HEADS_PER_GROUP
    rows = blk * blocks

    def spec(col, prev):
        base = (col + group) * heads
        if prev:
            return pl.BlockSpec((blk, ATT_HEAD_DIM), lambda n, hd: (jnp.maximum(n * blocks - 1, 0), base + hd))
        return pl.BlockSpec((rows, ATT_HEAD_DIM), lambda n, hd: (n, base + hd))

    scratch = [pltpu.VMEM((rows, LANES), F32)]
    if dilation > ATT_DIRECT_STRIDE:
        slab = lambda nrows: pltpu.VMEM((ATT_DIRECT_STRIDE, nrows // ATT_DIRECT_STRIDE, LANES), F32)
        scratch += [slab(rows), slab(rows), slab(blk), slab(rows), slab(blk), slab(rows), slab(rows)]
    return pl.pallas_call(
        functools.partial(_attn_kernel, dilation=dilation, blocks=blocks),
        grid=(s // rows, heads),
        in_specs=[spec(COL_Q, False), spec(COL_K, False), spec(COL_K, True), spec(COL_V, False), spec(COL_V, True),
                  pl.BlockSpec((None, ATT_SPAN, 2 * ATT_SPAN), lambda n, hd: (group * heads + hd, 0, 0))],
        out_specs=[pl.BlockSpec((rows, ATT_HEAD_DIM), lambda n, hd: (n, hd)),
                   pl.BlockSpec((rows, LANES), lambda n, hd: (n, 0))],
        out_shape=[jax.ShapeDtypeStruct((s, BR_WIDTH), F32), jax.ShapeDtypeStruct((s, LANES), F32)],
        scratch_shapes=scratch,
        compiler_params=_params(("parallel", "arbitrary"), 32),
        name=f"dil_attn_d{dilation}",
    )(h, h, h, h, h, bias)


def _attn_merge_kernel(o0, o1, o2, s0, s1, s2, gate_ref, y_ref):
    outs = (o0, o1, o2)
    stats = (s0[...], s1[...], s2[...])
    for hh in range(ATT_HEADS_PER_GROUP):
        sl = slice(hh * ATT_HEAD_DIM, (hh + 1) * ATT_HEAD_DIM)
        ms = [st[:, hh:hh + 1] for st in stats]
        dens = [st[:, ATT_HEADS_PER_GROUP + hh:ATT_HEADS_PER_GROUP + hh + 1] for st in stats]
        m_all = jnp.maximum(jnp.maximum(ms[0], ms[1]), ms[2])
        wts = [jnp.exp(m - m_all) * d for m, d in zip(ms, dens)]
        num = wts[0] * outs[0][:, sl] + wts[1] * outs[1][:, sl] + wts[2] * outs[2][:, sl]
        y = num / (wts[0] + wts[1] + wts[2])
        y_ref[:, sl] = (y * _silu(gate_ref[:, sl])).astype(BF16)


def _attn_merge(outs, stats, h, tile=512):
    s = h.shape[0]
    o_spec = pl.BlockSpec((tile, BR_WIDTH), lambda i: (i, 0))
    s_spec = pl.BlockSpec((tile, LANES), lambda i: (i, 0))
    return pl.pallas_call(
        _attn_merge_kernel,
        grid=(s // tile,),
        in_specs=[o_spec] * 3 + [s_spec] * 3 + [pl.BlockSpec((tile, BR_WIDTH), lambda i: (i, COL_B_GATE))],
        out_specs=o_spec,
        out_shape=jax.ShapeDtypeStruct((s, BR_WIDTH), BF16),
        compiler_params=_params(("parallel",), 32),
        name="attn_merge",
    )(*outs, *stats, h)


def _head_sums(x):
    ri = lax.broadcasted_iota(jnp.int32, (PAIR, PAIR), 0)
    ci = lax.broadcasted_iota(jnp.int32, (PAIR, PAIR), 1)
    same_head = jnp.where((ri < RWKV_HEAD) == (ci < RWKV_HEAD), 1.0, 0.0).astype(BF16)
    return jnp.concatenate([_split_dot(x[:, p * PAIR:(p + 1) * PAIR], same_head, 2, 1) for p in range(N_PAIRS)], axis=1)


def _rwkv_prepare(r_ref, k_ref, v_ref, lora_ref, mu_r, mu_k, mu_v, mu_l, w0_ref, wup_ref, a0_ref, aup_ref,
                  kk_ref, ka_ref, rk_ref, carry, carry_l):
    t = r_ref.shape[0]

    def shift_mix(x, mu, prev_row):
        row = lax.broadcasted_iota(jnp.int32, x.shape, 0)
        x_prev = jnp.where(row == 0, prev_row, pltpu.roll(x, 1, 0))
        return x + mu * (x_prev - x)

    r_in, k_in, v_in, l_in = r_ref[...], k_ref[...], v_ref[...], lora_ref[...]
    r = shift_mix(r_in, mu_r[...], carry[0:1, :])
    kx = shift_mix(k_in, mu_k[...], carry[1:2, :])
    vv = shift_mix(v_in, mu_v[...], carry[2:3, :])
    lo = shift_mix(l_in, mu_l[...], carry_l[0:1, :])
    carry[0:1, :] = r_in[t - 1:t, :]
    carry[1:2, :] = k_in[t - 1:t, :]
    carry[2:3, :] = v_in[t - 1:t, :]
    carry_l[0:1, :] = l_in[t - 1:t, :]

    w_log = -_softplus(-(w0_ref[...] + _bdot(jnp.tanh(lo), wup_ref[...]))) - 0.5
    log_decay = -jnp.exp(w_log)
    a_icl = jax.nn.sigmoid(a0_ref[...] + _bdot(lo, aup_ref[...]))

    kk = kx * kk_ref[...]
    kk = kk / jnp.maximum(jnp.sqrt(_head_sums(kk * kk)), 1e-12)
    kc = kx * (1.0 + (a_icl - 1.0) * ka_ref[...])
    bonus = _head_sums(r * kc * rk_ref[...]) * vv
    return log_decay, r, kc, vv, -kk, kk * a_icl, bonus


def _stack_heads(x):
    lane = lax.broadcasted_iota(jnp.int32, x.shape, 1)
    return jnp.concatenate([jnp.where(lane < RWKV_HEAD, x, 0.0), jnp.where(lane >= RWKV_HEAD, x, 0.0)], axis=0)


def _time_indices():
    t = lax.broadcasted_iota(jnp.int32, (RWKV_CHUNK, PAIR), 0)
    s = lax.broadcasted_iota(jnp.int32, (RWKV_CHUNK, PAIR), 1) & (RWKV_CHUNK - 1)
    return t, s


def _unit_lower_inverse(a_strict):
    ti, si = _time_indices()

    def same_block(bits):
        return (ti >> bits) == (si >> bits)

    pw = [jnp.where(same_block(4), a, 0.0) for a in a_strict]
    x = [jnp.where(ti == si, 1.0, 0.0) + p for p in pw]
    for _ in range(3):
        pw = [_bdot(p, _stack_heads(p)) for p in pw]
        x = [xi + _bdot(xi, _stack_heads(p)) for xi, p in zip(x, pw)]
    for bits in (5, 6):
        join = same_block(bits) & jnp.logical_not(same_block(bits - 1))
        xe = [_bdot(xi, _stack_heads(jnp.where(join, a, 0.0))) for xi, a in zip(x, a_strict)]
        x = [xi + _bdot(t, _stack_heads(xi)) for xi, t in zip(x, xe)]
    return x


def _rwkv_chunk_transforms(lw_all, r_all, k_all, v_all, a_all, b_all):
    c = RWKV_CHUNK
    n = 2 * c
    ti = lax.broadcasted_iota(jnp.int32, (c, c), 0)
    si = lax.broadcasted_iota(jnp.int32, (c, c), 1)
    lower_ones = jnp.where(si <= ti, 1.0, 0.0)
    tt, ss = _time_indices()
    strict = tt > ss
    incl = tt >= ss
    ri = lax.broadcasted_iota(jnp.int32, (n, n), 0)
    ci = lax.broadcasted_iota(jnp.int32, (n, n), 1)
    same_head = (ri < RWKV_HEAD) == (ci < RWKV_HEAD)
    eye = ri == ci

    units = [(ch, p) for ch in range(lw_all.shape[0] // c) for p in range(N_PAIRS)]
    each = lambda f, *cols: [f(*args) for args in zip(*cols)]

    def split(x):
        return [x[ch * c:(ch + 1) * c, p * PAIR:(p + 1) * PAIR] for ch, p in units]

    lw, r, k, v, a, b = (split(x) for x in (lw_all, r_all, k_all, v_all, a_all, b_all))
    cs = each(lambda x: _split_dot(lower_ones, x, 1, 3), lw)
    c_end = each(lambda x: x[c - 1:c, :], cs)
    r_d = each(lambda x, y: x * jnp.exp(y), r, cs)
    a_d = each(lambda x, y, z: x * jnp.exp(y - z), a, cs, lw)
    b_i = each(lambda x, y: x * jnp.exp(-y), b, cs)
    k_i = each(lambda x, y: x * jnp.exp(-y), k, cs)
    b_e = each(lambda x, y, e: x * jnp.exp(e - y), b, cs, c_end)
    k_e = each(lambda x, y, e: x * jnp.exp(e - y), k, cs, c_end)
    v_s = each(_stack_heads, v)

    aa = each(lambda ad, rd, bi, ki: _bdot_nt(jnp.concatenate([ad, rd], axis=0),
                                              jnp.concatenate([_stack_heads(bi), _stack_heads(ki)], axis=0)),
              a_d, r_d, b_i, k_i)
    a_ab = each(lambda x: jnp.where(strict, x[0:c, 0:n], 0.0), aa)
    a_ak = each(lambda x: jnp.where(strict, x[0:c, n:2 * n], 0.0), aa)
    a_rb = each(lambda x: jnp.where(incl, x[c:n, 0:n], 0.0), aa)
    a_rk = each(lambda x: jnp.where(incl, x[c:n, n:2 * n], 0.0), aa)

    minv = _unit_lower_inverse(a_ab)
    w = each(lambda m, ad: _bdot(m, _stack_heads(ad)), minv, a_d)
    t1 = each(_bdot, a_ak, v_s)
    uv = each(lambda m, x: _bdot(m, _stack_heads(x)), minv, t1)
    q = each(lambda rd, x, y: rd + _bdot(x, _stack_heads(y)), r_d, a_rb, w)
    yc = each(lambda x, y, z, t: _bdot(x, _stack_heads(y)) + _bdot(z, t), a_rb, uv, a_rk, v_s)
    g = each(lambda e, x, y: jnp.where(eye, jnp.exp(e), 0.0) + jnp.where(same_head, _bdot_tn(x, y), 0.0), c_end, w, b_e)
    z = each(lambda u_, v_, be, ke: jnp.where(same_head, _bdot_tn(jnp.concatenate([u_, v_], axis=0),
                                                                    jnp.concatenate([be, ke], axis=0)), 0.0),
             uv, v, b_e, k_e)
    return {unit: terms for unit, *terms in zip(units, q, yc, g, z)}


def _rwkv_init(carry, carry_l, state, ybuf):
    carry[...] = jnp.zeros_like(carry)
    carry_l[...] = jnp.zeros_like(carry_l)
    state[...] = jnp.zeros_like(state)


def _rwkv_body(r_ref, k_ref, v_ref, lora_ref, gate_ref, mu_r, mu_k, mu_v, mu_l, w0_ref, wup_ref, a0_ref, aup_ref,
               kk_ref, ka_ref, rk_ref, gn_g, gn_b, o_ref, carry, carry_l, state, ybuf):
    c = RWKV_CHUNK
    chunks = r_ref.shape[0] // c
    *scan_inputs, bonus = _rwkv_prepare(r_ref, k_ref, v_ref, lora_ref, mu_r, mu_k, mu_v, mu_l, w0_ref, wup_ref,
                                        a0_ref, aup_ref, kk_ref, ka_ref, rk_ref, carry, carry_l)
    terms = _rwkv_chunk_transforms(*scan_inputs)

    pairs = range(N_PAIRS)
    sts = [state[:, p * PAIR:(p + 1) * PAIR] for p in pairs]
    starts = []
    for ch in range(chunks):
        starts.append(sts)
        sts = [_split_dot(sts[p], terms[ch, p][2], 2, 2) + terms[ch, p][3] for p in pairs]
    for p in pairs:
        state[:, p * PAIR:(p + 1) * PAIR] = sts[p]
    for ch in range(chunks):
        for p in pairs:
            q, yc = terms[ch, p][0], terms[ch, p][1]
            ybuf[ch * c:(ch + 1) * c, p * PAIR:(p + 1) * PAIR] = _split_dot(q, starts[ch][p], 2, 2, NT_DIMS) + yc

    wy = ybuf[...]
    inv_n = 1.0 / RWKV_HEAD
    mu = _head_sums(wy) * inv_n
    d = wy - mu
    var = _head_sums(d * d) * inv_n
    wy = d * lax.rsqrt(var + RWKV_GN_EPS) * gn_g[...] + gn_b[...]
    o_ref[...] = ((wy + bonus) * _silu(gate_ref[...])).astype(BF16)


def _rwkv_specs(tile):
    vec, lora_w = _resident((1, BR_WIDTH)), _resident((LANES, BR_WIDTH))
    in_specs = [_row_block(tile, COL_C_R), _row_block(tile, COL_C_K), _row_block(tile, COL_C_V),
                pl.BlockSpec((tile, LANES), lambda i: (i, COL_C_LORA * (BR_WIDTH // LANES))),
                _row_block(tile, COL_C_GATE),
                vec, vec, vec, _resident((1, LANES)), vec, lora_w, vec, lora_w, vec, vec, vec, vec, vec]
    return in_specs, [pltpu.VMEM((SUBLANES, BR_WIDTH), F32), pltpu.VMEM((SUBLANES, LANES), F32),
                      pltpu.VMEM((PAIR, BR_WIDTH), F32), pltpu.VMEM((tile, BR_WIDTH), F32)]


def _recurrent_mixers_kernel(*refs):
    n_in = LRU_IN + CONF_IN + RWKV_IN
    ins, (o_a, o_d, o_c), scratch = refs[:n_in], refs[n_in:n_in + 3], refs[n_in + 3:]
    lru_in, conf_in, rwkv_in = ins[:LRU_IN], ins[LRU_IN:LRU_IN + CONF_IN], ins[LRU_IN + CONF_IN:]
    lru_s = scratch[:LRU_SCRATCH]
    conf_s = scratch[LRU_SCRATCH:LRU_SCRATCH + CONF_SCRATCH]
    rwkv_s = scratch[LRU_SCRATCH + CONF_SCRATCH:]

    @pl.when(pl.program_id(0) == 0)
    def _():
        _lru_init(*lru_s)
        _conf_init(*conf_s)
        _rwkv_init(*rwkv_s)

    _rwkv_body(*rwkv_in, o_c, *rwkv_s)
    _conf_body(*conf_in, o_d, *conf_s)
    _lru_body(*lru_in, o_a, *lru_s)


def _recurrent_mixers(h, lru_args, conf_args, rwkv_args, tile=4 * RWKV_CHUNK):
    s = h.shape[0]
    (lru_specs, lru_scr), (conf_specs, conf_scr), (rwkv_specs, rwkv_scr) = _lru_specs(tile), _conf_specs(tile), _rwkv_specs(tile)
    assert (len(lru_specs), len(conf_specs), len(rwkv_specs)) == (LRU_IN, CONF_IN, RWKV_IN)
    out = pl.BlockSpec((tile, BR_WIDTH), lambda i: (i, 0))
    return pl.pallas_call(
        _recurrent_mixers_kernel,
        grid=(s // tile,),
        in_specs=lru_specs + conf_specs + rwkv_specs,
        out_specs=[out] * 3,
        out_shape=[jax.ShapeDtypeStruct((s, BR_WIDTH), BF16)] * 3,
        scratch_shapes=lru_scr + conf_scr + rwkv_scr,
        compiler_params=_params(("arbitrary",), 40),
        name="recurrent_mixers",
    )(h, h, *lru_args, h, h, h, *conf_args, h, h, h, h, h, *rwkv_args)


def _mix_kernel(xb_ref, *refs):
    ygs, wms, bms, wbrs = (refs[k * N_BRANCH:(k + 1) * N_BRANCH] for k in range(4))
    o_ref = refs[4 * N_BRANCH]
    xb = xb_ref[...]
    acc = None
    for n in range(N_BRANCH):
        gate = jax.nn.sigmoid(jnp.dot(xb, wms[n][...], preferred_element_type=F32) + bms[n][...])
        val = gate * jnp.dot(ygs[n][...], wbrs[n][...], preferred_element_type=F32)
        acc = val if acc is None else acc + val
    o_ref[...] = acc.astype(BF16)


def _mix(xb, ygs, w_all, layer, bm, wbr, tm=1024, tn=512):
    s = xb.shape[0]
    nj = D_MODEL // tn
    per_branch = lambda make: [make(n) for n in range(N_BRANCH)]
    return pl.pallas_call(
        _mix_kernel,
        grid=(s // tm, nj),
        in_specs=[pl.BlockSpec((tm, D_MODEL), lambda i, j: (i, 0))]
        + per_branch(lambda n: pl.BlockSpec((tm, BR_WIDTH), lambda i, j: (i, 0)))
        + per_branch(lambda n: pl.BlockSpec((pl.Squeezed(), pl.Element(D_MODEL), pl.Element(tn)),
                                            lambda i, j: (layer, 0, ((BRANCH_IN + n * D_MODEL) // LANES
                                                                     + j * (tn // LANES)) * LANES)))
        + per_branch(lambda n: pl.BlockSpec((1, tn), lambda i, j: (0, n * nj + j)))
        + per_branch(lambda n: pl.BlockSpec((None, BR_WIDTH, tn), lambda i, j: (n, 0, j))),
        out_specs=pl.BlockSpec((tm, tn), lambda i, j: (i, j)),
        out_shape=jax.ShapeDtypeStruct((s, D_MODEL), BF16),
        compiler_params=_params(("parallel", "arbitrary"), 48),
        name="branch_mix",
    )(xb, *ygs, *([w_all] * N_BRANCH), *([bm] * N_BRANCH), *([wbr] * N_BRANCH))


def _out_kernel(mixed_ref, x_ref, w_ref, g_ref, b_ref, o_ref):
    y = ALPHA * x_ref[...] + jnp.dot(mixed_ref[...], w_ref[...], preferred_element_type=F32)
    mu = jnp.mean(y, axis=-1, keepdims=True)
    var = jnp.mean(jnp.square(y - mu), axis=-1, keepdims=True)
    o_ref[...] = (y - mu) * lax.rsqrt(var + LN_EPS) * g_ref[...] + b_ref[...]


def _out_proj(mixed, x, w, g, b, tm=512):
    s = x.shape[0]
    row = pl.BlockSpec((tm, D_MODEL), lambda i: (i, 0))
    vec = pl.BlockSpec((1, D_MODEL), lambda i: (0, 0))
    return pl.pallas_call(
        _out_kernel,
        grid=(s // tm,),
        in_specs=[row, row, pl.BlockSpec((D_MODEL, D_MODEL), lambda i: (0, 0)), vec, vec],
        out_specs=row,
        out_shape=jax.ShapeDtypeStruct((s, D_MODEL), F32),
        compiler_params=_params(("parallel",), 48),
        name="out_proj_ln",
    )(mixed, x, w, g, b)


def _block_diag(w):
    blocks, n, _ = w.shape
    eye = jnp.eye(blocks, dtype=w.dtype)
    return (eye[:, None, :, None] * w[:, :, None, :]).reshape(blocks * n, blocks * n)


def _layer(x, att_bias, w_in_bf16, layer, b_in, lru_conv_w, lru_conv_b, lru_gate_a_w, lru_gate_a_b, lru_gate_x_w, lru_gate_x_b,
           lru_lambda, rwkv_mu, rwkv_w0, rwkv_w_up, rwkv_a0, rwkv_a_up, rwkv_k_k, rwkv_k_a, rwkv_r_k, rwkv_gn_g,
           rwkv_gn_b, conf_dw_w, conf_dw_b, conf_ln_g, conf_ln_b, w_br, w_out, ln_g, ln_b):
    vec = lambda t: t.reshape(1, -1)
    b_h = jnp.concatenate([b_in[:H_SPLIT * BR_WIDTH], b_in[C_GATE_START:BRANCH_IN]])
    h, xb = _in_proj(x, w_in_bf16, layer, vec(b_h))

    outs, stats = [], []
    for g, (_, dil) in enumerate(ATT_GROUPS):
        o_g, st_g = _attention_group(h, att_bias, g, dil)
        outs.append(o_g)
        stats.append(st_g)
    yg_b = _attn_merge(outs, stats, h)

    mu = rwkv_mu
    zpad = jnp.zeros((DECAY_RANK, BR_WIDTH), F32)
    wup = jnp.concatenate([rwkv_w_up, zpad], axis=0).astype(BF16)
    aup = jnp.concatenate([zpad, rwkv_a_up], axis=0).astype(BF16)
    lru_args = (lru_conv_w, vec(lru_conv_b), _block_diag(lru_gate_a_w).astype(BF16), vec(lru_gate_a_b),
                _block_diag(lru_gate_x_w).astype(BF16), vec(lru_gate_x_b), vec(lru_lambda))
    conf_args = (conf_dw_w, vec(conf_dw_b), vec(conf_ln_g), vec(conf_ln_b))
    rwkv_args = (vec(mu[:BR_WIDTH]), vec(mu[BR_WIDTH:2 * BR_WIDTH]), vec(mu[2 * BR_WIDTH:3 * BR_WIDTH]),
                 vec(mu[3 * BR_WIDTH:]), vec(rwkv_w0), wup, vec(rwkv_a0), aup, vec(rwkv_k_k), vec(rwkv_k_a),
                 vec(rwkv_r_k), vec(rwkv_gn_g), vec(rwkv_gn_b))
    yg_a, yg_d, yg_c = _recurrent_mixers(h, lru_args, conf_args, rwkv_args)

    mixed = _mix(xb, (yg_a, yg_b, yg_c, yg_d), w_in_bf16, layer, vec(b_in[BRANCH_IN:]), w_br.astype(BF16))
    return _out_proj(mixed, x, w_out.astype(BF16), vec(ln_g), vec(ln_b))


def kernel(x, att_rel_bias, w_in, b_in, lru_conv_w, lru_conv_b, lru_gate_a_w, lru_gate_a_b, lru_gate_x_w, lru_gate_x_b, lru_lambda, rwkv_mu, rwkv_w0, rwkv_w_up, rwkv_a0, rwkv_a_up, rwkv_k_k, rwkv_k_a, rwkv_r_k, rwkv_gn_g, rwkv_gn_b, conf_dw_w, conf_dw_b, conf_ln_g, conf_ln_b, w_br, w_out, ln_g, ln_b):
    bsz, s, d = x.shape
    assert bsz == 1 and d == D_MODEL and s % (16 * ATT_SPAN) == 0
    per_layer = (b_in, lru_conv_w, lru_conv_b, lru_gate_a_w, lru_gate_a_b, lru_gate_x_w, lru_gate_x_b,
                 lru_lambda, rwkv_mu, rwkv_w0, rwkv_w_up, rwkv_a0, rwkv_a_up, rwkv_k_k, rwkv_k_a, rwkv_r_k,
                 rwkv_gn_g, rwkv_gn_b, conf_dw_w, conf_dw_b, conf_ln_g, conf_ln_b, w_br, w_out, ln_g, ln_b)
    y = x.reshape(s, d)
    att_bias = _attn_bias(att_rel_bias)
    w_in_bf16 = w_in.astype(BF16)
    for l in range(DEPTH):
        y = _layer(y, att_bias, w_in_bf16, l, *(t[l] for t in per_layer))
    return y.reshape(bsz, s, d)
```

```python
import functools
import math

import numpy as np
import jax
import jax.numpy as jnp
from jax import lax
from jax.experimental import pallas as pl
from jax.experimental.pallas import tpu as pltpu

D_MODEL = 2048
DEPTH = 2
N_BRANCH = 4
BR_WIDTH = 512
LRU_BLOCKS = 8
LRU_BLOCK = BR_WIDTH // LRU_BLOCKS
LRU_CONV = 4
LRU_C = 8.0
ATT_GROUPS = ((128, 1), (512, 4), (2048, 16))
ATT_HEADS_PER_GROUP = 4
ATT_HEAD_DIM = BR_WIDTH // ATT_HEADS_PER_GROUP
ATT_HEADS = len(ATT_GROUPS) * ATT_HEADS_PER_GROUP
ATT_QKV = ATT_HEADS * ATT_HEAD_DIM
ATT_SPAN = 128
N_BUCKETS = 32
MAX_DISTANCE = 2048
NEG_INF = -1e30
RWKV_HEAD = 64
RWKV_HEADS = BR_WIDTH // RWKV_HEAD
DECAY_RANK = 64
ICLR_RANK = 64
RWKV_GN_EPS = 64e-5
CONF_KERNEL = 31
LN_EPS = 1e-5
ALPHA = (2.0 * DEPTH) ** 0.25

LANES = 128
SUBLANES = 8
MIB = 1024 * 1024

BRANCH_IN = 2 * BR_WIDTH + 3 * ATT_QKV + BR_WIDTH + (4 * BR_WIDTH + DECAY_RANK + ICLR_RANK) + 3 * BR_WIDTH
C_GATE_START = BRANCH_IN - 4 * BR_WIDTH
H_SPLIT = 16
H_BLOCKS = 20
H_WIDTH = H_BLOCKS * BR_WIDTH
COL_A_X, COL_A_GATE = 0, 1
COL_Q, COL_K, COL_V, COL_B_GATE = 2, 5, 8, 11
COL_C_R, COL_C_K, COL_C_V, COL_C_LORA = 12, 13, 14, 15
COL_C_GATE, COL_D_VAL, COL_D_GLU, COL_D_GATE = 16, 17, 18, 19

CONF_HALO = 32
RWKV_CHUNK = 64
PAIR = 2 * RWKV_HEAD
N_PAIRS = BR_WIDTH // PAIR

F32 = jnp.float32
BF16 = jnp.bfloat16


def _params(semantics, vmem_mib):
    return pltpu.CompilerParams(dimension_semantics=semantics, vmem_limit_bytes=vmem_mib * MIB)


def _bdot(a, b):
    return jnp.dot(a.astype(BF16), b.astype(BF16), preferred_element_type=F32)


def _bdot_nt(a, b):
    return lax.dot_general(a.astype(BF16), b.astype(BF16), (((1,), (1,)), ((), ())), preferred_element_type=F32)


def _bdot_tn(a, b):
    return lax.dot_general(a.astype(BF16), b.astype(BF16), (((0,), (0,)), ((), ())), preferred_element_type=F32)


NN_DIMS = (((1,), (0,)), ((), ()))
NT_DIMS = (((1,), (1,)), ((), ()))


def _bf16_parts(x, parts):
    out = []
    for _ in range(parts):
        hi = x.astype(BF16)
        out.append(hi)
        x = x - hi.astype(F32)
    return out


def _split_dot(a, b, a_parts, b_parts, dims=NN_DIMS):
    acc = None
    b_terms = _bf16_parts(b, b_parts)
    for i, ai in enumerate(_bf16_parts(a, a_parts)):
        for j, bj in enumerate(b_terms):
            if i + j < max(a_parts, b_parts):
                term = lax.dot_general(ai, bj, dims, preferred_element_type=F32)
                acc = term if acc is None else acc + term
    return acc


def _softplus(z):
    return jnp.maximum(z, 0.0) + jnp.log1p(jnp.exp(-jnp.abs(z)))


def _expm1_nonpos(z):
    u = jnp.exp(z)
    safe = jnp.where(u == 1.0, 0.5, u)
    return jnp.where(u == 1.0, z, jnp.where(u == 0.0, -1.0, (safe - 1.0) * z / jnp.log(safe)))


def _silu(z):
    return z * jax.nn.sigmoid(z)


def _in_proj_kernel(x_ref, w_ref, b_ref, h_ref, xb_ref):
    @pl.when(pl.program_id(1) == 0)
    def _():
        xb_ref[...] = x_ref[...].astype(BF16)

    h_ref[...] = jnp.dot(xb_ref[...], w_ref[...], preferred_element_type=F32) + b_ref[...]


def _h_source_column(block):
    return block * BR_WIDTH if block < H_SPLIT else C_GATE_START + (block - H_SPLIT) * BR_WIDTH


def _in_proj(x, w_all, layer, b, tm=1024, tn=1024):
    s, k = x.shape
    assert (H_SPLIT * BR_WIDTH) % tn == 0 and tn % BR_WIDTH == 0
    per_tile = tn // BR_WIDTH
    starts = np.array([_h_source_column(j * per_tile) // LANES for j in range(H_WIDTH // tn)], np.int32)
    return pl.pallas_call(
        lambda starts_ref, *refs: _in_proj_kernel(*refs),
        grid_spec=pltpu.PrefetchScalarGridSpec(
            num_scalar_prefetch=1,
            grid=(s // tm, H_WIDTH // tn),
            in_specs=[
                pl.BlockSpec((tm, k), lambda i, j, st: (i, 0)),
                pl.BlockSpec((pl.Squeezed(), pl.Element(k), pl.Element(tn)),
                             lambda i, j, st: (layer, 0, st[j] * LANES)),
                pl.BlockSpec((1, tn), lambda i, j, st: (0, j)),
            ],
            out_specs=[
                pl.BlockSpec((tm, tn), lambda i, j, st: (i, j)),
                pl.BlockSpec((tm, k), lambda i, j, st: (i, 0)),
            ],
        ),
        out_shape=[jax.ShapeDtypeStruct((s, H_WIDTH), F32), jax.ShapeDtypeStruct((s, k), BF16)],
        compiler_params=_params(("parallel", "arbitrary"), 48),
        name="in_proj",
    )(jnp.asarray(starts), x, w_all, b)


LRU_IN, CONF_IN, RWKV_IN = 9, 7, 18
LRU_SCRATCH, CONF_SCRATCH, RWKV_SCRATCH = 2, 2, 4


def _lru_init(ebuf, hc):
    ebuf[0:SUBLANES, :] = jnp.zeros((SUBLANES, BR_WIDTH), F32)
    hc[...] = jnp.zeros_like(hc)


def _lru_body(ax_ref, ag_ref, cw_ref, cb_ref, wa_ref, ba_ref, wx_ref, bx_ref, lam_ref, o_ref, ebuf, hc):
    t = ax_ref.shape[0]
    halo = SUBLANES
    x = ax_ref[...]
    ebuf[halo:halo + t, :] = x
    u = cb_ref[...] + jnp.zeros((t, BR_WIDTH), F32)
    for j in range(LRU_CONV):
        u = u + cw_ref[j:j + 1, :] * ebuf[pl.ds(halo - (LRU_CONV - 1) + j, t), :]
    ebuf[0:halo, :] = x[t - halo:t, :]

    gate_r = jax.nn.sigmoid(_bdot(u, wa_ref[...]) + ba_ref[...])
    gate_i = jax.nn.sigmoid(_bdot(u, wx_ref[...]) + bx_ref[...])
    log_a = -LRU_C * gate_r * _softplus(-lam_ref[...])
    a = jnp.exp(log_a)
    b = jnp.sqrt(-_expm1_nonpos(2.0 * log_a)) * (gate_i * u)

    row = lax.broadcasted_iota(jnp.int32, (t, BR_WIDTH), 0)
    shift = 1
    while shift < t:
        valid = row >= shift
        b = jnp.where(valid, a * pltpu.roll(b, shift, 0), 0.0) + b
        a = jnp.where(valid, a * pltpu.roll(a, shift, 0), a)
        shift *= 2
    h = a * hc[0:1, :] + b
    hc[0:1, :] = h[t - 1:t, :]
    o_ref[...] = (h * _silu(ag_ref[...])).astype(BF16)


def _row_block(tile, col):
    return pl.BlockSpec((tile, BR_WIDTH), lambda i: (i, col))


def _resident(shape):
    return pl.BlockSpec(shape, lambda i: (0,) * len(shape))


def _lru_specs(tile):
    vec, mat = _resident((1, BR_WIDTH)), _resident((BR_WIDTH, BR_WIDTH))
    in_specs = [_row_block(tile, COL_A_X), _row_block(tile, COL_A_GATE), _resident((LRU_CONV, BR_WIDTH)), vec,
                mat, vec, mat, vec, vec]
    return in_specs, [pltpu.VMEM((tile + SUBLANES, BR_WIDTH), F32), pltpu.VMEM((SUBLANES, BR_WIDTH), F32)]


def _conf_init(ebuf, shifted):
    ebuf[0:CONF_HALO, :] = jnp.zeros((CONF_HALO, BR_WIDTH), F32)


def _conf_body(val_ref, glu_ref, gate_ref, w_ref, b_ref, g_ref, beta_ref, o_ref, ebuf, shifted):
    t = val_ref.shape[0]
    halo = CONF_HALO
    cu = val_ref[...] * jax.nn.sigmoid(glu_ref[...])
    ebuf[halo:halo + t, :] = cu
    for b in range(SUBLANES):
        span = t + (CONF_KERNEL - 1 - b) // SUBLANES * SUBLANES
        shifted[b, 0:span, :] = ebuf[pl.ds(halo - (CONF_KERNEL - 1) + b, span), :]
    acc = b_ref[...] + jnp.zeros((t, BR_WIDTH), F32)
    for j in range(CONF_KERNEL):
        b = j % SUBLANES
        acc = acc + w_ref[j:j + 1, :] * shifted[b, j - b:j - b + t, :]
    ebuf[0:halo, :] = cu[t - halo:t, :]

    mu = jnp.mean(acc, axis=-1, keepdims=True)
    var = jnp.mean(jnp.square(acc - mu), axis=-1, keepdims=True)
    ln = (acc - mu) * lax.rsqrt(var + LN_EPS) * g_ref[...] + beta_ref[...]
    o_ref[...] = (_silu(ln) * _silu(gate_ref[...])).astype(BF16)


def _conf_specs(tile):
    vec = _resident((1, BR_WIDTH))
    in_specs = [_row_block(tile, COL_D_VAL), _row_block(tile, COL_D_GLU), _row_block(tile, COL_D_GATE),
                _resident((CONF_KERNEL, BR_WIDTH)), vec, vec, vec]
    return in_specs, [pltpu.VMEM((tile + CONF_HALO, BR_WIDTH), F32),
                      pltpu.VMEM((SUBLANES, tile + CONF_HALO - SUBLANES, BR_WIDTH), F32)]


def _t5_bucket(dist):
    max_exact = N_BUCKETS // 2
    large = max_exact + (np.log(np.maximum(dist, 1) / max_exact) / math.log(MAX_DISTANCE / max_exact)
                         * (N_BUCKETS - max_exact)).astype(np.int32)
    large = np.minimum(large, N_BUCKETS - 1)
    return np.where(dist < max_exact, dist, large).astype(np.int32)


def _bucket_index():
    qi = np.arange(ATT_SPAN)[:, None]
    kj = np.arange(2 * ATT_SPAN)[None, :]
    dist = qi + ATT_SPAN - kj
    valid = (dist >= 0) & (dist <= ATT_SPAN)
    per_group = [np.where(valid, _t5_bucket(np.clip(dist, 0, ATT_SPAN) * dil), -1) for _, dil in ATT_GROUPS]
    return np.stack(per_group).astype(np.int32)


def _bias_kernel(table_ref, bucket_ref, o_ref):
    head = pl.program_id(0)
    bucket = bucket_ref[...]
    acc = jnp.full(bucket.shape, NEG_INF, F32)
    for bkt in range(N_BUCKETS):
        acc = jnp.where(bucket == bkt, table_ref[bkt, head], acc)
    o_ref[...] = acc


def _attn_bias(table):
    blk = (None, ATT_SPAN, 2 * ATT_SPAN)
    return pl.pallas_call(
        _bias_kernel,
        grid=(ATT_HEADS,),
        in_specs=[pl.BlockSpec(memory_space=pltpu.SMEM),
                  pl.BlockSpec(blk, lambda hd: (hd // ATT_HEADS_PER_GROUP, 0, 0))],
        out_specs=pl.BlockSpec(blk, lambda hd: (hd, 0, 0)),
        out_shape=jax.ShapeDtypeStruct((ATT_HEADS, ATT_SPAN, 2 * ATT_SPAN), F32),
        compiler_params=_params(("parallel",), 32),
        name="attn_bias",
    )(table, jnp.asarray(_bucket_index()))


ATT_DIRECT_STRIDE = 4


def _residue_reader(ref, slab, dilation):
    if dilation == 1:
        return lambda b, r: ref[b * ATT_SPAN:(b + 1) * ATT_SPAN, :]
    if dilation <= ATT_DIRECT_STRIDE:
        return lambda b, r: ref[pl.ds(b * ATT_SPAN * dilation + r, ATT_SPAN, stride=dilation), :]
    inner, outer = ATT_DIRECT_STRIDE, dilation // ATT_DIRECT_STRIDE
    per = ref.shape[0] // inner
    for r0 in range(inner):
        slab[r0] = ref[pl.ds(r0, per, stride=inner), :]
    return lambda b, r: slab[r % inner, pl.ds(b * ATT_SPAN * outer + r // inner, ATT_SPAN, stride=outer), :]


def _residue_writer(ref, slab, dilation):
    if dilation == 1:
        def write(b, r, val):
            ref[b * ATT_SPAN:(b + 1) * ATT_SPAN, :] = val
        return write, lambda: None
    if dilation <= ATT_DIRECT_STRIDE:
        def write(b, r, val):
            ref[pl.ds(b * ATT_SPAN * dilation + r, ATT_SPAN, stride=dilation), :] = val
        return write, lambda: None
    inner, outer = ATT_DIRECT_STRIDE, dilation // ATT_DIRECT_STRIDE
    per = ref.shape[0] // inner

    def write(b, r, val):
        slab[r % inner, pl.ds(b * ATT_SPAN * outer + r // inner, ATT_SPAN, stride=outer), :] = val

    def flush():
        for r0 in range(inner):
            ref[pl.ds(r0, per, stride=inner), :] = slab[r0]

    return write, flush


ATT_ROWS = ATT_SPAN * max(dil for _, dil in ATT_GROUPS)
ATT_GROUP_IN = 6
ATT_SLABS = 8


def _attn_group_outputs(q_ref, kc_ref, kp_ref, vc_ref, vp_ref, bias_ref, o_nat, m_nat, d_nat, slabs, first, dilation):
    blocks = ATT_ROWS // (ATT_SPAN * dilation)
    scale = ATT_HEAD_DIM ** -0.5
    read_q, read_kc, read_kp, read_vc, read_vp = (
        _residue_reader(ref, slab, dilation) for ref, slab in zip((q_ref, kc_ref, kp_ref, vc_ref, vp_ref), slabs[:5]))
    (write_o, flush_o), (write_m, flush_m), (write_d, flush_d) = (
        _residue_writer(ref, slab, dilation) for ref, slab in zip((o_nat, m_nat, d_nat), slabs[5:]))
    keys, values = {}, {}
    for r in range(dilation):
        keys[-1, r] = read_kp(0, r).astype(BF16)
        values[-1, r] = read_vp(0, r).astype(BF16)
        for b in range(blocks):
            keys[b, r] = read_kc(b, r).astype(BF16)
            values[b, r] = read_vc(b, r).astype(BF16)

    bias_p = bias_ref[:, 0:ATT_SPAN]
    bias_c = bias_ref[:, ATT_SPAN:2 * ATT_SPAN]
    units = [(b, r) for b in range(blocks) for r in range(dilation)]
    qs = [read_q(b, r).astype(BF16) for b, r in units]
    lps = [_bdot_nt(q, keys[b - 1, r]) * scale + bias_p for q, (b, r) in zip(qs, units)]
    lps = [jnp.where(first, NEG_INF, lp) if b == 0 else lp for lp, (b, r) in zip(lps, units)]
    lcs = [_bdot_nt(q, keys[b, r]) * scale + bias_c for q, (b, r) in zip(qs, units)]
    ms = [jnp.max(jnp.maximum(lp, lc), axis=-1, keepdims=True) for lp, lc in zip(lps, lcs)]
    pps = [jnp.exp(lp - m) for lp, m in zip(lps, ms)]
    pcs = [jnp.exp(lc - m) for lc, m in zip(lcs, ms)]
    dens = [jnp.sum(pp + pc, axis=-1, keepdims=True) for pp, pc in zip(pps, pcs)]
    full = (ATT_SPAN, LANES)
    for (b, r), pp, pc, m, den in zip(units, pps, pcs, ms, dens):
        write_o(b, r, (_bdot(pp, values[b - 1, r]) + _bdot(pc, values[b, r])) / den)
        write_m(b, r, jnp.broadcast_to(m, full))
        write_d(b, r, jnp.broadcast_to(den, full))
    flush_o()
    flush_m()
    flush_d()


def _attention_kernel(*refs):
    n_in = ATT_GROUP_IN * len(ATT_GROUPS)
    gate_ref, y_ref = refs[n_in], refs[n_in + 1]
    o_nat, m_nat, d_nat = refs[n_in + 2:n_in + 5]
    slabs = refs[n_in + 5:]
    first = pl.program_id(0) == 0
    for g, (_, dil) in enumerate(ATT_GROUPS):
        _attn_group_outputs(*refs[ATT_GROUP_IN * g:ATT_GROUP_IN * (g + 1)], o_nat.at[g], m_nat.at[g], d_nat.at[g],
                            slabs, first, dil)
    ms = [m_nat[g] for g in range(len(ATT_GROUPS))]
    m_all = functools.reduce(jnp.maximum, ms)
    wts = [jnp.exp(m - m_all) * d_nat[g] for g, m in enumerate(ms)]
    num = sum(w * o_nat[g] for g, w in enumerate(wts))
    y_ref[...] = (num / sum(wts) * _silu(gate_ref[...])).astype(BF16)


def _attention(h, bias):
    s = h.shape[0]
    heads = ATT_HEADS_PER_GROUP
    in_specs, operands = [], []
    for g, (_, dil) in enumerate(ATT_GROUPS):
        blk = ATT_SPAN * dil
        per_step = ATT_ROWS // blk

        def spec(col, prev, g=g, blk=blk, per_step=per_step):
            base = (col + g) * heads
            if prev:
                return pl.BlockSpec((blk, ATT_HEAD_DIM), lambda n, hd: (jnp.maximum(n * per_step - 1, 0), base + hd))
            return pl.BlockSpec((ATT_ROWS, ATT_HEAD_DIM), lambda n, hd: (n, base + hd))

        in_specs += [spec(COL_Q, False), spec(COL_K, False), spec(COL_K, True), spec(COL_V, False), spec(COL_V, True),
                     pl.BlockSpec((None, ATT_SPAN, 2 * ATT_SPAN), lambda n, hd, g=g: (g * heads + hd, 0, 0))]
        operands += [h, h, h, h, h, bias]
    in_specs.append(pl.BlockSpec((ATT_ROWS, ATT_HEAD_DIM), lambda n, hd: (n, COL_B_GATE * heads + hd)))
    token_order = pltpu.VMEM((len(ATT_GROUPS), ATT_ROWS, LANES), F32)
    slab = pltpu.VMEM((ATT_DIRECT_STRIDE, ATT_ROWS // ATT_DIRECT_STRIDE, LANES), F32)
    return pl.pallas_call(
        _attention_kernel,
        grid=(s // ATT_ROWS, heads),
        in_specs=in_specs,
        out_specs=pl.BlockSpec((ATT_ROWS, ATT_HEAD_DIM), lambda n, hd: (n, hd)),
        out_shape=jax.ShapeDtypeStruct((s, BR_WIDTH), BF16),
        scratch_shapes=[token_order] * 3 + [slab] * ATT_SLABS,
        compiler_params=_params(("parallel", "arbitrary"), 56),
        name="dilated_attention",
    )(*operands, h)


def _head_sums(x):
    ri = lax.broadcasted_iota(jnp.int32, (PAIR, PAIR), 0)
    ci = lax.broadcasted_iota(jnp.int32, (PAIR, PAIR), 1)
    same_head = jnp.where((ri < RWKV_HEAD) == (ci < RWKV_HEAD), 1.0, 0.0).astype(BF16)
    return jnp.concatenate([_split_dot(x[:, p * PAIR:(p + 1) * PAIR], same_head, 2, 1) for p in range(N_PAIRS)], axis=1)


def _rwkv_prepare(r_ref, k_ref, v_ref, lora_ref, mu_r, mu_k, mu_v, mu_l, w0_ref, wup_ref, a0_ref, aup_ref,
                  kk_ref, ka_ref, rk_ref, carry, carry_l):
    t = r_ref.shape[0]

    def shift_mix(x, mu, prev_row):
        row = lax.broadcasted_iota(jnp.int32, x.shape, 0)
        x_prev = jnp.where(row == 0, prev_row, pltpu.roll(x, 1, 0))
        return x + mu * (x_prev - x)

    r_in, k_in, v_in, l_in = r_ref[...], k_ref[...], v_ref[...], lora_ref[...]
    r = shift_mix(r_in, mu_r[...], carry[0:1, :])
    kx = shift_mix(k_in, mu_k[...], carry[1:2, :])
    vv = shift_mix(v_in, mu_v[...], carry[2:3, :])
    lo = shift_mix(l_in, mu_l[...], carry_l[0:1, :])
    carry[0:1, :] = r_in[t - 1:t, :]
    carry[1:2, :] = k_in[t - 1:t, :]
    carry[2:3, :] = v_in[t - 1:t, :]
    carry_l[0:1, :] = l_in[t - 1:t, :]

    w_log = -_softplus(-(w0_ref[...] + _bdot(jnp.tanh(lo), wup_ref[...]))) - 0.5
    log_decay = -jnp.exp(w_log)
    a_icl = jax.nn.sigmoid(a0_ref[...] + _bdot(lo, aup_ref[...]))

    kk = kx * kk_ref[...]
    kk = kk / jnp.maximum(jnp.sqrt(_head_sums(kk * kk)), 1e-12)
    kc = kx * (1.0 + (a_icl - 1.0) * ka_ref[...])
    bonus = _head_sums(r * kc * rk_ref[...]) * vv
    return log_decay, r, kc, vv, -kk, kk * a_icl, bonus


def _stack_heads(x):
    lane = lax.broadcasted_iota(jnp.int32, x.shape, 1)
    return jnp.concatenate([jnp.where(lane < RWKV_HEAD, x, 0.0), jnp.where(lane >= RWKV_HEAD, x, 0.0)], axis=0)


def _time_indices():
    t = lax.broadcasted_iota(jnp.int32, (RWKV_CHUNK, PAIR), 0)
    s = lax.broadcasted_iota(jnp.int32, (RWKV_CHUNK, PAIR), 1) & (RWKV_CHUNK - 1)
    return t, s


def _unit_lower_inverse(a_strict):
    ti, si = _time_indices()

    def same_block(bits):
        return (ti >> bits) == (si >> bits)

    pw = [jnp.where(same_block(4), a, 0.0) for a in a_strict]
    x = [jnp.where(ti == si, 1.0, 0.0) + p for p in pw]
    for _ in range(3):
        pw = [_bdot(p, _stack_heads(p)) for p in pw]
        x = [xi + _bdot(xi, _stack_heads(p)) for xi, p in zip(x, pw)]
    for bits in (5, 6):
        join = same_block(bits) & jnp.logical_not(same_block(bits - 1))
        xe = [_bdot(xi, _stack_heads(jnp.where(join, a, 0.0))) for xi, a in zip(x, a_strict)]
        x = [xi + _bdot(t, _stack_heads(xi)) for xi, t in zip(x, xe)]
    return x


def _rwkv_chunk_transforms(lw_all, r_all, k_all, v_all, a_all, b_all):
    c = RWKV_CHUNK
    n = 2 * c
    ti = lax.broadcasted_iota(jnp.int32, (c, c), 0)
    si = lax.broadcasted_iota(jnp.int32, (c, c), 1)
    lower_ones = jnp.where(si <= ti, 1.0, 0.0)
    tt, ss = _time_indices()
    strict = tt > ss
    incl = tt >= ss
    ri = lax.broadcasted_iota(jnp.int32, (n, n), 0)
    ci = lax.broadcasted_iota(jnp.int32, (n, n), 1)
    same_head = (ri < RWKV_HEAD) == (ci < RWKV_HEAD)
    eye = ri == ci

    units = [(ch, p) for ch in range(lw_all.shape[0] // c) for p in range(N_PAIRS)]
    each = lambda f, *cols: [f(*args) for args in zip(*cols)]

    def split(x):
        return [x[ch * c:(ch + 1) * c, p * PAIR:(p + 1) * PAIR] for ch, p in units]

    lw, r, k, v, a, b = (split(x) for x in (lw_all, r_all, k_all, v_all, a_all, b_all))
    cs = each(lambda x: _split_dot(lower_ones, x, 1, 3), lw)
    c_end = each(lambda x: x[c - 1:c, :], cs)
    r_d = each(lambda x, y: x * jnp.exp(y), r, cs)
    a_d = each(lambda x, y, z: x * jnp.exp(y - z), a, cs, lw)
    b_i = each(lambda x, y: x * jnp.exp(-y), b, cs)
    k_i = each(lambda x, y: x * jnp.exp(-y), k, cs)
    b_e = each(lambda x, y, e: x * jnp.exp(e - y), b, cs, c_end)
    k_e = each(lambda x, y, e: x * jnp.exp(e - y), k, cs, c_end)
    v_s = each(_stack_heads, v)

    aa = each(lambda ad, rd, bi, ki: _bdot_nt(jnp.concatenate([ad, rd], axis=0),
                                              jnp.concatenate([_stack_heads(bi), _stack_heads(ki)], axis=0)),
              a_d, r_d, b_i, k_i)
    a_ab = each(lambda x: jnp.where(strict, x[0:c, 0:n], 0.0), aa)
    a_ak = each(lambda x: jnp.where(strict, x[0:c, n:2 * n], 0.0), aa)
    a_rb = each(lambda x: jnp.where(incl, x[c:n, 0:n], 0.0), aa)
    a_rk = each(lambda x: jnp.where(incl, x[c:n, n:2 * n], 0.0), aa)

    minv = _unit_lower_inverse(a_ab)
    w = each(lambda m, ad: _bdot(m, _stack_heads(ad)), minv, a_d)
    t1 = each(_bdot, a_ak, v_s)
    uv = each(lambda m, x: _bdot(m, _stack_heads(x)), minv, t1)
    q = each(lambda rd, x, y: rd + _bdot(x, _stack_heads(y)), r_d, a_rb, w)
    yc = each(lambda x, y, z, t: _bdot(x, _stack_heads(y)) + _bdot(z, t), a_rb, uv, a_rk, v_s)
    g = each(lambda e, x, y: jnp.where(eye, jnp.exp(e), 0.0) + jnp.where(same_head, _bdot_tn(x, y), 0.0), c_end, w, b_e)
    z = each(lambda u_, v_, be, ke: jnp.where(same_head, _bdot_tn(jnp.concatenate([u_, v_], axis=0),
                                                                    jnp.concatenate([be, ke], axis=0)), 0.0),
             uv, v, b_e, k_e)
    return {unit: terms for unit, *terms in zip(units, q, yc, g, z)}


def _rwkv_init(carry, carry_l, state, ybuf):
    carry[...] = jnp.zeros_like(carry)
    carry_l[...] = jnp.zeros_like(carry_l)
    state[...] = jnp.zeros_like(state)


def _rwkv_body(r_ref, k_ref, v_ref, lora_ref, gate_ref, mu_r, mu_k, mu_v, mu_l, w0_ref, wup_ref, a0_ref, aup_ref,
               kk_ref, ka_ref, rk_ref, gn_g, gn_b, o_ref, carry, carry_l, state, ybuf):
    c = RWKV_CHUNK
    chunks = r_ref.shape[0] // c
    *scan_inputs, bonus = _rwkv_prepare(r_ref, k_ref, v_ref, lora_ref, mu_r, mu_k, mu_v, mu_l, w0_ref, wup_ref,
                                        a0_ref, aup_ref, kk_ref, ka_ref, rk_ref, carry, carry_l)
    terms = _rwkv_chunk_transforms(*scan_inputs)

    pairs = range(N_PAIRS)
    sts = [state[:, p * PAIR:(p + 1) * PAIR] for p in pairs]
    starts = []
    for ch in range(chunks):
        starts.append(sts)
        sts = [_split_dot(sts[p], terms[ch, p][2], 2, 2) + terms[ch, p][3] for p in pairs]
    for p in pairs:
        state[:, p * PAIR:(p + 1) * PAIR] = sts[p]
    for ch in range(chunks):
        for p in pairs:
            q, yc = terms[ch, p][0], terms[ch, p][1]
            ybuf[ch * c:(ch + 1) * c, p * PAIR:(p + 1) * PAIR] = _split_dot(q, starts[ch][p], 2, 2, NT_DIMS) + yc

    wy = ybuf[...]
    inv_n = 1.0 / RWKV_HEAD
    mu = _head_sums(wy) * inv_n
    d = wy - mu
    var = _head_sums(d * d) * inv_n
    wy = d * lax.rsqrt(var + RWKV_GN_EPS) * gn_g[...] + gn_b[...]
    o_ref[...] = ((wy + bonus) * _silu(gate_ref[...])).astype(BF16)


def _rwkv_specs(tile):
    vec, lora_w = _resident((1, BR_WIDTH)), _resident((LANES, BR_WIDTH))
    in_specs = [_row_block(tile, COL_C_R), _row_block(tile, COL_C_K), _row_block(tile, COL_C_V),
                pl.BlockSpec((tile, LANES), lambda i: (i, COL_C_LORA * (BR_WIDTH // LANES))),
                _row_block(tile, COL_C_GATE),
                vec, vec, vec, _resident((1, LANES)), vec, lora_w, vec, lora_w, vec, vec, vec, vec, vec]
    return in_specs, [pltpu.VMEM((SUBLANES, BR_WIDTH), F32), pltpu.VMEM((SUBLANES, LANES), F32),
                      pltpu.VMEM((PAIR, BR_WIDTH), F32), pltpu.VMEM((tile, BR_WIDTH), F32)]


def _recurrent_mixers_kernel(*refs):
    n_in = LRU_IN + CONF_IN + RWKV_IN
    ins, (o_a, o_d, o_c), scratch = refs[:n_in], refs[n_in:n_in + 3], refs[n_in + 3:]
    lru_in, conf_in, rwkv_in = ins[:LRU_IN], ins[LRU_IN:LRU_IN + CONF_IN], ins[LRU_IN + CONF_IN:]
    lru_s = scratch[:LRU_SCRATCH]
    conf_s = scratch[LRU_SCRATCH:LRU_SCRATCH + CONF_SCRATCH]
    rwkv_s = scratch[LRU_SCRATCH + CONF_SCRATCH:]

    @pl.when(pl.program_id(0) == 0)
    def _():
        _lru_init(*lru_s)
        _conf_init(*conf_s)
        _rwkv_init(*rwkv_s)

    _rwkv_body(*rwkv_in, o_c, *rwkv_s)
    _conf_body(*conf_in, o_d, *conf_s)
    _lru_body(*lru_in, o_a, *lru_s)


def _recurrent_mixers(h, lru_args, conf_args, rwkv_args, tile=4 * RWKV_CHUNK):
    s = h.shape[0]
    (lru_specs, lru_scr), (conf_specs, conf_scr), (rwkv_specs, rwkv_scr) = _lru_specs(tile), _conf_specs(tile), _rwkv_specs(tile)
    assert (len(lru_specs), len(conf_specs), len(rwkv_specs)) == (LRU_IN, CONF_IN, RWKV_IN)
    out = pl.BlockSpec((tile, BR_WIDTH), lambda i: (i, 0))
    return pl.pallas_call(
        _recurrent_mixers_kernel,
        grid=(s // tile,),
        in_specs=lru_specs + conf_specs + rwkv_specs,
        out_specs=[out] * 3,
        out_shape=[jax.ShapeDtypeStruct((s, BR_WIDTH), BF16)] * 3,
        scratch_shapes=lru_scr + conf_scr + rwkv_scr,
        compiler_params=_params(("arbitrary",), 40),
        name="recurrent_mixers",
    )(h, h, *lru_args, h, h, h, *conf_args, h, h, h, h, h, *rwkv_args)


def _mix_kernel(xb_ref, *refs):
    ygs, wms, bms, wbrs = (refs[k * N_BRANCH:(k + 1) * N_BRANCH] for k in range(4))
    o_ref = refs[4 * N_BRANCH]
    xb = xb_ref[...]
    acc = None
    for n in range(N_BRANCH):
        gate = jax.nn.sigmoid(jnp.dot(xb, wms[n][...], preferred_element_type=F32) + bms[n][...])
        val = gate * jnp.dot(ygs[n][...], wbrs[n][...], preferred_element_type=F32)
        acc = val if acc is None else acc + val
    o_ref[...] = acc.astype(BF16)


def _mix(xb, ygs, w_all, layer, bm, wbr, tm=1024, tn=512):
    s = xb.shape[0]
    nj = D_MODEL // tn
    per_branch = lambda make: [make(n) for n in range(N_BRANCH)]
    return pl.pallas_call(
        _mix_kernel,
        grid=(s // tm, nj),
        in_specs=[pl.BlockSpec((tm, D_MODEL), lambda i, j: (i, 0))]
        + per_branch(lambda n: pl.BlockSpec((tm, BR_WIDTH), lambda i, j: (i, 0)))
        + per_branch(lambda n: pl.BlockSpec((pl.Squeezed(), pl.Element(D_MODEL), pl.Element(tn)),
                                            lambda i, j: (layer, 0, ((BRANCH_IN + n * D_MODEL) // LANES
                                                                     + j * (tn // LANES)) * LANES)))
        + per_branch(lambda n: pl.BlockSpec((1, tn), lambda i, j: (0, n * nj + j)))
        + per_branch(lambda n: pl.BlockSpec((None, BR_WIDTH, tn), lambda i, j: (n, 0, j))),
        out_specs=pl.BlockSpec((tm, tn), lambda i, j: (i, j)),
        out_shape=jax.ShapeDtypeStruct((s, D_MODEL), BF16),
        compiler_params=_params(("parallel", "arbitrary"), 48),
        name="branch_mix",
    )(xb, *ygs, *([w_all] * N_BRANCH), *([bm] * N_BRANCH), *([wbr] * N_BRANCH))


def _out_kernel(mixed_ref, x_ref, w_ref, g_ref, b_ref, o_ref):
    y = ALPHA * x_ref[...] + jnp.dot(mixed_ref[...], w_ref[...], preferred_element_type=F32)
    mu = jnp.mean(y, axis=-1, keepdims=True)
    var = jnp.mean(jnp.square(y - mu), axis=-1, keepdims=True)
    o_ref[...] = (y - mu) * lax.rsqrt(var + LN_EPS) * g_ref[...] + b_ref[...]


def _out_proj(mixed, x, w, g, b, tm=512):
    s = x.shape[0]
    row = pl.BlockSpec((tm, D_MODEL), lambda i: (i, 0))
    vec = pl.BlockSpec((1, D_MODEL), lambda i: (0, 0))
    return pl.pallas_call(
        _out_kernel,
        grid=(s // tm,),
        in_specs=[row, row, pl.BlockSpec((D_MODEL, D_MODEL), lambda i: (0, 0)), vec, vec],
        out_specs=row,
        out_shape=jax.ShapeDtypeStruct((s, D_MODEL), F32),
        compiler_params=_params(("parallel",), 48),
        name="out_proj_ln",
    )(mixed, x, w, g, b)


def _block_diag(w):
    blocks, n, _ = w.shape
    eye = jnp.eye(blocks, dtype=w.dtype)
    return (eye[:, None, :, None] * w[:, :, None, :]).reshape(blocks * n, blocks * n)


def _layer(x, att_bias, w_in_bf16, layer, b_in, lru_conv_w, lru_conv_b, lru_gate_a_w, lru_gate_a_b, lru_gate_x_w, lru_gate_x_b,
           lru_lambda, rwkv_mu, rwkv_w0, rwkv_w_up, rwkv_a0, rwkv_a_up, rwkv_k_k, rwkv_k_a, rwkv_r_k, rwkv_gn_g,
           rwkv_gn_b, conf_dw_w, conf_dw_b, conf_ln_g, conf_ln_b, w_br, w_out, ln_g, ln_b):
    vec = lambda t: t.reshape(1, -1)
    b_h = jnp.concatenate([b_in[:H_SPLIT * BR_WIDTH], b_in[C_GATE_START:BRANCH_IN]])
    h, xb = _in_proj(x, w_in_bf16, layer, vec(b_h))

    yg_b = _attention(h, att_bias)

    mu = rwkv_mu
    zpad = jnp.zeros((DECAY_RANK, BR_WIDTH), F32)
    wup = jnp.concatenate([rwkv_w_up, zpad], axis=0).astype(BF16)
    aup = jnp.concatenate([zpad, rwkv_a_up], axis=0).astype(BF16)
    lru_args = (lru_conv_w, vec(lru_conv_b), _block_diag(lru_gate_a_w).astype(BF16), vec(lru_gate_a_b),
                _block_diag(lru_gate_x_w).astype(BF16), vec(lru_gate_x_b), vec(lru_lambda))
    conf_args = (conf_dw_w, vec(conf_dw_b), vec(conf_ln_g), vec(conf_ln_b))
    rwkv_args = (vec(mu[:BR_WIDTH]), vec(mu[BR_WIDTH:2 * BR_WIDTH]), vec(mu[2 * BR_WIDTH:3 * BR_WIDTH]),
                 vec(mu[3 * BR_WIDTH:]), vec(rwkv_w0), wup, vec(rwkv_a0), aup, vec(rwkv_k_k), vec(rwkv_k_a),
                 vec(rwkv_r_k), vec(rwkv_gn_g), vec(rwkv_gn_b))
    yg_a, yg_d, yg_c = _recurrent_mixers(h, lru_args, conf_args, rwkv_args)

    mixed = _mix(xb, (yg_a, yg_b, yg_c, yg_d), w_in_bf16, layer, vec(b_in[BRANCH_IN:]), w_br.astype(BF16))
    return _out_proj(mixed, x, w_out.astype(BF16), vec(ln_g), vec(ln_b))


def kernel(x, att_rel_bias, w_in, b_in, lru_conv_w, lru_conv_b, lru_gate_a_w, lru_gate_a_b, lru_gate_x_w, lru_gate_x_b, lru_lambda, rwkv_mu, rwkv_w0, rwkv_w_up, rwkv_a0, rwkv_a_up, rwkv_k_k, rwkv_k_a, rwkv_r_k, rwkv_gn_g, rwkv_gn_b, conf_dw_w, conf_dw_b, conf_ln_g, conf_ln_b, w_br, w_out, ln_g, ln_b):
    bsz, s, d = x.shape
    assert bsz == 1 and d == D_MODEL and s % (16 * ATT_SPAN) == 0
    per_layer = (b_in, lru_conv_w, lru_conv_b, lru_gate_a_w, lru_gate_a_b, lru_gate_x_w, lru_gate_x_b,
                 lru_lambda, rwkv_mu, rwkv_w0, rwkv_w_up, rwkv_a0, rwkv_a_up, rwkv_k_k, rwkv_k_a, rwkv_r_k,
                 rwkv_gn_g, rwkv_gn_b, conf_dw_w, conf_dw_b, conf_ln_g, conf_ln_b, w_br, w_out, ln_g, ln_b)
    y = x.reshape(s, d)
    att_bias = _attn_bias(att_rel_bias)
    w_in_bf16 = w_in.astype(BF16)
    for l in range(DEPTH):
        y = _layer(y, att_bias, w_in_bf16, l, *(t[l] for t in per_layer))
    return y.reshape(bsz, s, d)
```

```python
import functools
import math

import numpy as np
import jax
import jax.numpy as jnp
from jax import lax
from jax.experimental import pallas as pl
from jax.experimental.pallas import tpu as pltpu

D_MODEL = 2048
DEPTH = 2
N_BRANCH = 4
BR_WIDTH = 512
LRU_BLOCKS = 8
LRU_BLOCK = BR_WIDTH // LRU_BLOCKS
LRU_CONV = 4
LRU_C = 8.0
ATT_GROUPS = ((128, 1), (512, 4), (2048, 16))
ATT_HEADS_PER_GROUP = 4
ATT_HEAD_DIM = BR_WIDTH // ATT_HEADS_PER_GROUP
ATT_HEADS = len(ATT_GROUPS) * ATT_HEADS_PER_GROUP
ATT_QKV = ATT_HEADS * ATT_HEAD_DIM
ATT_SPAN = 128
N_BUCKETS = 32
MAX_DISTANCE = 2048
NEG_INF = -1e30
RWKV_HEAD = 64
RWKV_HEADS = BR_WIDTH // RWKV_HEAD
DECAY_RANK = 64
ICLR_RANK = 64
RWKV_GN_EPS = 64e-5
CONF_KERNEL = 31
LN_EPS = 1e-5
ALPHA = (2.0 * DEPTH) ** 0.25

LANES = 128
SUBLANES = 8
MIB = 1024 * 1024

BRANCH_IN = 2 * BR_WIDTH + 3 * ATT_QKV + BR_WIDTH + (4 * BR_WIDTH + DECAY_RANK + ICLR_RANK) + 3 * BR_WIDTH
C_GATE_START = BRANCH_IN - 4 * BR_WIDTH
H_SPLIT = 16
H_BLOCKS = 20
H_WIDTH = H_BLOCKS * BR_WIDTH
COL_A_X, COL_A_GATE = 0, 1
COL_Q, COL_K, COL_V, COL_B_GATE = 2, 5, 8, 11
COL_C_R, COL_C_K, COL_C_V, COL_C_LORA = 12, 13, 14, 15
COL_C_GATE, COL_D_VAL, COL_D_GLU, COL_D_GATE = 16, 17, 18, 19

CONF_HALO = 32
RWKV_CHUNK = 64
PAIR = 2 * RWKV_HEAD
N_PAIRS = BR_WIDTH // PAIR

F32 = jnp.float32
BF16 = jnp.bfloat16


def _params(semantics, vmem_mib):
    return pltpu.CompilerParams(dimension_semantics=semantics, vmem_limit_bytes=vmem_mib * MIB)


def _bdot(a, b):
    return jnp.dot(a.astype(BF16), b.astype(BF16), preferred_element_type=F32)


def _bdot_nt(a, b):
    return lax.dot_general(a.astype(BF16), b.astype(BF16), (((1,), (1,)), ((), ())), preferred_element_type=F32)


def _bdot_tn(a, b):
    return lax.dot_general(a.astype(BF16), b.astype(BF16), (((0,), (0,)), ((), ())), preferred_element_type=F32)


NN_DIMS = (((1,), (0,)), ((), ()))
NT_DIMS = (((1,), (1,)), ((), ()))


def _bf16_parts(x, parts):
    out = []
    for _ in range(parts):
        hi = x.astype(BF16)
        out.append(hi)
        x = x - hi.astype(F32)
    return out


def _split_dot(a, b, a_parts, b_parts, dims=NN_DIMS):
    acc = None
    b_terms = _bf16_parts(b, b_parts)
    for i, ai in enumerate(_bf16_parts(a, a_parts)):
        for j, bj in enumerate(b_terms):
            if i + j < max(a_parts, b_parts):
                term = lax.dot_general(ai, bj, dims, preferred_element_type=F32)
                acc = term if acc is None else acc + term
    return acc


def _softplus(z):
    return jnp.maximum(z, 0.0) + jnp.log1p(jnp.exp(-jnp.abs(z)))


def _expm1_nonpos(z):
    u = jnp.exp(z)
    safe = jnp.where(u == 1.0, 0.5, u)
    return jnp.where(u == 1.0, z, jnp.where(u == 0.0, -1.0, (safe - 1.0) * z / jnp.log(safe)))


def _silu(z):
    return z * jax.nn.sigmoid(z)


def _in_proj_kernel(x_ref, w_ref, b_ref, h_ref, xb_ref):
    @pl.when(pl.program_id(1) == 0)
    def _():
        xb_ref[...] = x_ref[...].astype(BF16)

    h_ref[...] = jnp.dot(xb_ref[...], w_ref[...], preferred_element_type=F32) + b_ref[...]


def _h_source_column(block):
    return block * BR_WIDTH if block < H_SPLIT else C_GATE_START + (block - H_SPLIT) * BR_WIDTH


def _in_proj(x, w_all, layer, b, tm=1024, tn=1024):
    s, k = x.shape
    assert (H_SPLIT * BR_WIDTH) % tn == 0 and tn % BR_WIDTH == 0
    per_tile = tn // BR_WIDTH
    starts = np.array([_h_source_column(j * per_tile) // LANES for j in range(H_WIDTH // tn)], np.int32)
    return pl.pallas_call(
        lambda starts_ref, *refs: _in_proj_kernel(*refs),
        grid_spec=pltpu.PrefetchScalarGridSpec(
            num_scalar_prefetch=1,
            grid=(s // tm, H_WIDTH // tn),
            in_specs=[
                pl.BlockSpec((tm, k), lambda i, j, st: (i, 0)),
                pl.BlockSpec((pl.Squeezed(), pl.Element(k), pl.Element(tn)),
                             lambda i, j, st: (layer, 0, st[j] * LANES)),
                pl.BlockSpec((1, tn), lambda i, j, st: (0, j)),
            ],
            out_specs=[
                pl.BlockSpec((tm, tn), lambda i, j, st: (i, j)),
                pl.BlockSpec((tm, k), lambda i, j, st: (i, 0)),
            ],
        ),
        out_shape=[jax.ShapeDtypeStruct((s, H_WIDTH), F32), jax.ShapeDtypeStruct((s, k), BF16)],
        compiler_params=_params(("parallel", "arbitrary"), 48),
        name="in_proj",
    )(jnp.asarray(starts), x, w_all, b)


LRU_IN, CONF_IN, RWKV_IN = 9, 7, 18
LRU_SCRATCH, CONF_SCRATCH, RWKV_SCRATCH = 2, 2, 4


def _lru_init(ebuf, hc):
    ebuf[0:SUBLANES, :] = jnp.zeros((SUBLANES, BR_WIDTH), F32)
    hc[...] = jnp.zeros_like(hc)


def _lru_body(ax_ref, ag_ref, cw_ref, cb_ref, wa_ref, ba_ref, wx_ref, bx_ref, lam_ref, o_ref, ebuf, hc):
    t = ax_ref.shape[0]
    halo = SUBLANES
    x = ax_ref[...]
    ebuf[halo:halo + t, :] = x
    u = cb_ref[...] + jnp.zeros((t, BR_WIDTH), F32)
    for j in range(LRU_CONV):
        u = u + cw_ref[j:j + 1, :] * ebuf[pl.ds(halo - (LRU_CONV - 1) + j, t), :]
    ebuf[0:halo, :] = x[t - halo:t, :]

    gate_r = jax.nn.sigmoid(_bdot(u, wa_ref[...]) + ba_ref[...])
    gate_i = jax.nn.sigmoid(_bdot(u, wx_ref[...]) + bx_ref[...])
    log_a = -LRU_C * gate_r * _softplus(-lam_ref[...])
    a = jnp.exp(log_a)
    b = jnp.sqrt(-_expm1_nonpos(2.0 * log_a)) * (gate_i * u)

    row = lax.broadcasted_iota(jnp.int32, (t, BR_WIDTH), 0)
    shift = 1
    while shift < t:
        valid = row >= shift
        b = jnp.where(valid, a * pltpu.roll(b, shift, 0), 0.0) + b
        a = jnp.where(valid, a * pltpu.roll(a, shift, 0), a)
        shift *= 2
    h = a * hc[0:1, :] + b
    hc[0:1, :] = h[t - 1:t, :]
    o_ref[...] = (h * _silu(ag_ref[...])).astype(BF16)


def _row_block(tile, col):
    return pl.BlockSpec((tile, BR_WIDTH), lambda i: (i, col))


def _resident(shape):
    return pl.BlockSpec(shape, lambda i: (0,) * len(shape))


def _lru_specs(tile):
    vec, mat = _resident((1, BR_WIDTH)), _resident((BR_WIDTH, BR_WIDTH))
    in_specs = [_row_block(tile, COL_A_X), _row_block(tile, COL_A_GATE), _resident((LRU_CONV, BR_WIDTH)), vec,
                mat, vec, mat, vec, vec]
    return in_specs, [pltpu.VMEM((tile + SUBLANES, BR_WIDTH), F32), pltpu.VMEM((SUBLANES, BR_WIDTH), F32)]


def _conf_init(ebuf, shifted):
    ebuf[0:CONF_HALO, :] = jnp.zeros((CONF_HALO, BR_WIDTH), F32)


def _conf_body(val_ref, glu_ref, gate_ref, w_ref, b_ref, g_ref, beta_ref, o_ref, ebuf, shifted):
    t = val_ref.shape[0]
    halo = CONF_HALO
    cu = val_ref[...] * jax.nn.sigmoid(glu_ref[...])
    ebuf[halo:halo + t, :] = cu
    for b in range(SUBLANES):
        span = t + (CONF_KERNEL - 1 - b) // SUBLANES * SUBLANES
        shifted[b, 0:span, :] = ebuf[pl.ds(halo - (CONF_KERNEL - 1) + b, span), :]
    acc = b_ref[...] + jnp.zeros((t, BR_WIDTH), F32)
    for j in range(CONF_KERNEL):
        b = j % SUBLANES
        acc = acc + w_ref[j:j + 1, :] * shifted[b, j - b:j - b + t, :]
    ebuf[0:halo, :] = cu[t - halo:t, :]

    mu = jnp.mean(acc, axis=-1, keepdims=True)
    var = jnp.mean(jnp.square(acc - mu), axis=-1, keepdims=True)
    ln = (acc - mu) * lax.rsqrt(var + LN_EPS) * g_ref[...] + beta_ref[...]
    o_ref[...] = (_silu(ln) * _silu(gate_ref[...])).astype(BF16)


def _conf_specs(tile):
    vec = _resident((1, BR_WIDTH))
    in_specs = [_row_block(tile, COL_D_VAL), _row_block(tile, COL_D_GLU), _row_block(tile, COL_D_GATE),
                _resident((CONF_KERNEL, BR_WIDTH)), vec, vec, vec]
    return in_specs, [pltpu.VMEM((tile + CONF_HALO, BR_WIDTH), F32),
                      pltpu.VMEM((SUBLANES, tile + CONF_HALO - SUBLANES, BR_WIDTH), F32)]


def _t5_bucket(dist):
    max_exact = N_BUCKETS // 2
    large = max_exact + (np.log(np.maximum(dist, 1) / max_exact) / math.log(MAX_DISTANCE / max_exact)
                         * (N_BUCKETS - max_exact)).astype(np.int32)
    large = np.minimum(large, N_BUCKETS - 1)
    return np.where(dist < max_exact, dist, large).astype(np.int32)


def _bucket_index():
    qi = np.arange(ATT_SPAN)[:, None]
    kj = np.arange(2 * ATT_SPAN)[None, :]
    dist = qi + ATT_SPAN - kj
    valid = (dist >= 0) & (dist <= ATT_SPAN)
    per_group = [np.where(valid, _t5_bucket(np.clip(dist, 0, ATT_SPAN) * dil), -1) for _, dil in ATT_GROUPS]
    return np.stack(per_group).astype(np.int32)


def _bias_kernel(table_ref, bucket_ref, o_ref):
    head = pl.program_id(0)
    bucket = bucket_ref[...]
    acc = jnp.full(bucket.shape, NEG_INF, F32)
    for bkt in range(N_BUCKETS):
        acc = jnp.where(bucket == bkt, table_ref[bkt, head], acc)
    o_ref[...] = acc


def _attn_bias(table):
    blk = (None, ATT_SPAN, 2 * ATT_SPAN)
    return pl.pallas_call(
        _bias_kernel,
        grid=(ATT_HEADS,),
        in_specs=[pl.BlockSpec(memory_space=pltpu.SMEM),
                  pl.BlockSpec(blk, lambda hd: (hd // ATT_HEADS_PER_GROUP, 0, 0))],
        out_specs=pl.BlockSpec(blk, lambda hd: (hd, 0, 0)),
        out_shape=jax.ShapeDtypeStruct((ATT_HEADS, ATT_SPAN, 2 * ATT_SPAN), F32),
        compiler_params=_params(("parallel",), 32),
        name="attn_bias",
    )(table, jnp.asarray(_bucket_index()))


ATT_DIRECT_STRIDE = 4


def _residue_reader(ref, slab, dilation):
    if dilation == 1:
        return lambda b, r: ref[b * ATT_SPAN:(b + 1) * ATT_SPAN, :]
    if dilation <= ATT_DIRECT_STRIDE:
        return lambda b, r: ref[pl.ds(b * ATT_SPAN * dilation + r, ATT_SPAN, stride=dilation), :]
    inner, outer = ATT_DIRECT_STRIDE, dilation // ATT_DIRECT_STRIDE
    per = ref.shape[0] // inner
    for r0 in range(inner):
        slab[r0] = ref[pl.ds(r0, per, stride=inner), :]
    return lambda b, r: slab[r % inner, pl.ds(b * ATT_SPAN * outer + r // inner, ATT_SPAN, stride=outer), :]


def _residue_writer(ref, slab, dilation):
    if dilation == 1:
        def write(b, r, val):
            ref[b * ATT_SPAN:(b + 1) * ATT_SPAN, :] = val
        return write, lambda: None
    if dilation <= ATT_DIRECT_STRIDE:
        def write(b, r, val):
            ref[pl.ds(b * ATT_SPAN * dilation + r, ATT_SPAN, stride=dilation), :] = val
        return write, lambda: None
    inner, outer = ATT_DIRECT_STRIDE, dilation // ATT_DIRECT_STRIDE
    per = ref.shape[0] // inner

    def write(b, r, val):
        slab[r % inner, pl.ds(b * ATT_SPAN * outer + r // inner, ATT_SPAN, stride=outer), :] = val

    def flush():
        for r0 in range(inner):
            ref[pl.ds(r0, per, stride=inner), :] = slab[r0]

    return write, flush


ATT_ROWS = ATT_SPAN * max(dil for _, dil in ATT_GROUPS)
ATT_GROUP_IN = 6
ATT_SLABS = 8


def _attn_group_outputs(q_ref, kc_ref, kp_ref, vc_ref, vp_ref, bias_ref, o_nat, m_nat, d_nat, slabs, first, dilation):
    blocks = ATT_ROWS // (ATT_SPAN * dilation)
    scale = ATT_HEAD_DIM ** -0.5
    read_q, read_kc, read_kp, read_vc, read_vp = (
        _residue_reader(ref, slab, dilation) for ref, slab in zip((q_ref, kc_ref, kp_ref, vc_ref, vp_ref), slabs[:5]))
    (write_o, flush_o), (write_m, flush_m), (write_d, flush_d) = (
        _residue_writer(ref, slab, dilation) for ref, slab in zip((o_nat, m_nat, d_nat), slabs[5:]))
    keys, values = {}, {}
    for r in range(dilation):
        keys[-1, r] = read_kp(0, r).astype(BF16)
        values[-1, r] = read_vp(0, r).astype(BF16)
        for b in range(blocks):
            keys[b, r] = read_kc(b, r).astype(BF16)
            values[b, r] = read_vc(b, r).astype(BF16)

    bias_p = bias_ref[:, 0:ATT_SPAN]
    bias_c = bias_ref[:, ATT_SPAN:2 * ATT_SPAN]
    units = [(b, r) for b in range(blocks) for r in range(dilation)]
    qs = [read_q(b, r).astype(BF16) for b, r in units]
    lps = [_bdot_nt(q, keys[b - 1, r]) * scale + bias_p for q, (b, r) in zip(qs, units)]
    lps = [jnp.where(first, NEG_INF, lp) if b == 0 else lp for lp, (b, r) in zip(lps, units)]
    lcs = [_bdot_nt(q, keys[b, r]) * scale + bias_c for q, (b, r) in zip(qs, units)]
    ms = [jnp.max(jnp.maximum(lp, lc), axis=-1, keepdims=True) for lp, lc in zip(lps, lcs)]
    pps = [jnp.exp(lp - m) for lp, m in zip(lps, ms)]
    pcs = [jnp.exp(lc - m) for lc, m in zip(lcs, ms)]
    dens = [jnp.sum(pp + pc, axis=-1, keepdims=True) for pp, pc in zip(pps, pcs)]
    full = (ATT_SPAN, LANES)
    for (b, r), pp, pc, m, den in zip(units, pps, pcs, ms, dens):
        write_o(b, r, (_bdot(pp, values[b - 1, r]) + _bdot(pc, values[b, r])) / den)
        write_m(b, r, jnp.broadcast_to(m, full))
        write_d(b, r, jnp.broadcast_to(den, full))
    flush_o()
    flush_m()
    flush_d()


def _attention_kernel(*refs):
    n_in = ATT_GROUP_IN * len(ATT_GROUPS)
    gate_ref, y_ref = refs[n_in], refs[n_in + 1]
    o_nat, m_nat, d_nat = refs[n_in + 2:n_in + 5]
    slabs = refs[n_in + 5:]
    first = pl.program_id(0) == 0
    for g, (_, dil) in enumerate(ATT_GROUPS):
        _attn_group_outputs(*refs[ATT_GROUP_IN * g:ATT_GROUP_IN * (g + 1)], o_nat.at[g], m_nat.at[g], d_nat.at[g],
                            slabs, first, dil)
    ms = [m_nat[g] for g in range(len(ATT_GROUPS))]
    m_all = functools.reduce(jnp.maximum, ms)
    wts = [jnp.exp(m - m_all) * d_nat[g] for g, m in enumerate(ms)]
    num = sum(w * o_nat[g] for g, w in enumerate(wts))
    y_ref[...] = (num / sum(wts) * _silu(gate_ref[...])).astype(BF16)


def _attention(h, bias):
    s = h.shape[0]
    heads = ATT_HEADS_PER_GROUP
    in_specs, operands = [], []
    for g, (_, dil) in enumerate(ATT_GROUPS):
        blk = ATT_SPAN * dil
        per_step = ATT_ROWS // blk

        def spec(col, prev, g=g, blk=blk, per_step=per_step):
            base = (col + g) * heads
            if prev:
                return pl.BlockSpec((blk, ATT_HEAD_DIM), lambda n, hd: (jnp.maximum(n * per_step - 1, 0), base + hd))
            return pl.BlockSpec((ATT_ROWS, ATT_HEAD_DIM), lambda n, hd: (n, base + hd))

        in_specs += [spec(COL_Q, False), spec(COL_K, False), spec(COL_K, True), spec(COL_V, False), spec(COL_V, True),
                     pl.BlockSpec((None, ATT_SPAN, 2 * ATT_SPAN), lambda n, hd, g=g: (g * heads + hd, 0, 0))]
        operands += [h, h, h, h, h, bias]
    in_specs.append(pl.BlockSpec((ATT_ROWS, ATT_HEAD_DIM), lambda n, hd: (n, COL_B_GATE * heads + hd)))
    token_order = pltpu.VMEM((len(ATT_GROUPS), ATT_ROWS, LANES), F32)
    slab = pltpu.VMEM((ATT_DIRECT_STRIDE, ATT_ROWS // ATT_DIRECT_STRIDE, LANES), F32)
    return pl.pallas_call(
        _attention_kernel,
        grid=(s // ATT_ROWS, heads),
        in_specs=in_specs,
        out_specs=pl.BlockSpec((ATT_ROWS, ATT_HEAD_DIM), lambda n, hd: (n, hd)),
        out_shape=jax.ShapeDtypeStruct((s, BR_WIDTH), BF16),
        scratch_shapes=[token_order] * 3 + [slab] * ATT_SLABS,
        compiler_params=_params(("parallel", "arbitrary"), 56),
        name="dilated_attention",
    )(*operands, h)


def _head_sums(x):
    ri = lax.broadcasted_iota(jnp.int32, (PAIR, PAIR), 0)
    ci = lax.broadcasted_iota(jnp.int32, (PAIR, PAIR), 1)
    same_head = jnp.where((ri < RWKV_HEAD) == (ci < RWKV_HEAD), 1.0, 0.0).astype(BF16)
    return jnp.concatenate([_split_dot(x[:, p * PAIR:(p + 1) * PAIR], same_head, 2, 1) for p in range(N_PAIRS)], axis=1)


def _rwkv_prepare(r_ref, k_ref, v_ref, lora_ref, mu_r, mu_k, mu_v, mu_l, w0_ref, wup_ref, a0_ref, aup_ref,
                  kk_ref, ka_ref, rk_ref, carry, carry_l):
    t = r_ref.shape[0]

    def shift_mix(x, mu, prev_row):
        row = lax.broadcasted_iota(jnp.int32, x.shape, 0)
        x_prev = jnp.where(row == 0, prev_row, pltpu.roll(x, 1, 0))
        return x + mu * (x_prev - x)

    r_in, k_in, v_in, l_in = r_ref[...], k_ref[...], v_ref[...], lora_ref[...]
    r = shift_mix(r_in, mu_r[...], carry[0:1, :])
    kx = shift_mix(k_in, mu_k[...], carry[1:2, :])
    vv = shift_mix(v_in, mu_v[...], carry[2:3, :])
    lo = shift_mix(l_in, mu_l[...], carry_l[0:1, :])
    carry[0:1, :] = r_in[t - 1:t, :]
    carry[1:2, :] = k_in[t - 1:t, :]
    carry[2:3, :] = v_in[t - 1:t, :]
    carry_l[0:1, :] = l_in[t - 1:t, :]

    w_log = -_softplus(-(w0_ref[...] + _bdot(jnp.tanh(lo), wup_ref[...]))) - 0.5
    log_decay = -jnp.exp(w_log)
    a_icl = jax.nn.sigmoid(a0_ref[...] + _bdot(lo, aup_ref[...]))

    kk = kx * kk_ref[...]
    kk = kk / jnp.maximum(jnp.sqrt(_head_sums(kk * kk)), 1e-12)
    kc = kx * (1.0 + (a_icl - 1.0) * ka_ref[...])
    bonus = _head_sums(r * kc * rk_ref[...]) * vv
    return log_decay, r, kc, vv, -kk, kk * a_icl, bonus


def _stack_heads(x):
    lane = lax.broadcasted_iota(jnp.int32, x.shape, 1)
    return jnp.concatenate([jnp.where(lane < RWKV_HEAD, x, 0.0), jnp.where(lane >= RWKV_HEAD, x, 0.0)], axis=0)


def _time_indices():
    t = lax.broadcasted_iota(jnp.int32, (RWKV_CHUNK, PAIR), 0)
    s = lax.broadcasted_iota(jnp.int32, (RWKV_CHUNK, PAIR), 1) & (RWKV_CHUNK - 1)
    return t, s


def _unit_lower_inverse(a_strict):
    ti, si = _time_indices()

    def same_block(bits):
        return (ti >> bits) == (si >> bits)

    pw = [jnp.where(same_block(4), a, 0.0) for a in a_strict]
    x = [jnp.where(ti == si, 1.0, 0.0) + p for p in pw]
    for _ in range(3):
        pw = [_bdot(p, _stack_heads(p)) for p in pw]
        x = [xi + _bdot(xi, _stack_heads(p)) for xi, p in zip(x, pw)]
    for bits in (5, 6):
        join = same_block(bits) & jnp.logical_not(same_block(bits - 1))
        xe = [_bdot(xi, _stack_heads(jnp.where(join, a, 0.0))) for xi, a in zip(x, a_strict)]
        x = [xi + _bdot(t, _stack_heads(xi)) for xi, t in zip(x, xe)]
    return x


def _rwkv_chunk_transforms(lw_all, r_all, k_all, v_all, a_all, b_all):
    c = RWKV_CHUNK
    n = 2 * c
    ti = lax.broadcasted_iota(jnp.int32, (c, c), 0)
    si = lax.broadcasted_iota(jnp.int32, (c, c), 1)
    lower_ones = jnp.where(si <= ti, 1.0, 0.0)
    tt, ss = _time_indices()
    strict = tt > ss
    incl = tt >= ss
    ri = lax.broadcasted_iota(jnp.int32, (n, n), 0)
    ci = lax.broadcasted_iota(jnp.int32, (n, n), 1)
    same_head = (ri < RWKV_HEAD) == (ci < RWKV_HEAD)
    eye = ri == ci

    units = [(ch, p) for ch in range(lw_all.shape[0] // c) for p in range(N_PAIRS)]
    each = lambda f, *cols: [f(*args) for args in zip(*cols)]

    def split(x):
        return [x[ch * c:(ch + 1) * c, p * PAIR:(p + 1) * PAIR] for ch, p in units]

    lw, r, k, v, a, b = (split(x) for x in (lw_all, r_all, k_all, v_all, a_all, b_all))
    cs = each(lambda x: _split_dot(lower_ones, x, 1, 3), lw)
    c_end = each(lambda x: x[c - 1:c, :], cs)
    r_d = each(lambda x, y: x * jnp.exp(y), r, cs)
    a_d = each(lambda x, y, z: x * jnp.exp(y - z), a, cs, lw)
    b_i = each(lambda x, y: x * jnp.exp(-y), b, cs)
    k_i = each(lambda x, y: x * jnp.exp(-y), k, cs)
    b_e = each(lambda x, y, e: x * jnp.exp(e - y), b, cs, c_end)
    k_e = each(lambda x, y, e: x * jnp.exp(e - y), k, cs, c_end)
    v_s = each(_stack_heads, v)

    aa = each(lambda ad, rd, bi, ki: _bdot_nt(jnp.concatenate([ad, rd], axis=0),
                                              jnp.concatenate([_stack_heads(bi), _stack_heads(ki)], axis=0)),
              a_d, r_d, b_i, k_i)
    a_ab = each(lambda x: jnp.where(strict, x[0:c, 0:n], 0.0), aa)
    a_ak = each(lambda x: jnp.where(strict, x[0:c, n:2 * n], 0.0), aa)
    a_rb = each(lambda x: jnp.where(incl, x[c:n, 0:n], 0.0), aa)
    a_rk = each(lambda x: jnp.where(incl, x[c:n, n:2 * n], 0.0), aa)

    minv = _unit_lower_inverse(a_ab)
    w = each(lambda m, ad: _bdot(m, _stack_heads(ad)), minv, a_d)
    t1 = each(_bdot, a_ak, v_s)
    uv = each(lambda m, x: _bdot(m, _stack_heads(x)), minv, t1)
    q = each(lambda rd, x, y: rd + _bdot(x, _stack_heads(y)), r_d, a_rb, w)
    yc = each(lambda x, y, z, t: _bdot(x, _stack_heads(y)) + _bdot(z, t), a_rb, uv, a_rk, v_s)
    g = each(lambda e, x, y: jnp.where(eye, jnp.exp(e), 0.0) + jnp.where(same_head, _bdot_tn(x, y), 0.0), c_end, w, b_e)
    z = each(lambda u_, v_, be, ke: jnp.where(same_head, _bdot_tn(jnp.concatenate([u_, v_], axis=0),
                                                                    jnp.concatenate([be, ke], axis=0)), 0.0),
             uv, v, b_e, k_e)
    return {unit: terms for unit, *terms in zip(units, q, yc, g, z)}


def _rwkv_init(carry, carry_l, state, ybuf):
    carry[...] = jnp.zeros_like(carry)
    carry_l[...] = jnp.zeros_like(carry_l)
    state[...] = jnp.zeros_like(state)


def _rwkv_body(r_ref, k_ref, v_ref, lora_ref, gate_ref, mu_r, mu_k, mu_v, mu_l, w0_ref, wup_ref, a0_ref, aup_ref,
               kk_ref, ka_ref, rk_ref, gn_g, gn_b, o_ref, carry, carry_l, state, ybuf):
    c = RWKV_CHUNK
    chunks = r_ref.shape[0] // c
    *scan_inputs, bonus = _rwkv_prepare(r_ref, k_ref, v_ref, lora_ref, mu_r, mu_k, mu_v, mu_l, w0_ref, wup_ref,
                                        a0_ref, aup_ref, kk_ref, ka_ref, rk_ref, carry, carry_l)
    terms = _rwkv_chunk_transforms(*scan_inputs)

    pairs = range(N_PAIRS)
    sts = [state[:, p * PAIR:(p + 1) * PAIR] for p in pairs]
    starts = []
    for ch in range(chunks):
        starts.append(sts)
        sts = [_split_dot(sts[p], terms[ch, p][2], 2, 2) + terms[ch, p][3] for p in pairs]
    for p in pairs:
        state[:, p * PAIR:(p + 1) * PAIR] = sts[p]
    for ch in range(chunks):
        for p in pairs:
            q, yc = terms[ch, p][0], terms[ch, p][1]
            ybuf[ch * c:(ch + 1) * c, p * PAIR:(p + 1) * PAIR] = _split_dot(q, starts[ch][p], 2, 2, NT_DIMS) + yc

    wy = ybuf[...]
    inv_n = 1.0 / RWKV_HEAD
    mu = _head_sums(wy) * inv_n
    d = wy - mu
    var = _head_sums(d * d) * inv_n
    wy = d * lax.rsqrt(var + RWKV_GN_EPS) * gn_g[...] + gn_b[...]
    o_ref[...] = ((wy + bonus) * _silu(gate_ref[...])).astype(BF16)


def _rwkv_specs(tile):
    vec, lora_w = _resident((1, BR_WIDTH)), _resident((LANES, BR_WIDTH))
    in_specs = [_row_block(tile, COL_C_R), _row_block(tile, COL_C_K), _row_block(tile, COL_C_V),
                pl.BlockSpec((tile, LANES), lambda i: (i, COL_C_LORA * (BR_WIDTH // LANES))),
                _row_block(tile, COL_C_GATE),
                vec, vec, vec, _resident((1, LANES)), vec, lora_w, vec, lora_w, vec, vec, vec, vec, vec]
    return in_specs, [pltpu.VMEM((SUBLANES, BR_WIDTH), F32), pltpu.VMEM((SUBLANES, LANES), F32),
                      pltpu.VMEM((PAIR, BR_WIDTH), F32), pltpu.VMEM((tile, BR_WIDTH), F32)]


def _recurrent_mixers_kernel(*refs):
    n_in = LRU_IN + CONF_IN + RWKV_IN
    ins, (o_a, o_d, o_c), scratch = refs[:n_in], refs[n_in:n_in + 3], refs[n_in + 3:]
    lru_in, conf_in, rwkv_in = ins[:LRU_IN], ins[LRU_IN:LRU_IN + CONF_IN], ins[LRU_IN + CONF_IN:]
    lru_s = scratch[:LRU_SCRATCH]
    conf_s = scratch[LRU_SCRATCH:LRU_SCRATCH + CONF_SCRATCH]
    rwkv_s = scratch[LRU_SCRATCH + CONF_SCRATCH:]

    @pl.when(pl.program_id(0) == 0)
    def _():
        _lru_init(*lru_s)
        _conf_init(*conf_s)
        _rwkv_init(*rwkv_s)

    _rwkv_body(*rwkv_in, o_c, *rwkv_s)
    _conf_body(*conf_in, o_d, *conf_s)
    _lru_body(*lru_in, o_a, *lru_s)


def _recurrent_mixers(h, lru_args, conf_args, rwkv_args, tile=4 * RWKV_CHUNK):
    s = h.shape[0]
    (lru_specs, lru_scr), (conf_specs, conf_scr), (rwkv_specs, rwkv_scr) = _lru_specs(tile), _conf_specs(tile), _rwkv_specs(tile)
    assert (len(lru_specs), len(conf_specs), len(rwkv_specs)) == (LRU_IN, CONF_IN, RWKV_IN)
    out = pl.BlockSpec((tile, BR_WIDTH), lambda i: (i, 0))
    return pl.pallas_call(
        _recurrent_mixers_kernel,
        grid=(s // tile,),
        in_specs=lru_specs + conf_specs + rwkv_specs,
        out_specs=[out] * 3,
        out_shape=[jax.ShapeDtypeStruct((s, BR_WIDTH), BF16)] * 3,
        scratch_shapes=lru_scr + conf_scr + rwkv_scr,
        compiler_params=_params(("arbitrary",), 40),
        name="recurrent_mixers",
    )(h, h, *lru_args, h, h, h, *conf_args, h, h, h, h, h, *rwkv_args)


def _mix_kernel(xb_ref, *refs):
    ygs, wms, bms, wbrs = (refs[k * N_BRANCH:(k + 1) * N_BRANCH] for k in range(4))
    o_ref = refs[4 * N_BRANCH]
    xb = xb_ref[...]
    acc = None
    for n in range(N_BRANCH):
        gate = jax.nn.sigmoid(_bdot(xb, wms[n][...]) + bms[n][...])
        val = gate * _bdot(ygs[n][...], wbrs[n][...])
        acc = val if acc is None else acc + val
    o_ref[...] = acc.astype(BF16)


def _mix(xb, ygs, w_all, layer, bm, wbr_all, tm=1024, tn=256):
    s = xb.shape[0]
    nj = D_MODEL // tn
    per_branch = lambda make: [make(n) for n in range(N_BRANCH)]
    return pl.pallas_call(
        _mix_kernel,
        grid=(s // tm, nj),
        in_specs=[pl.BlockSpec((tm, D_MODEL), lambda i, j: (i, 0))]
        + per_branch(lambda n: pl.BlockSpec((tm, BR_WIDTH), lambda i, j: (i, 0)))
        + per_branch(lambda n: pl.BlockSpec((pl.Squeezed(), pl.Element(D_MODEL), pl.Element(tn)),
                                            lambda i, j: (layer, 0, ((BRANCH_IN + n * D_MODEL) // LANES
                                                                     + j * (tn // LANES)) * LANES)))
        + per_branch(lambda n: pl.BlockSpec((1, tn), lambda i, j: (0, n * nj + j)))
        + per_branch(lambda n: pl.BlockSpec((None, None, BR_WIDTH, tn), lambda i, j: (layer, n, 0, j))),
        out_specs=pl.BlockSpec((tm, tn), lambda i, j: (i, j)),
        out_shape=jax.ShapeDtypeStruct((s, D_MODEL), BF16),
        compiler_params=_params(("parallel", "arbitrary"), 48),
        name="branch_mix",
    )(xb, *ygs, *([w_all] * N_BRANCH), *([bm] * N_BRANCH), *([wbr_all] * N_BRANCH))


def _out_kernel(mixed_ref, x_ref, w_ref, g_ref, b_ref, o_ref):
    y = ALPHA * x_ref[...] + jnp.dot(mixed_ref[...], w_ref[...], preferred_element_type=F32)
    mu = jnp.mean(y, axis=-1, keepdims=True)
    var = jnp.mean(jnp.square(y - mu), axis=-1, keepdims=True)
    o_ref[...] = (y - mu) * lax.rsqrt(var + LN_EPS) * g_ref[...] + b_ref[...]


def _out_proj(mixed, x, w, g, b, tm=512):
    s = x.shape[0]
    row = pl.BlockSpec((tm, D_MODEL), lambda i: (i, 0))
    vec = pl.BlockSpec((1, D_MODEL), lambda i: (0, 0))
    return pl.pallas_call(
        _out_kernel,
        grid=(s // tm,),
        in_specs=[row, row, pl.BlockSpec((D_MODEL, D_MODEL), lambda i: (0, 0)), vec, vec],
        out_specs=row,
        out_shape=jax.ShapeDtypeStruct((s, D_MODEL), F32),
        compiler_params=_params(("parallel",), 48),
        name="out_proj_ln",
    )(mixed, x, w, g, b)


def _block_diag(w):
    blocks, n, _ = w.shape
    eye = jnp.eye(blocks, dtype=w.dtype)
    return (eye[:, None, :, None] * w[:, :, None, :]).reshape(blocks * n, blocks * n)


def _layer(x, att_bias, w_branch_bf16, w_in_all, w_br_all, layer, b_in, lru_conv_w, lru_conv_b, lru_gate_a_w, lru_gate_a_b, lru_gate_x_w, lru_gate_x_b,
           lru_lambda, rwkv_mu, rwkv_w0, rwkv_w_up, rwkv_a0, rwkv_a_up, rwkv_k_k, rwkv_k_a, rwkv_r_k, rwkv_gn_g,
           rwkv_gn_b, conf_dw_w, conf_dw_b, conf_ln_g, conf_ln_b, w_out, ln_g, ln_b):
    vec = lambda t: t.reshape(1, -1)
    b_h = jnp.concatenate([b_in[:H_SPLIT * BR_WIDTH], b_in[C_GATE_START:BRANCH_IN]])
    h, xb = _in_proj(x, w_branch_bf16, layer, vec(b_h))

    yg_b = _attention(h, att_bias)

    mu = rwkv_mu
    zpad = jnp.zeros((DECAY_RANK, BR_WIDTH), F32)
    wup = jnp.concatenate([rwkv_w_up, zpad], axis=0).astype(BF16)
    aup = jnp.concatenate([zpad, rwkv_a_up], axis=0).astype(BF16)
    lru_args = (lru_conv_w, vec(lru_conv_b), _block_diag(lru_gate_a_w).astype(BF16), vec(lru_gate_a_b),
                _block_diag(lru_gate_x_w).astype(BF16), vec(lru_gate_x_b), vec(lru_lambda))
    conf_args = (conf_dw_w, vec(conf_dw_b), vec(conf_ln_g), vec(conf_ln_b))
    rwkv_args = (vec(mu[:BR_WIDTH]), vec(mu[BR_WIDTH:2 * BR_WIDTH]), vec(mu[2 * BR_WIDTH:3 * BR_WIDTH]),
                 vec(mu[3 * BR_WIDTH:]), vec(rwkv_w0), wup, vec(rwkv_a0), aup, vec(rwkv_k_k), vec(rwkv_k_a),
                 vec(rwkv_r_k), vec(rwkv_gn_g), vec(rwkv_gn_b))
    yg_a, yg_d, yg_c = _recurrent_mixers(h, lru_args, conf_args, rwkv_args)

    mixed = _mix(xb, (yg_a, yg_b, yg_c, yg_d), w_in_all, layer, vec(b_in[BRANCH_IN:]), w_br_all)
    return _out_proj(mixed, x, w_out.astype(BF16), vec(ln_g), vec(ln_b))


def kernel(x, att_rel_bias, w_in, b_in, lru_conv_w, lru_conv_b, lru_gate_a_w, lru_gate_a_b, lru_gate_x_w, lru_gate_x_b, lru_lambda, rwkv_mu, rwkv_w0, rwkv_w_up, rwkv_a0, rwkv_a_up, rwkv_k_k, rwkv_k_a, rwkv_r_k, rwkv_gn_g, rwkv_gn_b, conf_dw_w, conf_dw_b, conf_ln_g, conf_ln_b, w_br, w_out, ln_g, ln_b):
    bsz, s, d = x.shape
    assert bsz == 1 and d == D_MODEL and s % (16 * ATT_SPAN) == 0
    per_layer = (b_in, lru_conv_w, lru_conv_b, lru_gate_a_w, lru_gate_a_b, lru_gate_x_w, lru_gate_x_b,
                 lru_lambda, rwkv_mu, rwkv_w0, rwkv_w_up, rwkv_a0, rwkv_a_up, rwkv_k_k, rwkv_k_a, rwkv_r_k,
                 rwkv_gn_g, rwkv_gn_b, conf_dw_w, conf_dw_b, conf_ln_g, conf_ln_b, w_out, ln_g, ln_b)
    y = x.reshape(s, d)
    att_bias = _attn_bias(att_rel_bias)
    w_branch_bf16 = w_in[:, :, :BRANCH_IN].astype(BF16)
    for l in range(DEPTH):
        y = _layer(y, att_bias, w_branch_bf16, w_in, w_br, l, *(t[l] for t in per_layer))
    return y.reshape(bsz, s, d)
```

```python
import functools
import math

import numpy as np
import jax
import jax.numpy as jnp
from jax import lax
from jax.experimental import pallas as pl
from jax.experimental.pallas import tpu as pltpu

D_MODEL = 2048
DEPTH = 2
N_BRANCH = 4
BR_WIDTH = 512
LRU_BLOCKS = 8
LRU_BLOCK = BR_WIDTH // LRU_BLOCKS
LRU_CONV = 4
LRU_C = 8.0
ATT_GROUPS = ((128, 1), (512, 4), (2048, 16))
ATT_HEADS_PER_GROUP = 4
ATT_HEAD_DIM = BR_WIDTH // ATT_HEADS_PER_GROUP
ATT_HEADS = len(ATT_GROUPS) * ATT_HEADS_PER_GROUP
ATT_QKV = ATT_HEADS * ATT_HEAD_DIM
ATT_SPAN = 128
N_BUCKETS = 32
MAX_DISTANCE = 2048
NEG_INF = -1e30
RWKV_HEAD = 64
RWKV_HEADS = BR_WIDTH // RWKV_HEAD
DECAY_RANK = 64
ICLR_RANK = 64
RWKV_GN_EPS = 64e-5
CONF_KERNEL = 31
LN_EPS = 1e-5
ALPHA = (2.0 * DEPTH) ** 0.25

LANES = 128
SUBLANES = 8
MIB = 1024 * 1024

BRANCH_IN = 2 * BR_WIDTH + 3 * ATT_QKV + BR_WIDTH + (4 * BR_WIDTH + DECAY_RANK + ICLR_RANK) + 3 * BR_WIDTH
C_GATE_START = BRANCH_IN - 4 * BR_WIDTH
H_SPLIT = 16
H_BLOCKS = 20
H_WIDTH = H_BLOCKS * BR_WIDTH
COL_A_X, COL_A_GATE = 0, 1
COL_Q, COL_K, COL_V, COL_B_GATE = 2, 5, 8, 11
COL_C_R, COL_C_K, COL_C_V, COL_C_LORA = 12, 13, 14, 15
COL_C_GATE, COL_D_VAL, COL_D_GLU, COL_D_GATE = 16, 17, 18, 19

CONF_HALO = 32
RWKV_CHUNK = 64
PAIR = 2 * RWKV_HEAD
N_PAIRS = BR_WIDTH // PAIR

F32 = jnp.float32
BF16 = jnp.bfloat16


def _params(semantics, vmem_mib):
    return pltpu.CompilerParams(dimension_semantics=semantics, vmem_limit_bytes=vmem_mib * MIB)


def _bdot(a, b):
    return jnp.dot(a.astype(BF16), b.astype(BF16), preferred_element_type=F32)


def _bdot_nt(a, b):
    return lax.dot_general(a.astype(BF16), b.astype(BF16), (((1,), (1,)), ((), ())), preferred_element_type=F32)


def _bdot_tn(a, b):
    return lax.dot_general(a.astype(BF16), b.astype(BF16), (((0,), (0,)), ((), ())), preferred_element_type=F32)


NN_DIMS = (((1,), (0,)), ((), ()))
NT_DIMS = (((1,), (1,)), ((), ()))


def _bf16_parts(x, parts):
    out = []
    for _ in range(parts):
        hi = x.astype(BF16)
        out.append(hi)
        x = x - hi.astype(F32)
    return out


def _split_dot(a, b, a_parts, b_parts, dims=NN_DIMS):
    acc = None
    b_terms = _bf16_parts(b, b_parts)
    for i, ai in enumerate(_bf16_parts(a, a_parts)):
        for j, bj in enumerate(b_terms):
            if i + j < max(a_parts, b_parts):
                term = lax.dot_general(ai, bj, dims, preferred_element_type=F32)
                acc = term if acc is None else acc + term
    return acc


def _softplus(z):
    return jnp.maximum(z, 0.0) + jnp.log1p(jnp.exp(-jnp.abs(z)))


def _expm1_nonpos(z):
    u = jnp.exp(z)
    safe = jnp.where(u == 1.0, 0.5, u)
    return jnp.where(u == 1.0, z, jnp.where(u == 0.0, -1.0, (safe - 1.0) * z / jnp.log(safe)))


def _silu(z):
    return z * jax.nn.sigmoid(z)


def _in_proj_kernel(x_ref, w_ref, b_ref, h_ref, xb_ref):
    @pl.when(pl.program_id(1) == 0)
    def _():
        xb_ref[...] = x_ref[...].astype(BF16)

    h_ref[...] = _bdot(xb_ref[...], w_ref[...]) + b_ref[...]


def _h_source_column(block):
    return block * BR_WIDTH if block < H_SPLIT else C_GATE_START + (block - H_SPLIT) * BR_WIDTH


def _in_proj(x, w_all, layer, b, tm=1024, tn=1024):
    s, k = x.shape
    assert (H_SPLIT * BR_WIDTH) % tn == 0 and tn % BR_WIDTH == 0
    per_tile = tn // BR_WIDTH
    starts = np.array([_h_source_column(j * per_tile) // LANES for j in range(H_WIDTH // tn)], np.int32)
    return pl.pallas_call(
        lambda starts_ref, *refs: _in_proj_kernel(*refs),
        grid_spec=pltpu.PrefetchScalarGridSpec(
            num_scalar_prefetch=1,
            grid=(s // tm, H_WIDTH // tn),
            in_specs=[
                pl.BlockSpec((tm, k), lambda i, j, st: (i, 0)),
                pl.BlockSpec((pl.Squeezed(), pl.Element(k), pl.Element(tn)),
                             lambda i, j, st: (layer, 0, st[j] * LANES)),
                pl.BlockSpec((1, tn), lambda i, j, st: (0, j)),
            ],
            out_specs=[
                pl.BlockSpec((tm, tn), lambda i, j, st: (i, j)),
                pl.BlockSpec((tm, k), lambda i, j, st: (i, 0)),
            ],
        ),
        out_shape=[jax.ShapeDtypeStruct((s, H_WIDTH), F32), jax.ShapeDtypeStruct((s, k), BF16)],
        compiler_params=_params(("parallel", "arbitrary"), 56),
        name="in_proj",
    )(jnp.asarray(starts), x, w_all, b)


LRU_IN, CONF_IN, RWKV_IN = 9, 7, 18
LRU_SCRATCH, CONF_SCRATCH, RWKV_SCRATCH = 2, 2, 4


def _lru_init(ebuf, hc):
    ebuf[0:SUBLANES, :] = jnp.zeros((SUBLANES, BR_WIDTH), F32)
    hc[...] = jnp.zeros_like(hc)


def _lru_body(ax_ref, ag_ref, cw_ref, cb_ref, wa_ref, ba_ref, wx_ref, bx_ref, lam_ref, o_ref, ebuf, hc):
    t = ax_ref.shape[0]
    halo = SUBLANES
    x = ax_ref[...]
    ebuf[halo:halo + t, :] = x
    u = cb_ref[...] + jnp.zeros((t, BR_WIDTH), F32)
    for j in range(LRU_CONV):
        u = u + cw_ref[j:j + 1, :] * ebuf[pl.ds(halo - (LRU_CONV - 1) + j, t), :]
    ebuf[0:halo, :] = x[t - halo:t, :]

    gate_r = jax.nn.sigmoid(_bdot(u, wa_ref[...]) + ba_ref[...])
    gate_i = jax.nn.sigmoid(_bdot(u, wx_ref[...]) + bx_ref[...])
    log_a = -LRU_C * gate_r * _softplus(-lam_ref[...])
    a = jnp.exp(log_a)
    b = jnp.sqrt(-_expm1_nonpos(2.0 * log_a)) * (gate_i * u)

    row = lax.broadcasted_iota(jnp.int32, (t, BR_WIDTH), 0)
    shift = 1
    while shift < t:
        valid = row >= shift
        b = jnp.where(valid, a * pltpu.roll(b, shift, 0), 0.0) + b
        a = jnp.where(valid, a * pltpu.roll(a, shift, 0), a)
        shift *= 2
    h = a * hc[0:1, :] + b
    hc[0:1, :] = h[t - 1:t, :]
    o_ref[...] = (h * _silu(ag_ref[...])).astype(BF16)


def _row_block(tile, col):
    return pl.BlockSpec((tile, BR_WIDTH), lambda i: (i, col))


def _resident(shape):
    return pl.BlockSpec(shape, lambda i: (0,) * len(shape))


def _lru_specs(tile):
    vec, mat = _resident((1, BR_WIDTH)), _resident((BR_WIDTH, BR_WIDTH))
    in_specs = [_row_block(tile, COL_A_X), _row_block(tile, COL_A_GATE), _resident((LRU_CONV, BR_WIDTH)), vec,
                mat, vec, mat, vec, vec]
    return in_specs, [pltpu.VMEM((tile + SUBLANES, BR_WIDTH), F32), pltpu.VMEM((SUBLANES, BR_WIDTH), F32)]


def _conf_init(ebuf, shifted):
    ebuf[0:CONF_HALO, :] = jnp.zeros((CONF_HALO, BR_WIDTH), F32)


def _conf_body(val_ref, glu_ref, gate_ref, w_ref, b_ref, g_ref, beta_ref, o_ref, ebuf, shifted):
    t = val_ref.shape[0]
    halo = CONF_HALO
    cu = val_ref[...] * jax.nn.sigmoid(glu_ref[...])
    ebuf[halo:halo + t, :] = cu
    for b in range(SUBLANES):
        span = t + (CONF_KERNEL - 1 - b) // SUBLANES * SUBLANES
        shifted[b, 0:span, :] = ebuf[pl.ds(halo - (CONF_KERNEL - 1) + b, span), :]
    acc = b_ref[...] + jnp.zeros((t, BR_WIDTH), F32)
    for j in range(CONF_KERNEL):
        b = j % SUBLANES
        acc = acc + w_ref[j:j + 1, :] * shifted[b, j - b:j - b + t, :]
    ebuf[0:halo, :] = cu[t - halo:t, :]

    mu = jnp.mean(acc, axis=-1, keepdims=True)
    var = jnp.mean(jnp.square(acc - mu), axis=-1, keepdims=True)
    ln = (acc - mu) * lax.rsqrt(var + LN_EPS) * g_ref[...] + beta_ref[...]
    o_ref[...] = (_silu(ln) * _silu(gate_ref[...])).astype(BF16)


def _conf_specs(tile):
    vec = _resident((1, BR_WIDTH))
    in_specs = [_row_block(tile, COL_D_VAL), _row_block(tile, COL_D_GLU), _row_block(tile, COL_D_GATE),
                _resident((CONF_KERNEL, BR_WIDTH)), vec, vec, vec]
    return in_specs, [pltpu.VMEM((tile + CONF_HALO, BR_WIDTH), F32),
                      pltpu.VMEM((SUBLANES, tile + CONF_HALO - SUBLANES, BR_WIDTH), F32)]


def _t5_bucket(dist):
    max_exact = N_BUCKETS // 2
    large = max_exact + (np.log(np.maximum(dist, 1) / max_exact) / math.log(MAX_DISTANCE / max_exact)
                         * (N_BUCKETS - max_exact)).astype(np.int32)
    large = np.minimum(large, N_BUCKETS - 1)
    return np.where(dist < max_exact, dist, large).astype(np.int32)


def _bucket_index():
    qi = np.arange(ATT_SPAN)[:, None]
    kj = np.arange(2 * ATT_SPAN)[None, :]
    dist = qi + ATT_SPAN - kj
    valid = (dist >= 0) & (dist <= ATT_SPAN)
    per_group = [np.where(valid, _t5_bucket(np.clip(dist, 0, ATT_SPAN) * dil), -1) for _, dil in ATT_GROUPS]
    return np.stack(per_group).astype(np.int32)


def _bias_kernel(table_ref, bucket_ref, o_ref):
    head = pl.program_id(0)
    bucket = bucket_ref[...]
    acc = jnp.full(bucket.shape, NEG_INF, F32)
    for bkt in range(N_BUCKETS):
        acc = jnp.where(bucket == bkt, table_ref[bkt, head], acc)
    o_ref[...] = acc


def _attn_bias(table):
    blk = (None, ATT_SPAN, 2 * ATT_SPAN)
    return pl.pallas_call(
        _bias_kernel,
        grid=(ATT_HEADS,),
        in_specs=[pl.BlockSpec(memory_space=pltpu.SMEM),
                  pl.BlockSpec(blk, lambda hd: (hd // ATT_HEADS_PER_GROUP, 0, 0))],
        out_specs=pl.BlockSpec(blk, lambda hd: (hd, 0, 0)),
        out_shape=jax.ShapeDtypeStruct((ATT_HEADS, ATT_SPAN, 2 * ATT_SPAN), F32),
        compiler_params=_params(("parallel",), 32),
        name="attn_bias",
    )(table, jnp.asarray(_bucket_index()))


ATT_DIRECT_STRIDE = 4


def _residue_reader(ref, slab, dilation):
    if dilation == 1:
        return lambda b, r: ref[b * ATT_SPAN:(b + 1) * ATT_SPAN, :]
    if dilation <= ATT_DIRECT_STRIDE:
        return lambda b, r: ref[pl.ds(b * ATT_SPAN * dilation + r, ATT_SPAN, stride=dilation), :]
    inner, outer = ATT_DIRECT_STRIDE, dilation // ATT_DIRECT_STRIDE
    per = ref.shape[0] // inner
    for r0 in range(inner):
        slab[r0] = ref[pl.ds(r0, per, stride=inner), :]
    return lambda b, r: slab[r % inner, pl.ds(b * ATT_SPAN * outer + r // inner, ATT_SPAN, stride=outer), :]


def _residue_writer(ref, slab, dilation):
    if dilation == 1:
        def write(b, r, val):
            ref[b * ATT_SPAN:(b + 1) * ATT_SPAN, :] = val
        return write, lambda: None
    if dilation <= ATT_DIRECT_STRIDE:
        def write(b, r, val):
            ref[pl.ds(b * ATT_SPAN * dilation + r, ATT_SPAN, stride=dilation), :] = val
        return write, lambda: None
    inner, outer = ATT_DIRECT_STRIDE, dilation // ATT_DIRECT_STRIDE
    per = ref.shape[0] // inner

    def write(b, r, val):
        slab[r % inner, pl.ds(b * ATT_SPAN * outer + r // inner, ATT_SPAN, stride=outer), :] = val

    def flush():
        for r0 in range(inner):
            ref[pl.ds(r0, per, stride=inner), :] = slab[r0]

    return write, flush


ATT_ROWS = ATT_SPAN * max(dil for _, dil in ATT_GROUPS)
ATT_GROUP_IN = 6
ATT_SLABS = 8


def _attn_group_outputs(q_ref, kc_ref, kp_ref, vc_ref, vp_ref, bias_ref, o_nat, m_nat, d_nat, slabs, first, dilation):
    blocks = ATT_ROWS // (ATT_SPAN * dilation)
    scale = ATT_HEAD_DIM ** -0.5
    read_q, read_kc, read_kp, read_vc, read_vp = (
        _residue_reader(ref, slab, dilation) for ref, slab in zip((q_ref, kc_ref, kp_ref, vc_ref, vp_ref), slabs[:5]))
    (write_o, flush_o), (write_m, flush_m), (write_d, flush_d) = (
        _residue_writer(ref, slab, dilation) for ref, slab in zip((o_nat, m_nat, d_nat), slabs[5:]))
    keys, values = {}, {}
    for r in range(dilation):
        keys[-1, r] = read_kp(0, r).astype(BF16)
        values[-1, r] = read_vp(0, r).astype(BF16)
        for b in range(blocks):
            keys[b, r] = read_kc(b, r).astype(BF16)
            values[b, r] = read_vc(b, r).astype(BF16)

    bias_p = bias_ref[:, 0:ATT_SPAN]
    bias_c = bias_ref[:, ATT_SPAN:2 * ATT_SPAN]
    units = [(b, r) for b in range(blocks) for r in range(dilation)]
    qs = [read_q(b, r).astype(BF16) for b, r in units]
    lps = [_bdot_nt(q, keys[b - 1, r]) * scale + bias_p for q, (b, r) in zip(qs, units)]
    lps = [jnp.where(first, NEG_INF, lp) if b == 0 else lp for lp, (b, r) in zip(lps, units)]
    lcs = [_bdot_nt(q, keys[b, r]) * scale + bias_c for q, (b, r) in zip(qs, units)]
    ms = [jnp.max(jnp.maximum(lp, lc), axis=-1, keepdims=True) for lp, lc in zip(lps, lcs)]
    pps = [jnp.exp(lp - m) for lp, m in zip(lps, ms)]
    pcs = [jnp.exp(lc - m) for lc, m in zip(lcs, ms)]
    dens = [jnp.sum(pp + pc, axis=-1, keepdims=True) for pp, pc in zip(pps, pcs)]
    full = (ATT_SPAN, LANES)
    for (b, r), pp, pc, m, den in zip(units, pps, pcs, ms, dens):
        write_o(b, r, (_bdot(pp, values[b - 1, r]) + _bdot(pc, values[b, r])) / den)
        write_m(b, r, jnp.broadcast_to(m, full))
        write_d(b, r, jnp.broadcast_to(den, full))
    flush_o()
    flush_m()
    flush_d()


def _attention_kernel(*refs):
    n_in = ATT_GROUP_IN * len(ATT_GROUPS)
    gate_ref, y_ref = refs[n_in], refs[n_in + 1]
    o_nat, m_nat, d_nat = refs[n_in + 2:n_in + 5]
    slabs = refs[n_in + 5:]
    first = pl.program_id(0) == 0
    for g, (_, dil) in enumerate(ATT_GROUPS):
        _attn_group_outputs(*refs[ATT_GROUP_IN * g:ATT_GROUP_IN * (g + 1)], o_nat.at[g], m_nat.at[g], d_nat.at[g],
                            slabs, first, dil)
    ms = [m_nat[g] for g in range(len(ATT_GROUPS))]
    m_all = functools.reduce(jnp.maximum, ms)
    wts = [jnp.exp(m - m_all) * d_nat[g] for g, m in enumerate(ms)]
    num = sum(w * o_nat[g] for g, w in enumerate(wts))
    y_ref[...] = (num / sum(wts) * _silu(gate_ref[...])).astype(BF16)


def _attention(h, bias):
    s = h.shape[0]
    heads = ATT_HEADS_PER_GROUP
    in_specs, operands = [], []
    for g, (_, dil) in enumerate(ATT_GROUPS):
        blk = ATT_SPAN * dil
        per_step = ATT_ROWS // blk

        def spec(col, prev, g=g, blk=blk, per_step=per_step):
            base = (col + g) * heads
            if prev:
                return pl.BlockSpec((blk, ATT_HEAD_DIM), lambda n, hd: (jnp.maximum(n * per_step - 1, 0), base + hd))
            return pl.BlockSpec((ATT_ROWS, ATT_HEAD_DIM), lambda n, hd: (n, base + hd))

        in_specs += [spec(COL_Q, False), spec(COL_K, False), spec(COL_K, True), spec(COL_V, False), spec(COL_V, True),
                     pl.BlockSpec((None, ATT_SPAN, 2 * ATT_SPAN), lambda n, hd, g=g: (g * heads + hd, 0, 0))]
        operands += [h, h, h, h, h, bias]
    in_specs.append(pl.BlockSpec((ATT_ROWS, ATT_HEAD_DIM), lambda n, hd: (n, COL_B_GATE * heads + hd)))
    token_order = pltpu.VMEM((len(ATT_GROUPS), ATT_ROWS, LANES), F32)
    slab = pltpu.VMEM((ATT_DIRECT_STRIDE, ATT_ROWS // ATT_DIRECT_STRIDE, LANES), F32)
    return pl.pallas_call(
        _attention_kernel,
        grid=(s // ATT_ROWS, heads),
        in_specs=in_specs,
        out_specs=pl.BlockSpec((ATT_ROWS, ATT_HEAD_DIM), lambda n, hd: (n, hd)),
        out_shape=jax.ShapeDtypeStruct((s, BR_WIDTH), BF16),
        scratch_shapes=[token_order] * 3 + [slab] * ATT_SLABS,
        compiler_params=_params(("parallel", "arbitrary"), 56),
        name="dilated_attention",
    )(*operands, h)


def _head_sums(x):
    ri = lax.broadcasted_iota(jnp.int32, (PAIR, PAIR), 0)
    ci = lax.broadcasted_iota(jnp.int32, (PAIR, PAIR), 1)
    same_head = jnp.where((ri < RWKV_HEAD) == (ci < RWKV_HEAD), 1.0, 0.0).astype(BF16)
    return jnp.concatenate([_split_dot(x[:, p * PAIR:(p + 1) * PAIR], same_head, 2, 1) for p in range(N_PAIRS)], axis=1)


def _rwkv_prepare(r_ref, k_ref, v_ref, lora_ref, mu_r, mu_k, mu_v, mu_l, w0_ref, wup_ref, a0_ref, aup_ref,
                  kk_ref, ka_ref, rk_ref, carry, carry_l):
    t = r_ref.shape[0]

    def shift_mix(x, mu, prev_row):
        row = lax.broadcasted_iota(jnp.int32, x.shape, 0)
        x_prev = jnp.where(row == 0, prev_row, pltpu.roll(x, 1, 0))
        return x + mu * (x_prev - x)

    r_in, k_in, v_in, l_in = r_ref[...], k_ref[...], v_ref[...], lora_ref[...]
    r = shift_mix(r_in, mu_r[...], carry[0:1, :])
    kx = shift_mix(k_in, mu_k[...], carry[1:2, :])
    vv = shift_mix(v_in, mu_v[...], carry[2:3, :])
    lo = shift_mix(l_in, mu_l[...], carry_l[0:1, :])
    carry[0:1, :] = r_in[t - 1:t, :]
    carry[1:2, :] = k_in[t - 1:t, :]
    carry[2:3, :] = v_in[t - 1:t, :]
    carry_l[0:1, :] = l_in[t - 1:t, :]

    w_log = -_softplus(-(w0_ref[...] + _bdot(jnp.tanh(lo), wup_ref[...]))) - 0.5
    log_decay = -jnp.exp(w_log)
    a_icl = jax.nn.sigmoid(a0_ref[...] + _bdot(lo, aup_ref[...]))

    kk = kx * kk_ref[...]
    kk = kk / jnp.maximum(jnp.sqrt(_head_sums(kk * kk)), 1e-12)
    kc = kx * (1.0 + (a_icl - 1.0) * ka_ref[...])
    bonus = _head_sums(r * kc * rk_ref[...]) * vv
    return log_decay, r, kc, vv, -kk, kk * a_icl, bonus


def _stack_heads(x):
    lane = lax.broadcasted_iota(jnp.int32, x.shape, 1)
    return jnp.concatenate([jnp.where(lane < RWKV_HEAD, x, 0.0), jnp.where(lane >= RWKV_HEAD, x, 0.0)], axis=0)


def _time_indices():
    t = lax.broadcasted_iota(jnp.int32, (RWKV_CHUNK, PAIR), 0)
    s = lax.broadcasted_iota(jnp.int32, (RWKV_CHUNK, PAIR), 1) & (RWKV_CHUNK - 1)
    return t, s


def _unit_lower_inverse(a_strict):
    ti, si = _time_indices()

    def same_block(bits):
        return (ti >> bits) == (si >> bits)

    pw = [jnp.where(same_block(4), a, 0.0) for a in a_strict]
    x = [jnp.where(ti == si, 1.0, 0.0) + p for p in pw]
    for _ in range(3):
        pw = [_bdot(p, _stack_heads(p)) for p in pw]
        x = [xi + _bdot(xi, _stack_heads(p)) for xi, p in zip(x, pw)]
    for bits in (5, 6):
        join = same_block(bits) & jnp.logical_not(same_block(bits - 1))
        xe = [_bdot(xi, _stack_heads(jnp.where(join, a, 0.0))) for xi, a in zip(x, a_strict)]
        x = [xi + _bdot(t, _stack_heads(xi)) for xi, t in zip(x, xe)]
    return x


def _rwkv_chunk_transforms(lw_all, r_all, k_all, v_all, a_all, b_all):
    c = RWKV_CHUNK
    n = 2 * c
    ti = lax.broadcasted_iota(jnp.int32, (c, c), 0)
    si = lax.broadcasted_iota(jnp.int32, (c, c), 1)
    lower_ones = jnp.where(si <= ti, 1.0, 0.0)
    tt, ss = _time_indices()
    strict = tt > ss
    incl = tt >= ss
    ri = lax.broadcasted_iota(jnp.int32, (n, n), 0)
    ci = lax.broadcasted_iota(jnp.int32, (n, n), 1)
    same_head = (ri < RWKV_HEAD) == (ci < RWKV_HEAD)
    eye = ri == ci

    units = [(ch, p) for ch in range(lw_all.shape[0] // c) for p in range(N_PAIRS)]
    each = lambda f, *cols: [f(*args) for args in zip(*cols)]

    def split(x):
        return [x[ch * c:(ch + 1) * c, p * PAIR:(p + 1) * PAIR] for ch, p in units]

    lw, r, k, v, a, b = (split(x) for x in (lw_all, r_all, k_all, v_all, a_all, b_all))
    cs = each(lambda x: _split_dot(lower_ones, x, 1, 3), lw)
    c_end = each(lambda x: x[c - 1:c, :], cs)
    r_d = each(lambda x, y: x * jnp.exp(y), r, cs)
    a_d = each(lambda x, y, z: x * jnp.exp(y - z), a, cs, lw)
    b_i = each(lambda x, y: x * jnp.exp(-y), b, cs)
    k_i = each(lambda x, y: x * jnp.exp(-y), k, cs)
    b_e = each(lambda x, y, e: x * jnp.exp(e - y), b, cs, c_end)
    k_e = each(lambda x, y, e: x * jnp.exp(e - y), k, cs, c_end)
    v_s = each(_stack_heads, v)

    aa = each(lambda ad, rd, bi, ki: _bdot_nt(jnp.concatenate([ad, rd], axis=0),
                                              jnp.concatenate([_stack_heads(bi), _stack_heads(ki)], axis=0)),
              a_d, r_d, b_i, k_i)
    a_ab = each(lambda x: jnp.where(strict, x[0:c, 0:n], 0.0), aa)
    a_ak = each(lambda x: jnp.where(strict, x[0:c, n:2 * n], 0.0), aa)
    a_rb = each(lambda x: jnp.where(incl, x[c:n, 0:n], 0.0), aa)
    a_rk = each(lambda x: jnp.where(incl, x[c:n, n:2 * n], 0.0), aa)

    minv = _unit_lower_inverse(a_ab)
    w = each(lambda m, ad: _bdot(m, _stack_heads(ad)), minv, a_d)
    t1 = each(_bdot, a_ak, v_s)
    uv = each(lambda m, x: _bdot(m, _stack_heads(x)), minv, t1)
    q = each(lambda rd, x, y: rd + _bdot(x, _stack_heads(y)), r_d, a_rb, w)
    yc = each(lambda x, y, z, t: _bdot(x, _stack_heads(y)) + _bdot(z, t), a_rb, uv, a_rk, v_s)
    g = each(lambda e, x, y: jnp.where(eye, jnp.exp(e), 0.0) + jnp.where(same_head, _bdot_tn(x, y), 0.0), c_end, w, b_e)
    z = each(lambda u_, v_, be, ke: jnp.where(same_head, _bdot_tn(jnp.concatenate([u_, v_], axis=0),
                                                                    jnp.concatenate([be, ke], axis=0)), 0.0),
             uv, v, b_e, k_e)
    return {unit: terms for unit, *terms in zip(units, q, yc, g, z)}


def _rwkv_init(carry, carry_l, state, ybuf):
    carry[...] = jnp.zeros_like(carry)
    carry_l[...] = jnp.zeros_like(carry_l)
    state[...] = jnp.zeros_like(state)


def _rwkv_body(r_ref, k_ref, v_ref, lora_ref, gate_ref, mu_r, mu_k, mu_v, mu_l, w0_ref, wup_ref, a0_ref, aup_ref,
               kk_ref, ka_ref, rk_ref, gn_g, gn_b, o_ref, carry, carry_l, state, ybuf):
    c = RWKV_CHUNK
    chunks = r_ref.shape[0] // c
    *scan_inputs, bonus = _rwkv_prepare(r_ref, k_ref, v_ref, lora_ref, mu_r, mu_k, mu_v, mu_l, w0_ref, wup_ref,
                                        a0_ref, aup_ref, kk_ref, ka_ref, rk_ref, carry, carry_l)
    terms = _rwkv_chunk_transforms(*scan_inputs)

    pairs = range(N_PAIRS)
    sts = [state[:, p * PAIR:(p + 1) * PAIR] for p in pairs]
    starts = []
    for ch in range(chunks):
        starts.append(sts)
        sts = [_split_dot(sts[p], terms[ch, p][2], 2, 2) + terms[ch, p][3] for p in pairs]
    for p in pairs:
        state[:, p * PAIR:(p + 1) * PAIR] = sts[p]
    for ch in range(chunks):
        for p in pairs:
            q, yc = terms[ch, p][0], terms[ch, p][1]
            ybuf[ch * c:(ch + 1) * c, p * PAIR:(p + 1) * PAIR] = _split_dot(q, starts[ch][p], 2, 2, NT_DIMS) + yc

    wy = ybuf[...]
    inv_n = 1.0 / RWKV_HEAD
    mu = _head_sums(wy) * inv_n
    d = wy - mu
    var = _head_sums(d * d) * inv_n
    wy = d * lax.rsqrt(var + RWKV_GN_EPS) * gn_g[...] + gn_b[...]
    o_ref[...] = ((wy + bonus) * _silu(gate_ref[...])).astype(BF16)


def _rwkv_specs(tile):
    vec, lora_w = _resident((1, BR_WIDTH)), _resident((LANES, BR_WIDTH))
    in_specs = [_row_block(tile, COL_C_R), _row_block(tile, COL_C_K), _row_block(tile, COL_C_V),
                pl.BlockSpec((tile, LANES), lambda i: (i, COL_C_LORA * (BR_WIDTH // LANES))),
                _row_block(tile, COL_C_GATE),
                vec, vec, vec, _resident((1, LANES)), vec, lora_w, vec, lora_w, vec, vec, vec, vec, vec]
    return in_specs, [pltpu.VMEM((SUBLANES, BR_WIDTH), F32), pltpu.VMEM((SUBLANES, LANES), F32),
                      pltpu.VMEM((PAIR, BR_WIDTH), F32), pltpu.VMEM((tile, BR_WIDTH), F32)]


def _recurrent_mixers_kernel(*refs):
    n_in = LRU_IN + CONF_IN + RWKV_IN
    ins, (o_a, o_d, o_c), scratch = refs[:n_in], refs[n_in:n_in + 3], refs[n_in + 3:]
    lru_in, conf_in, rwkv_in = ins[:LRU_IN], ins[LRU_IN:LRU_IN + CONF_IN], ins[LRU_IN + CONF_IN:]
    lru_s = scratch[:LRU_SCRATCH]
    conf_s = scratch[LRU_SCRATCH:LRU_SCRATCH + CONF_SCRATCH]
    rwkv_s = scratch[LRU_SCRATCH + CONF_SCRATCH:]

    @pl.when(pl.program_id(0) == 0)
    def _():
        _lru_init(*lru_s)
        _conf_init(*conf_s)
        _rwkv_init(*rwkv_s)

    _rwkv_body(*rwkv_in, o_c, *rwkv_s)
    _conf_body(*conf_in, o_d, *conf_s)
    _lru_body(*lru_in, o_a, *lru_s)


def _recurrent_mixers(h, lru_args, conf_args, rwkv_args, tile=4 * RWKV_CHUNK):
    s = h.shape[0]
    (lru_specs, lru_scr), (conf_specs, conf_scr), (rwkv_specs, rwkv_scr) = _lru_specs(tile), _conf_specs(tile), _rwkv_specs(tile)
    assert (len(lru_specs), len(conf_specs), len(rwkv_specs)) == (LRU_IN, CONF_IN, RWKV_IN)
    out = pl.BlockSpec((tile, BR_WIDTH), lambda i: (i, 0))
    return pl.pallas_call(
        _recurrent_mixers_kernel,
        grid=(s // tile,),
        in_specs=lru_specs + conf_specs + rwkv_specs,
        out_specs=[out] * 3,
        out_shape=[jax.ShapeDtypeStruct((s, BR_WIDTH), BF16)] * 3,
        scratch_shapes=lru_scr + conf_scr + rwkv_scr,
        compiler_params=_params(("arbitrary",), 40),
        name="recurrent_mixers",
    )(h, h, *lru_args, h, h, h, *conf_args, h, h, h, h, h, *rwkv_args)


def _mix_kernel(xb_ref, *refs):
    ygs, wms, bms, wbrs = (refs[k * N_BRANCH:(k + 1) * N_BRANCH] for k in range(4))
    o_ref = refs[4 * N_BRANCH]
    xb = xb_ref[...]
    acc = None
    for n in range(N_BRANCH):
        gate = jax.nn.sigmoid(_bdot(xb, wms[n][...]) + bms[n][...])
        val = gate * _bdot(ygs[n][...], wbrs[n][...])
        acc = val if acc is None else acc + val
    o_ref[...] = acc.astype(BF16)


def _mix(xb, ygs, w_all, layer, bm, wbr_all, tm=1024, tn=256):
    s = xb.shape[0]
    nj = D_MODEL // tn
    per_branch = lambda make: [make(n) for n in range(N_BRANCH)]
    return pl.pallas_call(
        _mix_kernel,
        grid=(s // tm, nj),
        in_specs=[pl.BlockSpec((tm, D_MODEL), lambda i, j: (i, 0))]
        + per_branch(lambda n: pl.BlockSpec((tm, BR_WIDTH), lambda i, j: (i, 0)))
        + per_branch(lambda n: pl.BlockSpec((pl.Squeezed(), pl.Element(D_MODEL), pl.Element(tn)),
                                            lambda i, j: (layer, 0, ((BRANCH_IN + n * D_MODEL) // LANES
                                                                     + j * (tn // LANES)) * LANES)))
        + per_branch(lambda n: pl.BlockSpec((1, tn), lambda i, j: (0, n * nj + j)))
        + per_branch(lambda n: pl.BlockSpec((None, None, BR_WIDTH, tn), lambda i, j: (layer, n, 0, j))),
        out_specs=pl.BlockSpec((tm, tn), lambda i, j: (i, j)),
        out_shape=jax.ShapeDtypeStruct((s, D_MODEL), BF16),
        compiler_params=_params(("parallel", "arbitrary"), 48),
        name="branch_mix",
    )(xb, *ygs, *([w_all] * N_BRANCH), *([bm] * N_BRANCH), *([wbr_all] * N_BRANCH))


def _out_kernel(mixed_ref, x_ref, w_ref, g_ref, b_ref, o_ref):
    y = ALPHA * x_ref[...] + jnp.dot(mixed_ref[...], w_ref[...], preferred_element_type=F32)
    mu = jnp.mean(y, axis=-1, keepdims=True)
    var = jnp.mean(jnp.square(y - mu), axis=-1, keepdims=True)
    o_ref[...] = (y - mu) * lax.rsqrt(var + LN_EPS) * g_ref[...] + b_ref[...]


def _out_proj(mixed, x, w, g, b, tm=512):
    s = x.shape[0]
    row = pl.BlockSpec((tm, D_MODEL), lambda i: (i, 0))
    vec = pl.BlockSpec((1, D_MODEL), lambda i: (0, 0))
    return pl.pallas_call(
        _out_kernel,
        grid=(s // tm,),
        in_specs=[row, row, pl.BlockSpec((D_MODEL, D_MODEL), lambda i: (0, 0)), vec, vec],
        out_specs=row,
        out_shape=jax.ShapeDtypeStruct((s, D_MODEL), F32),
        compiler_params=_params(("parallel",), 48),
        name="out_proj_ln",
    )(mixed, x, w, g, b)


def _block_diag(w):
    blocks, n, _ = w.shape
    eye = jnp.eye(blocks, dtype=w.dtype)
    return (eye[:, None, :, None] * w[:, :, None, :]).reshape(blocks * n, blocks * n)


def _layer(x, att_bias, w_in_all, w_br_all, layer, b_in, lru_conv_w, lru_conv_b, lru_gate_a_w, lru_gate_a_b, lru_gate_x_w, lru_gate_x_b,
           lru_lambda, rwkv_mu, rwkv_w0, rwkv_w_up, rwkv_a0, rwkv_a_up, rwkv_k_k, rwkv_k_a, rwkv_r_k, rwkv_gn_g,
           rwkv_gn_b, conf_dw_w, conf_dw_b, conf_ln_g, conf_ln_b, w_out, ln_g, ln_b):
    vec = lambda t: t.reshape(1, -1)
    b_h = jnp.concatenate([b_in[:H_SPLIT * BR_WIDTH], b_in[C_GATE_START:BRANCH_IN]])
    h, xb = _in_proj(x, w_in_all, layer, vec(b_h))

    yg_b = _attention(h, att_bias)

    mu = rwkv_mu
    zpad = jnp.zeros((DECAY_RANK, BR_WIDTH), F32)
    wup = jnp.concatenate([rwkv_w_up, zpad], axis=0).astype(BF16)
    aup = jnp.concatenate([zpad, rwkv_a_up], axis=0).astype(BF16)
    lru_args = (lru_conv_w, vec(lru_conv_b), _block_diag(lru_gate_a_w).astype(BF16), vec(lru_gate_a_b),
                _block_diag(lru_gate_x_w).astype(BF16), vec(lru_gate_x_b), vec(lru_lambda))
    conf_args = (conf_dw_w, vec(conf_dw_b), vec(conf_ln_g), vec(conf_ln_b))
    rwkv_args = (vec(mu[:BR_WIDTH]), vec(mu[BR_WIDTH:2 * BR_WIDTH]), vec(mu[2 * BR_WIDTH:3 * BR_WIDTH]),
                 vec(mu[3 * BR_WIDTH:]), vec(rwkv_w0), wup, vec(rwkv_a0), aup, vec(rwkv_k_k), vec(rwkv_k_a),
                 vec(rwkv_r_k), vec(rwkv_gn_g), vec(rwkv_gn_b))
    yg_a, yg_d, yg_c = _recurrent_mixers(h, lru_args, conf_args, rwkv_args)

    mixed = _mix(xb, (yg_a, yg_b, yg_c, yg_d), w_in_all, layer, vec(b_in[BRANCH_IN:]), w_br_all)
    return _out_proj(mixed, x, w_out.astype(BF16), vec(ln_g), vec(ln_b))


def kernel(x, att_rel_bias, w_in, b_in, lru_conv_w, lru_conv_b, lru_gate_a_w, lru_gate_a_b, lru_gate_x_w, lru_gate_x_b, lru_lambda, rwkv_mu, rwkv_w0, rwkv_w_up, rwkv_a0, rwkv_a_up, rwkv_k_k, rwkv_k_a, rwkv_r_k, rwkv_gn_g, rwkv_gn_b, conf_dw_w, conf_dw_b, conf_ln_g, conf_ln_b, w_br, w_out, ln_g, ln_b):
    bsz, s, d = x.shape
    assert bsz == 1 and d == D_MODEL and s % (16 * ATT_SPAN) == 0
    per_layer = (b_in, lru_conv_w, lru_conv_b, lru_gate_a_w, lru_gate_a_b, lru_gate_x_w, lru_gate_x_b,
                 lru_lambda, rwkv_mu, rwkv_w0, rwkv_w_up, rwkv_a0, rwkv_a_up, rwkv_k_k, rwkv_k_a, rwkv_r_k,
                 rwkv_gn_g, rwkv_gn_b, conf_dw_w, conf_dw_b, conf_ln_g, conf_ln_b, w_out, ln_g, ln_b)
    y = x.reshape(s, d)
    att_bias = _attn_bias(att_rel_bias)
    for l in range(DEPTH):
        y = _layer(y, att_bias, w_in, w_br, l, *(t[l] for t in per_layer))
    return y.reshape(bsz, s, d)
```

```python
import functools
import math

import numpy as np
import jax
import jax.numpy as jnp
from jax import lax
from jax.experimental import pallas as pl
from jax.experimental.pallas import tpu as pltpu

D_MODEL = 2048
DEPTH = 2
N_BRANCH = 4
BR_WIDTH = 512
LRU_BLOCKS = 8
LRU_BLOCK = BR_WIDTH // LRU_BLOCKS
LRU_CONV = 4
LRU_C = 8.0
ATT_GROUPS = ((128, 1), (512, 4), (2048, 16))
ATT_HEADS_PER_GROUP = 4
ATT_HEAD_DIM = BR_WIDTH // ATT_HEADS_PER_GROUP
ATT_HEADS = len(ATT_GROUPS) * ATT_HEADS_PER_GROUP
ATT_QKV = ATT_HEADS * ATT_HEAD_DIM
ATT_SPAN = 128
N_BUCKETS = 32
MAX_DISTANCE = 2048
NEG_INF = -1e30
RWKV_HEAD = 64
RWKV_HEADS = BR_WIDTH // RWKV_HEAD
DECAY_RANK = 64
ICLR_RANK = 64
RWKV_GN_EPS = 64e-5
CONF_KERNEL = 31
LN_EPS = 1e-5
ALPHA = (2.0 * DEPTH) ** 0.25

LANES = 128
SUBLANES = 8
MIB = 1024 * 1024

BRANCH_IN = 2 * BR_WIDTH + 3 * ATT_QKV + BR_WIDTH + (4 * BR_WIDTH + DECAY_RANK + ICLR_RANK) + 3 * BR_WIDTH
C_GATE_START = BRANCH_IN - 4 * BR_WIDTH
H_SPLIT = 16
H_BLOCKS = 20
H_WIDTH = H_BLOCKS * BR_WIDTH
COL_A_X, COL_A_GATE = 0, 1
COL_Q, COL_K, COL_V, COL_B_GATE = 2, 5, 8, 11
COL_C_R, COL_C_K, COL_C_V, COL_C_LORA = 12, 13, 14, 15
COL_C_GATE, COL_D_VAL, COL_D_GLU, COL_D_GATE = 16, 17, 18, 19

CONF_HALO = 32
RWKV_CHUNK = 64
PAIR = 2 * RWKV_HEAD
N_PAIRS = BR_WIDTH // PAIR

F32 = jnp.float32
BF16 = jnp.bfloat16


def _params(semantics, vmem_mib):
    return pltpu.CompilerParams(dimension_semantics=semantics, vmem_limit_bytes=vmem_mib * MIB)


def _bdot(a, b):
    return jnp.dot(a.astype(BF16), b.astype(BF16), preferred_element_type=F32)


def _bdot_nt(a, b):
    return lax.dot_general(a.astype(BF16), b.astype(BF16), (((1,), (1,)), ((), ())), preferred_element_type=F32)


def _bdot_tn(a, b):
    return lax.dot_general(a.astype(BF16), b.astype(BF16), (((0,), (0,)), ((), ())), preferred_element_type=F32)


NN_DIMS = (((1,), (0,)), ((), ()))
NT_DIMS = (((1,), (1,)), ((), ()))


def _bf16_parts(x, parts):
    out = []
    for _ in range(parts):
        hi = x.astype(BF16)
        out.append(hi)
        x = x - hi.astype(F32)
    return out


def _split_dot(a, b, a_parts, b_parts, dims=NN_DIMS):
    acc = None
    b_terms = _bf16_parts(b, b_parts)
    for i, ai in enumerate(_bf16_parts(a, a_parts)):
        for j, bj in enumerate(b_terms):
            if i + j < max(a_parts, b_parts):
                term = lax.dot_general(ai, bj, dims, preferred_element_type=F32)
                acc = term if acc is None else acc + term
    return acc


def _softplus(z):
    return jnp.maximum(z, 0.0) + jnp.log1p(jnp.exp(-jnp.abs(z)))


def _expm1_nonpos(z):
    u = jnp.exp(z)
    safe = jnp.where(u == 1.0, 0.5, u)
    return jnp.where(u == 1.0, z, jnp.where(u == 0.0, -1.0, (safe - 1.0) * z / jnp.log(safe)))


def _silu(z):
    return z * jax.nn.sigmoid(z)


def _in_proj_kernel(x_ref, w_ref, b_ref, h_ref, xb_ref):
    @pl.when(pl.program_id(1) == 0)
    def _():
        xb_ref[...] = x_ref[...].astype(BF16)

    h_ref[...] = _bdot(xb_ref[...], w_ref[...]) + b_ref[...]


def _h_source_column(block):
    return block * BR_WIDTH if block < H_SPLIT else C_GATE_START + (block - H_SPLIT) * BR_WIDTH


def _in_proj(x, w_all, layer, b, tm=1024, tn=1024):
    s, k = x.shape
    assert (H_SPLIT * BR_WIDTH) % tn == 0 and tn % BR_WIDTH == 0
    per_tile = tn // BR_WIDTH
    starts = np.array([_h_source_column(j * per_tile) // LANES for j in range(H_WIDTH // tn)], np.int32)
    return pl.pallas_call(
        lambda starts_ref, *refs: _in_proj_kernel(*refs),
        grid_spec=pltpu.PrefetchScalarGridSpec(
            num_scalar_prefetch=1,
            grid=(s // tm, H_WIDTH // tn),
            in_specs=[
                pl.BlockSpec((tm, k), lambda i, j, st: (i, 0)),
                pl.BlockSpec((pl.Squeezed(), pl.Element(k), pl.Element(tn)),
                             lambda i, j, st: (layer, 0, st[j] * LANES)),
                pl.BlockSpec((1, tn), lambda i, j, st: (0, j)),
            ],
            out_specs=[
                pl.BlockSpec((tm, tn), lambda i, j, st: (i, j)),
                pl.BlockSpec((tm, k), lambda i, j, st: (i, 0)),
            ],
        ),
        out_shape=[jax.ShapeDtypeStruct((s, H_WIDTH), F32), jax.ShapeDtypeStruct((s, k), BF16)],
        compiler_params=_params(("parallel", "arbitrary"), 56),
        name="in_proj",
    )(jnp.asarray(starts), x, w_all, b)


LRU_IN, CONF_IN, RWKV_IN = 9, 7, 18
LRU_SCRATCH, CONF_SCRATCH, RWKV_SCRATCH = 2, 2, 4


def _lru_init(ebuf, hc):
    ebuf[0:SUBLANES, :] = jnp.zeros((SUBLANES, BR_WIDTH), F32)
    hc[...] = jnp.zeros_like(hc)


def _lru_body(ax_ref, ag_ref, cw_ref, cb_ref, wa_ref, ba_ref, wx_ref, bx_ref, lam_ref, o_ref, ebuf, hc):
    t = ax_ref.shape[0]
    halo = SUBLANES
    x = ax_ref[...]
    ebuf[halo:halo + t, :] = x
    u = cb_ref[...] + jnp.zeros((t, BR_WIDTH), F32)
    for j in range(LRU_CONV):
        u = u + cw_ref[j:j + 1, :] * ebuf[pl.ds(halo - (LRU_CONV - 1) + j, t), :]
    ebuf[0:halo, :] = x[t - halo:t, :]

    gate_r = jax.nn.sigmoid(_bdot(u, wa_ref[...]) + ba_ref[...])
    gate_i = jax.nn.sigmoid(_bdot(u, wx_ref[...]) + bx_ref[...])
    log_a = -LRU_C * gate_r * _softplus(-lam_ref[...])
    a = jnp.exp(log_a)
    b = jnp.sqrt(-_expm1_nonpos(2.0 * log_a)) * (gate_i * u)

    row = lax.broadcasted_iota(jnp.int32, (t, BR_WIDTH), 0)
    shift = 1
    while shift < t:
        valid = row >= shift
        b = jnp.where(valid, a * pltpu.roll(b, shift, 0), 0.0) + b
        a = jnp.where(valid, a * pltpu.roll(a, shift, 0), a)
        shift *= 2
    h = a * hc[0:1, :] + b
    hc[0:1, :] = h[t - 1:t, :]
    o_ref[...] = (h * _silu(ag_ref[...])).astype(BF16)


def _row_block(tile, col):
    return pl.BlockSpec((tile, BR_WIDTH), lambda i: (i, col))


def _resident(shape):
    return pl.BlockSpec(shape, lambda i: (0,) * len(shape))


def _lru_specs(tile):
    vec, mat = _resident((1, BR_WIDTH)), _resident((BR_WIDTH, BR_WIDTH))
    in_specs = [_row_block(tile, COL_A_X), _row_block(tile, COL_A_GATE), _resident((LRU_CONV, BR_WIDTH)), vec,
                mat, vec, mat, vec, vec]
    return in_specs, [pltpu.VMEM((tile + SUBLANES, BR_WIDTH), F32), pltpu.VMEM((SUBLANES, BR_WIDTH), F32)]


def _conf_init(ebuf, shifted):
    ebuf[0:CONF_HALO, :] = jnp.zeros((CONF_HALO, BR_WIDTH), F32)


def _conf_body(val_ref, glu_ref, gate_ref, w_ref, b_ref, g_ref, beta_ref, o_ref, ebuf, shifted):
    t = val_ref.shape[0]
    halo = CONF_HALO
    cu = val_ref[...] * jax.nn.sigmoid(glu_ref[...])
    ebuf[halo:halo + t, :] = cu
    for b in range(SUBLANES):
        span = t + (CONF_KERNEL - 1 - b) // SUBLANES * SUBLANES
        shifted[b, 0:span, :] = ebuf[pl.ds(halo - (CONF_KERNEL - 1) + b, span), :]
    acc = b_ref[...] + jnp.zeros((t, BR_WIDTH), F32)
    for j in range(CONF_KERNEL):
        b = j % SUBLANES
        acc = acc + w_ref[j:j + 1, :] * shifted[b, j - b:j - b + t, :]
    ebuf[0:halo, :] = cu[t - halo:t, :]

    mu = jnp.mean(acc, axis=-1, keepdims=True)
    var = jnp.mean(jnp.square(acc - mu), axis=-1, keepdims=True)
    ln = (acc - mu) * lax.rsqrt(var + LN_EPS) * g_ref[...] + beta_ref[...]
    o_ref[...] = (_silu(ln) * _silu(gate_ref[...])).astype(BF16)


def _conf_specs(tile):
    vec = _resident((1, BR_WIDTH))
    in_specs = [_row_block(tile, COL_D_VAL), _row_block(tile, COL_D_GLU), _row_block(tile, COL_D_GATE),
                _resident((CONF_KERNEL, BR_WIDTH)), vec, vec, vec]
    return in_specs, [pltpu.VMEM((tile + CONF_HALO, BR_WIDTH), F32),
                      pltpu.VMEM((SUBLANES, tile + CONF_HALO - SUBLANES, BR_WIDTH), F32)]


def _t5_bucket(dist):
    max_exact = N_BUCKETS // 2
    large = max_exact + (np.log(np.maximum(dist, 1) / max_exact) / math.log(MAX_DISTANCE / max_exact)
                         * (N_BUCKETS - max_exact)).astype(np.int32)
    large = np.minimum(large, N_BUCKETS - 1)
    return np.where(dist < max_exact, dist, large).astype(np.int32)


def _bucket_index():
    qi = np.arange(ATT_SPAN)[:, None]
    kj = np.arange(2 * ATT_SPAN)[None, :]
    dist = qi + ATT_SPAN - kj
    valid = (dist >= 0) & (dist <= ATT_SPAN)
    per_group = [np.where(valid, _t5_bucket(np.clip(dist, 0, ATT_SPAN) * dil), -1) for _, dil in ATT_GROUPS]
    return np.stack(per_group).astype(np.int32)


def _bias_kernel(table_ref, bucket_ref, o_ref):
    head = pl.program_id(0)
    bucket = bucket_ref[...]
    acc = jnp.full(bucket.shape, NEG_INF, F32)
    for bkt in range(N_BUCKETS):
        acc = jnp.where(bucket == bkt, table_ref[bkt, head], acc)
    o_ref[...] = acc


def _attn_bias(table):
    blk = (None, ATT_SPAN, 2 * ATT_SPAN)
    return pl.pallas_call(
        _bias_kernel,
        grid=(ATT_HEADS,),
        in_specs=[pl.BlockSpec(memory_space=pltpu.SMEM),
                  pl.BlockSpec(blk, lambda hd: (hd // ATT_HEADS_PER_GROUP, 0, 0))],
        out_specs=pl.BlockSpec(blk, lambda hd: (hd, 0, 0)),
        out_shape=jax.ShapeDtypeStruct((ATT_HEADS, ATT_SPAN, 2 * ATT_SPAN), F32),
        compiler_params=_params(("parallel",), 32),
        name="attn_bias",
    )(table, jnp.asarray(_bucket_index()))


ATT_DIRECT_STRIDE = 4


def _residue_reader(ref, slab, dilation):
    if dilation == 1:
        return lambda b, r: ref[b * ATT_SPAN:(b + 1) * ATT_SPAN, :]
    if dilation <= ATT_DIRECT_STRIDE:
        return lambda b, r: ref[pl.ds(b * ATT_SPAN * dilation + r, ATT_SPAN, stride=dilation), :]
    inner, outer = ATT_DIRECT_STRIDE, dilation // ATT_DIRECT_STRIDE
    per = ref.shape[0] // inner
    for r0 in range(inner):
        slab[r0] = ref[pl.ds(r0, per, stride=inner), :]
    return lambda b, r: slab[r % inner, pl.ds(b * ATT_SPAN * outer + r // inner, ATT_SPAN, stride=outer), :]


def _residue_writer(ref, slab, dilation):
    if dilation == 1:
        def write(b, r, val):
            ref[b * ATT_SPAN:(b + 1) * ATT_SPAN, :] = val
        return write, lambda: None
    if dilation <= ATT_DIRECT_STRIDE:
        def write(b, r, val):
            ref[pl.ds(b * ATT_SPAN * dilation + r, ATT_SPAN, stride=dilation), :] = val
        return write, lambda: None
    inner, outer = ATT_DIRECT_STRIDE, dilation // ATT_DIRECT_STRIDE
    per = ref.shape[0] // inner

    def write(b, r, val):
        slab[r % inner, pl.ds(b * ATT_SPAN * outer + r // inner, ATT_SPAN, stride=outer), :] = val

    def flush():
        for r0 in range(inner):
            ref[pl.ds(r0, per, stride=inner), :] = slab[r0]

    return write, flush


ATT_ROWS = ATT_SPAN * max(dil for _, dil in ATT_GROUPS)
ATT_GROUP_IN = 6
ATT_SLABS = 8
ATT_UNITS_PER_STAGE = 4


def _attn_group_outputs(q_ref, kc_ref, kp_ref, vc_ref, vp_ref, bias_ref, o_nat, m_nat, d_nat, slabs, first, dilation):
    blocks = ATT_ROWS // (ATT_SPAN * dilation)
    scale = ATT_HEAD_DIM ** -0.5
    read_q, read_kc, read_kp, read_vc, read_vp = (
        _residue_reader(ref, slab, dilation) for ref, slab in zip((q_ref, kc_ref, kp_ref, vc_ref, vp_ref), slabs[:5]))
    (write_o, flush_o), (write_m, flush_m), (write_d, flush_d) = (
        _residue_writer(ref, slab, dilation) for ref, slab in zip((o_nat, m_nat, d_nat), slabs[5:]))
    def key(b, r):
        return (read_kp(0, r) if b < 0 else read_kc(b, r)).astype(BF16)

    def value(b, r):
        return (read_vp(0, r) if b < 0 else read_vc(b, r)).astype(BF16)

    bias_p = bias_ref[:, 0:ATT_SPAN]
    bias_c = bias_ref[:, ATT_SPAN:2 * ATT_SPAN]
    full = (ATT_SPAN, LANES)
    def logits(u):
        q = read_q(u["b"], u["r"]).astype(BF16)
        lp = _bdot_nt(q, key(u["b"] - 1, u["r"])) * scale + bias_p
        u["lp"] = jnp.where(first, NEG_INF, lp) if u["b"] == 0 else lp
        u["lc"] = _bdot_nt(q, key(u["b"], u["r"])) * scale + bias_c

    def row_max(u):
        u["m"] = jnp.max(jnp.maximum(u["lp"], u["lc"]), axis=-1, keepdims=True)

    def weights(u):
        u["pp"] = jnp.exp(u.pop("lp") - u["m"])
        u["pc"] = jnp.exp(u.pop("lc") - u["m"])

    def denominator(u):
        u["den"] = jnp.sum(u["pp"] + u["pc"], axis=-1, keepdims=True)

    def outputs(u):
        b, r = u["b"], u["r"]
        write_o(b, r, (_bdot(u.pop("pp"), value(b - 1, r)) + _bdot(u.pop("pc"), value(b, r))) / u["den"])
        write_m(b, r, jnp.broadcast_to(u["m"], full))
        write_d(b, r, jnp.broadcast_to(u["den"], full))

    stages = (logits, row_max, weights, denominator, outputs)
    units = [dict(b=b, r=r) for b in range(blocks) for r in range(dilation)]
    groups = [units[i:i + ATT_UNITS_PER_STAGE] for i in range(0, len(units), ATT_UNITS_PER_STAGE)]
    for tick in range(len(groups) + len(stages) - 1):
        for s, stage in reversed(list(enumerate(stages))):
            if 0 <= tick - s < len(groups):
                for u in groups[tick - s]:
                    stage(u)
    flush_o()
    flush_m()
    flush_d()


def _attention_kernel(*refs):
    n_in = ATT_GROUP_IN * len(ATT_GROUPS)
    gate_ref, y_ref = refs[n_in], refs[n_in + 1]
    o_nat, m_nat, d_nat = refs[n_in + 2:n_in + 5]
    slabs = refs[n_in + 5:]
    first = pl.program_id(0) == 0
    for g, (_, dil) in enumerate(ATT_GROUPS):
        _attn_group_outputs(*refs[ATT_GROUP_IN * g:ATT_GROUP_IN * (g + 1)], o_nat.at[g], m_nat.at[g], d_nat.at[g],
                            slabs, first, dil)
    ms = [m_nat[g] for g in range(len(ATT_GROUPS))]
    m_all = functools.reduce(jnp.maximum, ms)
    wts = [jnp.exp(m - m_all) * d_nat[g] for g, m in enumerate(ms)]
    num = sum(w * o_nat[g] for g, w in enumerate(wts))
    y_ref[...] = (num / sum(wts) * _silu(gate_ref[...])).astype(BF16)


def _attention(h, bias):
    s = h.shape[0]
    heads = ATT_HEADS_PER_GROUP
    in_specs, operands = [], []
    for g, (_, dil) in enumerate(ATT_GROUPS):
        blk = ATT_SPAN * dil
        per_step = ATT_ROWS // blk

        def spec(col, prev, g=g, blk=blk, per_step=per_step):
            base = (col + g) * heads
            if prev:
                return pl.BlockSpec((blk, ATT_HEAD_DIM), lambda n, hd: (jnp.maximum(n * per_step - 1, 0), base + hd))
            return pl.BlockSpec((ATT_ROWS, ATT_HEAD_DIM), lambda n, hd: (n, base + hd))

        in_specs += [spec(COL_Q, False), spec(COL_K, False), spec(COL_K, True), spec(COL_V, False), spec(COL_V, True),
                     pl.BlockSpec((None, ATT_SPAN, 2 * ATT_SPAN), lambda n, hd, g=g: (g * heads + hd, 0, 0))]
        operands += [h, h, h, h, h, bias]
    in_specs.append(pl.BlockSpec((ATT_ROWS, ATT_HEAD_DIM), lambda n, hd: (n, COL_B_GATE * heads + hd)))
    token_order = pltpu.VMEM((len(ATT_GROUPS), ATT_ROWS, LANES), F32)
    slab = pltpu.VMEM((ATT_DIRECT_STRIDE, ATT_ROWS // ATT_DIRECT_STRIDE, LANES), F32)
    return pl.pallas_call(
        _attention_kernel,
        grid=(s // ATT_ROWS, heads),
        in_specs=in_specs,
        out_specs=pl.BlockSpec((ATT_ROWS, ATT_HEAD_DIM), lambda n, hd: (n, hd)),
        out_shape=jax.ShapeDtypeStruct((s, BR_WIDTH), BF16),
        scratch_shapes=[token_order] * 3 + [slab] * ATT_SLABS,
        compiler_params=_params(("parallel", "arbitrary"), 56),
        name="dilated_attention",
    )(*operands, h)


def _head_sums(x):
    ri = lax.broadcasted_iota(jnp.int32, (PAIR, PAIR), 0)
    ci = lax.broadcasted_iota(jnp.int32, (PAIR, PAIR), 1)
    same_head = jnp.where((ri < RWKV_HEAD) == (ci < RWKV_HEAD), 1.0, 0.0).astype(BF16)
    return jnp.concatenate([_split_dot(x[:, p * PAIR:(p + 1) * PAIR], same_head, 2, 1) for p in range(N_PAIRS)], axis=1)


def _rwkv_prepare(r_ref, k_ref, v_ref, lora_ref, mu_r, mu_k, mu_v, mu_l, w0_ref, wup_ref, a0_ref, aup_ref,
                  kk_ref, ka_ref, rk_ref, carry, carry_l):
    t = r_ref.shape[0]

    def shift_mix(x, mu, prev_row):
        row = lax.broadcasted_iota(jnp.int32, x.shape, 0)
        x_prev = jnp.where(row == 0, prev_row, pltpu.roll(x, 1, 0))
        return x + mu * (x_prev - x)

    r_in, k_in, v_in, l_in = r_ref[...], k_ref[...], v_ref[...], lora_ref[...]
    r = shift_mix(r_in, mu_r[...], carry[0:1, :])
    kx = shift_mix(k_in, mu_k[...], carry[1:2, :])
    vv = shift_mix(v_in, mu_v[...], carry[2:3, :])
    lo = shift_mix(l_in, mu_l[...], carry_l[0:1, :])
    carry[0:1, :] = r_in[t - 1:t, :]
    carry[1:2, :] = k_in[t - 1:t, :]
    carry[2:3, :] = v_in[t - 1:t, :]
    carry_l[0:1, :] = l_in[t - 1:t, :]

    w_log = -_softplus(-(w0_ref[...] + _bdot(jnp.tanh(lo), wup_ref[...]))) - 0.5
    log_decay = -jnp.exp(w_log)
    a_icl = jax.nn.sigmoid(a0_ref[...] + _bdot(lo, aup_ref[...]))

    kk = kx * kk_ref[...]
    kk = kk / jnp.maximum(jnp.sqrt(_head_sums(kk * kk)), 1e-12)
    kc = kx * (1.0 + (a_icl - 1.0) * ka_ref[...])
    bonus = _head_sums(r * kc * rk_ref[...]) * vv
    return log_decay, r, kc, vv, -kk, kk * a_icl, bonus


def _stack_heads(x):
    lane = lax.broadcasted_iota(jnp.int32, x.shape, 1)
    return jnp.concatenate([jnp.where(lane < RWKV_HEAD, x, 0.0), jnp.where(lane >= RWKV_HEAD, x, 0.0)], axis=0)


def _time_indices():
    t = lax.broadcasted_iota(jnp.int32, (RWKV_CHUNK, PAIR), 0)
    s = lax.broadcasted_iota(jnp.int32, (RWKV_CHUNK, PAIR), 1) & (RWKV_CHUNK - 1)
    return t, s


def _unit_lower_inverse(a_strict):
    ti, si = _time_indices()

    def same_block(bits):
        return (ti >> bits) == (si >> bits)

    pw = [jnp.where(same_block(4), a, 0.0) for a in a_strict]
    x = [jnp.where(ti == si, 1.0, 0.0) + p for p in pw]
    for _ in range(3):
        pw = [_bdot(p, _stack_heads(p)) for p in pw]
        x = [xi + _bdot(xi, _stack_heads(p)) for xi, p in zip(x, pw)]
    for bits in (5, 6):
        join = same_block(bits) & jnp.logical_not(same_block(bits - 1))
        xe = [_bdot(xi, _stack_heads(jnp.where(join, a, 0.0))) for xi, a in zip(x, a_strict)]
        x = [xi + _bdot(t, _stack_heads(xi)) for xi, t in zip(x, xe)]
    return x


def _rwkv_chunk_transforms(lw_all, r_all, k_all, v_all, a_all, b_all):
    c = RWKV_CHUNK
    n = 2 * c
    ti = lax.broadcasted_iota(jnp.int32, (c, c), 0)
    si = lax.broadcasted_iota(jnp.int32, (c, c), 1)
    lower_ones = jnp.where(si <= ti, 1.0, 0.0)
    tt, ss = _time_indices()
    strict = tt > ss
    incl = tt >= ss
    ri = lax.broadcasted_iota(jnp.int32, (n, n), 0)
    ci = lax.broadcasted_iota(jnp.int32, (n, n), 1)
    same_head = (ri < RWKV_HEAD) == (ci < RWKV_HEAD)
    eye = ri == ci

    units = [(ch, p) for ch in range(lw_all.shape[0] // c) for p in range(N_PAIRS)]
    each = lambda f, *cols: [f(*args) for args in zip(*cols)]

    def split(x):
        return [x[ch * c:(ch + 1) * c, p * PAIR:(p + 1) * PAIR] for ch, p in units]

    lw, r, k, v, a, b = (split(x) for x in (lw_all, r_all, k_all, v_all, a_all, b_all))
    cs = each(lambda x: _split_dot(lower_ones, x, 1, 3), lw)
    c_end = each(lambda x: x[c - 1:c, :], cs)
    r_d = each(lambda x, y: x * jnp.exp(y), r, cs)
    a_d = each(lambda x, y, z: x * jnp.exp(y - z), a, cs, lw)
    b_i = each(lambda x, y: x * jnp.exp(-y), b, cs)
    k_i = each(lambda x, y: x * jnp.exp(-y), k, cs)
    b_e = each(lambda x, y, e: x * jnp.exp(e - y), b, cs, c_end)
    k_e = each(lambda x, y, e: x * jnp.exp(e - y), k, cs, c_end)
    v_s = each(_stack_heads, v)

    aa = each(lambda ad, rd, bi, ki: _bdot_nt(jnp.concatenate([ad, rd], axis=0),
                                              jnp.concatenate([_stack_heads(bi), _stack_heads(ki)], axis=0)),
              a_d, r_d, b_i, k_i)
    a_ab = each(lambda x: jnp.where(strict, x[0:c, 0:n], 0.0), aa)
    a_ak = each(lambda x: jnp.where(strict, x[0:c, n:2 * n], 0.0), aa)
    a_rb = each(lambda x: jnp.where(incl, x[c:n, 0:n], 0.0), aa)
    a_rk = each(lambda x: jnp.where(incl, x[c:n, n:2 * n], 0.0), aa)

    minv = _unit_lower_inverse(a_ab)
    w = each(lambda m, ad: _bdot(m, _stack_heads(ad)), minv, a_d)
    t1 = each(_bdot, a_ak, v_s)
    uv = each(lambda m, x: _bdot(m, _stack_heads(x)), minv, t1)
    q = each(lambda rd, x, y: rd + _bdot(x, _stack_heads(y)), r_d, a_rb, w)
    yc = each(lambda x, y, z, t: _bdot(x, _stack_heads(y)) + _bdot(z, t), a_rb, uv, a_rk, v_s)
    g = each(lambda e, x, y: jnp.where(eye, jnp.exp(e), 0.0) + jnp.where(same_head, _bdot_tn(x, y), 0.0), c_end, w, b_e)
    z = each(lambda u_, v_, be, ke: jnp.where(same_head, _bdot_tn(jnp.concatenate([u_, v_], axis=0),
                                                                    jnp.concatenate([be, ke], axis=0)), 0.0),
             uv, v, b_e, k_e)
    return {unit: terms for unit, *terms in zip(units, q, yc, g, z)}


def _rwkv_init(carry, carry_l, state, ybuf):
    carry[...] = jnp.zeros_like(carry)
    carry_l[...] = jnp.zeros_like(carry_l)
    state[...] = jnp.zeros_like(state)


def _rwkv_body(r_ref, k_ref, v_ref, lora_ref, gate_ref, mu_r, mu_k, mu_v, mu_l, w0_ref, wup_ref, a0_ref, aup_ref,
               kk_ref, ka_ref, rk_ref, gn_g, gn_b, o_ref, carry, carry_l, state, ybuf):
    c = RWKV_CHUNK
    chunks = r_ref.shape[0] // c
    *scan_inputs, bonus = _rwkv_prepare(r_ref, k_ref, v_ref, lora_ref, mu_r, mu_k, mu_v, mu_l, w0_ref, wup_ref,
                                        a0_ref, aup_ref, kk_ref, ka_ref, rk_ref, carry, carry_l)
    terms = _rwkv_chunk_transforms(*scan_inputs)

    pairs = range(N_PAIRS)
    sts = [state[:, p * PAIR:(p + 1) * PAIR] for p in pairs]
    starts = []
    for ch in range(chunks):
        starts.append(sts)
        sts = [_split_dot(sts[p], terms[ch, p][2], 2, 2) + terms[ch, p][3] for p in pairs]
    for p in pairs:
        state[:, p * PAIR:(p + 1) * PAIR] = sts[p]
    for ch in range(chunks):
        for p in pairs:
            q, yc = terms[ch, p][0], terms[ch, p][1]
            ybuf[ch * c:(ch + 1) * c, p * PAIR:(p + 1) * PAIR] = _split_dot(q, starts[ch][p], 2, 2, NT_DIMS) + yc

    wy = ybuf[...]
    inv_n = 1.0 / RWKV_HEAD
    mu = _head_sums(wy) * inv_n
    d = wy - mu
    var = _head_sums(d * d) * inv_n
    wy = d * lax.rsqrt(var + RWKV_GN_EPS) * gn_g[...] + gn_b[...]
    o_ref[...] = ((wy + bonus) * _silu(gate_ref[...])).astype(BF16)


def _rwkv_specs(tile):
    vec, lora_w = _resident((1, BR_WIDTH)), _resident((LANES, BR_WIDTH))
    in_specs = [_row_block(tile, COL_C_R), _row_block(tile, COL_C_K), _row_block(tile, COL_C_V),
                pl.BlockSpec((tile, LANES), lambda i: (i, COL_C_LORA * (BR_WIDTH // LANES))),
                _row_block(tile, COL_C_GATE),
                vec, vec, vec, _resident((1, LANES)), vec, lora_w, vec, lora_w, vec, vec, vec, vec, vec]
    return in_specs, [pltpu.VMEM((SUBLANES, BR_WIDTH), F32), pltpu.VMEM((SUBLANES, LANES), F32),
                      pltpu.VMEM((PAIR, BR_WIDTH), F32), pltpu.VMEM((tile, BR_WIDTH), F32)]


def _recurrent_mixers_kernel(*refs):
    n_in = LRU_IN + CONF_IN + RWKV_IN
    ins, (o_a, o_d, o_c), scratch = refs[:n_in], refs[n_in:n_in + 3], refs[n_in + 3:]
    lru_in, conf_in, rwkv_in = ins[:LRU_IN], ins[LRU_IN:LRU_IN + CONF_IN], ins[LRU_IN + CONF_IN:]
    lru_s = scratch[:LRU_SCRATCH]
    conf_s = scratch[LRU_SCRATCH:LRU_SCRATCH + CONF_SCRATCH]
    rwkv_s = scratch[LRU_SCRATCH + CONF_SCRATCH:]

    @pl.when(pl.program_id(0) == 0)
    def _():
        _lru_init(*lru_s)
        _conf_init(*conf_s)
        _rwkv_init(*rwkv_s)

    _rwkv_body(*rwkv_in, o_c, *rwkv_s)
    _conf_body(*conf_in, o_d, *conf_s)
    _lru_body(*lru_in, o_a, *lru_s)


def _recurrent_mixers(h, lru_args, conf_args, rwkv_args, tile=4 * RWKV_CHUNK):
    s = h.shape[0]
    (lru_specs, lru_scr), (conf_specs, conf_scr), (rwkv_specs, rwkv_scr) = _lru_specs(tile), _conf_specs(tile), _rwkv_specs(tile)
    assert (len(lru_specs), len(conf_specs), len(rwkv_specs)) == (LRU_IN, CONF_IN, RWKV_IN)
    out = pl.BlockSpec((tile, BR_WIDTH), lambda i: (i, 0))
    return pl.pallas_call(
        _recurrent_mixers_kernel,
        grid=(s // tile,),
        in_specs=lru_specs + conf_specs + rwkv_specs,
        out_specs=[out] * 3,
        out_shape=[jax.ShapeDtypeStruct((s, BR_WIDTH), BF16)] * 3,
        scratch_shapes=lru_scr + conf_scr + rwkv_scr,
        compiler_params=_params(("arbitrary",), 40),
        name="recurrent_mixers",
    )(h, h, *lru_args, h, h, h, *conf_args, h, h, h, h, h, *rwkv_args)


def _mix_kernel(xb_ref, *refs):
    ygs, wms, bms, wbrs = (refs[k * N_BRANCH:(k + 1) * N_BRANCH] for k in range(4))
    o_ref = refs[4 * N_BRANCH]
    xb = xb_ref[...]
    acc = None
    for n in range(N_BRANCH):
        gate = jax.nn.sigmoid(_bdot(xb, wms[n][...]) + bms[n][...])
        val = gate * _bdot(ygs[n][...], wbrs[n][...])
        acc = val if acc is None else acc + val
    o_ref[...] = acc.astype(BF16)


def _mix(xb, ygs, w_all, layer, bm, wbr_all, tm=1024, tn=256):
    s = xb.shape[0]
    nj = D_MODEL // tn
    per_branch = lambda make: [make(n) for n in range(N_BRANCH)]
    return pl.pallas_call(
        _mix_kernel,
        grid=(s // tm, nj),
        in_specs=[pl.BlockSpec((tm, D_MODEL), lambda i, j: (i, 0))]
        + per_branch(lambda n: pl.BlockSpec((tm, BR_WIDTH), lambda i, j: (i, 0)))
        + per_branch(lambda n: pl.BlockSpec((pl.Squeezed(), pl.Element(D_MODEL), pl.Element(tn)),
                                            lambda i, j: (layer, 0, ((BRANCH_IN + n * D_MODEL) // LANES
                                                                     + j * (tn // LANES)) * LANES)))
        + per_branch(lambda n: pl.BlockSpec((1, tn), lambda i, j: (0, n * nj + j)))
        + per_branch(lambda n: pl.BlockSpec((None, None, BR_WIDTH, tn), lambda i, j: (layer, n, 0, j))),
        out_specs=pl.BlockSpec((tm, tn), lambda i, j: (i, j)),
        out_shape=jax.ShapeDtypeStruct((s, D_MODEL), BF16),
        compiler_params=_params(("parallel", "arbitrary"), 48),
        name="branch_mix",
    )(xb, *ygs, *([w_all] * N_BRANCH), *([bm] * N_BRANCH), *([wbr_all] * N_BRANCH))


def _out_kernel(mixed_ref, x_ref, w_ref, g_ref, b_ref, o_ref):
    y = ALPHA * x_ref[...] + jnp.dot(mixed_ref[...], w_ref[...], preferred_element_type=F32)
    mu = jnp.mean(y, axis=-1, keepdims=True)
    var = jnp.mean(jnp.square(y - mu), axis=-1, keepdims=True)
    o_ref[...] = (y - mu) * lax.rsqrt(var + LN_EPS) * g_ref[...] + b_ref[...]


def _out_proj(mixed, x, w, g, b, tm=512):
    s = x.shape[0]
    row = pl.BlockSpec((tm, D_MODEL), lambda i: (i, 0))
    vec = pl.BlockSpec((1, D_MODEL), lambda i: (0, 0))
    return pl.pallas_call(
        _out_kernel,
        grid=(s // tm,),
        in_specs=[row, row, pl.BlockSpec((D_MODEL, D_MODEL), lambda i: (0, 0)), vec, vec],
        out_specs=row,
        out_shape=jax.ShapeDtypeStruct((s, D_MODEL), F32),
        compiler_params=_params(("parallel",), 48),
        name="out_proj_ln",
    )(mixed, x, w, g, b)


def _block_diag(w):
    blocks, n, _ = w.shape
    eye = jnp.eye(blocks, dtype=w.dtype)
    return (eye[:, None, :, None] * w[:, :, None, :]).reshape(blocks * n, blocks * n)


def _layer(x, att_bias, w_in_all, w_br_all, layer, b_in, lru_conv_w, lru_conv_b, lru_gate_a_w, lru_gate_a_b, lru_gate_x_w, lru_gate_x_b,
           lru_lambda, rwkv_mu, rwkv_w0, rwkv_w_up, rwkv_a0, rwkv_a_up, rwkv_k_k, rwkv_k_a, rwkv_r_k, rwkv_gn_g,
           rwkv_gn_b, conf_dw_w, conf_dw_b, conf_ln_g, conf_ln_b, w_out, ln_g, ln_b):
    vec = lambda t: t.reshape(1, -1)
    b_h = jnp.concatenate([b_in[:H_SPLIT * BR_WIDTH], b_in[C_GATE_START:BRANCH_IN]])
    h, xb = _in_proj(x, w_in_all, layer, vec(b_h))

    yg_b = _attention(h, att_bias)

    mu = rwkv_mu
    zpad = jnp.zeros((DECAY_RANK, BR_WIDTH), F32)
    wup = jnp.concatenate([rwkv_w_up, zpad], axis=0).astype(BF16)
    aup = jnp.concatenate([zpad, rwkv_a_up], axis=0).astype(BF16)
    lru_args = (lru_conv_w, vec(lru_conv_b), _block_diag(lru_gate_a_w).astype(BF16), vec(lru_gate_a_b),
                _block_diag(lru_gate_x_w).astype(BF16), vec(lru_gate_x_b), vec(lru_lambda))
    conf_args = (conf_dw_w, vec(conf_dw_b), vec(conf_ln_g), vec(conf_ln_b))
    rwkv_args = (vec(mu[:BR_WIDTH]), vec(mu[BR_WIDTH:2 * BR_WIDTH]), vec(mu[2 * BR_WIDTH:3 * BR_WIDTH]),
                 vec(mu[3 * BR_WIDTH:]), vec(rwkv_w0), wup, vec(rwkv_a0), aup, vec(rwkv_k_k), vec(rwkv_k_a),
                 vec(rwkv_r_k), vec(rwkv_gn_g), vec(rwkv_gn_b))
    yg_a, yg_d, yg_c = _recurrent_mixers(h, lru_args, conf_args, rwkv_args)

    mixed = _mix(xb, (yg_a, yg_b, yg_c, yg_d), w_in_all, layer, vec(b_in[BRANCH_IN:]), w_br_all)
    return _out_proj(mixed, x, w_out.astype(BF16), vec(ln_g), vec(ln_b))


def kernel(x, att_rel_bias, w_in, b_in, lru_conv_w, lru_conv_b, lru_gate_a_w, lru_gate_a_b, lru_gate_x_w, lru_gate_x_b, lru_lambda, rwkv_mu, rwkv_w0, rwkv_w_up, rwkv_a0, rwkv_a_up, rwkv_k_k, rwkv_k_a, rwkv_r_k, rwkv_gn_g, rwkv_gn_b, conf_dw_w, conf_dw_b, conf_ln_g, conf_ln_b, w_br, w_out, ln_g, ln_b):
    bsz, s, d = x.shape
    assert bsz == 1 and d == D_MODEL and s % (16 * ATT_SPAN) == 0
    per_layer = (b_in, lru_conv_w, lru_conv_b, lru_gate_a_w, lru_gate_a_b, lru_gate_x_w, lru_gate_x_b,
                 lru_lambda, rwkv_mu, rwkv_w0, rwkv_w_up, rwkv_a0, rwkv_a_up, rwkv_k_k, rwkv_k_a, rwkv_r_k,
                 rwkv_gn_g, rwkv_gn_b, conf_dw_w, conf_dw_b, conf_ln_g, conf_ln_b, w_out, ln_g, ln_b)
    y = x.reshape(s, d)
    att_bias = _attn_bias(att_rel_bias)
    for l in range(DEPTH):
        y = _layer(y, att_bias, w_in, w_br, l, *(t[l] for t in per_layer))
    return y.reshape(bsz, s, d)
```

```python
import functools
import math

import numpy as np
import jax
import jax.numpy as jnp
from jax import lax
from jax.experimental import pallas as pl
from jax.experimental.pallas import tpu as pltpu

D_MODEL = 2048
DEPTH = 2
N_BRANCH = 4
BR_WIDTH = 512
LRU_BLOCKS = 8
LRU_BLOCK = BR_WIDTH // LRU_BLOCKS
LRU_CONV = 4
LRU_C = 8.0
ATT_GROUPS = ((128, 1), (512, 4), (2048, 16))
ATT_HEADS_PER_GROUP = 4
ATT_HEAD_DIM = BR_WIDTH // ATT_HEADS_PER_GROUP
ATT_HEADS = len(ATT_GROUPS) * ATT_HEADS_PER_GROUP
ATT_QKV = ATT_HEADS * ATT_HEAD_DIM
ATT_SPAN = 128
N_BUCKETS = 32
MAX_DISTANCE = 2048
NEG_INF = -1e30
RWKV_HEAD = 64
RWKV_HEADS = BR_WIDTH // RWKV_HEAD
DECAY_RANK = 64
ICLR_RANK = 64
RWKV_GN_EPS = 64e-5
CONF_KERNEL = 31
LN_EPS = 1e-5
ALPHA = (2.0 * DEPTH) ** 0.25

LANES = 128
SUBLANES = 8
MIB = 1024 * 1024

BRANCH_IN = 2 * BR_WIDTH + 3 * ATT_QKV + BR_WIDTH + (4 * BR_WIDTH + DECAY_RANK + ICLR_RANK) + 3 * BR_WIDTH
C_GATE_START = BRANCH_IN - 4 * BR_WIDTH
H_SPLIT = 16
H_BLOCKS = 20
H_WIDTH = H_BLOCKS * BR_WIDTH
COL_A_X, COL_A_GATE = 0, 1
COL_Q, COL_K, COL_V, COL_B_GATE = 2, 5, 8, 11
COL_C_R, COL_C_K, COL_C_V, COL_C_LORA = 12, 13, 14, 15
COL_C_GATE, COL_D_VAL, COL_D_GLU, COL_D_GATE = 16, 17, 18, 19

CONF_HALO = 32
RWKV_CHUNK = 64
PAIR = 2 * RWKV_HEAD
N_PAIRS = BR_WIDTH // PAIR

F32 = jnp.float32
BF16 = jnp.bfloat16


def _params(semantics, vmem_mib):
    return pltpu.CompilerParams(dimension_semantics=semantics, vmem_limit_bytes=vmem_mib * MIB)


def _bdot(a, b):
    return jnp.dot(a.astype(BF16), b.astype(BF16), preferred_element_type=F32)


def _bdot_nt(a, b):
    return lax.dot_general(a.astype(BF16), b.astype(BF16), (((1,), (1,)), ((), ())), preferred_element_type=F32)


def _bdot_tn(a, b):
    return lax.dot_general(a.astype(BF16), b.astype(BF16), (((0,), (0,)), ((), ())), preferred_element_type=F32)


NN_DIMS = (((1,), (0,)), ((), ()))
NT_DIMS = (((1,), (1,)), ((), ()))


def _bf16_parts(x, parts):
    out = []
    for _ in range(parts):
        hi = x.astype(BF16)
        out.append(hi)
        x = x - hi.astype(F32)
    return out


def _split_dot(a, b, a_parts, b_parts, dims=NN_DIMS):
    acc = None
    b_terms = _bf16_parts(b, b_parts)
    for i, ai in enumerate(_bf16_parts(a, a_parts)):
        for j, bj in enumerate(b_terms):
            if i + j < max(a_parts, b_parts):
                term = lax.dot_general(ai, bj, dims, preferred_element_type=F32)
                acc = term if acc is None else acc + term
    return acc


def _softplus(z):
    return jnp.maximum(z, 0.0) + jnp.log1p(jnp.exp(-jnp.abs(z)))


def _expm1_nonpos(z):
    u = jnp.exp(z)
    safe = jnp.where(u == 1.0, 0.5, u)
    return jnp.where(u == 1.0, z, jnp.where(u == 0.0, -1.0, (safe - 1.0) * z / jnp.log(safe)))


def _silu(z):
    return z * jax.nn.sigmoid(z)


def _in_proj_kernel(x_ref, w_ref, b_ref, h_ref, xb_ref):
    @pl.when(pl.program_id(1) == 0)
    def _():
        xb_ref[...] = x_ref[...].astype(BF16)

    h_ref[...] = _bdot(xb_ref[...], w_ref[...]) + b_ref[...]


def _h_source_column(block):
    return block * BR_WIDTH if block < H_SPLIT else C_GATE_START + (block - H_SPLIT) * BR_WIDTH


def _in_proj(x, w_all, layer, b, tm=1024, tn=1024):
    s, k = x.shape
    assert (H_SPLIT * BR_WIDTH) % tn == 0 and tn % BR_WIDTH == 0
    per_tile = tn // BR_WIDTH
    starts = np.array([_h_source_column(j * per_tile) // LANES for j in range(H_WIDTH // tn)], np.int32)
    return pl.pallas_call(
        lambda starts_ref, *refs: _in_proj_kernel(*refs),
        grid_spec=pltpu.PrefetchScalarGridSpec(
            num_scalar_prefetch=1,
            grid=(s // tm, H_WIDTH // tn),
            in_specs=[
                pl.BlockSpec((tm, k), lambda i, j, st: (i, 0)),
                pl.BlockSpec((pl.Squeezed(), pl.Element(k), pl.Element(tn)),
                             lambda i, j, st: (layer, 0, st[j] * LANES)),
                pl.BlockSpec((None, 1, tn), lambda i, j, st: (layer, 0, j)),
            ],
            out_specs=[
                pl.BlockSpec((tm, tn), lambda i, j, st: (i, j)),
                pl.BlockSpec((tm, k), lambda i, j, st: (i, 0)),
            ],
        ),
        out_shape=[jax.ShapeDtypeStruct((s, H_WIDTH), F32), jax.ShapeDtypeStruct((s, k), BF16)],
        compiler_params=_params(("parallel", "arbitrary"), 56),
        name="in_proj",
    )(jnp.asarray(starts), x, w_all, b)


LRU_IN, CONF_IN, RWKV_IN = 9, 7, 18
LRU_SCRATCH, CONF_SCRATCH, RWKV_SCRATCH = 2, 2, 4


def _lru_init(ebuf, hc):
    ebuf[0:SUBLANES, :] = jnp.zeros((SUBLANES, BR_WIDTH), F32)
    hc[...] = jnp.zeros_like(hc)


def _lru_body(ax_ref, ag_ref, cw_ref, cb_ref, wa_ref, ba_ref, wx_ref, bx_ref, lam_ref, o_ref, ebuf, hc):
    t = ax_ref.shape[0]
    halo = SUBLANES
    x = ax_ref[...]
    ebuf[halo:halo + t, :] = x
    u = cb_ref[...] + jnp.zeros((t, BR_WIDTH), F32)
    for j in range(LRU_CONV):
        u = u + cw_ref[j:j + 1, :] * ebuf[pl.ds(halo - (LRU_CONV - 1) + j, t), :]
    ebuf[0:halo, :] = x[t - halo:t, :]

    gate_r = jax.nn.sigmoid(_bdot(u, wa_ref[...]) + ba_ref[...])
    gate_i = jax.nn.sigmoid(_bdot(u, wx_ref[...]) + bx_ref[...])
    log_a = -LRU_C * gate_r * _softplus(-lam_ref[...])
    a = jnp.exp(log_a)
    b = jnp.sqrt(-_expm1_nonpos(2.0 * log_a)) * (gate_i * u)

    row = lax.broadcasted_iota(jnp.int32, (t, BR_WIDTH), 0)
    shift = 1
    while shift < t:
        valid = row >= shift
        b = jnp.where(valid, a * pltpu.roll(b, shift, 0), 0.0) + b
        a = jnp.where(valid, a * pltpu.roll(a, shift, 0), a)
        shift *= 2
    h = a * hc[0:1, :] + b
    hc[0:1, :] = h[t - 1:t, :]
    o_ref[...] = (h * _silu(ag_ref[...])).astype(BF16)


def _row_block(tile, col):
    return pl.BlockSpec((tile, BR_WIDTH), lambda i: (i, col))


def _layer_param(layer, shape):
    return pl.BlockSpec((None,) + shape, lambda *_: (layer,) + (0,) * len(shape))


def _lru_specs(tile, layer):
    vec, mat = _layer_param(layer, (1, BR_WIDTH)), _layer_param(layer, (BR_WIDTH, BR_WIDTH))
    in_specs = [_row_block(tile, COL_A_X), _row_block(tile, COL_A_GATE), _layer_param(layer, (LRU_CONV, BR_WIDTH)),
                vec, mat, vec, mat, vec, vec]
    return in_specs, [pltpu.VMEM((tile + SUBLANES, BR_WIDTH), F32), pltpu.VMEM((SUBLANES, BR_WIDTH), F32)]


def _conf_init(ebuf, shifted):
    ebuf[0:CONF_HALO, :] = jnp.zeros((CONF_HALO, BR_WIDTH), F32)


def _conf_body(val_ref, glu_ref, gate_ref, w_ref, b_ref, g_ref, beta_ref, o_ref, ebuf, shifted):
    t = val_ref.shape[0]
    halo = CONF_HALO
    cu = val_ref[...] * jax.nn.sigmoid(glu_ref[...])
    ebuf[halo:halo + t, :] = cu
    for b in range(SUBLANES):
        span = t + (CONF_KERNEL - 1 - b) // SUBLANES * SUBLANES
        shifted[b, 0:span, :] = ebuf[pl.ds(halo - (CONF_KERNEL - 1) + b, span), :]
    acc = b_ref[...] + jnp.zeros((t, BR_WIDTH), F32)
    for j in range(CONF_KERNEL):
        b = j % SUBLANES
        acc = acc + w_ref[j:j + 1, :] * shifted[b, j - b:j - b + t, :]
    ebuf[0:halo, :] = cu[t - halo:t, :]

    mu = jnp.mean(acc, axis=-1, keepdims=True)
    var = jnp.mean(jnp.square(acc - mu), axis=-1, keepdims=True)
    ln = (acc - mu) * lax.rsqrt(var + LN_EPS) * g_ref[...] + beta_ref[...]
    o_ref[...] = (_silu(ln) * _silu(gate_ref[...])).astype(BF16)


def _conf_specs(tile, layer):
    vec = _layer_param(layer, (1, BR_WIDTH))
    in_specs = [_row_block(tile, COL_D_VAL), _row_block(tile, COL_D_GLU), _row_block(tile, COL_D_GATE),
                _layer_param(layer, (CONF_KERNEL, BR_WIDTH)), vec, vec, vec]
    return in_specs, [pltpu.VMEM((tile + CONF_HALO, BR_WIDTH), F32),
                      pltpu.VMEM((SUBLANES, tile + CONF_HALO - SUBLANES, BR_WIDTH), F32)]


def _t5_bucket(dist):
    max_exact = N_BUCKETS // 2
    large = max_exact + (np.log(np.maximum(dist, 1) / max_exact) / math.log(MAX_DISTANCE / max_exact)
                         * (N_BUCKETS - max_exact)).astype(np.int32)
    large = np.minimum(large, N_BUCKETS - 1)
    return np.where(dist < max_exact, dist, large).astype(np.int32)


def _bucket_index():
    qi = np.arange(ATT_SPAN)[:, None]
    kj = np.arange(2 * ATT_SPAN)[None, :]
    dist = qi + ATT_SPAN - kj
    valid = (dist >= 0) & (dist <= ATT_SPAN)
    per_group = [np.where(valid, _t5_bucket(np.clip(dist, 0, ATT_SPAN) * dil), -1) for _, dil in ATT_GROUPS]
    return np.stack(per_group).astype(np.int32)


def _bias_kernel(table_ref, bucket_ref, o_ref):
    head = pl.program_id(0)
    bucket = bucket_ref[...]
    acc = jnp.full(bucket.shape, NEG_INF, F32)
    for bkt in range(N_BUCKETS):
        acc = jnp.where(bucket == bkt, table_ref[bkt, head], acc)
    o_ref[...] = acc


def _attn_bias(table):
    blk = (None, ATT_SPAN, 2 * ATT_SPAN)
    return pl.pallas_call(
        _bias_kernel,
        grid=(ATT_HEADS,),
        in_specs=[pl.BlockSpec(memory_space=pltpu.SMEM),
                  pl.BlockSpec(blk, lambda hd: (hd // ATT_HEADS_PER_GROUP, 0, 0))],
        out_specs=pl.BlockSpec(blk, lambda hd: (hd, 0, 0)),
        out_shape=jax.ShapeDtypeStruct((ATT_HEADS, ATT_SPAN, 2 * ATT_SPAN), F32),
        compiler_params=_params(("parallel",), 32),
        name="attn_bias",
    )(table, jnp.asarray(_bucket_index()))


ATT_DIRECT_STRIDE = 4


def _residue_reader(ref, slab, dilation):
    if dilation == 1:
        return lambda b, r: ref[b * ATT_SPAN:(b + 1) * ATT_SPAN, :]
    if dilation <= ATT_DIRECT_STRIDE:
        return lambda b, r: ref[pl.ds(b * ATT_SPAN * dilation + r, ATT_SPAN, stride=dilation), :]
    inner, outer = ATT_DIRECT_STRIDE, dilation // ATT_DIRECT_STRIDE
    per = ref.shape[0] // inner
    for r0 in range(inner):
        slab[r0] = ref[pl.ds(r0, per, stride=inner), :]
    return lambda b, r: slab[r % inner, pl.ds(b * ATT_SPAN * outer + r // inner, ATT_SPAN, stride=outer), :]


def _residue_writer(ref, slab, dilation):
    if dilation == 1:
        def write(b, r, val):
            ref[b * ATT_SPAN:(b + 1) * ATT_SPAN, :] = val
        return write, lambda: None
    if dilation <= ATT_DIRECT_STRIDE:
        def write(b, r, val):
            ref[pl.ds(b * ATT_SPAN * dilation + r, ATT_SPAN, stride=dilation), :] = val
        return write, lambda: None
    inner, outer = ATT_DIRECT_STRIDE, dilation // ATT_DIRECT_STRIDE
    per = ref.shape[0] // inner

    def write(b, r, val):
        slab[r % inner, pl.ds(b * ATT_SPAN * outer + r // inner, ATT_SPAN, stride=outer), :] = val

    def flush():
        for r0 in range(inner):
            ref[pl.ds(r0, per, stride=inner), :] = slab[r0]

    return write, flush


ATT_ROWS = ATT_SPAN * max(dil for _, dil in ATT_GROUPS)
ATT_GROUP_IN = 6
ATT_SLABS = 8
ATT_UNITS_PER_STAGE = 4


def _attn_group_outputs(q_ref, kc_ref, kp_ref, vc_ref, vp_ref, bias_ref, o_nat, m_nat, d_nat, slabs, first, dilation):
    blocks = ATT_ROWS // (ATT_SPAN * dilation)
    scale = ATT_HEAD_DIM ** -0.5
    read_q, read_kc, read_kp, read_vc, read_vp = (
        _residue_reader(ref, slab, dilation) for ref, slab in zip((q_ref, kc_ref, kp_ref, vc_ref, vp_ref), slabs[:5]))
    (write_o, flush_o), (write_m, flush_m), (write_d, flush_d) = (
        _residue_writer(ref, slab, dilation) for ref, slab in zip((o_nat, m_nat, d_nat), slabs[5:]))
    def key(b, r):
        return (read_kp(0, r) if b < 0 else read_kc(b, r)).astype(BF16)

    def value(b, r):
        return (read_vp(0, r) if b < 0 else read_vc(b, r)).astype(BF16)

    bias_p = bias_ref[:, 0:ATT_SPAN]
    bias_c = bias_ref[:, ATT_SPAN:2 * ATT_SPAN]
    full = (ATT_SPAN, LANES)
    def logits(u):
        q = read_q(u["b"], u["r"]).astype(BF16)
        lp = _bdot_nt(q, key(u["b"] - 1, u["r"])) * scale + bias_p
        u["lp"] = jnp.where(first, NEG_INF, lp) if u["b"] == 0 else lp
        u["lc"] = _bdot_nt(q, key(u["b"], u["r"])) * scale + bias_c

    def row_max(u):
        u["m"] = jnp.max(jnp.maximum(u["lp"], u["lc"]), axis=-1, keepdims=True)

    def weights(u):
        u["pp"] = jnp.exp(u.pop("lp") - u["m"])
        u["pc"] = jnp.exp(u.pop("lc") - u["m"])

    def denominator(u):
        u["den"] = jnp.sum(u["pp"] + u["pc"], axis=-1, keepdims=True)

    def outputs(u):
        b, r = u["b"], u["r"]
        write_o(b, r, (_bdot(u.pop("pp"), value(b - 1, r)) + _bdot(u.pop("pc"), value(b, r))) / u["den"])
        write_m(b, r, jnp.broadcast_to(u["m"], full))
        write_d(b, r, jnp.broadcast_to(u["den"], full))

    stages = (logits, row_max, weights, denominator, outputs)
    units = [dict(b=b, r=r) for b in range(blocks) for r in range(dilation)]
    groups = [units[i:i + ATT_UNITS_PER_STAGE] for i in range(0, len(units), ATT_UNITS_PER_STAGE)]
    for tick in range(len(groups) + len(stages) - 1):
        for s, stage in reversed(list(enumerate(stages))):
            if 0 <= tick - s < len(groups):
                for u in groups[tick - s]:
                    stage(u)
    flush_o()
    flush_m()
    flush_d()


def _attention_kernel(*refs):
    n_in = ATT_GROUP_IN * len(ATT_GROUPS)
    gate_ref, y_ref = refs[n_in], refs[n_in + 1]
    o_nat, m_nat, d_nat = refs[n_in + 2:n_in + 5]
    slabs = refs[n_in + 5:]
    first = pl.program_id(0) == 0
    for g, (_, dil) in enumerate(ATT_GROUPS):
        _attn_group_outputs(*refs[ATT_GROUP_IN * g:ATT_GROUP_IN * (g + 1)], o_nat.at[g], m_nat.at[g], d_nat.at[g],
                            slabs, first, dil)
    ms = [m_nat[g] for g in range(len(ATT_GROUPS))]
    m_all = functools.reduce(jnp.maximum, ms)
    wts = [jnp.exp(m - m_all) * d_nat[g] for g, m in enumerate(ms)]
    num = sum(w * o_nat[g] for g, w in enumerate(wts))
    y_ref[...] = (num / sum(wts) * _silu(gate_ref[...])).astype(BF16)


def _attention(h, bias):
    s = h.shape[0]
    heads = ATT_HEADS_PER_GROUP
    in_specs, operands = [], []
    for g, (_, dil) in enumerate(ATT_GROUPS):
        blk = ATT_SPAN * dil
        per_step = ATT_ROWS // blk

        def spec(col, prev, g=g, blk=blk, per_step=per_step):
            base = (col + g) * heads
            if prev:
                return pl.BlockSpec((blk, ATT_HEAD_DIM), lambda n, hd: (jnp.maximum(n * per_step - 1, 0), base + hd))
            return pl.BlockSpec((ATT_ROWS, ATT_HEAD_DIM), lambda n, hd: (n, base + hd))

        in_specs += [spec(COL_Q, False), spec(COL_K, False), spec(COL_K, True), spec(COL_V, False), spec(COL_V, True),
                     pl.BlockSpec((None, ATT_SPAN, 2 * ATT_SPAN), lambda n, hd, g=g: (g * heads + hd, 0, 0))]
        operands += [h, h, h, h, h, bias]
    in_specs.append(pl.BlockSpec((ATT_ROWS, ATT_HEAD_DIM), lambda n, hd: (n, COL_B_GATE * heads + hd)))
    token_order = pltpu.VMEM((len(ATT_GROUPS), ATT_ROWS, LANES), F32)
    slab = pltpu.VMEM((ATT_DIRECT_STRIDE, ATT_ROWS // ATT_DIRECT_STRIDE, LANES), F32)
    return pl.pallas_call(
        _attention_kernel,
        grid=(s // ATT_ROWS, heads),
        in_specs=in_specs,
        out_specs=pl.BlockSpec((ATT_ROWS, ATT_HEAD_DIM), lambda n, hd: (n, hd)),
        out_shape=jax.ShapeDtypeStruct((s, BR_WIDTH), BF16),
        scratch_shapes=[token_order] * 3 + [slab] * ATT_SLABS,
        compiler_params=_params(("parallel", "arbitrary"), 56),
        name="dilated_attention",
    )(*operands, h)


def _head_sums(x):
    ri = lax.broadcasted_iota(jnp.int32, (PAIR, PAIR), 0)
    ci = lax.broadcasted_iota(jnp.int32, (PAIR, PAIR), 1)
    same_head = jnp.where((ri < RWKV_HEAD) == (ci < RWKV_HEAD), 1.0, 0.0).astype(BF16)
    return jnp.concatenate([_split_dot(x[:, p * PAIR:(p + 1) * PAIR], same_head, 2, 1) for p in range(N_PAIRS)], axis=1)


def _rwkv_prepare(r_ref, k_ref, v_ref, lora_ref, mu_r, mu_k, mu_v, mu_l, w0_ref, wup_ref, a0_ref, aup_ref,
                  kk_ref, ka_ref, rk_ref, carry, carry_l):
    t = r_ref.shape[0]

    def shift_mix(x, mu, prev_row):
        row = lax.broadcasted_iota(jnp.int32, x.shape, 0)
        x_prev = jnp.where(row == 0, prev_row, pltpu.roll(x, 1, 0))
        return x + mu * (x_prev - x)

    r_in, k_in, v_in, l_in = r_ref[...], k_ref[...], v_ref[...], lora_ref[...]
    r = shift_mix(r_in, mu_r[...], carry[0:1, :])
    kx = shift_mix(k_in, mu_k[...], carry[1:2, :])
    vv = shift_mix(v_in, mu_v[...], carry[2:3, :])
    lo = shift_mix(l_in, mu_l[...], carry_l[0:1, :])
    carry[0:1, :] = r_in[t - 1:t, :]
    carry[1:2, :] = k_in[t - 1:t, :]
    carry[2:3, :] = v_in[t - 1:t, :]
    carry_l[0:1, :] = l_in[t - 1:t, :]

    w_log = -_softplus(-(w0_ref[...] + _bdot(jnp.tanh(lo), wup_ref[...]))) - 0.5
    log_decay = -jnp.exp(w_log)
    a_icl = jax.nn.sigmoid(a0_ref[...] + _bdot(lo, aup_ref[...]))

    kk = kx * kk_ref[...]
    kk = kk / jnp.maximum(jnp.sqrt(_head_sums(kk * kk)), 1e-12)
    kc = kx * (1.0 + (a_icl - 1.0) * ka_ref[...])
    bonus = _head_sums(r * kc * rk_ref[...]) * vv
    return log_decay, r, kc, vv, -kk, kk * a_icl, bonus


def _stack_heads(x):
    lane = lax.broadcasted_iota(jnp.int32, x.shape, 1)
    return jnp.concatenate([jnp.where(lane < RWKV_HEAD, x, 0.0), jnp.where(lane >= RWKV_HEAD, x, 0.0)], axis=0)


def _time_indices():
    t = lax.broadcasted_iota(jnp.int32, (RWKV_CHUNK, PAIR), 0)
    s = lax.broadcasted_iota(jnp.int32, (RWKV_CHUNK, PAIR), 1) & (RWKV_CHUNK - 1)
    return t, s


def _unit_lower_inverse(a_strict):
    ti, si = _time_indices()

    def same_block(bits):
        return (ti >> bits) == (si >> bits)

    pw = [jnp.where(same_block(4), a, 0.0) for a in a_strict]
    x = [jnp.where(ti == si, 1.0, 0.0) + p for p in pw]
    for _ in range(3):
        pw = [_bdot(p, _stack_heads(p)) for p in pw]
        x = [xi + _bdot(xi, _stack_heads(p)) for xi, p in zip(x, pw)]
    for bits in (5, 6):
        join = same_block(bits) & jnp.logical_not(same_block(bits - 1))
        xe = [_bdot(xi, _stack_heads(jnp.where(join, a, 0.0))) for xi, a in zip(x, a_strict)]
        x = [xi + _bdot(t, _stack_heads(xi)) for xi, t in zip(x, xe)]
    return x


def _rwkv_chunk_transforms(lw_all, r_all, k_all, v_all, a_all, b_all):
    c = RWKV_CHUNK
    n = 2 * c
    ti = lax.broadcasted_iota(jnp.int32, (c, c), 0)
    si = lax.broadcasted_iota(jnp.int32, (c, c), 1)
    lower_ones = jnp.where(si <= ti, 1.0, 0.0)
    tt, ss = _time_indices()
    strict = tt > ss
    incl = tt >= ss
    ri = lax.broadcasted_iota(jnp.int32, (n, n), 0)
    ci = lax.broadcasted_iota(jnp.int32, (n, n), 1)
    same_head = (ri < RWKV_HEAD) == (ci < RWKV_HEAD)
    eye = ri == ci

    units = [(ch, p) for ch in range(lw_all.shape[0] // c) for p in range(N_PAIRS)]
    each = lambda f, *cols: [f(*args) for args in zip(*cols)]

    def split(x):
        return [x[ch * c:(ch + 1) * c, p * PAIR:(p + 1) * PAIR] for ch, p in units]

    lw, r, k, v, a, b = (split(x) for x in (lw_all, r_all, k_all, v_all, a_all, b_all))
    cs = each(lambda x: _split_dot(lower_ones, x, 1, 3), lw)
    c_end = each(lambda x: x[c - 1:c, :], cs)
    r_d = each(lambda x, y: x * jnp.exp(y), r, cs)
    a_d = each(lambda x, y, z: x * jnp.exp(y - z), a, cs, lw)
    b_i = each(lambda x, y: x * jnp.exp(-y), b, cs)
    k_i = each(lambda x, y: x * jnp.exp(-y), k, cs)
    b_e = each(lambda x, y, e: x * jnp.exp(e - y), b, cs, c_end)
    k_e = each(lambda x, y, e: x * jnp.exp(e - y), k, cs, c_end)
    v_s = each(_stack_heads, v)

    aa = each(lambda ad, rd, bi, ki: _bdot_nt(jnp.concatenate([ad, rd], axis=0),
                                              jnp.concatenate([_stack_heads(bi), _stack_heads(ki)], axis=0)),
              a_d, r_d, b_i, k_i)
    a_ab = each(lambda x: jnp.where(strict, x[0:c, 0:n], 0.0), aa)
    a_ak = each(lambda x: jnp.where(strict, x[0:c, n:2 * n], 0.0), aa)
    a_rb = each(lambda x: jnp.where(incl, x[c:n, 0:n], 0.0), aa)
    a_rk = each(lambda x: jnp.where(incl, x[c:n, n:2 * n], 0.0), aa)

    minv = _unit_lower_inverse(a_ab)
    w = each(lambda m, ad: _bdot(m, _stack_heads(ad)), minv, a_d)
    t1 = each(_bdot, a_ak, v_s)
    uv = each(lambda m, x: _bdot(m, _stack_heads(x)), minv, t1)
    q = each(lambda rd, x, y: rd + _bdot(x, _stack_heads(y)), r_d, a_rb, w)
    yc = each(lambda x, y, z, t: _bdot(x, _stack_heads(y)) + _bdot(z, t), a_rb, uv, a_rk, v_s)
    g = each(lambda e, x, y: jnp.where(eye, jnp.exp(e), 0.0) + jnp.where(same_head, _bdot_tn(x, y), 0.0), c_end, w, b_e)
    z = each(lambda u_, v_, be, ke: jnp.where(same_head, _bdot_tn(jnp.concatenate([u_, v_], axis=0),
                                                                    jnp.concatenate([be, ke], axis=0)), 0.0),
             uv, v, b_e, k_e)
    return {unit: terms for unit, *terms in zip(units, q, yc, g, z)}


def _rwkv_init(carry, carry_l, state, ybuf):
    carry[...] = jnp.zeros_like(carry)
    carry_l[...] = jnp.zeros_like(carry_l)
    state[...] = jnp.zeros_like(state)


def _rwkv_body(r_ref, k_ref, v_ref, lora_ref, gate_ref, mu_r, mu_k, mu_v, mu_l, w0_ref, wup_ref, a0_ref, aup_ref,
               kk_ref, ka_ref, rk_ref, gn_g, gn_b, o_ref, carry, carry_l, state, ybuf):
    c = RWKV_CHUNK
    chunks = r_ref.shape[0] // c
    *scan_inputs, bonus = _rwkv_prepare(r_ref, k_ref, v_ref, lora_ref, mu_r, mu_k, mu_v, mu_l, w0_ref, wup_ref,
                                        a0_ref, aup_ref, kk_ref, ka_ref, rk_ref, carry, carry_l)
    terms = _rwkv_chunk_transforms(*scan_inputs)

    pairs = range(N_PAIRS)
    sts = [state[:, p * PAIR:(p + 1) * PAIR] for p in pairs]
    starts = []
    for ch in range(chunks):
        starts.append(sts)
        sts = [_split_dot(sts[p], terms[ch, p][2], 2, 2) + terms[ch, p][3] for p in pairs]
    for p in pairs:
        state[:, p * PAIR:(p + 1) * PAIR] = sts[p]
    for ch in range(chunks):
        for p in pairs:
            q, yc = terms[ch, p][0], terms[ch, p][1]
            ybuf[ch * c:(ch + 1) * c, p * PAIR:(p + 1) * PAIR] = _split_dot(q, starts[ch][p], 2, 2, NT_DIMS) + yc

    wy = ybuf[...]
    inv_n = 1.0 / RWKV_HEAD
    mu = _head_sums(wy) * inv_n
    d = wy - mu
    var = _head_sums(d * d) * inv_n
    wy = d * lax.rsqrt(var + RWKV_GN_EPS) * gn_g[...] + gn_b[...]
    o_ref[...] = ((wy + bonus) * _silu(gate_ref[...])).astype(BF16)


def _rwkv_specs(tile, layer):
    vec, lora_w = _layer_param(layer, (1, BR_WIDTH)), _layer_param(layer, (LANES, BR_WIDTH))
    in_specs = [_row_block(tile, COL_C_R), _row_block(tile, COL_C_K), _row_block(tile, COL_C_V),
                pl.BlockSpec((tile, LANES), lambda i: (i, COL_C_LORA * (BR_WIDTH // LANES))),
                _row_block(tile, COL_C_GATE),
                vec, vec, vec, _layer_param(layer, (1, LANES)), vec, lora_w, vec, lora_w, vec, vec, vec, vec, vec]
    return in_specs, [pltpu.VMEM((SUBLANES, BR_WIDTH), F32), pltpu.VMEM((SUBLANES, LANES), F32),
                      pltpu.VMEM((PAIR, BR_WIDTH), F32), pltpu.VMEM((tile, BR_WIDTH), F32)]


def _recurrent_mixers_kernel(*refs):
    n_in = LRU_IN + CONF_IN + RWKV_IN
    ins, (o_a, o_d, o_c), scratch = refs[:n_in], refs[n_in:n_in + 3], refs[n_in + 3:]
    lru_in, conf_in, rwkv_in = ins[:LRU_IN], ins[LRU_IN:LRU_IN + CONF_IN], ins[LRU_IN + CONF_IN:]
    lru_s = scratch[:LRU_SCRATCH]
    conf_s = scratch[LRU_SCRATCH:LRU_SCRATCH + CONF_SCRATCH]
    rwkv_s = scratch[LRU_SCRATCH + CONF_SCRATCH:]

    @pl.when(pl.program_id(0) == 0)
    def _():
        _lru_init(*lru_s)
        _conf_init(*conf_s)
        _rwkv_init(*rwkv_s)

    _rwkv_body(*rwkv_in, o_c, *rwkv_s)
    _conf_body(*conf_in, o_d, *conf_s)
    _lru_body(*lru_in, o_a, *lru_s)


def _recurrent_mixers(h, layer, lru_args, conf_args, rwkv_args, tile=4 * RWKV_CHUNK):
    s = h.shape[0]
    (lru_specs, lru_scr), (conf_specs, conf_scr), (rwkv_specs, rwkv_scr) = (
        _lru_specs(tile, layer), _conf_specs(tile, layer), _rwkv_specs(tile, layer))
    assert (len(lru_specs), len(conf_specs), len(rwkv_specs)) == (LRU_IN, CONF_IN, RWKV_IN)
    out = pl.BlockSpec((tile, BR_WIDTH), lambda i: (i, 0))
    return pl.pallas_call(
        _recurrent_mixers_kernel,
        grid=(s // tile,),
        in_specs=lru_specs + conf_specs + rwkv_specs,
        out_specs=[out] * 3,
        out_shape=[jax.ShapeDtypeStruct((s, BR_WIDTH), BF16)] * 3,
        scratch_shapes=lru_scr + conf_scr + rwkv_scr,
        compiler_params=_params(("arbitrary",), 40),
        name="recurrent_mixers",
    )(h, h, *lru_args, h, h, h, *conf_args, h, h, h, h, h, *rwkv_args)


def _mix_kernel(xb_ref, *refs):
    ygs, wms, bms, wbrs = (refs[k * N_BRANCH:(k + 1) * N_BRANCH] for k in range(4))
    o_ref = refs[4 * N_BRANCH]
    xb = xb_ref[...]
    acc = None
    for n in range(N_BRANCH):
        gate = jax.nn.sigmoid(_bdot(xb, wms[n][...]) + bms[n][...])
        val = gate * _bdot(ygs[n][...], wbrs[n][...])
        acc = val if acc is None else acc + val
    o_ref[...] = acc.astype(BF16)


def _mix(xb, ygs, w_all, layer, bm, wbr_all, tm=1024, tn=256):
    s = xb.shape[0]
    nj = D_MODEL // tn
    per_branch = lambda make: [make(n) for n in range(N_BRANCH)]
    return pl.pallas_call(
        _mix_kernel,
        grid=(s // tm, nj),
        in_specs=[pl.BlockSpec((tm, D_MODEL), lambda i, j: (i, 0))]
        + per_branch(lambda n: pl.BlockSpec((tm, BR_WIDTH), lambda i, j: (i, 0)))
        + per_branch(lambda n: pl.BlockSpec((pl.Squeezed(), pl.Element(D_MODEL), pl.Element(tn)),
                                            lambda i, j: (layer, 0, ((BRANCH_IN + n * D_MODEL) // LANES
                                                                     + j * (tn // LANES)) * LANES)))
        + per_branch(lambda n: pl.BlockSpec((None, 1, tn), lambda i, j: (layer, 0, n * nj + j)))
        + per_branch(lambda n: pl.BlockSpec((None, None, BR_WIDTH, tn), lambda i, j: (layer, n, 0, j))),
        out_specs=pl.BlockSpec((tm, tn), lambda i, j: (i, j)),
        out_shape=jax.ShapeDtypeStruct((s, D_MODEL), BF16),
        compiler_params=_params(("parallel", "arbitrary"), 48),
        name="branch_mix",
    )(xb, *ygs, *([w_all] * N_BRANCH), *([bm] * N_BRANCH), *([wbr_all] * N_BRANCH))


def _out_kernel(mixed_ref, x_ref, w_ref, g_ref, b_ref, o_ref):
    y = ALPHA * x_ref[...] + jnp.dot(mixed_ref[...], w_ref[...], preferred_element_type=F32)
    mu = jnp.mean(y, axis=-1, keepdims=True)
    var = jnp.mean(jnp.square(y - mu), axis=-1, keepdims=True)
    o_ref[...] = (y - mu) * lax.rsqrt(var + LN_EPS) * g_ref[...] + b_ref[...]


def _out_proj(mixed, x, layer, w, g, b, tm=512):
    s = x.shape[0]
    row = pl.BlockSpec((tm, D_MODEL), lambda i: (i, 0))
    vec = _layer_param(layer, (1, D_MODEL))
    return pl.pallas_call(
        _out_kernel,
        grid=(s // tm,),
        in_specs=[row, row, _layer_param(layer, (D_MODEL, D_MODEL)), vec, vec],
        out_specs=row,
        out_shape=jax.ShapeDtypeStruct((s, D_MODEL), F32),
        compiler_params=_params(("parallel",), 48),
        name="out_proj_ln",
    )(mixed, x, w, g, b)


def _block_diag(w):
    depth, blocks, n, _ = w.shape
    eye = jnp.eye(blocks, dtype=w.dtype)
    return (eye[None, :, None, :, None] * w[:, :, :, None, :]).reshape(depth, blocks * n, blocks * n)


def kernel(x, att_rel_bias, w_in, b_in, lru_conv_w, lru_conv_b, lru_gate_a_w, lru_gate_a_b, lru_gate_x_w, lru_gate_x_b, lru_lambda, rwkv_mu, rwkv_w0, rwkv_w_up, rwkv_a0, rwkv_a_up, rwkv_k_k, rwkv_k_a, rwkv_r_k, rwkv_gn_g, rwkv_gn_b, conf_dw_w, conf_dw_b, conf_ln_g, conf_ln_b, w_br, w_out, ln_g, ln_b):
    bsz, s, d = x.shape
    assert bsz == 1 and d == D_MODEL and s % ATT_ROWS == 0
    vec = lambda t: t.reshape(DEPTH, 1, -1)
    b_h = vec(jnp.concatenate([b_in[:, :H_SPLIT * BR_WIDTH], b_in[:, C_GATE_START:BRANCH_IN]], axis=1))
    b_merge = vec(b_in[:, BRANCH_IN:])
    mu = rwkv_mu
    zpad = jnp.zeros((DEPTH, DECAY_RANK, BR_WIDTH), F32)
    wup = jnp.concatenate([rwkv_w_up, zpad], axis=1).astype(BF16)
    aup = jnp.concatenate([zpad, rwkv_a_up], axis=1).astype(BF16)
    lru_args = (lru_conv_w, vec(lru_conv_b), _block_diag(lru_gate_a_w).astype(BF16), vec(lru_gate_a_b),
                _block_diag(lru_gate_x_w).astype(BF16), vec(lru_gate_x_b), vec(lru_lambda))
    conf_args = (conf_dw_w, vec(conf_dw_b), vec(conf_ln_g), vec(conf_ln_b))
    rwkv_args = (vec(mu[:, :BR_WIDTH]), vec(mu[:, BR_WIDTH:2 * BR_WIDTH]), vec(mu[:, 2 * BR_WIDTH:3 * BR_WIDTH]),
                 vec(mu[:, 3 * BR_WIDTH:]), vec(rwkv_w0), wup, vec(rwkv_a0), aup, vec(rwkv_k_k), vec(rwkv_k_a),
                 vec(rwkv_r_k), vec(rwkv_gn_g), vec(rwkv_gn_b))
    w_out_bf16, ln_g, ln_b = w_out.astype(BF16), vec(ln_g), vec(ln_b)
    att_bias = _attn_bias(att_rel_bias)

    y = x.reshape(s, d)
    for layer in range(DEPTH):
        h, xb = _in_proj(y, w_in, layer, b_h)
        yg_b = _attention(h, att_bias)
        yg_a, yg_d, yg_c = _recurrent_mixers(h, layer, lru_args, conf_args, rwkv_args)
        mixed = _mix(xb, (yg_a, yg_b, yg_c, yg_d), w_in, layer, b_merge, w_br)
        y = _out_proj(mixed, y, layer, w_out_bf16, ln_g, ln_b)
    return y.reshape(bsz, s, d)
```

```python
import functools
import math

import numpy as np
import jax
import jax.numpy as jnp
from jax import lax
from jax.experimental import pallas as pl
from jax.experimental.pallas import tpu as pltpu

D_MODEL = 2048
DEPTH = 2
N_BRANCH = 4
BR_WIDTH = 512
LRU_BLOCKS = 8
LRU_BLOCK = BR_WIDTH // LRU_BLOCKS
LRU_CONV = 4
LRU_C = 8.0
ATT_GROUPS = ((128, 1), (512, 4), (2048, 16))
ATT_HEADS_PER_GROUP = 4
ATT_HEAD_DIM = BR_WIDTH // ATT_HEADS_PER_GROUP
ATT_HEADS = len(ATT_GROUPS) * ATT_HEADS_PER_GROUP
ATT_QKV = ATT_HEADS * ATT_HEAD_DIM
ATT_SPAN = 128
N_BUCKETS = 32
MAX_DISTANCE = 2048
NEG_INF = -1e30
RWKV_HEAD = 64
RWKV_HEADS = BR_WIDTH // RWKV_HEAD
DECAY_RANK = 64
ICLR_RANK = 64
RWKV_GN_EPS = 64e-5
CONF_KERNEL = 31
LN_EPS = 1e-5
ALPHA = (2.0 * DEPTH) ** 0.25

LANES = 128
SUBLANES = 8
MIB = 1024 * 1024

BRANCH_IN = 2 * BR_WIDTH + 3 * ATT_QKV + BR_WIDTH + (4 * BR_WIDTH + DECAY_RANK + ICLR_RANK) + 3 * BR_WIDTH
C_GATE_START = BRANCH_IN - 4 * BR_WIDTH
H_SPLIT = 16
H_BLOCKS = 20
H_WIDTH = H_BLOCKS * BR_WIDTH
COL_A_X, COL_A_GATE = 0, 1
COL_Q, COL_K, COL_V, COL_B_GATE = 2, 5, 8, 11
COL_C_R, COL_C_K, COL_C_V, COL_C_LORA = 12, 13, 14, 15
COL_C_GATE, COL_D_VAL, COL_D_GLU, COL_D_GATE = 16, 17, 18, 19

CONF_HALO = 32
RWKV_CHUNK = 64
PAIR = 2 * RWKV_HEAD
N_PAIRS = BR_WIDTH // PAIR

F32 = jnp.float32
BF16 = jnp.bfloat16


def _params(semantics, vmem_mib):
    return pltpu.CompilerParams(dimension_semantics=semantics, vmem_limit_bytes=vmem_mib * MIB)


def _bdot(a, b):
    return jnp.dot(a.astype(BF16), b.astype(BF16), preferred_element_type=F32)


def _bdot_nt(a, b):
    return lax.dot_general(a.astype(BF16), b.astype(BF16), (((1,), (1,)), ((), ())), preferred_element_type=F32)


def _bdot_tn(a, b):
    return lax.dot_general(a.astype(BF16), b.astype(BF16), (((0,), (0,)), ((), ())), preferred_element_type=F32)


NN_DIMS = (((1,), (0,)), ((), ()))
NT_DIMS = (((1,), (1,)), ((), ()))


def _bf16_parts(x, parts):
    out = []
    for _ in range(parts):
        hi = x.astype(BF16)
        out.append(hi)
        x = x - hi.astype(F32)
    return out


def _split_dot(a, b, a_parts, b_parts, dims=NN_DIMS):
    acc = None
    b_terms = _bf16_parts(b, b_parts)
    for i, ai in enumerate(_bf16_parts(a, a_parts)):
        for j, bj in enumerate(b_terms):
            if i + j < max(a_parts, b_parts):
                term = lax.dot_general(ai, bj, dims, preferred_element_type=F32)
                acc = term if acc is None else acc + term
    return acc


def _softplus(z):
    return jnp.maximum(z, 0.0) + jnp.log1p(jnp.exp(-jnp.abs(z)))


def _expm1_nonpos(z):
    u = jnp.exp(z)
    safe = jnp.where(u == 1.0, 0.5, u)
    return jnp.where(u == 1.0, z, jnp.where(u == 0.0, -1.0, (safe - 1.0) * z / jnp.log(safe)))


def _silu(z):
    return z * jax.nn.sigmoid(z)


def _in_proj_kernel(xb_ref, w_ref, b_ref, h_ref):
    h_ref[...] = _bdot(xb_ref[...], w_ref[...]) + b_ref[...]


def _h_source_column(block):
    return block * BR_WIDTH if block < H_SPLIT else C_GATE_START + (block - H_SPLIT) * BR_WIDTH


def _in_proj(xb, w_all, layer, b, tm=2048, tn=512):
    s, k = xb.shape
    assert (H_SPLIT * BR_WIDTH) % tn == 0 and tn % BR_WIDTH == 0
    per_tile = tn // BR_WIDTH
    starts = np.array([_h_source_column(j * per_tile) // LANES for j in range(H_WIDTH // tn)], np.int32)
    return pl.pallas_call(
        lambda starts_ref, *refs: _in_proj_kernel(*refs),
        grid_spec=pltpu.PrefetchScalarGridSpec(
            num_scalar_prefetch=1,
            grid=(s // tm, H_WIDTH // tn),
            in_specs=[
                pl.BlockSpec((tm, k), lambda i, j, st: (i, 0)),
                pl.BlockSpec((pl.Squeezed(), pl.Element(k), pl.Element(tn)),
                             lambda i, j, st: (layer, 0, st[j] * LANES)),
                pl.BlockSpec((None, 1, tn), lambda i, j, st: (layer, 0, j)),
            ],
            out_specs=pl.BlockSpec((tm, tn), lambda i, j, st: (i, j)),
        ),
        out_shape=jax.ShapeDtypeStruct((s, H_WIDTH), F32),
        compiler_params=_params(("parallel", "arbitrary"), 56),
        name="in_proj",
    )(jnp.asarray(starts), xb, w_all, b)


LRU_IN, CONF_IN, RWKV_IN = 9, 7, 18
LRU_SCRATCH, CONF_SCRATCH, RWKV_SCRATCH = 2, 2, 4


def _lru_init(ebuf, hc):
    ebuf[0:SUBLANES, :] = jnp.zeros((SUBLANES, BR_WIDTH), F32)
    hc[...] = jnp.zeros_like(hc)


def _lru_body(ax_ref, ag_ref, cw_ref, cb_ref, wa_ref, ba_ref, wx_ref, bx_ref, lam_ref, o_ref, ebuf, hc):
    t = ax_ref.shape[0]
    halo = SUBLANES
    x = ax_ref[...]
    ebuf[halo:halo + t, :] = x
    u = cb_ref[...] + jnp.zeros((t, BR_WIDTH), F32)
    for j in range(LRU_CONV):
        u = u + cw_ref[j:j + 1, :] * ebuf[pl.ds(halo - (LRU_CONV - 1) + j, t), :]
    ebuf[0:halo, :] = x[t - halo:t, :]

    gate_r = jax.nn.sigmoid(_bdot(u, wa_ref[...]) + ba_ref[...])
    gate_i = jax.nn.sigmoid(_bdot(u, wx_ref[...]) + bx_ref[...])
    log_a = -LRU_C * gate_r * _softplus(-lam_ref[...])
    a = jnp.exp(log_a)
    b = jnp.sqrt(-_expm1_nonpos(2.0 * log_a)) * (gate_i * u)

    row = lax.broadcasted_iota(jnp.int32, (t, BR_WIDTH), 0)
    shift = 1
    while shift < t:
        valid = row >= shift
        b = jnp.where(valid, a * pltpu.roll(b, shift, 0), 0.0) + b
        a = jnp.where(valid, a * pltpu.roll(a, shift, 0), a)
        shift *= 2
    h = a * hc[0:1, :] + b
    hc[0:1, :] = h[t - 1:t, :]
    o_ref[...] = (h * _silu(ag_ref[...])).astype(BF16)


def _row_block(tile, col):
    return pl.BlockSpec((tile, BR_WIDTH), lambda i: (i, col))


def _layer_param(layer, shape):
    return pl.BlockSpec((None,) + shape, lambda *_: (layer,) + (0,) * len(shape))


def _lru_specs(tile, layer):
    vec, mat = _layer_param(layer, (1, BR_WIDTH)), _layer_param(layer, (BR_WIDTH, BR_WIDTH))
    in_specs = [_row_block(tile, COL_A_X), _row_block(tile, COL_A_GATE), _layer_param(layer, (LRU_CONV, BR_WIDTH)),
                vec, mat, vec, mat, vec, vec]
    return in_specs, [pltpu.VMEM((tile + SUBLANES, BR_WIDTH), F32), pltpu.VMEM((SUBLANES, BR_WIDTH), F32)]


def _conf_init(ebuf, shifted):
    ebuf[0:CONF_HALO, :] = jnp.zeros((CONF_HALO, BR_WIDTH), F32)


def _conf_body(val_ref, glu_ref, gate_ref, w_ref, b_ref, g_ref, beta_ref, o_ref, ebuf, shifted):
    t = val_ref.shape[0]
    halo = CONF_HALO
    cu = val_ref[...] * jax.nn.sigmoid(glu_ref[...])
    ebuf[halo:halo + t, :] = cu
    for b in range(SUBLANES):
        span = t + (CONF_KERNEL - 1 - b) // SUBLANES * SUBLANES
        shifted[b, 0:span, :] = ebuf[pl.ds(halo - (CONF_KERNEL - 1) + b, span), :]
    acc = b_ref[...] + jnp.zeros((t, BR_WIDTH), F32)
    for j in range(CONF_KERNEL):
        b = j % SUBLANES
        acc = acc + w_ref[j:j + 1, :] * shifted[b, j - b:j - b + t, :]
    ebuf[0:halo, :] = cu[t - halo:t, :]

    mu = jnp.mean(acc, axis=-1, keepdims=True)
    var = jnp.mean(jnp.square(acc - mu), axis=-1, keepdims=True)
    ln = (acc - mu) * lax.rsqrt(var + LN_EPS) * g_ref[...] + beta_ref[...]
    o_ref[...] = (_silu(ln) * _silu(gate_ref[...])).astype(BF16)


def _conf_specs(tile, layer):
    vec = _layer_param(layer, (1, BR_WIDTH))
    in_specs = [_row_block(tile, COL_D_VAL), _row_block(tile, COL_D_GLU), _row_block(tile, COL_D_GATE),
                _layer_param(layer, (CONF_KERNEL, BR_WIDTH)), vec, vec, vec]
    return in_specs, [pltpu.VMEM((tile + CONF_HALO, BR_WIDTH), F32),
                      pltpu.VMEM((SUBLANES, tile + CONF_HALO - SUBLANES, BR_WIDTH), F32)]


def _t5_bucket(dist):
    max_exact = N_BUCKETS // 2
    large = max_exact + (np.log(np.maximum(dist, 1) / max_exact) / math.log(MAX_DISTANCE / max_exact)
                         * (N_BUCKETS - max_exact)).astype(np.int32)
    large = np.minimum(large, N_BUCKETS - 1)
    return np.where(dist < max_exact, dist, large).astype(np.int32)


def _bucket_index():
    qi = np.arange(ATT_SPAN)[:, None]
    kj = np.arange(2 * ATT_SPAN)[None, :]
    dist = qi + ATT_SPAN - kj
    valid = (dist >= 0) & (dist <= ATT_SPAN)
    per_group = [np.where(valid, _t5_bucket(np.clip(dist, 0, ATT_SPAN) * dil), -1) for _, dil in ATT_GROUPS]
    return np.stack(per_group).astype(np.int32)


def _bias_kernel(table_ref, bucket_ref, o_ref):
    head = pl.program_id(0)
    bucket = bucket_ref[...]
    acc = jnp.full(bucket.shape, NEG_INF, F32)
    for bkt in range(N_BUCKETS):
        acc = jnp.where(bucket == bkt, table_ref[bkt, head], acc)
    o_ref[...] = acc


def _attn_bias(table):
    blk = (None, ATT_SPAN, 2 * ATT_SPAN)
    return pl.pallas_call(
        _bias_kernel,
        grid=(ATT_HEADS,),
        in_specs=[pl.BlockSpec(memory_space=pltpu.SMEM),
                  pl.BlockSpec(blk, lambda hd: (hd // ATT_HEADS_PER_GROUP, 0, 0))],
        out_specs=pl.BlockSpec(blk, lambda hd: (hd, 0, 0)),
        out_shape=jax.ShapeDtypeStruct((ATT_HEADS, ATT_SPAN, 2 * ATT_SPAN), F32),
        compiler_params=_params(("parallel",), 32),
        name="attn_bias",
    )(table, jnp.asarray(_bucket_index()))


ATT_DIRECT_STRIDE = 4


def _residue_reader(ref, slab, dilation):
    if dilation == 1:
        return lambda b, r: ref[b * ATT_SPAN:(b + 1) * ATT_SPAN, :]
    if dilation <= ATT_DIRECT_STRIDE:
        return lambda b, r: ref[pl.ds(b * ATT_SPAN * dilation + r, ATT_SPAN, stride=dilation), :]
    inner, outer = ATT_DIRECT_STRIDE, dilation // ATT_DIRECT_STRIDE
    per = ref.shape[0] // inner
    for r0 in range(inner):
        slab[r0] = ref[pl.ds(r0, per, stride=inner), :]
    return lambda b, r: slab[r % inner, pl.ds(b * ATT_SPAN * outer + r // inner, ATT_SPAN, stride=outer), :]


def _residue_writer(ref, slab, dilation):
    if dilation == 1:
        def write(b, r, val):
            ref[b * ATT_SPAN:(b + 1) * ATT_SPAN, :] = val
        return write, lambda: None
    if dilation <= ATT_DIRECT_STRIDE:
        def write(b, r, val):
            ref[pl.ds(b * ATT_SPAN * dilation + r, ATT_SPAN, stride=dilation), :] = val
        return write, lambda: None
    inner, outer = ATT_DIRECT_STRIDE, dilation // ATT_DIRECT_STRIDE
    per = ref.shape[0] // inner

    def write(b, r, val):
        slab[r % inner, pl.ds(b * ATT_SPAN * outer + r // inner, ATT_SPAN, stride=outer), :] = val

    def flush():
        for r0 in range(inner):
            ref[pl.ds(r0, per, stride=inner), :] = slab[r0]

    return write, flush


ATT_ROWS = ATT_SPAN * max(dil for _, dil in ATT_GROUPS)
ATT_GROUP_IN = 6
ATT_SLABS = 8
ATT_UNITS_PER_STAGE = 4


def _attn_group_outputs(q_ref, kc_ref, kp_ref, vc_ref, vp_ref, bias_ref, o_nat, m_nat, d_nat, slabs, first, dilation):
    blocks = ATT_ROWS // (ATT_SPAN * dilation)
    scale = ATT_HEAD_DIM ** -0.5
    read_q, read_kc, read_kp, read_vc, read_vp = (
        _residue_reader(ref, slab, dilation) for ref, slab in zip((q_ref, kc_ref, kp_ref, vc_ref, vp_ref), slabs[:5]))
    (write_o, flush_o), (write_m, flush_m), (write_d, flush_d) = (
        _residue_writer(ref, slab, dilation) for ref, slab in zip((o_nat, m_nat, d_nat), slabs[5:]))
    def key(b, r):
        return (read_kp(0, r) if b < 0 else read_kc(b, r)).astype(BF16)

    def value(b, r):
        return (read_vp(0, r) if b < 0 else read_vc(b, r)).astype(BF16)

    bias_p = bias_ref[:, 0:ATT_SPAN]
    bias_c = bias_ref[:, ATT_SPAN:2 * ATT_SPAN]
    full = (ATT_SPAN, LANES)
    def logits(u):
        q = read_q(u["b"], u["r"]).astype(BF16)
        lp = _bdot_nt(q, key(u["b"] - 1, u["r"])) * scale + bias_p
        u["lp"] = jnp.where(first, NEG_INF, lp) if u["b"] == 0 else lp
        u["lc"] = _bdot_nt(q, key(u["b"], u["r"])) * scale + bias_c

    def row_max(u):
        u["m"] = jnp.max(jnp.maximum(u["lp"], u["lc"]), axis=-1, keepdims=True)

    def weights(u):
        u["pp"] = jnp.exp(u.pop("lp") - u["m"])
        u["pc"] = jnp.exp(u.pop("lc") - u["m"])

    def denominator(u):
        u["den"] = jnp.sum(u["pp"] + u["pc"], axis=-1, keepdims=True)

    def outputs(u):
        b, r = u["b"], u["r"]
        write_o(b, r, (_bdot(u.pop("pp"), value(b - 1, r)) + _bdot(u.pop("pc"), value(b, r))) / u["den"])
        write_m(b, r, jnp.broadcast_to(u["m"], full))
        write_d(b, r, jnp.broadcast_to(u["den"], full))

    stages = (logits, row_max, weights, denominator, outputs)
    units = [dict(b=b, r=r) for b in range(blocks) for r in range(dilation)]
    groups = [units[i:i + ATT_UNITS_PER_STAGE] for i in range(0, len(units), ATT_UNITS_PER_STAGE)]
    for tick in range(len(groups) + len(stages) - 1):
        for s, stage in reversed(list(enumerate(stages))):
            if 0 <= tick - s < len(groups):
                for u in groups[tick - s]:
                    stage(u)
    flush_o()
    flush_m()
    flush_d()


def _attention_kernel(*refs):
    n_in = ATT_GROUP_IN * len(ATT_GROUPS)
    gate_ref, y_ref = refs[n_in], refs[n_in + 1]
    o_nat, m_nat, d_nat = refs[n_in + 2:n_in + 5]
    slabs = refs[n_in + 5:]
    first = pl.program_id(0) == 0
    for g, (_, dil) in enumerate(ATT_GROUPS):
        _attn_group_outputs(*refs[ATT_GROUP_IN * g:ATT_GROUP_IN * (g + 1)], o_nat.at[g], m_nat.at[g], d_nat.at[g],
                            slabs, first, dil)
    ms = [m_nat[g] for g in range(len(ATT_GROUPS))]
    m_all = functools.reduce(jnp.maximum, ms)
    wts = [jnp.exp(m - m_all) * d_nat[g] for g, m in enumerate(ms)]
    num = sum(w * o_nat[g] for g, w in enumerate(wts))
    y_ref[...] = (num / sum(wts) * _silu(gate_ref[...])).astype(BF16)


def _attention(h, bias):
    s = h.shape[0]
    heads = ATT_HEADS_PER_GROUP
    in_specs, operands = [], []
    for g, (_, dil) in enumerate(ATT_GROUPS):
        blk = ATT_SPAN * dil
        per_step = ATT_ROWS // blk

        def spec(col, prev, g=g, blk=blk, per_step=per_step):
            base = (col + g) * heads
            if prev:
                return pl.BlockSpec((blk, ATT_HEAD_DIM), lambda n, hd: (jnp.maximum(n * per_step - 1, 0), base + hd))
            return pl.BlockSpec((ATT_ROWS, ATT_HEAD_DIM), lambda n, hd: (n, base + hd))

        in_specs += [spec(COL_Q, False), spec(COL_K, False), spec(COL_K, True), spec(COL_V, False), spec(COL_V, True),
                     pl.BlockSpec((None, ATT_SPAN, 2 * ATT_SPAN), lambda n, hd, g=g: (g * heads + hd, 0, 0))]
        operands += [h, h, h, h, h, bias]
    in_specs.append(pl.BlockSpec((ATT_ROWS, ATT_HEAD_DIM), lambda n, hd: (n, COL_B_GATE * heads + hd)))
    token_order = pltpu.VMEM((len(ATT_GROUPS), ATT_ROWS, LANES), F32)
    slab = pltpu.VMEM((ATT_DIRECT_STRIDE, ATT_ROWS // ATT_DIRECT_STRIDE, LANES), F32)
    return pl.pallas_call(
        _attention_kernel,
        grid=(s // ATT_ROWS, heads),
        in_specs=in_specs,
        out_specs=pl.BlockSpec((ATT_ROWS, ATT_HEAD_DIM), lambda n, hd: (n, hd)),
        out_shape=jax.ShapeDtypeStruct((s, BR_WIDTH), BF16),
        scratch_shapes=[token_order] * 3 + [slab] * ATT_SLABS,
        compiler_params=_params(("parallel", "arbitrary"), 56),
        name="dilated_attention",
    )(*operands, h)


def _head_sums(x):
    ri = lax.broadcasted_iota(jnp.int32, (PAIR, PAIR), 0)
    ci = lax.broadcasted_iota(jnp.int32, (PAIR, PAIR), 1)
    same_head = jnp.where((ri < RWKV_HEAD) == (ci < RWKV_HEAD), 1.0, 0.0).astype(BF16)
    return jnp.concatenate([_split_dot(x[:, p * PAIR:(p + 1) * PAIR], same_head, 2, 1) for p in range(N_PAIRS)], axis=1)


def _rwkv_prepare(r_ref, k_ref, v_ref, lora_ref, mu_r, mu_k, mu_v, mu_l, w0_ref, wup_ref, a0_ref, aup_ref,
                  kk_ref, ka_ref, rk_ref, carry, carry_l):
    t = r_ref.shape[0]

    def shift_mix(x, mu, prev_row):
        row = lax.broadcasted_iota(jnp.int32, x.shape, 0)
        x_prev = jnp.where(row == 0, prev_row, pltpu.roll(x, 1, 0))
        return x + mu * (x_prev - x)

    r_in, k_in, v_in, l_in = r_ref[...], k_ref[...], v_ref[...], lora_ref[...]
    r = shift_mix(r_in, mu_r[...], carry[0:1, :])
    kx = shift_mix(k_in, mu_k[...], carry[1:2, :])
    vv = shift_mix(v_in, mu_v[...], carry[2:3, :])
    lo = shift_mix(l_in, mu_l[...], carry_l[0:1, :])
    carry[0:1, :] = r_in[t - 1:t, :]
    carry[1:2, :] = k_in[t - 1:t, :]
    carry[2:3, :] = v_in[t - 1:t, :]
    carry_l[0:1, :] = l_in[t - 1:t, :]

    w_log = -_softplus(-(w0_ref[...] + _bdot(jnp.tanh(lo), wup_ref[...]))) - 0.5
    log_decay = -jnp.exp(w_log)
    a_icl = jax.nn.sigmoid(a0_ref[...] + _bdot(lo, aup_ref[...]))

    kk = kx * kk_ref[...]
    kk = kk / jnp.maximum(jnp.sqrt(_head_sums(kk * kk)), 1e-12)
    kc = kx * (1.0 + (a_icl - 1.0) * ka_ref[...])
    bonus = _head_sums(r * kc * rk_ref[...]) * vv
    return log_decay, r, kc, vv, -kk, kk * a_icl, bonus


def _stack_heads(x):
    lane = lax.broadcasted_iota(jnp.int32, x.shape, 1)
    return jnp.concatenate([jnp.where(lane < RWKV_HEAD, x, 0.0), jnp.where(lane >= RWKV_HEAD, x, 0.0)], axis=0)


def _time_indices():
    t = lax.broadcasted_iota(jnp.int32, (RWKV_CHUNK, PAIR), 0)
    s = lax.broadcasted_iota(jnp.int32, (RWKV_CHUNK, PAIR), 1) & (RWKV_CHUNK - 1)
    return t, s


def _unit_lower_inverse(a_strict):
    ti, si = _time_indices()

    def same_block(bits):
        return (ti >> bits) == (si >> bits)

    pw = [jnp.where(same_block(4), a, 0.0) for a in a_strict]
    x = [jnp.where(ti == si, 1.0, 0.0) + p for p in pw]
    for _ in range(3):
        pw = [_bdot(p, _stack_heads(p)) for p in pw]
        x = [xi + _bdot(xi, _stack_heads(p)) for xi, p in zip(x, pw)]
    for bits in (5, 6):
        join = same_block(bits) & jnp.logical_not(same_block(bits - 1))
        xe = [_bdot(xi, _stack_heads(jnp.where(join, a, 0.0))) for xi, a in zip(x, a_strict)]
        x = [xi + _bdot(t, _stack_heads(xi)) for xi, t in zip(x, xe)]
    return x


def _rwkv_chunk_transforms(lw_all, r_all, k_all, v_all, a_all, b_all):
    c = RWKV_CHUNK
    n = 2 * c
    ti = lax.broadcasted_iota(jnp.int32, (c, c), 0)
    si = lax.broadcasted_iota(jnp.int32, (c, c), 1)
    lower_ones = jnp.where(si <= ti, 1.0, 0.0)
    tt, ss = _time_indices()
    strict = tt > ss
    incl = tt >= ss
    ri = lax.broadcasted_iota(jnp.int32, (n, n), 0)
    ci = lax.broadcasted_iota(jnp.int32, (n, n), 1)
    same_head = (ri < RWKV_HEAD) == (ci < RWKV_HEAD)
    eye = ri == ci

    units = [(ch, p) for ch in range(lw_all.shape[0] // c) for p in range(N_PAIRS)]
    each = lambda f, *cols: [f(*args) for args in zip(*cols)]

    def split(x):
        return [x[ch * c:(ch + 1) * c, p * PAIR:(p + 1) * PAIR] for ch, p in units]

    lw, r, k, v, a, b = (split(x) for x in (lw_all, r_all, k_all, v_all, a_all, b_all))
    cs = each(lambda x: _split_dot(lower_ones, x, 1, 3), lw)
    c_end = each(lambda x: x[c - 1:c, :], cs)
    r_d = each(lambda x, y: x * jnp.exp(y), r, cs)
    a_d = each(lambda x, y, z: x * jnp.exp(y - z), a, cs, lw)
    b_i = each(lambda x, y: x * jnp.exp(-y), b, cs)
    k_i = each(lambda x, y: x * jnp.exp(-y), k, cs)
    b_e = each(lambda x, y, e: x * jnp.exp(e - y), b, cs, c_end)
    k_e = each(lambda x, y, e: x * jnp.exp(e - y), k, cs, c_end)
    v_s = each(_stack_heads, v)

    aa = each(lambda ad, rd, bi, ki: _bdot_nt(jnp.concatenate([ad, rd], axis=0),
                                              jnp.concatenate([_stack_heads(bi), _stack_heads(ki)], axis=0)),
              a_d, r_d, b_i, k_i)
    a_ab = each(lambda x: jnp.where(strict, x[0:c, 0:n], 0.0), aa)
    a_ak = each(lambda x: jnp.where(strict, x[0:c, n:2 * n], 0.0), aa)
    a_rb = each(lambda x: jnp.where(incl, x[c:n, 0:n], 0.0), aa)
    a_rk = each(lambda x: jnp.where(incl, x[c:n, n:2 * n], 0.0), aa)

    minv = _unit_lower_inverse(a_ab)
    w = each(lambda m, ad: _bdot(m, _stack_heads(ad)), minv, a_d)
    t1 = each(_bdot, a_ak, v_s)
    uv = each(lambda m, x: _bdot(m, _stack_heads(x)), minv, t1)
    q = each(lambda rd, x, y: rd + _bdot(x, _stack_heads(y)), r_d, a_rb, w)
    yc = each(lambda x, y, z, t: _bdot(x, _stack_heads(y)) + _bdot(z, t), a_rb, uv, a_rk, v_s)
    g = each(lambda e, x, y: jnp.where(eye, jnp.exp(e), 0.0) + jnp.where(same_head, _bdot_tn(x, y), 0.0), c_end, w, b_e)
    z = each(lambda u_, v_, be, ke: jnp.where(same_head, _bdot_tn(jnp.concatenate([u_, v_], axis=0),
                                                                    jnp.concatenate([be, ke], axis=0)), 0.0),
             uv, v, b_e, k_e)
    return {unit: terms for unit, *terms in zip(units, q, yc, g, z)}


def _rwkv_init(carry, carry_l, state, ybuf):
    carry[...] = jnp.zeros_like(carry)
    carry_l[...] = jnp.zeros_like(carry_l)
    state[...] = jnp.zeros_like(state)


def _rwkv_body(r_ref, k_ref, v_ref, lora_ref, gate_ref, mu_r, mu_k, mu_v, mu_l, w0_ref, wup_ref, a0_ref, aup_ref,
               kk_ref, ka_ref, rk_ref, gn_g, gn_b, o_ref, carry, carry_l, state, ybuf):
    c = RWKV_CHUNK
    chunks = r_ref.shape[0] // c
    *scan_inputs, bonus = _rwkv_prepare(r_ref, k_ref, v_ref, lora_ref, mu_r, mu_k, mu_v, mu_l, w0_ref, wup_ref,
                                        a0_ref, aup_ref, kk_ref, ka_ref, rk_ref, carry, carry_l)
    terms = _rwkv_chunk_transforms(*scan_inputs)

    pairs = range(N_PAIRS)
    sts = [state[:, p * PAIR:(p + 1) * PAIR] for p in pairs]
    starts = []
    for ch in range(chunks):
        starts.append(sts)
        sts = [_split_dot(sts[p], terms[ch, p][2], 2, 2) + terms[ch, p][3] for p in pairs]
    for p in pairs:
        state[:, p * PAIR:(p + 1) * PAIR] = sts[p]
    for ch in range(chunks):
        for p in pairs:
            q, yc = terms[ch, p][0], terms[ch, p][1]
            ybuf[ch * c:(ch + 1) * c, p * PAIR:(p + 1) * PAIR] = _split_dot(q, starts[ch][p], 2, 2, NT_DIMS) + yc

    wy = ybuf[...]
    inv_n = 1.0 / RWKV_HEAD
    mu = _head_sums(wy) * inv_n
    d = wy - mu
    var = _head_sums(d * d) * inv_n
    wy = d * lax.rsqrt(var + RWKV_GN_EPS) * gn_g[...] + gn_b[...]
    o_ref[...] = ((wy + bonus) * _silu(gate_ref[...])).astype(BF16)


def _rwkv_specs(tile, layer):
    vec, lora_w = _layer_param(layer, (1, BR_WIDTH)), _layer_param(layer, (LANES, BR_WIDTH))
    in_specs = [_row_block(tile, COL_C_R), _row_block(tile, COL_C_K), _row_block(tile, COL_C_V),
                pl.BlockSpec((tile, LANES), lambda i: (i, COL_C_LORA * (BR_WIDTH // LANES))),
                _row_block(tile, COL_C_GATE),
                vec, vec, vec, _layer_param(layer, (1, LANES)), vec, lora_w, vec, lora_w, vec, vec, vec, vec, vec]
    return in_specs, [pltpu.VMEM((SUBLANES, BR_WIDTH), F32), pltpu.VMEM((SUBLANES, LANES), F32),
                      pltpu.VMEM((PAIR, BR_WIDTH), F32), pltpu.VMEM((tile, BR_WIDTH), F32)]


def _recurrent_mixers_kernel(*refs):
    n_in = LRU_IN + CONF_IN + RWKV_IN
    ins, (o_a, o_d, o_c), scratch = refs[:n_in], refs[n_in:n_in + 3], refs[n_in + 3:]
    lru_in, conf_in, rwkv_in = ins[:LRU_IN], ins[LRU_IN:LRU_IN + CONF_IN], ins[LRU_IN + CONF_IN:]
    lru_s = scratch[:LRU_SCRATCH]
    conf_s = scratch[LRU_SCRATCH:LRU_SCRATCH + CONF_SCRATCH]
    rwkv_s = scratch[LRU_SCRATCH + CONF_SCRATCH:]

    @pl.when(pl.program_id(0) == 0)
    def _():
        _lru_init(*lru_s)
        _conf_init(*conf_s)
        _rwkv_init(*rwkv_s)

    _rwkv_body(*rwkv_in, o_c, *rwkv_s)
    _conf_body(*conf_in, o_d, *conf_s)
    _lru_body(*lru_in, o_a, *lru_s)


def _recurrent_mixers(h, layer, lru_args, conf_args, rwkv_args, tile=4 * RWKV_CHUNK):
    s = h.shape[0]
    (lru_specs, lru_scr), (conf_specs, conf_scr), (rwkv_specs, rwkv_scr) = (
        _lru_specs(tile, layer), _conf_specs(tile, layer), _rwkv_specs(tile, layer))
    assert (len(lru_specs), len(conf_specs), len(rwkv_specs)) == (LRU_IN, CONF_IN, RWKV_IN)
    out = pl.BlockSpec((tile, BR_WIDTH), lambda i: (i, 0))
    return pl.pallas_call(
        _recurrent_mixers_kernel,
        grid=(s // tile,),
        in_specs=lru_specs + conf_specs + rwkv_specs,
        out_specs=[out] * 3,
        out_shape=[jax.ShapeDtypeStruct((s, BR_WIDTH), BF16)] * 3,
        scratch_shapes=lru_scr + conf_scr + rwkv_scr,
        compiler_params=_params(("arbitrary",), 40),
        name="recurrent_mixers",
    )(h, h, *lru_args, h, h, h, *conf_args, h, h, h, h, h, *rwkv_args)


def _mix_kernel(xb_ref, *refs):
    ygs, wms, bms, wbrs = (refs[k * N_BRANCH:(k + 1) * N_BRANCH] for k in range(4))
    o_ref = refs[4 * N_BRANCH]
    xb = xb_ref[...]
    acc = None
    for n in range(N_BRANCH):
        gate = jax.nn.sigmoid(_bdot(xb, wms[n][...]) + bms[n][...])
        val = gate * _bdot(ygs[n][...], wbrs[n][...])
        acc = val if acc is None else acc + val
    o_ref[...] = acc.astype(BF16)


def _mix(xb, ygs, w_all, layer, bm, wbr_all, tm=1024, tn=256):
    s = xb.shape[0]
    nj = D_MODEL // tn
    per_branch = lambda make: [make(n) for n in range(N_BRANCH)]
    return pl.pallas_call(
        _mix_kernel,
        grid=(s // tm, nj),
        in_specs=[pl.BlockSpec((tm, D_MODEL), lambda i, j: (i, 0))]
        + per_branch(lambda n: pl.BlockSpec((tm, BR_WIDTH), lambda i, j: (i, 0)))
        + per_branch(lambda n: pl.BlockSpec((pl.Squeezed(), pl.Element(D_MODEL), pl.Element(tn)),
                                            lambda i, j: (layer, 0, ((BRANCH_IN + n * D_MODEL) // LANES
                                                                     + j * (tn // LANES)) * LANES)))
        + per_branch(lambda n: pl.BlockSpec((None, 1, tn), lambda i, j: (layer, 0, n * nj + j)))
        + per_branch(lambda n: pl.BlockSpec((None, None, BR_WIDTH, tn), lambda i, j: (layer, n, 0, j))),
        out_specs=pl.BlockSpec((tm, tn), lambda i, j: (i, j)),
        out_shape=jax.ShapeDtypeStruct((s, D_MODEL), BF16),
        compiler_params=_params(("parallel", "arbitrary"), 48),
        name="branch_mix",
    )(xb, *ygs, *([w_all] * N_BRANCH), *([bm] * N_BRANCH), *([wbr_all] * N_BRANCH))


def _out_kernel(mixed_ref, x_ref, w_ref, g_ref, b_ref, o_ref, ob_ref):
    y = ALPHA * x_ref[...] + jnp.dot(mixed_ref[...], w_ref[...], preferred_element_type=F32)
    mu = jnp.mean(y, axis=-1, keepdims=True)
    var = jnp.mean(jnp.square(y - mu), axis=-1, keepdims=True)
    out = (y - mu) * lax.rsqrt(var + LN_EPS) * g_ref[...] + b_ref[...]
    o_ref[...] = out
    ob_ref[...] = out.astype(BF16)


def _out_proj(mixed, x, layer, w, g, b, tm=512):
    s = x.shape[0]
    row = pl.BlockSpec((tm, D_MODEL), lambda i: (i, 0))
    vec = _layer_param(layer, (1, D_MODEL))
    return pl.pallas_call(
        _out_kernel,
        grid=(s // tm,),
        in_specs=[row, row, _layer_param(layer, (D_MODEL, D_MODEL)), vec, vec],
        out_specs=[row, row],
        out_shape=[jax.ShapeDtypeStruct((s, D_MODEL), F32), jax.ShapeDtypeStruct((s, D_MODEL), BF16)],
        compiler_params=_params(("parallel",), 52),
        name="out_proj_ln",
    )(mixed, x, w, g, b)


def _block_diag(w):
    depth, blocks, n, _ = w.shape
    eye = jnp.eye(blocks, dtype=w.dtype)
    return (eye[None, :, None, :, None] * w[:, :, :, None, :]).reshape(depth, blocks * n, blocks * n)


def kernel(x, att_rel_bias, w_in, b_in, lru_conv_w, lru_conv_b, lru_gate_a_w, lru_gate_a_b, lru_gate_x_w, lru_gate_x_b, lru_lambda, rwkv_mu, rwkv_w0, rwkv_w_up, rwkv_a0, rwkv_a_up, rwkv_k_k, rwkv_k_a, rwkv_r_k, rwkv_gn_g, rwkv_gn_b, conf_dw_w, conf_dw_b, conf_ln_g, conf_ln_b, w_br, w_out, ln_g, ln_b):
    bsz, s, d = x.shape
    assert bsz == 1 and d == D_MODEL and s % ATT_ROWS == 0
    vec = lambda t: t.reshape(DEPTH, 1, -1)
    b_h = vec(jnp.concatenate([b_in[:, :H_SPLIT * BR_WIDTH], b_in[:, C_GATE_START:BRANCH_IN]], axis=1))
    b_merge = vec(b_in[:, BRANCH_IN:])
    mu = rwkv_mu
    zpad = jnp.zeros((DEPTH, DECAY_RANK, BR_WIDTH), F32)
    wup = jnp.concatenate([rwkv_w_up, zpad], axis=1).astype(BF16)
    aup = jnp.concatenate([zpad, rwkv_a_up], axis=1).astype(BF16)
    lru_args = (lru_conv_w, vec(lru_conv_b), _block_diag(lru_gate_a_w).astype(BF16), vec(lru_gate_a_b),
                _block_diag(lru_gate_x_w).astype(BF16), vec(lru_gate_x_b), vec(lru_lambda))
    conf_args = (conf_dw_w, vec(conf_dw_b), vec(conf_ln_g), vec(conf_ln_b))
    rwkv_args = (vec(mu[:, :BR_WIDTH]), vec(mu[:, BR_WIDTH:2 * BR_WIDTH]), vec(mu[:, 2 * BR_WIDTH:3 * BR_WIDTH]),
                 vec(mu[:, 3 * BR_WIDTH:]), vec(rwkv_w0), wup, vec(rwkv_a0), aup, vec(rwkv_k_k), vec(rwkv_k_a),
                 vec(rwkv_r_k), vec(rwkv_gn_g), vec(rwkv_gn_b))
    w_out_bf16, ln_g, ln_b = w_out.astype(BF16), vec(ln_g), vec(ln_b)
    att_bias = _attn_bias(att_rel_bias)

    y = x.reshape(s, d)
    yb = y.astype(BF16)
    for layer in range(DEPTH):
        h = _in_proj(yb, w_in, layer, b_h)
        yg_b = _attention(h, att_bias)
        yg_a, yg_d, yg_c = _recurrent_mixers(h, layer, lru_args, conf_args, rwkv_args)
        mixed = _mix(yb, (yg_a, yg_b, yg_c, yg_d), w_in, layer, b_merge, w_br)
        y, yb = _out_proj(mixed, y, layer, w_out_bf16, ln_g, ln_b)
    return y.reshape(bsz, s, d)
```

```python
import functools
import math

import numpy as np
import jax
import jax.numpy as jnp
from jax import lax
from jax.experimental import pallas as pl
from jax.experimental.pallas import tpu as pltpu

D_MODEL = 2048
DEPTH = 2
N_BRANCH = 4
BR_WIDTH = 512
LRU_CONV = 4
LRU_C = 8.0
ATT_GROUPS = ((128, 1), (512, 4), (2048, 16))
ATT_HEADS_PER_GROUP = 4
ATT_HEAD_DIM = BR_WIDTH // ATT_HEADS_PER_GROUP
ATT_HEADS = len(ATT_GROUPS) * ATT_HEADS_PER_GROUP
ATT_QKV = ATT_HEADS * ATT_HEAD_DIM
ATT_SPAN = 128
N_BUCKETS = 32
MAX_DISTANCE = 2048
NEG_INF = -1e30
RWKV_HEAD = 64
DECAY_RANK = 64
ICLR_RANK = 64
RWKV_GN_EPS = 64e-5
CONF_KERNEL = 31
LN_EPS = 1e-5
ALPHA = (2.0 * DEPTH) ** 0.25

LANES = 128
SUBLANES = 8
MIB = 1024 * 1024

BRANCH_IN = 2 * BR_WIDTH + 3 * ATT_QKV + BR_WIDTH + (4 * BR_WIDTH + DECAY_RANK + ICLR_RANK) + 3 * BR_WIDTH
C_GATE_START = BRANCH_IN - 4 * BR_WIDTH
H_SPLIT = 16
H_BLOCKS = 20
H_WIDTH = H_BLOCKS * BR_WIDTH
COL_A_X, COL_A_GATE = 0, 1
COL_Q, COL_K, COL_V, COL_B_GATE = 2, 5, 8, 11
COL_C_R, COL_C_K, COL_C_V, COL_C_LORA = 12, 13, 14, 15
COL_C_GATE, COL_D_VAL, COL_D_GLU, COL_D_GATE = 16, 17, 18, 19

CONF_HALO = 32
RWKV_CHUNK = 64
PAIR = 2 * RWKV_HEAD
N_PAIRS = BR_WIDTH // PAIR

F32 = jnp.float32
BF16 = jnp.bfloat16


def _params(semantics, vmem_mib):
    return pltpu.CompilerParams(dimension_semantics=semantics, vmem_limit_bytes=vmem_mib * MIB)


def _bdot(a, b):
    return jnp.dot(a.astype(BF16), b.astype(BF16), preferred_element_type=F32)


def _bdot_nt(a, b):
    return lax.dot_general(a.astype(BF16), b.astype(BF16), (((1,), (1,)), ((), ())), preferred_element_type=F32)


def _bdot_tn(a, b):
    return lax.dot_general(a.astype(BF16), b.astype(BF16), (((0,), (0,)), ((), ())), preferred_element_type=F32)


NN_DIMS = (((1,), (0,)), ((), ()))
NT_DIMS = (((1,), (1,)), ((), ()))


def _bf16_parts(x, parts):
    out = []
    for _ in range(parts):
        hi = x.astype(BF16)
        out.append(hi)
        x = x - hi.astype(F32)
    return out


def _split_dot(a, b, a_parts, b_parts, dims=NN_DIMS):
    acc = None
    b_terms = _bf16_parts(b, b_parts)
    for i, ai in enumerate(_bf16_parts(a, a_parts)):
        for j, bj in enumerate(b_terms):
            if i + j < max(a_parts, b_parts):
                term = lax.dot_general(ai, bj, dims, preferred_element_type=F32)
                acc = term if acc is None else acc + term
    return acc


def _softplus(z):
    return jnp.maximum(z, 0.0) + jnp.log1p(jnp.exp(-jnp.abs(z)))


def _expm1_nonpos(z):
    u = jnp.exp(z)
    safe = jnp.where(u == 1.0, 0.5, u)
    return jnp.where(u == 1.0, z, jnp.where(u == 0.0, -1.0, (safe - 1.0) * z / jnp.log(safe)))


def _silu(z):
    return z * jax.nn.sigmoid(z)


def _in_proj_kernel(xb_ref, w_ref, b_ref, h_ref):
    h_ref[...] = _bdot(xb_ref[...], w_ref[...]) + b_ref[...]


def _h_source_column(block):
    return block * BR_WIDTH if block < H_SPLIT else C_GATE_START + (block - H_SPLIT) * BR_WIDTH


def _in_proj(xb, w_all, layer, b, tm=2048, tn=512):
    s, k = xb.shape
    assert (H_SPLIT * BR_WIDTH) % tn == 0 and tn % BR_WIDTH == 0
    per_tile = tn // BR_WIDTH
    starts = np.array([_h_source_column(j * per_tile) // LANES for j in range(H_WIDTH // tn)], np.int32)
    return pl.pallas_call(
        lambda starts_ref, *refs: _in_proj_kernel(*refs),
        grid_spec=pltpu.PrefetchScalarGridSpec(
            num_scalar_prefetch=1,
            grid=(s // tm, H_WIDTH // tn),
            in_specs=[
                pl.BlockSpec((tm, k), lambda i, j, st: (i, 0)),
                pl.BlockSpec((pl.Squeezed(), pl.Element(k), pl.Element(tn)),
                             lambda i, j, st: (layer, 0, st[j] * LANES)),
                pl.BlockSpec((None, 1, tn), lambda i, j, st: (layer, 0, j)),
            ],
            out_specs=pl.BlockSpec((tm, tn), lambda i, j, st: (i, j)),
        ),
        out_shape=jax.ShapeDtypeStruct((s, H_WIDTH), F32),
        compiler_params=_params(("parallel", "arbitrary"), 56),
        name="in_proj",
    )(jnp.asarray(starts), xb, w_all, b)


LRU_IN, CONF_IN, RWKV_IN = 9, 7, 18
LRU_SCRATCH, CONF_SCRATCH, RWKV_SCRATCH = 2, 2, 4


def _lru_init(ebuf, hc):
    ebuf[0:SUBLANES, :] = jnp.zeros((SUBLANES, BR_WIDTH), F32)
    hc[...] = jnp.zeros_like(hc)


def _lru_body(ax_ref, ag_ref, cw_ref, cb_ref, wa_ref, ba_ref, wx_ref, bx_ref, lam_ref, o_ref, ebuf, hc):
    t = ax_ref.shape[0]
    halo = SUBLANES
    x = ax_ref[...]
    ebuf[halo:halo + t, :] = x
    u = cb_ref[...] + jnp.zeros((t, BR_WIDTH), F32)
    for j in range(LRU_CONV):
        u = u + cw_ref[j:j + 1, :] * ebuf[pl.ds(halo - (LRU_CONV - 1) + j, t), :]
    ebuf[0:halo, :] = x[t - halo:t, :]

    gate_r = jax.nn.sigmoid(_bdot(u, wa_ref[...]) + ba_ref[...])
    gate_i = jax.nn.sigmoid(_bdot(u, wx_ref[...]) + bx_ref[...])
    log_a = -LRU_C * gate_r * _softplus(-lam_ref[...])
    a = jnp.exp(log_a)
    b = jnp.sqrt(-_expm1_nonpos(2.0 * log_a)) * (gate_i * u)

    row = lax.broadcasted_iota(jnp.int32, (t, BR_WIDTH), 0)
    shift = 1
    while shift < t:
        valid = row >= shift
        b = jnp.where(valid, a * pltpu.roll(b, shift, 0), 0.0) + b
        a = jnp.where(valid, a * pltpu.roll(a, shift, 0), a)
        shift *= 2
    h = a * hc[0:1, :] + b
    hc[0:1, :] = h[t - 1:t, :]
    o_ref[...] = (h * _silu(ag_ref[...])).astype(BF16)


def _row_block(tile, col):
    return pl.BlockSpec((tile, BR_WIDTH), lambda i: (i, col))


def _layer_param(layer, shape):
    return pl.BlockSpec((None,) + shape, lambda *_: (layer,) + (0,) * len(shape))


def _lru_specs(tile, layer):
    vec, mat = _layer_param(layer, (1, BR_WIDTH)), _layer_param(layer, (BR_WIDTH, BR_WIDTH))
    in_specs = [_row_block(tile, COL_A_X), _row_block(tile, COL_A_GATE), _layer_param(layer, (LRU_CONV, BR_WIDTH)),
                vec, mat, vec, mat, vec, vec]
    return in_specs, [pltpu.VMEM((tile + SUBLANES, BR_WIDTH), F32), pltpu.VMEM((SUBLANES, BR_WIDTH), F32)]


def _conf_init(ebuf, shifted):
    ebuf[0:CONF_HALO, :] = jnp.zeros((CONF_HALO, BR_WIDTH), F32)


def _conf_body(val_ref, glu_ref, gate_ref, w_ref, b_ref, g_ref, beta_ref, o_ref, ebuf, shifted):
    t = val_ref.shape[0]
    halo = CONF_HALO
    cu = val_ref[...] * jax.nn.sigmoid(glu_ref[...])
    ebuf[halo:halo + t, :] = cu
    for b in range(SUBLANES):
        span = t + (CONF_KERNEL - 1 - b) // SUBLANES * SUBLANES
        shifted[b, 0:span, :] = ebuf[pl.ds(halo - (CONF_KERNEL - 1) + b, span), :]
    acc = b_ref[...] + jnp.zeros((t, BR_WIDTH), F32)
    for j in range(CONF_KERNEL):
        b = j % SUBLANES
        acc = acc + w_ref[j:j + 1, :] * shifted[b, j - b:j - b + t, :]
    ebuf[0:halo, :] = cu[t - halo:t, :]

    mu = jnp.mean(acc, axis=-1, keepdims=True)
    var = jnp.mean(jnp.square(acc - mu), axis=-1, keepdims=True)
    ln = (acc - mu) * lax.rsqrt(var + LN_EPS) * g_ref[...] + beta_ref[...]
    o_ref[...] = (_silu(ln) * _silu(gate_ref[...])).astype(BF16)


def _conf_specs(tile, layer):
    vec = _layer_param(layer, (1, BR_WIDTH))
    in_specs = [_row_block(tile, COL_D_VAL), _row_block(tile, COL_D_GLU), _row_block(tile, COL_D_GATE),
                _layer_param(layer, (CONF_KERNEL, BR_WIDTH)), vec, vec, vec]
    return in_specs, [pltpu.VMEM((tile + CONF_HALO, BR_WIDTH), F32),
                      pltpu.VMEM((SUBLANES, tile + CONF_HALO - SUBLANES, BR_WIDTH), F32)]


def _t5_bucket(dist):
    max_exact = N_BUCKETS // 2
    large = max_exact + (np.log(np.maximum(dist, 1) / max_exact) / math.log(MAX_DISTANCE / max_exact)
                         * (N_BUCKETS - max_exact)).astype(np.int32)
    large = np.minimum(large, N_BUCKETS - 1)
    return np.where(dist < max_exact, dist, large).astype(np.int32)


def _bucket_index():
    qi = np.arange(ATT_SPAN)[:, None]
    kj = np.arange(2 * ATT_SPAN)[None, :]
    dist = qi + ATT_SPAN - kj
    valid = (dist >= 0) & (dist <= ATT_SPAN)
    per_group = [np.where(valid, _t5_bucket(np.clip(dist, 0, ATT_SPAN) * dil), -1) for _, dil in ATT_GROUPS]
    return np.stack(per_group).astype(np.int32)


def _bias_kernel(table_ref, bucket_ref, o_ref):
    head = pl.program_id(0)
    bucket = bucket_ref[...]
    acc = jnp.full(bucket.shape, NEG_INF, F32)
    for bkt in range(N_BUCKETS):
        acc = jnp.where(bucket == bkt, table_ref[bkt, head], acc)
    o_ref[...] = acc


def _attn_bias(table):
    blk = (None, ATT_SPAN, 2 * ATT_SPAN)
    return pl.pallas_call(
        _bias_kernel,
        grid=(ATT_HEADS,),
        in_specs=[pl.BlockSpec(memory_space=pltpu.SMEM),
                  pl.BlockSpec(blk, lambda hd: (hd // ATT_HEADS_PER_GROUP, 0, 0))],
        out_specs=pl.BlockSpec(blk, lambda hd: (hd, 0, 0)),
        out_shape=jax.ShapeDtypeStruct((ATT_HEADS, ATT_SPAN, 2 * ATT_SPAN), F32),
        compiler_params=_params(("parallel",), 32),
        name="attn_bias",
    )(table, jnp.asarray(_bucket_index()))


ATT_DIRECT_STRIDE = 4


def _residue_reader(ref, slab, dilation):
    if dilation == 1:
        return lambda b, r: ref[b * ATT_SPAN:(b + 1) * ATT_SPAN, :]
    if dilation <= ATT_DIRECT_STRIDE:
        return lambda b, r: ref[pl.ds(b * ATT_SPAN * dilation + r, ATT_SPAN, stride=dilation), :]
    inner, outer = ATT_DIRECT_STRIDE, dilation // ATT_DIRECT_STRIDE
    per = ref.shape[0] // inner
    for r0 in range(inner):
        slab[r0] = ref[pl.ds(r0, per, stride=inner), :]
    return lambda b, r: slab[r % inner, pl.ds(b * ATT_SPAN * outer + r // inner, ATT_SPAN, stride=outer), :]


def _residue_writer(ref, slab, dilation):
    if dilation == 1:
        def write(b, r, val):
            ref[b * ATT_SPAN:(b + 1) * ATT_SPAN, :] = val
        return write, lambda: None
    if dilation <= ATT_DIRECT_STRIDE:
        def write(b, r, val):
            ref[pl.ds(b * ATT_SPAN * dilation + r, ATT_SPAN, stride=dilation), :] = val
        return write, lambda: None
    inner, outer = ATT_DIRECT_STRIDE, dilation // ATT_DIRECT_STRIDE
    per = ref.shape[0] // inner

    def write(b, r, val):
        slab[r % inner, pl.ds(b * ATT_SPAN * outer + r // inner, ATT_SPAN, stride=outer), :] = val

    def flush():
        for r0 in range(inner):
            ref[pl.ds(r0, per, stride=inner), :] = slab[r0]

    return write, flush


ATT_ROWS = ATT_SPAN * max(dil for _, dil in ATT_GROUPS)
ATT_GROUP_IN = 6
ATT_SLABS = 8
ATT_UNITS_PER_STAGE = 4


def _attn_group_outputs(q_ref, kc_ref, kp_ref, vc_ref, vp_ref, bias_ref, o_nat, m_nat, d_nat, slabs, first, dilation):
    blocks = ATT_ROWS // (ATT_SPAN * dilation)
    scale = ATT_HEAD_DIM ** -0.5
    read_q, read_kc, read_kp, read_vc, read_vp = (
        _residue_reader(ref, slab, dilation) for ref, slab in zip((q_ref, kc_ref, kp_ref, vc_ref, vp_ref), slabs[:5]))
    (write_o, flush_o), (write_m, flush_m), (write_d, flush_d) = (
        _residue_writer(ref, slab, dilation) for ref, slab in zip((o_nat, m_nat, d_nat), slabs[5:]))
    def key(b, r):
        return (read_kp(0, r) if b < 0 else read_kc(b, r)).astype(BF16)

    def value(b, r):
        return (read_vp(0, r) if b < 0 else read_vc(b, r)).astype(BF16)

    bias_p = bias_ref[:, 0:ATT_SPAN]
    bias_c = bias_ref[:, ATT_SPAN:2 * ATT_SPAN]
    full = (ATT_SPAN, LANES)
    def logits(u):
        q = read_q(u["b"], u["r"]).astype(BF16)
        lp = _bdot_nt(q, key(u["b"] - 1, u["r"])) * scale + bias_p
        u["lp"] = jnp.where(first, NEG_INF, lp) if u["b"] == 0 else lp
        u["lc"] = _bdot_nt(q, key(u["b"], u["r"])) * scale + bias_c

    def row_max(u):
        u["m"] = jnp.max(jnp.maximum(u["lp"], u["lc"]), axis=-1, keepdims=True)

    def weights(u):
        u["pp"] = jnp.exp(u.pop("lp") - u["m"])
        u["pc"] = jnp.exp(u.pop("lc") - u["m"])

    def denominator(u):
        u["den"] = jnp.sum(u["pp"] + u["pc"], axis=-1, keepdims=True)

    def outputs(u):
        b, r = u["b"], u["r"]
        write_o(b, r, (_bdot(u.pop("pp"), value(b - 1, r)) + _bdot(u.pop("pc"), value(b, r))) / u["den"])
        write_m(b, r, jnp.broadcast_to(u["m"], full))
        write_d(b, r, jnp.broadcast_to(u["den"], full))

    stages = (logits, row_max, weights, denominator, outputs)
    units = [dict(b=b, r=r) for b in range(blocks) for r in range(dilation)]
    groups = [units[i:i + ATT_UNITS_PER_STAGE] for i in range(0, len(units), ATT_UNITS_PER_STAGE)]
    for tick in range(len(groups) + len(stages) - 1):
        for s, stage in reversed(list(enumerate(stages))):
            if 0 <= tick - s < len(groups):
                for u in groups[tick - s]:
                    stage(u)
    flush_o()
    flush_m()
    flush_d()


def _attention_kernel(*refs):
    n_in = ATT_GROUP_IN * len(ATT_GROUPS)
    gate_ref, y_ref = refs[n_in], refs[n_in + 1]
    o_nat, m_nat, d_nat = refs[n_in + 2:n_in + 5]
    slabs = refs[n_in + 5:]
    first = pl.program_id(0) == 0
    for g, (_, dil) in enumerate(ATT_GROUPS):
        _attn_group_outputs(*refs[ATT_GROUP_IN * g:ATT_GROUP_IN * (g + 1)], o_nat.at[g], m_nat.at[g], d_nat.at[g],
                            slabs, first, dil)
    ms = [m_nat[g] for g in range(len(ATT_GROUPS))]
    m_all = functools.reduce(jnp.maximum, ms)
    wts = [jnp.exp(m - m_all) * d_nat[g] for g, m in enumerate(ms)]
    num = sum(w * o_nat[g] for g, w in enumerate(wts))
    y_ref[...] = (num / sum(wts) * _silu(gate_ref[...])).astype(BF16)


def _attention(h, bias):
    s = h.shape[0]
    heads = ATT_HEADS_PER_GROUP
    in_specs, operands = [], []
    for g, (_, dil) in enumerate(ATT_GROUPS):
        blk = ATT_SPAN * dil
        per_step = ATT_ROWS // blk

        def spec(col, prev, g=g, blk=blk, per_step=per_step):
            base = (col + g) * heads
            if prev:
                return pl.BlockSpec((blk, ATT_HEAD_DIM), lambda n, hd: (jnp.maximum(n * per_step - 1, 0), base + hd))
            return pl.BlockSpec((ATT_ROWS, ATT_HEAD_DIM), lambda n, hd: (n, base + hd))

        in_specs += [spec(COL_Q, False), spec(COL_K, False), spec(COL_K, True), spec(COL_V, False), spec(COL_V, True),
                     pl.BlockSpec((None, ATT_SPAN, 2 * ATT_SPAN), lambda n, hd, g=g: (g * heads + hd, 0, 0))]
        operands += [h, h, h, h, h, bias]
    in_specs.append(pl.BlockSpec((ATT_ROWS, ATT_HEAD_DIM), lambda n, hd: (n, COL_B_GATE * heads + hd)))
    token_order = pltpu.VMEM((len(ATT_GROUPS), ATT_ROWS, LANES), F32)
    slab = pltpu.VMEM((ATT_DIRECT_STRIDE, ATT_ROWS // ATT_DIRECT_STRIDE, LANES), F32)
    return pl.pallas_call(
        _attention_kernel,
        grid=(s // ATT_ROWS, heads),
        in_specs=in_specs,
        out_specs=pl.BlockSpec((ATT_ROWS, ATT_HEAD_DIM), lambda n, hd: (n, hd)),
        out_shape=jax.ShapeDtypeStruct((s, BR_WIDTH), BF16),
        scratch_shapes=[token_order] * 3 + [slab] * ATT_SLABS,
        compiler_params=_params(("parallel", "arbitrary"), 56),
        name="dilated_attention",
    )(*operands, h)


def _head_sums(x):
    ri = lax.broadcasted_iota(jnp.int32, (PAIR, PAIR), 0)
    ci = lax.broadcasted_iota(jnp.int32, (PAIR, PAIR), 1)
    same_head = jnp.where((ri < RWKV_HEAD) == (ci < RWKV_HEAD), 1.0, 0.0).astype(BF16)
    return jnp.concatenate([_split_dot(x[:, p * PAIR:(p + 1) * PAIR], same_head, 2, 1) for p in range(N_PAIRS)], axis=1)


def _rwkv_prepare(r_ref, k_ref, v_ref, lora_ref, mu_r, mu_k, mu_v, mu_l, w0_ref, wup_ref, a0_ref, aup_ref,
                  kk_ref, ka_ref, rk_ref, carry, carry_l):
    t = r_ref.shape[0]

    def shift_mix(x, mu, prev_row):
        row = lax.broadcasted_iota(jnp.int32, x.shape, 0)
        x_prev = jnp.where(row == 0, prev_row, pltpu.roll(x, 1, 0))
        return x + mu * (x_prev - x)

    r_in, k_in, v_in, l_in = r_ref[...], k_ref[...], v_ref[...], lora_ref[...]
    r = shift_mix(r_in, mu_r[...], carry[0:1, :])
    kx = shift_mix(k_in, mu_k[...], carry[1:2, :])
    vv = shift_mix(v_in, mu_v[...], carry[2:3, :])
    lo = shift_mix(l_in, mu_l[...], carry_l[0:1, :])
    carry[0:1, :] = r_in[t - 1:t, :]
    carry[1:2, :] = k_in[t - 1:t, :]
    carry[2:3, :] = v_in[t - 1:t, :]
    carry_l[0:1, :] = l_in[t - 1:t, :]

    w_log = -_softplus(-(w0_ref[...] + _bdot(jnp.tanh(lo), wup_ref[...]))) - 0.5
    log_decay = -jnp.exp(w_log)
    a_icl = jax.nn.sigmoid(a0_ref[...] + _bdot(lo, aup_ref[...]))

    kk = kx * kk_ref[...]
    kk = kk / jnp.maximum(jnp.sqrt(_head_sums(kk * kk)), 1e-12)
    kc = kx * (1.0 + (a_icl - 1.0) * ka_ref[...])
    bonus = _head_sums(r * kc * rk_ref[...]) * vv
    return log_decay, r, kc, vv, -kk, kk * a_icl, bonus


def _stack_heads(x):
    lane = lax.broadcasted_iota(jnp.int32, x.shape, 1)
    return jnp.concatenate([jnp.where(lane < RWKV_HEAD, x, 0.0), jnp.where(lane >= RWKV_HEAD, x, 0.0)], axis=0)


def _time_indices():
    t = lax.broadcasted_iota(jnp.int32, (RWKV_CHUNK, PAIR), 0)
    s = lax.broadcasted_iota(jnp.int32, (RWKV_CHUNK, PAIR), 1) & (RWKV_CHUNK - 1)
    return t, s


def _unit_lower_inverse(a_strict):
    ti, si = _time_indices()

    def same_block(bits):
        return (ti >> bits) == (si >> bits)

    pw = [jnp.where(same_block(4), a, 0.0) for a in a_strict]
    x = [jnp.where(ti == si, 1.0, 0.0) + p for p in pw]
    for _ in range(3):
        pw = [_bdot(p, _stack_heads(p)) for p in pw]
        x = [xi + _bdot(xi, _stack_heads(p)) for xi, p in zip(x, pw)]
    for bits in (5, 6):
        join = same_block(bits) & jnp.logical_not(same_block(bits - 1))
        xe = [_bdot(xi, _stack_heads(jnp.where(join, a, 0.0))) for xi, a in zip(x, a_strict)]
        x = [xi + _bdot(t, _stack_heads(xi)) for xi, t in zip(x, xe)]
    return x


def _rwkv_chunk_transforms(lw_all, r_all, k_all, v_all, a_all, b_all):
    c = RWKV_CHUNK
    n = 2 * c
    ti = lax.broadcasted_iota(jnp.int32, (c, c), 0)
    si = lax.broadcasted_iota(jnp.int32, (c, c), 1)
    lower_ones = jnp.where(si <= ti, 1.0, 0.0)
    tt, ss = _time_indices()
    strict = tt > ss
    incl = tt >= ss
    ri = lax.broadcasted_iota(jnp.int32, (n, n), 0)
    ci = lax.broadcasted_iota(jnp.int32, (n, n), 1)
    same_head = (ri < RWKV_HEAD) == (ci < RWKV_HEAD)
    eye = ri == ci

    units = [(ch, p) for ch in range(lw_all.shape[0] // c) for p in range(N_PAIRS)]
    each = lambda f, *cols: [f(*args) for args in zip(*cols)]

    def split(x):
        return [x[ch * c:(ch + 1) * c, p * PAIR:(p + 1) * PAIR] for ch, p in units]

    lw, r, k, v, a, b = (split(x) for x in (lw_all, r_all, k_all, v_all, a_all, b_all))
    cs = each(lambda x: _split_dot(lower_ones, x, 1, 3), lw)
    c_end = each(lambda x: x[c - 1:c, :], cs)
    r_d = each(lambda x, y: x * jnp.exp(y), r, cs)
    a_d = each(lambda x, y, z: x * jnp.exp(y - z), a, cs, lw)
    b_i = each(lambda x, y: x * jnp.exp(-y), b, cs)
    k_i = each(lambda x, y: x * jnp.exp(-y), k, cs)
    b_e = each(lambda x, y, e: x * jnp.exp(e - y), b, cs, c_end)
    k_e = each(lambda x, y, e: x * jnp.exp(e - y), k, cs, c_end)
    v_s = each(_stack_heads, v)

    aa = each(lambda ad, rd, bi, ki: _bdot_nt(jnp.concatenate([ad, rd], axis=0),
                                              jnp.concatenate([_stack_heads(bi), _stack_heads(ki)], axis=0)),
              a_d, r_d, b_i, k_i)
    a_ab = each(lambda x: jnp.where(strict, x[0:c, 0:n], 0.0), aa)
    a_ak = each(lambda x: jnp.where(strict, x[0:c, n:2 * n], 0.0), aa)
    a_rb = each(lambda x: jnp.where(incl, x[c:n, 0:n], 0.0), aa)
    a_rk = each(lambda x: jnp.where(incl, x[c:n, n:2 * n], 0.0), aa)

    minv = _unit_lower_inverse(a_ab)
    w = each(lambda m, ad: _bdot(m, _stack_heads(ad)), minv, a_d)
    t1 = each(_bdot, a_ak, v_s)
    uv = each(lambda m, x: _bdot(m, _stack_heads(x)), minv, t1)
    q = each(lambda rd, x, y: rd + _bdot(x, _stack_heads(y)), r_d, a_rb, w)
    yc = each(lambda x, y, z, t: _bdot(x, _stack_heads(y)) + _bdot(z, t), a_rb, uv, a_rk, v_s)
    g = each(lambda e, x, y: jnp.where(eye, jnp.exp(e), 0.0) + jnp.where(same_head, _bdot_tn(x, y), 0.0), c_end, w, b_e)
    z = each(lambda u_, v_, be, ke: jnp.where(same_head, _bdot_tn(jnp.concatenate([u_, v_], axis=0),
                                                                    jnp.concatenate([be, ke], axis=0)), 0.0),
             uv, v, b_e, k_e)
    return {unit: terms for unit, *terms in zip(units, q, yc, g, z)}


def _rwkv_init(carry, carry_l, state, ybuf):
    carry[...] = jnp.zeros_like(carry)
    carry_l[...] = jnp.zeros_like(carry_l)
    state[...] = jnp.zeros_like(state)


def _rwkv_body(r_ref, k_ref, v_ref, lora_ref, gate_ref, mu_r, mu_k, mu_v, mu_l, w0_ref, wup_ref, a0_ref, aup_ref,
               kk_ref, ka_ref, rk_ref, gn_g, gn_b, o_ref, carry, carry_l, state, ybuf):
    c = RWKV_CHUNK
    chunks = r_ref.shape[0] // c
    *scan_inputs, bonus = _rwkv_prepare(r_ref, k_ref, v_ref, lora_ref, mu_r, mu_k, mu_v, mu_l, w0_ref, wup_ref,
                                        a0_ref, aup_ref, kk_ref, ka_ref, rk_ref, carry, carry_l)
    terms = _rwkv_chunk_transforms(*scan_inputs)

    pairs = range(N_PAIRS)
    sts = [state[:, p * PAIR:(p + 1) * PAIR] for p in pairs]
    starts = []
    for ch in range(chunks):
        starts.append(sts)
        sts = [_split_dot(sts[p], terms[ch, p][2], 2, 2) + terms[ch, p][3] for p in pairs]
    for p in pairs:
        state[:, p * PAIR:(p + 1) * PAIR] = sts[p]
    for ch in range(chunks):
        for p in pairs:
            q, yc = terms[ch, p][0], terms[ch, p][1]
            ybuf[ch * c:(ch + 1) * c, p * PAIR:(p + 1) * PAIR] = _split_dot(q, starts[ch][p], 2, 2, NT_DIMS) + yc

    wy = ybuf[...]
    inv_n = 1.0 / RWKV_HEAD
    mu = _head_sums(wy) * inv_n
    d = wy - mu
    var = _head_sums(d * d) * inv_n
    wy = d * lax.rsqrt(var + RWKV_GN_EPS) * gn_g[...] + gn_b[...]
    o_ref[...] = ((wy + bonus) * _silu(gate_ref[...])).astype(BF16)


def _rwkv_specs(tile, layer):
    vec, lora_w = _layer_param(layer, (1, BR_WIDTH)), _layer_param(layer, (LANES, BR_WIDTH))
    in_specs = [_row_block(tile, COL_C_R), _row_block(tile, COL_C_K), _row_block(tile, COL_C_V),
                pl.BlockSpec((tile, LANES), lambda i: (i, COL_C_LORA * (BR_WIDTH // LANES))),
                _row_block(tile, COL_C_GATE),
                vec, vec, vec, _layer_param(layer, (1, LANES)), vec, lora_w, vec, lora_w, vec, vec, vec, vec, vec]
    return in_specs, [pltpu.VMEM((SUBLANES, BR_WIDTH), F32), pltpu.VMEM((SUBLANES, LANES), F32),
                      pltpu.VMEM((PAIR, BR_WIDTH), F32), pltpu.VMEM((tile, BR_WIDTH), F32)]


def _recurrent_mixers_kernel(*refs):
    n_in = LRU_IN + CONF_IN + RWKV_IN
    ins, (o_a, o_d, o_c), scratch = refs[:n_in], refs[n_in:n_in + 3], refs[n_in + 3:]
    lru_in, conf_in, rwkv_in = ins[:LRU_IN], ins[LRU_IN:LRU_IN + CONF_IN], ins[LRU_IN + CONF_IN:]
    lru_s = scratch[:LRU_SCRATCH]
    conf_s = scratch[LRU_SCRATCH:LRU_SCRATCH + CONF_SCRATCH]
    rwkv_s = scratch[LRU_SCRATCH + CONF_SCRATCH:]

    @pl.when(pl.program_id(0) == 0)
    def _():
        _lru_init(*lru_s)
        _conf_init(*conf_s)
        _rwkv_init(*rwkv_s)

    _rwkv_body(*rwkv_in, o_c, *rwkv_s)
    _conf_body(*conf_in, o_d, *conf_s)
    _lru_body(*lru_in, o_a, *lru_s)


def _recurrent_mixers(h, layer, lru_args, conf_args, rwkv_args, tile=4 * RWKV_CHUNK):
    s = h.shape[0]
    (lru_specs, lru_scr), (conf_specs, conf_scr), (rwkv_specs, rwkv_scr) = (
        _lru_specs(tile, layer), _conf_specs(tile, layer), _rwkv_specs(tile, layer))
    assert (len(lru_specs), len(conf_specs), len(rwkv_specs)) == (LRU_IN, CONF_IN, RWKV_IN)
    out = pl.BlockSpec((tile, BR_WIDTH), lambda i: (i, 0))
    return pl.pallas_call(
        _recurrent_mixers_kernel,
        grid=(s // tile,),
        in_specs=lru_specs + conf_specs + rwkv_specs,
        out_specs=[out] * 3,
        out_shape=[jax.ShapeDtypeStruct((s, BR_WIDTH), BF16)] * 3,
        scratch_shapes=lru_scr + conf_scr + rwkv_scr,
        compiler_params=_params(("arbitrary",), 40),
        name="recurrent_mixers",
    )(h, h, *lru_args, h, h, h, *conf_args, h, h, h, h, h, *rwkv_args)


def _mix_kernel(xb_ref, *refs):
    ygs, wms, bms, wbrs = (refs[k * N_BRANCH:(k + 1) * N_BRANCH] for k in range(4))
    o_ref = refs[4 * N_BRANCH]
    xb = xb_ref[...]
    acc = None
    for n in range(N_BRANCH):
        gate = jax.nn.sigmoid(_bdot(xb, wms[n][...]) + bms[n][...])
        val = gate * _bdot(ygs[n][...], wbrs[n][...])
        acc = val if acc is None else acc + val
    o_ref[...] = acc.astype(BF16)


def _mix(xb, ygs, w_all, layer, bm, wbr_all, tm=1024, tn=256):
    s = xb.shape[0]
    nj = D_MODEL // tn
    per_branch = lambda make: [make(n) for n in range(N_BRANCH)]
    return pl.pallas_call(
        _mix_kernel,
        grid=(s // tm, nj),
        in_specs=[pl.BlockSpec((tm, D_MODEL), lambda i, j: (i, 0))]
        + per_branch(lambda n: pl.BlockSpec((tm, BR_WIDTH), lambda i, j: (i, 0)))
        + per_branch(lambda n: pl.BlockSpec((pl.Squeezed(), pl.Element(D_MODEL), pl.Element(tn)),
                                            lambda i, j: (layer, 0, ((BRANCH_IN + n * D_MODEL) // LANES
                                                                     + j * (tn // LANES)) * LANES)))
        + per_branch(lambda n: pl.BlockSpec((None, 1, tn), lambda i, j: (layer, 0, n * nj + j)))
        + per_branch(lambda n: pl.BlockSpec((None, None, BR_WIDTH, tn), lambda i, j: (layer, n, 0, j))),
        out_specs=pl.BlockSpec((tm, tn), lambda i, j: (i, j)),
        out_shape=jax.ShapeDtypeStruct((s, D_MODEL), BF16),
        compiler_params=_params(("parallel", "arbitrary"), 48),
        name="branch_mix",
    )(xb, *ygs, *([w_all] * N_BRANCH), *([bm] * N_BRANCH), *([wbr_all] * N_BRANCH))


def _out_kernel(mixed_ref, x_ref, w_ref, g_ref, b_ref, o_ref, ob_ref):
    y = ALPHA * x_ref[...] + jnp.dot(mixed_ref[...], w_ref[...], preferred_element_type=F32)
    mu = jnp.mean(y, axis=-1, keepdims=True)
    var = jnp.mean(jnp.square(y - mu), axis=-1, keepdims=True)
    out = (y - mu) * lax.rsqrt(var + LN_EPS) * g_ref[...] + b_ref[...]
    o_ref[...] = out
    ob_ref[...] = out.astype(BF16)


def _out_proj(mixed, x, layer, w, g, b, tm=512):
    s = x.shape[0]
    row = pl.BlockSpec((tm, D_MODEL), lambda i: (i, 0))
    vec = _layer_param(layer, (1, D_MODEL))
    return pl.pallas_call(
        _out_kernel,
        grid=(s // tm,),
        in_specs=[row, row, _layer_param(layer, (D_MODEL, D_MODEL)), vec, vec],
        out_specs=[row, row],
        out_shape=[jax.ShapeDtypeStruct((s, D_MODEL), F32), jax.ShapeDtypeStruct((s, D_MODEL), BF16)],
        compiler_params=_params(("parallel",), 52),
        name="out_proj_ln",
    )(mixed, x, w, g, b)


def _block_diag(w):
    depth, blocks, n, _ = w.shape
    eye = jnp.eye(blocks, dtype=w.dtype)
    return (eye[None, :, None, :, None] * w[:, :, :, None, :]).reshape(depth, blocks * n, blocks * n)


def kernel(x, att_rel_bias, w_in, b_in, lru_conv_w, lru_conv_b, lru_gate_a_w, lru_gate_a_b, lru_gate_x_w, lru_gate_x_b, lru_lambda, rwkv_mu, rwkv_w0, rwkv_w_up, rwkv_a0, rwkv_a_up, rwkv_k_k, rwkv_k_a, rwkv_r_k, rwkv_gn_g, rwkv_gn_b, conf_dw_w, conf_dw_b, conf_ln_g, conf_ln_b, w_br, w_out, ln_g, ln_b):
    bsz, s, d = x.shape
    assert bsz == 1 and d == D_MODEL and s % ATT_ROWS == 0
    vec = lambda t: t.reshape(DEPTH, 1, -1)
    b_h = vec(jnp.concatenate([b_in[:, :H_SPLIT * BR_WIDTH], b_in[:, C_GATE_START:BRANCH_IN]], axis=1))
    b_merge = vec(b_in[:, BRANCH_IN:])
    mu = rwkv_mu
    zpad = jnp.zeros((DEPTH, DECAY_RANK, BR_WIDTH), F32)
    wup = jnp.concatenate([rwkv_w_up, zpad], axis=1).astype(BF16)
    aup = jnp.concatenate([zpad, rwkv_a_up], axis=1).astype(BF16)
    lru_args = (lru_conv_w, vec(lru_conv_b), _block_diag(lru_gate_a_w).astype(BF16), vec(lru_gate_a_b),
                _block_diag(lru_gate_x_w).astype(BF16), vec(lru_gate_x_b), vec(lru_lambda))
    conf_args = (conf_dw_w, vec(conf_dw_b), vec(conf_ln_g), vec(conf_ln_b))
    rwkv_args = (vec(mu[:, :BR_WIDTH]), vec(mu[:, BR_WIDTH:2 * BR_WIDTH]), vec(mu[:, 2 * BR_WIDTH:3 * BR_WIDTH]),
                 vec(mu[:, 3 * BR_WIDTH:]), vec(rwkv_w0), wup, vec(rwkv_a0), aup, vec(rwkv_k_k), vec(rwkv_k_a),
                 vec(rwkv_r_k), vec(rwkv_gn_g), vec(rwkv_gn_b))
    w_out_bf16, ln_g, ln_b = w_out.astype(BF16), vec(ln_g), vec(ln_b)
    att_bias = _attn_bias(att_rel_bias)

    y = x.reshape(s, d)
    yb = y.astype(BF16)
    for layer in range(DEPTH):
        h = _in_proj(yb, w_in, layer, b_h)
        yg_b = _attention(h, att_bias)
        yg_a, yg_d, yg_c = _recurrent_mixers(h, layer, lru_args, conf_args, rwkv_args)
        mixed = _mix(yb, (yg_a, yg_b, yg_c, yg_d), w_in, layer, b_merge, w_br)
        y, yb = _out_proj(mixed, y, layer, w_out_bf16, ln_g, ln_b)
    return y.reshape(bsz, s, d)
```

```python
import functools
import math

import numpy as np
import jax
import jax.numpy as jnp
from jax import lax
from jax.experimental import pallas as pl
from jax.experimental.pallas import tpu as pltpu

D_MODEL = 2048
DEPTH = 2
N_BRANCH = 4
BR_WIDTH = 512
LRU_CONV = 4
LRU_C = 8.0
ATT_GROUPS = ((128, 1), (512, 4), (2048, 16))
ATT_HEADS_PER_GROUP = 4
ATT_HEAD_DIM = BR_WIDTH // ATT_HEADS_PER_GROUP
ATT_HEADS = len(ATT_GROUPS) * ATT_HEADS_PER_GROUP
ATT_QKV = ATT_HEADS * ATT_HEAD_DIM
ATT_SPAN = 128
N_BUCKETS = 32
MAX_DISTANCE = 2048
NEG_INF = -1e30
RWKV_HEAD = 64
DECAY_RANK = 64
ICLR_RANK = 64
RWKV_GN_EPS = 64e-5
CONF_KERNEL = 31
LN_EPS = 1e-5
ALPHA = (2.0 * DEPTH) ** 0.25

LANES = 128
SUBLANES = 8
MIB = 1024 * 1024

BRANCH_IN = 2 * BR_WIDTH + 3 * ATT_QKV + BR_WIDTH + (4 * BR_WIDTH + DECAY_RANK + ICLR_RANK) + 3 * BR_WIDTH
C_GATE_START = BRANCH_IN - 4 * BR_WIDTH
LORA_START = C_GATE_START - (DECAY_RANK + ICLR_RANK)
H_SPLIT = LORA_START // BR_WIDTH
H_BLOCKS = H_SPLIT + 4
H_WIDTH = H_BLOCKS * BR_WIDTH
COL_A_X, COL_A_GATE = 0, 1
COL_Q, COL_K, COL_V, COL_B_GATE = 2, 5, 8, 11
COL_C_R, COL_C_K, COL_C_V = 12, 13, 14
COL_C_GATE, COL_D_VAL, COL_D_GLU, COL_D_GATE = 15, 16, 17, 18

CONF_HALO = 32
RWKV_CHUNK = 64
PAIR = 2 * RWKV_HEAD
N_PAIRS = BR_WIDTH // PAIR

F32 = jnp.float32
BF16 = jnp.bfloat16


def _params(semantics, vmem_mib):
    return pltpu.CompilerParams(dimension_semantics=semantics, vmem_limit_bytes=vmem_mib * MIB)


def _bdot(a, b):
    return jnp.dot(a.astype(BF16), b.astype(BF16), preferred_element_type=F32)


def _bdot_nt(a, b):
    return lax.dot_general(a.astype(BF16), b.astype(BF16), (((1,), (1,)), ((), ())), preferred_element_type=F32)


def _bdot_tn(a, b):
    return lax.dot_general(a.astype(BF16), b.astype(BF16), (((0,), (0,)), ((), ())), preferred_element_type=F32)


NN_DIMS = (((1,), (0,)), ((), ()))
NT_DIMS = (((1,), (1,)), ((), ()))


def _bf16_parts(x, parts):
    out = []
    for _ in range(parts):
        hi = x.astype(BF16)
        out.append(hi)
        x = x - hi.astype(F32)
    return out


def _split_dot(a, b, a_parts, b_parts, dims=NN_DIMS):
    acc = None
    b_terms = _bf16_parts(b, b_parts)
    for i, ai in enumerate(_bf16_parts(a, a_parts)):
        for j, bj in enumerate(b_terms):
            if i + j < max(a_parts, b_parts):
                term = lax.dot_general(ai, bj, dims, preferred_element_type=F32)
                acc = term if acc is None else acc + term
    return acc


def _softplus(z):
    return jnp.maximum(z, 0.0) + jnp.log1p(jnp.exp(-jnp.abs(z)))


def _expm1_nonpos(z):
    u = jnp.exp(z)
    safe = jnp.where(u == 1.0, 0.5, u)
    return jnp.where(u == 1.0, z, jnp.where(u == 0.0, -1.0, (safe - 1.0) * z / jnp.log(safe)))


def _silu(z):
    return z * jax.nn.sigmoid(z)


def _in_proj_kernel(xb_ref, w_ref, b_ref, h_ref):
    h_ref[...] = _bdot(xb_ref[...], w_ref[...]) + b_ref[...]


def _h_source_column(block):
    return block * BR_WIDTH if block < H_SPLIT else C_GATE_START + (block - H_SPLIT) * BR_WIDTH


def _in_proj(xb, w_all, layer, b, tm=2048, tn=512):
    s, k = xb.shape
    assert (H_SPLIT * BR_WIDTH) % tn == 0 and tn % BR_WIDTH == 0
    per_tile = tn // BR_WIDTH
    starts = np.array([_h_source_column(j * per_tile) // LANES for j in range(H_WIDTH // tn)], np.int32)
    return pl.pallas_call(
        lambda starts_ref, *refs: _in_proj_kernel(*refs),
        grid_spec=pltpu.PrefetchScalarGridSpec(
            num_scalar_prefetch=1,
            grid=(s // tm, H_WIDTH // tn),
            in_specs=[
                pl.BlockSpec((tm, k), lambda i, j, st: (i, 0)),
                pl.BlockSpec((pl.Squeezed(), pl.Element(k), pl.Element(tn)),
                             lambda i, j, st: (layer, 0, st[j] * LANES)),
                pl.BlockSpec((None, 1, tn), lambda i, j, st: (layer, 0, j)),
            ],
            out_specs=pl.BlockSpec((tm, tn), lambda i, j, st: (i, j)),
        ),
        out_shape=jax.ShapeDtypeStruct((s, H_WIDTH), F32),
        compiler_params=_params(("parallel", "arbitrary"), 56),
        name="in_proj",
    )(jnp.asarray(starts), xb, w_all, b)


LRU_IN, CONF_IN, RWKV_IN = 9, 7, 20
LRU_SCRATCH, CONF_SCRATCH, RWKV_SCRATCH = 2, 2, 5
RWKV_LORA_W = 4


def _lru_init(ebuf, hc):
    ebuf[0:SUBLANES, :] = jnp.zeros((SUBLANES, BR_WIDTH), F32)
    hc[...] = jnp.zeros_like(hc)


def _lru_body(ax_ref, ag_ref, cw_ref, cb_ref, wa_ref, ba_ref, wx_ref, bx_ref, lam_ref, o_ref, ebuf, hc):
    t = ax_ref.shape[0]
    halo = SUBLANES
    x = ax_ref[...]
    ebuf[halo:halo + t, :] = x
    u = cb_ref[...] + jnp.zeros((t, BR_WIDTH), F32)
    for j in range(LRU_CONV):
        u = u + cw_ref[j:j + 1, :] * ebuf[pl.ds(halo - (LRU_CONV - 1) + j, t), :]
    ebuf[0:halo, :] = x[t - halo:t, :]

    gate_r = jax.nn.sigmoid(_bdot(u, wa_ref[...]) + ba_ref[...])
    gate_i = jax.nn.sigmoid(_bdot(u, wx_ref[...]) + bx_ref[...])
    log_a = -LRU_C * gate_r * _softplus(-lam_ref[...])
    a = jnp.exp(log_a)
    b = jnp.sqrt(-_expm1_nonpos(2.0 * log_a)) * (gate_i * u)

    row = lax.broadcasted_iota(jnp.int32, (t, BR_WIDTH), 0)
    shift = 1
    while shift < t:
        valid = row >= shift
        b = jnp.where(valid, a * pltpu.roll(b, shift, 0), 0.0) + b
        a = jnp.where(valid, a * pltpu.roll(a, shift, 0), a)
        shift *= 2
    h = a * hc[0:1, :] + b
    hc[0:1, :] = h[t - 1:t, :]
    o_ref[...] = (h * _silu(ag_ref[...])).astype(BF16)


def _row_block(tile, col):
    return pl.BlockSpec((tile, BR_WIDTH), lambda i: (i, col))


def _layer_param(layer, shape):
    return pl.BlockSpec((None,) + shape, lambda *_: (layer,) + (0,) * len(shape))


def _lru_specs(tile, layer):
    vec, mat = _layer_param(layer, (1, BR_WIDTH)), _layer_param(layer, (BR_WIDTH, BR_WIDTH))
    in_specs = [_row_block(tile, COL_A_X), _row_block(tile, COL_A_GATE), _layer_param(layer, (LRU_CONV, BR_WIDTH)),
                vec, mat, vec, mat, vec, vec]
    return in_specs, [pltpu.VMEM((tile + SUBLANES, BR_WIDTH), F32), pltpu.VMEM((SUBLANES, BR_WIDTH), F32)]


def _conf_init(ebuf, shifted):
    ebuf[0:CONF_HALO, :] = jnp.zeros((CONF_HALO, BR_WIDTH), F32)


def _conf_body(val_ref, glu_ref, gate_ref, w_ref, b_ref, g_ref, beta_ref, o_ref, ebuf, shifted):
    t = val_ref.shape[0]
    halo = CONF_HALO
    cu = val_ref[...] * jax.nn.sigmoid(glu_ref[...])
    ebuf[halo:halo + t, :] = cu
    for b in range(SUBLANES):
        span = t + (CONF_KERNEL - 1 - b) // SUBLANES * SUBLANES
        shifted[b, 0:span, :] = ebuf[pl.ds(halo - (CONF_KERNEL - 1) + b, span), :]
    acc = b_ref[...] + jnp.zeros((t, BR_WIDTH), F32)
    for j in range(CONF_KERNEL):
        b = j % SUBLANES
        acc = acc + w_ref[j:j + 1, :] * shifted[b, j - b:j - b + t, :]
    ebuf[0:halo, :] = cu[t - halo:t, :]

    mu = jnp.mean(acc, axis=-1, keepdims=True)
    var = jnp.mean(jnp.square(acc - mu), axis=-1, keepdims=True)
    ln = (acc - mu) * lax.rsqrt(var + LN_EPS) * g_ref[...] + beta_ref[...]
    o_ref[...] = (_silu(ln) * _silu(gate_ref[...])).astype(BF16)


def _conf_specs(tile, layer):
    vec = _layer_param(layer, (1, BR_WIDTH))
    in_specs = [_row_block(tile, COL_D_VAL), _row_block(tile, COL_D_GLU), _row_block(tile, COL_D_GATE),
                _layer_param(layer, (CONF_KERNEL, BR_WIDTH)), vec, vec, vec]
    return in_specs, [pltpu.VMEM((tile + CONF_HALO, BR_WIDTH), F32),
                      pltpu.VMEM((SUBLANES, tile + CONF_HALO - SUBLANES, BR_WIDTH), F32)]


def _t5_bucket(dist):
    max_exact = N_BUCKETS // 2
    large = max_exact + (np.log(np.maximum(dist, 1) / max_exact) / math.log(MAX_DISTANCE / max_exact)
                         * (N_BUCKETS - max_exact)).astype(np.int32)
    large = np.minimum(large, N_BUCKETS - 1)
    return np.where(dist < max_exact, dist, large).astype(np.int32)


def _bucket_index():
    qi = np.arange(ATT_SPAN)[:, None]
    kj = np.arange(2 * ATT_SPAN)[None, :]
    dist = qi + ATT_SPAN - kj
    valid = (dist >= 0) & (dist <= ATT_SPAN)
    per_group = [np.where(valid, _t5_bucket(np.clip(dist, 0, ATT_SPAN) * dil), -1) for _, dil in ATT_GROUPS]
    return np.stack(per_group).astype(np.int32)


def _bias_kernel(table_ref, bucket_ref, o_ref):
    head = pl.program_id(0)
    bucket = bucket_ref[...]
    acc = jnp.full(bucket.shape, NEG_INF, F32)
    for bkt in range(N_BUCKETS):
        acc = jnp.where(bucket == bkt, table_ref[bkt, head], acc)
    o_ref[...] = acc


def _attn_bias(table):
    blk = (None, ATT_SPAN, 2 * ATT_SPAN)
    return pl.pallas_call(
        _bias_kernel,
        grid=(ATT_HEADS,),
        in_specs=[pl.BlockSpec(memory_space=pltpu.SMEM),
                  pl.BlockSpec(blk, lambda hd: (hd // ATT_HEADS_PER_GROUP, 0, 0))],
        out_specs=pl.BlockSpec(blk, lambda hd: (hd, 0, 0)),
        out_shape=jax.ShapeDtypeStruct((ATT_HEADS, ATT_SPAN, 2 * ATT_SPAN), F32),
        compiler_params=_params(("parallel",), 32),
        name="attn_bias",
    )(table, jnp.asarray(_bucket_index()))


ATT_DIRECT_STRIDE = 4


def _residue_reader(ref, slab, dilation):
    if dilation == 1:
        return lambda b, r: ref[b * ATT_SPAN:(b + 1) * ATT_SPAN, :]
    if dilation <= ATT_DIRECT_STRIDE:
        return lambda b, r: ref[pl.ds(b * ATT_SPAN * dilation + r, ATT_SPAN, stride=dilation), :]
    inner, outer = ATT_DIRECT_STRIDE, dilation // ATT_DIRECT_STRIDE
    per = ref.shape[0] // inner
    for r0 in range(inner):
        slab[r0] = ref[pl.ds(r0, per, stride=inner), :]
    return lambda b, r: slab[r % inner, pl.ds(b * ATT_SPAN * outer + r // inner, ATT_SPAN, stride=outer), :]


def _residue_writer(ref, slab, dilation):
    if dilation == 1:
        def write(b, r, val):
            ref[b * ATT_SPAN:(b + 1) * ATT_SPAN, :] = val
        return write, lambda: None
    if dilation <= ATT_DIRECT_STRIDE:
        def write(b, r, val):
            ref[pl.ds(b * ATT_SPAN * dilation + r, ATT_SPAN, stride=dilation), :] = val
        return write, lambda: None
    inner, outer = ATT_DIRECT_STRIDE, dilation // ATT_DIRECT_STRIDE
    per = ref.shape[0] // inner

    def write(b, r, val):
        slab[r % inner, pl.ds(b * ATT_SPAN * outer + r // inner, ATT_SPAN, stride=outer), :] = val

    def flush():
        for r0 in range(inner):
            ref[pl.ds(r0, per, stride=inner), :] = slab[r0]

    return write, flush


ATT_ROWS = ATT_SPAN * max(dil for _, dil in ATT_GROUPS)
ATT_GROUP_IN = 6
ATT_SLABS = 8
ATT_UNITS_PER_STAGE = 4


def _attn_group_outputs(q_ref, kc_ref, kp_ref, vc_ref, vp_ref, bias_ref, o_nat, m_nat, d_nat, slabs, first, dilation):
    blocks = ATT_ROWS // (ATT_SPAN * dilation)
    scale = ATT_HEAD_DIM ** -0.5
    read_q, read_kc, read_kp, read_vc, read_vp = (
        _residue_reader(ref, slab, dilation) for ref, slab in zip((q_ref, kc_ref, kp_ref, vc_ref, vp_ref), slabs[:5]))
    (write_o, flush_o), (write_m, flush_m), (write_d, flush_d) = (
        _residue_writer(ref, slab, dilation) for ref, slab in zip((o_nat, m_nat, d_nat), slabs[5:]))
    def key(b, r):
        return (read_kp(0, r) if b < 0 else read_kc(b, r)).astype(BF16)

    def value(b, r):
        return (read_vp(0, r) if b < 0 else read_vc(b, r)).astype(BF16)

    bias_p = bias_ref[:, 0:ATT_SPAN]
    bias_c = bias_ref[:, ATT_SPAN:2 * ATT_SPAN]
    full = (ATT_SPAN, LANES)
    def logits(u):
        q = read_q(u["b"], u["r"]).astype(BF16)
        lp = _bdot_nt(q, key(u["b"] - 1, u["r"])) * scale + bias_p
        u["lp"] = jnp.where(first, NEG_INF, lp) if u["b"] == 0 else lp
        u["lc"] = _bdot_nt(q, key(u["b"], u["r"])) * scale + bias_c

    def row_max(u):
        u["m"] = jnp.max(jnp.maximum(u["lp"], u["lc"]), axis=-1, keepdims=True)

    def weights(u):
        u["pp"] = jnp.exp(u.pop("lp") - u["m"])
        u["pc"] = jnp.exp(u.pop("lc") - u["m"])

    def denominator(u):
        u["den"] = jnp.sum(u["pp"] + u["pc"], axis=-1, keepdims=True)

    def outputs(u):
        b, r = u["b"], u["r"]
        write_o(b, r, (_bdot(u.pop("pp"), value(b - 1, r)) + _bdot(u.pop("pc"), value(b, r))) / u["den"])
        write_m(b, r, jnp.broadcast_to(u["m"], full))
        write_d(b, r, jnp.broadcast_to(u["den"], full))

    stages = (logits, row_max, weights, denominator, outputs)
    units = [dict(b=b, r=r) for b in range(blocks) for r in range(dilation)]
    groups = [units[i:i + ATT_UNITS_PER_STAGE] for i in range(0, len(units), ATT_UNITS_PER_STAGE)]
    for tick in range(len(groups) + len(stages) - 1):
        for s, stage in reversed(list(enumerate(stages))):
            if 0 <= tick - s < len(groups):
                for u in groups[tick - s]:
                    stage(u)
    flush_o()
    flush_m()
    flush_d()


def _attention_kernel(*refs):
    n_in = ATT_GROUP_IN * len(ATT_GROUPS)
    gate_ref, y_ref = refs[n_in], refs[n_in + 1]
    o_nat, m_nat, d_nat = refs[n_in + 2:n_in + 5]
    slabs = refs[n_in + 5:]
    first = pl.program_id(0) == 0
    for g, (_, dil) in enumerate(ATT_GROUPS):
        _attn_group_outputs(*refs[ATT_GROUP_IN * g:ATT_GROUP_IN * (g + 1)], o_nat.at[g], m_nat.at[g], d_nat.at[g],
                            slabs, first, dil)
    ms = [m_nat[g] for g in range(len(ATT_GROUPS))]
    m_all = functools.reduce(jnp.maximum, ms)
    wts = [jnp.exp(m - m_all) * d_nat[g] for g, m in enumerate(ms)]
    num = sum(w * o_nat[g] for g, w in enumerate(wts))
    y_ref[...] = (num / sum(wts) * _silu(gate_ref[...])).astype(BF16)


def _attention(h, bias):
    s = h.shape[0]
    heads = ATT_HEADS_PER_GROUP
    in_specs, operands = [], []
    for g, (_, dil) in enumerate(ATT_GROUPS):
        blk = ATT_SPAN * dil
        per_step = ATT_ROWS // blk

        def spec(col, prev, g=g, blk=blk, per_step=per_step):
            base = (col + g) * heads
            if prev:
                return pl.BlockSpec((blk, ATT_HEAD_DIM), lambda n, hd: (jnp.maximum(n * per_step - 1, 0), base + hd))
            return pl.BlockSpec((ATT_ROWS, ATT_HEAD_DIM), lambda n, hd: (n, base + hd))

        in_specs += [spec(COL_Q, False), spec(COL_K, False), spec(COL_K, True), spec(COL_V, False), spec(COL_V, True),
                     pl.BlockSpec((None, ATT_SPAN, 2 * ATT_SPAN), lambda n, hd, g=g: (g * heads + hd, 0, 0))]
        operands += [h, h, h, h, h, bias]
    in_specs.append(pl.BlockSpec((ATT_ROWS, ATT_HEAD_DIM), lambda n, hd: (n, COL_B_GATE * heads + hd)))
    token_order = pltpu.VMEM((len(ATT_GROUPS), ATT_ROWS, LANES), F32)
    slab = pltpu.VMEM((ATT_DIRECT_STRIDE, ATT_ROWS // ATT_DIRECT_STRIDE, LANES), F32)
    return pl.pallas_call(
        _attention_kernel,
        grid=(s // ATT_ROWS, heads),
        in_specs=in_specs,
        out_specs=pl.BlockSpec((ATT_ROWS, ATT_HEAD_DIM), lambda n, hd: (n, hd)),
        out_shape=jax.ShapeDtypeStruct((s, BR_WIDTH), BF16),
        scratch_shapes=[token_order] * 3 + [slab] * ATT_SLABS,
        compiler_params=_params(("parallel", "arbitrary"), 56),
        name="dilated_attention",
    )(*operands, h)


def _head_sums(x):
    ri = lax.broadcasted_iota(jnp.int32, (PAIR, PAIR), 0)
    ci = lax.broadcasted_iota(jnp.int32, (PAIR, PAIR), 1)
    same_head = jnp.where((ri < RWKV_HEAD) == (ci < RWKV_HEAD), 1.0, 0.0).astype(BF16)
    return jnp.concatenate([_split_dot(x[:, p * PAIR:(p + 1) * PAIR], same_head, 2, 1) for p in range(N_PAIRS)], axis=1)


def _rwkv_prepare(r_ref, k_ref, v_ref, xb_ref, wl_ref, bl_ref, mu_r, mu_k, mu_v, mu_l, w0_ref, wup_ref, a0_ref, aup_ref,
                  kk_ref, ka_ref, rk_ref, carry, carry_l):
    t = r_ref.shape[0]

    def shift_mix(x, mu, prev_row):
        row = lax.broadcasted_iota(jnp.int32, x.shape, 0)
        x_prev = jnp.where(row == 0, prev_row, pltpu.roll(x, 1, 0))
        return x + mu * (x_prev - x)

    r_in, k_in, v_in = r_ref[...], k_ref[...], v_ref[...]
    l_in = _bdot(xb_ref[...], wl_ref[...]) + bl_ref[...]
    r = shift_mix(r_in, mu_r[...], carry[0:1, :])
    kx = shift_mix(k_in, mu_k[...], carry[1:2, :])
    vv = shift_mix(v_in, mu_v[...], carry[2:3, :])
    lo = shift_mix(l_in, mu_l[...], carry_l[0:1, :])
    carry[0:1, :] = r_in[t - 1:t, :]
    carry[1:2, :] = k_in[t - 1:t, :]
    carry[2:3, :] = v_in[t - 1:t, :]
    carry_l[0:1, :] = l_in[t - 1:t, :]

    w_log = -_softplus(-(w0_ref[...] + _bdot(jnp.tanh(lo), wup_ref[...]))) - 0.5
    log_decay = -jnp.exp(w_log)
    a_icl = jax.nn.sigmoid(a0_ref[...] + _bdot(lo, aup_ref[...]))

    kk = kx * kk_ref[...]
    kk = kk / jnp.maximum(jnp.sqrt(_head_sums(kk * kk)), 1e-12)
    kc = kx * (1.0 + (a_icl - 1.0) * ka_ref[...])
    bonus = _head_sums(r * kc * rk_ref[...]) * vv
    return log_decay, r, kc, vv, -kk, kk * a_icl, bonus


def _stack_heads(x):
    lane = lax.broadcasted_iota(jnp.int32, x.shape, 1)
    return jnp.concatenate([jnp.where(lane < RWKV_HEAD, x, 0.0), jnp.where(lane >= RWKV_HEAD, x, 0.0)], axis=0)


def _time_indices():
    t = lax.broadcasted_iota(jnp.int32, (RWKV_CHUNK, PAIR), 0)
    s = lax.broadcasted_iota(jnp.int32, (RWKV_CHUNK, PAIR), 1) & (RWKV_CHUNK - 1)
    return t, s


def _unit_lower_inverse(a_strict):
    ti, si = _time_indices()

    def same_block(bits):
        return (ti >> bits) == (si >> bits)

    pw = [jnp.where(same_block(4), a, 0.0) for a in a_strict]
    x = [jnp.where(ti == si, 1.0, 0.0) + p for p in pw]
    for _ in range(3):
        pw = [_bdot(p, _stack_heads(p)) for p in pw]
        x = [xi + _bdot(xi, _stack_heads(p)) for xi, p in zip(x, pw)]
    for bits in (5, 6):
        join = same_block(bits) & jnp.logical_not(same_block(bits - 1))
        xe = [_bdot(xi, _stack_heads(jnp.where(join, a, 0.0))) for xi, a in zip(x, a_strict)]
        x = [xi + _bdot(t, _stack_heads(xi)) for xi, t in zip(x, xe)]
    return x


def _rwkv_chunk_transforms(lw_all, r_all, k_all, v_all, a_all, b_all):
    c = RWKV_CHUNK
    n = 2 * c
    ti = lax.broadcasted_iota(jnp.int32, (c, c), 0)
    si = lax.broadcasted_iota(jnp.int32, (c, c), 1)
    lower_ones = jnp.where(si <= ti, 1.0, 0.0)
    tt, ss = _time_indices()
    strict = tt > ss
    incl = tt >= ss
    ri = lax.broadcasted_iota(jnp.int32, (n, n), 0)
    ci = lax.broadcasted_iota(jnp.int32, (n, n), 1)
    same_head = (ri < RWKV_HEAD) == (ci < RWKV_HEAD)
    eye = ri == ci

    units = [(ch, p) for ch in range(lw_all.shape[0] // c) for p in range(N_PAIRS)]
    each = lambda f, *cols: [f(*args) for args in zip(*cols)]

    def split(x):
        return [x[ch * c:(ch + 1) * c, p * PAIR:(p + 1) * PAIR] for ch, p in units]

    lw, r, k, v, a, b = (split(x) for x in (lw_all, r_all, k_all, v_all, a_all, b_all))
    cs = each(lambda x: _split_dot(lower_ones, x, 1, 3), lw)
    c_end = each(lambda x: x[c - 1:c, :], cs)
    r_d = each(lambda x, y: x * jnp.exp(y), r, cs)
    a_d = each(lambda x, y, z: x * jnp.exp(y - z), a, cs, lw)
    b_i = each(lambda x, y: x * jnp.exp(-y), b, cs)
    k_i = each(lambda x, y: x * jnp.exp(-y), k, cs)
    b_e = each(lambda x, y, e: x * jnp.exp(e - y), b, cs, c_end)
    k_e = each(lambda x, y, e: x * jnp.exp(e - y), k, cs, c_end)
    v_s = each(_stack_heads, v)

    aa = each(lambda ad, rd, bi, ki: _bdot_nt(jnp.concatenate([ad, rd], axis=0),
                                              jnp.concatenate([_stack_heads(bi), _stack_heads(ki)], axis=0)),
              a_d, r_d, b_i, k_i)
    a_ab = each(lambda x: jnp.where(strict, x[0:c, 0:n], 0.0), aa)
    a_ak = each(lambda x: jnp.where(strict, x[0:c, n:2 * n], 0.0), aa)
    a_rb = each(lambda x: jnp.where(incl, x[c:n, 0:n], 0.0), aa)
    a_rk = each(lambda x: jnp.where(incl, x[c:n, n:2 * n], 0.0), aa)

    minv = _unit_lower_inverse(a_ab)
    w = each(lambda m, ad: _bdot(m, _stack_heads(ad)), minv, a_d)
    t1 = each(_bdot, a_ak, v_s)
    uv = each(lambda m, x: _bdot(m, _stack_heads(x)), minv, t1)
    q = each(lambda rd, x, y: rd + _bdot(x, _stack_heads(y)), r_d, a_rb, w)
    yc = each(lambda x, y, z, t: _bdot(x, _stack_heads(y)) + _bdot(z, t), a_rb, uv, a_rk, v_s)
    g = each(lambda e, x, y: jnp.where(eye, jnp.exp(e), 0.0) + jnp.where(same_head, _bdot_tn(x, y), 0.0), c_end, w, b_e)
    z = each(lambda u_, v_, be, ke: jnp.where(same_head, _bdot_tn(jnp.concatenate([u_, v_], axis=0),
                                                                    jnp.concatenate([be, ke], axis=0)), 0.0),
             uv, v, b_e, k_e)
    return {unit: terms for unit, *terms in zip(units, q, yc, g, z)}


def _rwkv_init(carry, carry_l, state, ybuf, wl_bf16, wl_ref):
    carry[...] = jnp.zeros_like(carry)
    carry_l[...] = jnp.zeros_like(carry_l)
    state[...] = jnp.zeros_like(state)
    wl_bf16[...] = wl_ref[...].astype(BF16)


def _rwkv_body(r_ref, k_ref, v_ref, xb_ref, wl_ref, bl_ref, gate_ref, mu_r, mu_k, mu_v, mu_l, w0_ref, wup_ref, a0_ref,
               aup_ref, kk_ref, ka_ref, rk_ref, gn_g, gn_b, o_ref, carry, carry_l, state, ybuf, wl_bf16):
    c = RWKV_CHUNK
    chunks = r_ref.shape[0] // c
    *scan_inputs, bonus = _rwkv_prepare(r_ref, k_ref, v_ref, xb_ref, wl_bf16, bl_ref, mu_r, mu_k, mu_v, mu_l, w0_ref,
                                        wup_ref, a0_ref, aup_ref, kk_ref, ka_ref, rk_ref, carry, carry_l)
    terms = _rwkv_chunk_transforms(*scan_inputs)

    pairs = range(N_PAIRS)
    sts = [state[:, p * PAIR:(p + 1) * PAIR] for p in pairs]
    starts = []
    for ch in range(chunks):
        starts.append(sts)
        sts = [_split_dot(sts[p], terms[ch, p][2], 2, 2) + terms[ch, p][3] for p in pairs]
    for p in pairs:
        state[:, p * PAIR:(p + 1) * PAIR] = sts[p]
    for ch in range(chunks):
        for p in pairs:
            q, yc = terms[ch, p][0], terms[ch, p][1]
            ybuf[ch * c:(ch + 1) * c, p * PAIR:(p + 1) * PAIR] = _split_dot(q, starts[ch][p], 2, 2, NT_DIMS) + yc

    wy = ybuf[...]
    inv_n = 1.0 / RWKV_HEAD
    mu = _head_sums(wy) * inv_n
    d = wy - mu
    var = _head_sums(d * d) * inv_n
    wy = d * lax.rsqrt(var + RWKV_GN_EPS) * gn_g[...] + gn_b[...]
    o_ref[...] = ((wy + bonus) * _silu(gate_ref[...])).astype(BF16)


def _rwkv_specs(tile, layer):
    vec, lora_w = _layer_param(layer, (1, BR_WIDTH)), _layer_param(layer, (LANES, BR_WIDTH))
    lora_in = pl.BlockSpec((pl.Squeezed(), pl.Element(D_MODEL), pl.Element(DECAY_RANK + ICLR_RANK)),
                           lambda i: (layer, 0, LORA_START))
    in_specs = [_row_block(tile, COL_C_R), _row_block(tile, COL_C_K), _row_block(tile, COL_C_V),
                pl.BlockSpec((tile, D_MODEL), lambda i: (i, 0)), lora_in, _layer_param(layer, (1, LANES)),
                _row_block(tile, COL_C_GATE),
                vec, vec, vec, _layer_param(layer, (1, LANES)), vec, lora_w, vec, lora_w, vec, vec, vec, vec, vec]
    return in_specs, [pltpu.VMEM((SUBLANES, BR_WIDTH), F32), pltpu.VMEM((SUBLANES, LANES), F32),
                      pltpu.VMEM((PAIR, BR_WIDTH), F32), pltpu.VMEM((tile, BR_WIDTH), F32),
                      pltpu.VMEM((D_MODEL, DECAY_RANK + ICLR_RANK), BF16)]


def _recurrent_mixers_kernel(*refs):
    n_in = LRU_IN + CONF_IN + RWKV_IN
    ins, (o_a, o_d, o_c), scratch = refs[:n_in], refs[n_in:n_in + 3], refs[n_in + 3:]
    lru_in, conf_in, rwkv_in = ins[:LRU_IN], ins[LRU_IN:LRU_IN + CONF_IN], ins[LRU_IN + CONF_IN:]
    lru_s = scratch[:LRU_SCRATCH]
    conf_s = scratch[LRU_SCRATCH:LRU_SCRATCH + CONF_SCRATCH]
    rwkv_s = scratch[LRU_SCRATCH + CONF_SCRATCH:]

    @pl.when(pl.program_id(0) == 0)
    def _():
        _lru_init(*lru_s)
        _conf_init(*conf_s)
        _rwkv_init(*rwkv_s, rwkv_in[RWKV_LORA_W])

    _rwkv_body(*rwkv_in, o_c, *rwkv_s)
    _conf_body(*conf_in, o_d, *conf_s)
    _lru_body(*lru_in, o_a, *lru_s)


def _recurrent_mixers(h, xb, w_in, layer, lru_args, conf_args, lora_bias, rwkv_args, tile=4 * RWKV_CHUNK):
    s = h.shape[0]
    (lru_specs, lru_scr), (conf_specs, conf_scr), (rwkv_specs, rwkv_scr) = (
        _lru_specs(tile, layer), _conf_specs(tile, layer), _rwkv_specs(tile, layer))
    assert (len(lru_specs), len(conf_specs), len(rwkv_specs)) == (LRU_IN, CONF_IN, RWKV_IN)
    out = pl.BlockSpec((tile, BR_WIDTH), lambda i: (i, 0))
    return pl.pallas_call(
        _recurrent_mixers_kernel,
        grid=(s // tile,),
        in_specs=lru_specs + conf_specs + rwkv_specs,
        out_specs=[out] * 3,
        out_shape=[jax.ShapeDtypeStruct((s, BR_WIDTH), BF16)] * 3,
        scratch_shapes=lru_scr + conf_scr + rwkv_scr,
        compiler_params=_params(("arbitrary",), 40),
        name="recurrent_mixers",
    )(h, h, *lru_args, h, h, h, *conf_args, h, h, h, xb, w_in, lora_bias, h, *rwkv_args)


def _mix_kernel(xb_ref, *refs):
    ygs, wms, bms, wbrs = (refs[k * N_BRANCH:(k + 1) * N_BRANCH] for k in range(4))
    o_ref = refs[4 * N_BRANCH]
    xb = xb_ref[...]
    acc = None
    for n in range(N_BRANCH):
        gate = jax.nn.sigmoid(_bdot(xb, wms[n][...]) + bms[n][...])
        val = gate * _bdot(ygs[n][...], wbrs[n][...])
        acc = val if acc is None else acc + val
    o_ref[...] = acc.astype(BF16)


def _mix(xb, ygs, w_all, layer, bm, wbr_all, tm=1024, tn=256):
    s = xb.shape[0]
    nj = D_MODEL // tn
    per_branch = lambda make: [make(n) for n in range(N_BRANCH)]
    return pl.pallas_call(
        _mix_kernel,
        grid=(s // tm, nj),
        in_specs=[pl.BlockSpec((tm, D_MODEL), lambda i, j: (i, 0))]
        + per_branch(lambda n: pl.BlockSpec((tm, BR_WIDTH), lambda i, j: (i, 0)))
        + per_branch(lambda n: pl.BlockSpec((pl.Squeezed(), pl.Element(D_MODEL), pl.Element(tn)),
                                            lambda i, j: (layer, 0, ((BRANCH_IN + n * D_MODEL) // LANES
                                                                     + j * (tn // LANES)) * LANES)))
        + per_branch(lambda n: pl.BlockSpec((None, 1, tn), lambda i, j: (layer, 0, n * nj + j)))
        + per_branch(lambda n: pl.BlockSpec((None, None, BR_WIDTH, tn), lambda i, j: (layer, n, 0, j))),
        out_specs=pl.BlockSpec((tm, tn), lambda i, j: (i, j)),
        out_shape=jax.ShapeDtypeStruct((s, D_MODEL), BF16),
        compiler_params=_params(("parallel", "arbitrary"), 48),
        name="branch_mix",
    )(xb, *ygs, *([w_all] * N_BRANCH), *([bm] * N_BRANCH), *([wbr_all] * N_BRANCH))


def _out_kernel(mixed_ref, x_ref, w_ref, g_ref, b_ref, o_ref, ob_ref):
    y = ALPHA * x_ref[...] + jnp.dot(mixed_ref[...], w_ref[...], preferred_element_type=F32)
    mu = jnp.mean(y, axis=-1, keepdims=True)
    var = jnp.mean(jnp.square(y - mu), axis=-1, keepdims=True)
    out = (y - mu) * lax.rsqrt(var + LN_EPS) * g_ref[...] + b_ref[...]
    o_ref[...] = out
    ob_ref[...] = out.astype(BF16)


def _out_proj(mixed, x, layer, w, g, b, tm=512):
    s = x.shape[0]
    row = pl.BlockSpec((tm, D_MODEL), lambda i: (i, 0))
    vec = _layer_param(layer, (1, D_MODEL))
    return pl.pallas_call(
        _out_kernel,
        grid=(s // tm,),
        in_specs=[row, row, _layer_param(layer, (D_MODEL, D_MODEL)), vec, vec],
        out_specs=[row, row],
        out_shape=[jax.ShapeDtypeStruct((s, D_MODEL), F32), jax.ShapeDtypeStruct((s, D_MODEL), BF16)],
        compiler_params=_params(("parallel",), 52),
        name="out_proj_ln",
    )(mixed, x, w, g, b)


def _block_diag(w):
    depth, blocks, n, _ = w.shape
    eye = jnp.eye(blocks, dtype=w.dtype)
    return (eye[None, :, None, :, None] * w[:, :, :, None, :]).reshape(depth, blocks * n, blocks * n)


def kernel(x, att_rel_bias, w_in, b_in, lru_conv_w, lru_conv_b, lru_gate_a_w, lru_gate_a_b, lru_gate_x_w, lru_gate_x_b, lru_lambda, rwkv_mu, rwkv_w0, rwkv_w_up, rwkv_a0, rwkv_a_up, rwkv_k_k, rwkv_k_a, rwkv_r_k, rwkv_gn_g, rwkv_gn_b, conf_dw_w, conf_dw_b, conf_ln_g, conf_ln_b, w_br, w_out, ln_g, ln_b):
    bsz, s, d = x.shape
    assert bsz == 1 and d == D_MODEL and s % ATT_ROWS == 0
    vec = lambda t: t.reshape(DEPTH, 1, -1)
    b_h = vec(jnp.concatenate([b_in[:, :H_SPLIT * BR_WIDTH], b_in[:, C_GATE_START:BRANCH_IN]], axis=1))
    b_merge = vec(b_in[:, BRANCH_IN:])
    b_lora = vec(b_in[:, LORA_START:C_GATE_START])
    mu = rwkv_mu
    zpad = jnp.zeros((DEPTH, DECAY_RANK, BR_WIDTH), F32)
    wup = jnp.concatenate([rwkv_w_up, zpad], axis=1).astype(BF16)
    aup = jnp.concatenate([zpad, rwkv_a_up], axis=1).astype(BF16)
    lru_args = (lru_conv_w, vec(lru_conv_b), _block_diag(lru_gate_a_w).astype(BF16), vec(lru_gate_a_b),
                _block_diag(lru_gate_x_w).astype(BF16), vec(lru_gate_x_b), vec(lru_lambda))
    conf_args = (conf_dw_w, vec(conf_dw_b), vec(conf_ln_g), vec(conf_ln_b))
    rwkv_args = (vec(mu[:, :BR_WIDTH]), vec(mu[:, BR_WIDTH:2 * BR_WIDTH]), vec(mu[:, 2 * BR_WIDTH:3 * BR_WIDTH]),
                 vec(mu[:, 3 * BR_WIDTH:]), vec(rwkv_w0), wup, vec(rwkv_a0), aup, vec(rwkv_k_k), vec(rwkv_k_a),
                 vec(rwkv_r_k), vec(rwkv_gn_g), vec(rwkv_gn_b))
    w_out_bf16, ln_g, ln_b = w_out.astype(BF16), vec(ln_g), vec(ln_b)
    att_bias = _attn_bias(att_rel_bias)

    y = x.reshape(s, d)
    yb = y.astype(BF16)
    for layer in range(DEPTH):
        h = _in_proj(yb, w_in, layer, b_h)
        yg_b = _attention(h, att_bias)
        yg_a, yg_d, yg_c = _recurrent_mixers(h, yb, w_in, layer, lru_args, conf_args, b_lora, rwkv_args)
        mixed = _mix(yb, (yg_a, yg_b, yg_c, yg_d), w_in, layer, b_merge, w_br)
        y, yb = _out_proj(mixed, y, layer, w_out_bf16, ln_g, ln_b)
    return y.reshape(bsz, s, d)
```

```python
import functools
import math

import numpy as np
import jax
import jax.numpy as jnp
from jax import lax
from jax.experimental import pallas as pl
from jax.experimental.pallas import tpu as pltpu

D_MODEL = 2048
DEPTH = 2
N_BRANCH = 4
BR_WIDTH = 512
LRU_CONV = 4
LRU_C = 8.0
ATT_GROUPS = ((128, 1), (512, 4), (2048, 16))
ATT_HEADS_PER_GROUP = 4
ATT_HEAD_DIM = BR_WIDTH // ATT_HEADS_PER_GROUP
ATT_HEADS = len(ATT_GROUPS) * ATT_HEADS_PER_GROUP
ATT_QKV = ATT_HEADS * ATT_HEAD_DIM
ATT_SPAN = 128
N_BUCKETS = 32
MAX_DISTANCE = 2048
NEG_INF = -1e30
RWKV_HEAD = 64
DECAY_RANK = 64
ICLR_RANK = 64
RWKV_GN_EPS = 64e-5
CONF_KERNEL = 31
LN_EPS = 1e-5
ALPHA = (2.0 * DEPTH) ** 0.25

LANES = 128
SUBLANES = 8
MIB = 1024 * 1024

BRANCH_IN = 2 * BR_WIDTH + 3 * ATT_QKV + BR_WIDTH + (4 * BR_WIDTH + DECAY_RANK + ICLR_RANK) + 3 * BR_WIDTH
C_GATE_START = BRANCH_IN - 4 * BR_WIDTH
LORA_START = C_GATE_START - (DECAY_RANK + ICLR_RANK)
H_SPLIT = LORA_START // BR_WIDTH
H_BLOCKS = H_SPLIT + 4
H_WIDTH = H_BLOCKS * BR_WIDTH
COL_A_X, COL_A_GATE = 0, 1
COL_Q, COL_K, COL_V, COL_B_GATE = 2, 5, 8, 11
COL_C_R, COL_C_K, COL_C_V = 12, 13, 14
COL_C_GATE, COL_D_VAL, COL_D_GLU, COL_D_GATE = 15, 16, 17, 18

CONF_HALO = 32
RWKV_CHUNK = 64
PAIR = 2 * RWKV_HEAD
N_PAIRS = BR_WIDTH // PAIR

F32 = jnp.float32
BF16 = jnp.bfloat16


def _params(semantics, vmem_mib):
    return pltpu.CompilerParams(dimension_semantics=semantics, vmem_limit_bytes=vmem_mib * MIB)


def _bdot(a, b):
    return jnp.dot(a.astype(BF16), b.astype(BF16), preferred_element_type=F32)


def _bdot_nt(a, b):
    return lax.dot_general(a.astype(BF16), b.astype(BF16), (((1,), (1,)), ((), ())), preferred_element_type=F32)


def _bdot_tn(a, b):
    return lax.dot_general(a.astype(BF16), b.astype(BF16), (((0,), (0,)), ((), ())), preferred_element_type=F32)


NN_DIMS = (((1,), (0,)), ((), ()))
NT_DIMS = (((1,), (1,)), ((), ()))


def _bf16_parts(x, parts):
    out = []
    for _ in range(parts):
        hi = x.astype(BF16)
        out.append(hi)
        x = x - hi.astype(F32)
    return out


def _split_dot(a, b, a_parts, b_parts, dims=NN_DIMS):
    acc = None
    b_terms = _bf16_parts(b, b_parts)
    for i, ai in enumerate(_bf16_parts(a, a_parts)):
        for j, bj in enumerate(b_terms):
            if i + j < max(a_parts, b_parts):
                term = lax.dot_general(ai, bj, dims, preferred_element_type=F32)
                acc = term if acc is None else acc + term
    return acc


def _softplus(z):
    return jnp.maximum(z, 0.0) + jnp.log1p(jnp.exp(-jnp.abs(z)))


def _expm1_nonpos(z):
    u = jnp.exp(z)
    safe = jnp.where(u == 1.0, 0.5, u)
    return jnp.where(u == 1.0, z, jnp.where(u == 0.0, -1.0, (safe - 1.0) * z / jnp.log(safe)))


def _silu(z):
    return z * jax.nn.sigmoid(z)


def _in_proj_kernel(xb_ref, w_ref, b_ref, h_ref):
    h_ref[...] = _bdot(xb_ref[...], w_ref[...]) + b_ref[...]


def _h_source_column(block):
    return block * BR_WIDTH if block < H_SPLIT else C_GATE_START + (block - H_SPLIT) * BR_WIDTH


def _in_proj(xb, w_all, layer, b, tm=2048, tn=512):
    s, k = xb.shape
    assert (H_SPLIT * BR_WIDTH) % tn == 0 and tn % BR_WIDTH == 0
    per_tile = tn // BR_WIDTH
    starts = np.array([_h_source_column(j * per_tile) // LANES for j in range(H_WIDTH // tn)], np.int32)
    return pl.pallas_call(
        lambda starts_ref, *refs: _in_proj_kernel(*refs),
        grid_spec=pltpu.PrefetchScalarGridSpec(
            num_scalar_prefetch=1,
            grid=(s // tm, H_WIDTH // tn),
            in_specs=[
                pl.BlockSpec((tm, k), lambda i, j, st: (i, 0)),
                pl.BlockSpec((pl.Squeezed(), pl.Element(k), pl.Element(tn)),
                             lambda i, j, st: (layer, 0, st[j] * LANES)),
                pl.BlockSpec((None, 1, tn), lambda i, j, st: (layer, 0, j)),
            ],
            out_specs=pl.BlockSpec((tm, tn), lambda i, j, st: (i, j)),
        ),
        out_shape=jax.ShapeDtypeStruct((s, H_WIDTH), F32),
        compiler_params=_params(("parallel", "arbitrary"), 56),
        name="in_proj",
    )(jnp.asarray(starts), xb, w_all, b)


LRU_IN, CONF_IN, RWKV_IN = 9, 7, 20
LRU_SCRATCH, CONF_SCRATCH, RWKV_SCRATCH = 2, 2, 5
RWKV_LORA_W = 4


def _lru_init(ebuf, hc):
    ebuf[0:SUBLANES, :] = jnp.zeros((SUBLANES, BR_WIDTH), F32)
    hc[...] = jnp.zeros_like(hc)


def _lru_body(ax_ref, ag_ref, cw_ref, cb_ref, wa_ref, ba_ref, wx_ref, bx_ref, lam_ref, o_ref, ebuf, hc):
    t = ax_ref.shape[0]
    halo = SUBLANES
    x = ax_ref[...]
    ebuf[halo:halo + t, :] = x
    u = cb_ref[...] + jnp.zeros((t, BR_WIDTH), F32)
    for j in range(LRU_CONV):
        u = u + cw_ref[j:j + 1, :] * ebuf[pl.ds(halo - (LRU_CONV - 1) + j, t), :]
    ebuf[0:halo, :] = x[t - halo:t, :]

    gate_r = jax.nn.sigmoid(_bdot(u, wa_ref[...]) + ba_ref[...])
    gate_i = jax.nn.sigmoid(_bdot(u, wx_ref[...]) + bx_ref[...])
    log_a = -LRU_C * gate_r * _softplus(-lam_ref[...])
    a = jnp.exp(log_a)
    b = jnp.sqrt(-_expm1_nonpos(2.0 * log_a)) * (gate_i * u)

    row = lax.broadcasted_iota(jnp.int32, (t, BR_WIDTH), 0)
    shift = 1
    while shift < t:
        valid = row >= shift
        b = jnp.where(valid, a * pltpu.roll(b, shift, 0), 0.0) + b
        a = jnp.where(valid, a * pltpu.roll(a, shift, 0), a)
        shift *= 2
    h = a * hc[0:1, :] + b
    hc[0:1, :] = h[t - 1:t, :]
    o_ref[...] = (h * _silu(ag_ref[...])).astype(BF16)


def _row_block(tile, col):
    return pl.BlockSpec((tile, BR_WIDTH), lambda i: (i, col))


def _layer_param(layer, shape):
    return pl.BlockSpec((None,) + shape, lambda *_: (layer,) + (0,) * len(shape))


def _lru_specs(tile, layer):
    vec, mat = _layer_param(layer, (1, BR_WIDTH)), _layer_param(layer, (BR_WIDTH, BR_WIDTH))
    in_specs = [_row_block(tile, COL_A_X), _row_block(tile, COL_A_GATE), _layer_param(layer, (LRU_CONV, BR_WIDTH)),
                vec, mat, vec, mat, vec, vec]
    return in_specs, [pltpu.VMEM((tile + SUBLANES, BR_WIDTH), F32), pltpu.VMEM((SUBLANES, BR_WIDTH), F32)]


def _conf_init(ebuf, shifted):
    ebuf[0:CONF_HALO, :] = jnp.zeros((CONF_HALO, BR_WIDTH), F32)


def _conf_body(val_ref, glu_ref, gate_ref, w_ref, b_ref, g_ref, beta_ref, o_ref, ebuf, shifted):
    t = val_ref.shape[0]
    halo = CONF_HALO
    cu = val_ref[...] * jax.nn.sigmoid(glu_ref[...])
    ebuf[halo:halo + t, :] = cu
    for b in range(SUBLANES):
        span = t + (CONF_KERNEL - 1 - b) // SUBLANES * SUBLANES
        shifted[b, 0:span, :] = ebuf[pl.ds(halo - (CONF_KERNEL - 1) + b, span), :]
    acc = b_ref[...] + jnp.zeros((t, BR_WIDTH), F32)
    for j in range(CONF_KERNEL):
        b = j % SUBLANES
        acc = acc + w_ref[j:j + 1, :] * shifted[b, j - b:j - b + t, :]
    ebuf[0:halo, :] = cu[t - halo:t, :]

    mu = jnp.mean(acc, axis=-1, keepdims=True)
    var = jnp.mean(jnp.square(acc - mu), axis=-1, keepdims=True)
    ln = (acc - mu) * lax.rsqrt(var + LN_EPS) * g_ref[...] + beta_ref[...]
    o_ref[...] = (_silu(ln) * _silu(gate_ref[...])).astype(BF16)


def _conf_specs(tile, layer):
    vec = _layer_param(layer, (1, BR_WIDTH))
    in_specs = [_row_block(tile, COL_D_VAL), _row_block(tile, COL_D_GLU), _row_block(tile, COL_D_GATE),
                _layer_param(layer, (CONF_KERNEL, BR_WIDTH)), vec, vec, vec]
    return in_specs, [pltpu.VMEM((tile + CONF_HALO, BR_WIDTH), F32),
                      pltpu.VMEM((SUBLANES, tile + CONF_HALO - SUBLANES, BR_WIDTH), F32)]


def _t5_bucket(dist):
    max_exact = N_BUCKETS // 2
    large = max_exact + (np.log(np.maximum(dist, 1) / max_exact) / math.log(MAX_DISTANCE / max_exact)
                         * (N_BUCKETS - max_exact)).astype(np.int32)
    large = np.minimum(large, N_BUCKETS - 1)
    return np.where(dist < max_exact, dist, large).astype(np.int32)


def _bucket_index():
    qi = np.arange(ATT_SPAN)[:, None]
    kj = np.arange(2 * ATT_SPAN)[None, :]
    dist = qi + ATT_SPAN - kj
    valid = (dist >= 0) & (dist <= ATT_SPAN)
    per_group = [np.where(valid, _t5_bucket(np.clip(dist, 0, ATT_SPAN) * dil), -1) for _, dil in ATT_GROUPS]
    return np.stack(per_group).astype(np.int32)


def _bias_kernel(table_ref, bucket_ref, o_ref):
    head = pl.program_id(0)
    bucket = bucket_ref[...]
    acc = jnp.full(bucket.shape, NEG_INF, F32)
    for bkt in range(N_BUCKETS):
        acc = jnp.where(bucket == bkt, table_ref[bkt, head], acc)
    o_ref[...] = acc


def _attn_bias(table):
    blk = (None, ATT_SPAN, 2 * ATT_SPAN)
    return pl.pallas_call(
        _bias_kernel,
        grid=(ATT_HEADS,),
        in_specs=[pl.BlockSpec(memory_space=pltpu.SMEM),
                  pl.BlockSpec(blk, lambda hd: (hd // ATT_HEADS_PER_GROUP, 0, 0))],
        out_specs=pl.BlockSpec(blk, lambda hd: (hd, 0, 0)),
        out_shape=jax.ShapeDtypeStruct((ATT_HEADS, ATT_SPAN, 2 * ATT_SPAN), F32),
        compiler_params=_params(("parallel",), 32),
        name="attn_bias",
    )(table, jnp.asarray(_bucket_index()))


ATT_DIRECT_STRIDE = 4


def _residue_reader(ref, slab, dilation):
    if dilation == 1:
        return lambda b, r: ref[b * ATT_SPAN:(b + 1) * ATT_SPAN, :]
    if dilation <= ATT_DIRECT_STRIDE:
        return lambda b, r: ref[pl.ds(b * ATT_SPAN * dilation + r, ATT_SPAN, stride=dilation), :]
    inner, outer = ATT_DIRECT_STRIDE, dilation // ATT_DIRECT_STRIDE
    per = ref.shape[0] // inner
    for r0 in range(inner):
        slab[r0] = ref[pl.ds(r0, per, stride=inner), :]
    return lambda b, r: slab[r % inner, pl.ds(b * ATT_SPAN * outer + r // inner, ATT_SPAN, stride=outer), :]


def _residue_writer(ref, slab, dilation):
    if dilation == 1:
        def write(b, r, val):
            ref[b * ATT_SPAN:(b + 1) * ATT_SPAN, :] = val
        return write, lambda: None
    if dilation <= ATT_DIRECT_STRIDE:
        def write(b, r, val):
            ref[pl.ds(b * ATT_SPAN * dilation + r, ATT_SPAN, stride=dilation), :] = val
        return write, lambda: None
    inner, outer = ATT_DIRECT_STRIDE, dilation // ATT_DIRECT_STRIDE
    per = ref.shape[0] // inner

    def write(b, r, val):
        slab[r % inner, pl.ds(b * ATT_SPAN * outer + r // inner, ATT_SPAN, stride=outer), :] = val

    def flush():
        for r0 in range(inner):
            ref[pl.ds(r0, per, stride=inner), :] = slab[r0]

    return write, flush


ATT_ROWS = ATT_SPAN * max(dil for _, dil in ATT_GROUPS)
ATT_GROUP_IN = 6
ATT_SLABS = 8
ATT_UNITS_PER_STAGE = 4


def _attn_group_outputs(q_ref, kc_ref, kp_ref, vc_ref, vp_ref, bias_ref, o_nat, m_nat, d_nat, slabs, first, dilation):
    blocks = ATT_ROWS // (ATT_SPAN * dilation)
    scale = ATT_HEAD_DIM ** -0.5
    read_q, read_kc, read_kp, read_vc, read_vp = (
        _residue_reader(ref, slab, dilation) for ref, slab in zip((q_ref, kc_ref, kp_ref, vc_ref, vp_ref), slabs[:5]))
    (write_o, flush_o), (write_m, flush_m), (write_d, flush_d) = (
        _residue_writer(ref, slab, dilation) for ref, slab in zip((o_nat, m_nat, d_nat), slabs[5:]))
    def key(b, r):
        return (read_kp(0, r) if b < 0 else read_kc(b, r)).astype(BF16)

    def value(b, r):
        return (read_vp(0, r) if b < 0 else read_vc(b, r)).astype(BF16)

    bias = bias_ref[...]
    before_start = first & (lax.broadcasted_iota(jnp.int32, bias.shape, 1) < ATT_SPAN)
    full = (ATT_SPAN, LANES)

    def logits(u):
        b, r = u["b"], u["r"]
        q = read_q(b, r).astype(BF16)
        lg = _bdot_nt(q, jnp.concatenate([key(b - 1, r), key(b, r)], axis=0)) * scale + bias
        u["lg"] = jnp.where(before_start, NEG_INF, lg) if b == 0 else lg

    def row_max(u):
        u["m"] = jnp.max(u["lg"], axis=-1, keepdims=True)

    def weights(u):
        u["p"] = jnp.exp(u.pop("lg") - u["m"])

    def denominator(u):
        u["den"] = jnp.sum(u["p"], axis=-1, keepdims=True)

    def outputs(u):
        b, r = u["b"], u["r"]
        write_o(b, r, _bdot(u.pop("p"), jnp.concatenate([value(b - 1, r), value(b, r)], axis=0)) / u["den"])
        write_m(b, r, jnp.broadcast_to(u["m"], full))
        write_d(b, r, jnp.broadcast_to(u["den"], full))

    stages = (logits, row_max, weights, denominator, outputs)
    units = [dict(b=b, r=r) for b in range(blocks) for r in range(dilation)]
    groups = [units[i:i + ATT_UNITS_PER_STAGE] for i in range(0, len(units), ATT_UNITS_PER_STAGE)]
    for tick in range(len(groups) + len(stages) - 1):
        for s, stage in reversed(list(enumerate(stages))):
            if 0 <= tick - s < len(groups):
                for u in groups[tick - s]:
                    stage(u)
    flush_o()
    flush_m()
    flush_d()


def _attention_kernel(*refs):
    n_in = ATT_GROUP_IN * len(ATT_GROUPS)
    gate_ref, y_ref = refs[n_in], refs[n_in + 1]
    o_nat, m_nat, d_nat = refs[n_in + 2:n_in + 5]
    slabs = refs[n_in + 5:]
    first = pl.program_id(0) == 0
    for g, (_, dil) in enumerate(ATT_GROUPS):
        _attn_group_outputs(*refs[ATT_GROUP_IN * g:ATT_GROUP_IN * (g + 1)], o_nat.at[g], m_nat.at[g], d_nat.at[g],
                            slabs, first, dil)
    ms = [m_nat[g] for g in range(len(ATT_GROUPS))]
    m_all = functools.reduce(jnp.maximum, ms)
    wts = [jnp.exp(m - m_all) * d_nat[g] for g, m in enumerate(ms)]
    num = sum(w * o_nat[g] for g, w in enumerate(wts))
    y_ref[...] = (num / sum(wts) * _silu(gate_ref[...])).astype(BF16)


def _attention(h, bias):
    s = h.shape[0]
    heads = ATT_HEADS_PER_GROUP
    in_specs, operands = [], []
    for g, (_, dil) in enumerate(ATT_GROUPS):
        blk = ATT_SPAN * dil
        per_step = ATT_ROWS // blk

        def spec(col, prev, g=g, blk=blk, per_step=per_step):
            base = (col + g) * heads
            if prev:
                return pl.BlockSpec((blk, ATT_HEAD_DIM), lambda n, hd: (jnp.maximum(n * per_step - 1, 0), base + hd))
            return pl.BlockSpec((ATT_ROWS, ATT_HEAD_DIM), lambda n, hd: (n, base + hd))

        in_specs += [spec(COL_Q, False), spec(COL_K, False), spec(COL_K, True), spec(COL_V, False), spec(COL_V, True),
                     pl.BlockSpec((None, ATT_SPAN, 2 * ATT_SPAN), lambda n, hd, g=g: (g * heads + hd, 0, 0))]
        operands += [h, h, h, h, h, bias]
    in_specs.append(pl.BlockSpec((ATT_ROWS, ATT_HEAD_DIM), lambda n, hd: (n, COL_B_GATE * heads + hd)))
    token_order = pltpu.VMEM((len(ATT_GROUPS), ATT_ROWS, LANES), F32)
    slab = pltpu.VMEM((ATT_DIRECT_STRIDE, ATT_ROWS // ATT_DIRECT_STRIDE, LANES), F32)
    return pl.pallas_call(
        _attention_kernel,
        grid=(s // ATT_ROWS, heads),
        in_specs=in_specs,
        out_specs=pl.BlockSpec((ATT_ROWS, ATT_HEAD_DIM), lambda n, hd: (n, hd)),
        out_shape=jax.ShapeDtypeStruct((s, BR_WIDTH), BF16),
        scratch_shapes=[token_order] * 3 + [slab] * ATT_SLABS,
        compiler_params=_params(("parallel", "arbitrary"), 56),
        name="dilated_attention",
    )(*operands, h)


def _head_sums(x):
    ri = lax.broadcasted_iota(jnp.int32, (PAIR, PAIR), 0)
    ci = lax.broadcasted_iota(jnp.int32, (PAIR, PAIR), 1)
    same_head = jnp.where((ri < RWKV_HEAD) == (ci < RWKV_HEAD), 1.0, 0.0).astype(BF16)
    return jnp.concatenate([_split_dot(x[:, p * PAIR:(p + 1) * PAIR], same_head, 2, 1) for p in range(N_PAIRS)], axis=1)


def _rwkv_prepare(r_ref, k_ref, v_ref, xb_ref, wl_ref, bl_ref, mu_r, mu_k, mu_v, mu_l, w0_ref, wup_ref, a0_ref, aup_ref,
                  kk_ref, ka_ref, rk_ref, carry, carry_l):
    t = r_ref.shape[0]

    def shift_mix(x, mu, prev_row):
        row = lax.broadcasted_iota(jnp.int32, x.shape, 0)
        x_prev = jnp.where(row == 0, prev_row, pltpu.roll(x, 1, 0))
        return x + mu * (x_prev - x)

    r_in, k_in, v_in = r_ref[...], k_ref[...], v_ref[...]
    l_in = _bdot(xb_ref[...], wl_ref[...]) + bl_ref[...]
    r = shift_mix(r_in, mu_r[...], carry[0:1, :])
    kx = shift_mix(k_in, mu_k[...], carry[1:2, :])
    vv = shift_mix(v_in, mu_v[...], carry[2:3, :])
    lo = shift_mix(l_in, mu_l[...], carry_l[0:1, :])
    carry[0:1, :] = r_in[t - 1:t, :]
    carry[1:2, :] = k_in[t - 1:t, :]
    carry[2:3, :] = v_in[t - 1:t, :]
    carry_l[0:1, :] = l_in[t - 1:t, :]

    w_log = -_softplus(-(w0_ref[...] + _bdot(jnp.tanh(lo), wup_ref[...]))) - 0.5
    log_decay = -jnp.exp(w_log)
    a_icl = jax.nn.sigmoid(a0_ref[...] + _bdot(lo, aup_ref[...]))

    kk = kx * kk_ref[...]
    kk = kk / jnp.maximum(jnp.sqrt(_head_sums(kk * kk)), 1e-12)
    kc = kx * (1.0 + (a_icl - 1.0) * ka_ref[...])
    bonus = _head_sums(r * kc * rk_ref[...]) * vv
    return log_decay, r, kc, vv, -kk, kk * a_icl, bonus


def _stack_heads(x):
    lane = lax.broadcasted_iota(jnp.int32, x.shape, 1)
    return jnp.concatenate([jnp.where(lane < RWKV_HEAD, x, 0.0), jnp.where(lane >= RWKV_HEAD, x, 0.0)], axis=0)


def _time_indices():
    t = lax.broadcasted_iota(jnp.int32, (RWKV_CHUNK, PAIR), 0)
    s = lax.broadcasted_iota(jnp.int32, (RWKV_CHUNK, PAIR), 1) & (RWKV_CHUNK - 1)
    return t, s


def _unit_lower_inverse(a_strict):
    ti, si = _time_indices()

    def same_block(bits):
        return (ti >> bits) == (si >> bits)

    pw = [jnp.where(same_block(4), a, 0.0) for a in a_strict]
    x = [jnp.where(ti == si, 1.0, 0.0) + p for p in pw]
    for _ in range(3):
        pw = [_bdot(p, _stack_heads(p)) for p in pw]
        x = [xi + _bdot(xi, _stack_heads(p)) for xi, p in zip(x, pw)]
    for bits in (5, 6):
        join = same_block(bits) & jnp.logical_not(same_block(bits - 1))
        xe = [_bdot(xi, _stack_heads(jnp.where(join, a, 0.0))) for xi, a in zip(x, a_strict)]
        x = [xi + _bdot(t, _stack_heads(xi)) for xi, t in zip(x, xe)]
    return x


def _rwkv_chunk_transforms(lw_all, r_all, k_all, v_all, a_all, b_all):
    c = RWKV_CHUNK
    n = 2 * c
    ti = lax.broadcasted_iota(jnp.int32, (c, c), 0)
    si = lax.broadcasted_iota(jnp.int32, (c, c), 1)
    lower_ones = jnp.where(si <= ti, 1.0, 0.0)
    tt, ss = _time_indices()
    strict = tt > ss
    incl = tt >= ss
    ri = lax.broadcasted_iota(jnp.int32, (n, n), 0)
    ci = lax.broadcasted_iota(jnp.int32, (n, n), 1)
    same_head = (ri < RWKV_HEAD) == (ci < RWKV_HEAD)
    eye = ri == ci

    units = [(ch, p) for ch in range(lw_all.shape[0] // c) for p in range(N_PAIRS)]
    each = lambda f, *cols: [f(*args) for args in zip(*cols)]

    def split(x):
        return [x[ch * c:(ch + 1) * c, p * PAIR:(p + 1) * PAIR] for ch, p in units]

    lw, r, k, v, a, b = (split(x) for x in (lw_all, r_all, k_all, v_all, a_all, b_all))
    cs = each(lambda x: _split_dot(lower_ones, x, 1, 3), lw)
    c_end = each(lambda x: x[c - 1:c, :], cs)
    r_d = each(lambda x, y: x * jnp.exp(y), r, cs)
    a_d = each(lambda x, y, z: x * jnp.exp(y - z), a, cs, lw)
    b_i = each(lambda x, y: x * jnp.exp(-y), b, cs)
    k_i = each(lambda x, y: x * jnp.exp(-y), k, cs)
    b_e = each(lambda x, y, e: x * jnp.exp(e - y), b, cs, c_end)
    k_e = each(lambda x, y, e: x * jnp.exp(e - y), k, cs, c_end)
    v_s = each(_stack_heads, v)

    aa = each(lambda ad, rd, bi, ki: _bdot_nt(jnp.concatenate([ad, rd], axis=0),
                                              jnp.concatenate([_stack_heads(bi), _stack_heads(ki)], axis=0)),
              a_d, r_d, b_i, k_i)
    a_ab = each(lambda x: jnp.where(strict, x[0:c, 0:n], 0.0), aa)
    a_ak = each(lambda x: jnp.where(strict, x[0:c, n:2 * n], 0.0), aa)
    a_rb = each(lambda x: jnp.where(incl, x[c:n, 0:n], 0.0), aa)
    a_rk = each(lambda x: jnp.where(incl, x[c:n, n:2 * n], 0.0), aa)

    minv = _unit_lower_inverse(a_ab)
    w = each(lambda m, ad: _bdot(m, _stack_heads(ad)), minv, a_d)
    t1 = each(_bdot, a_ak, v_s)
    uv = each(lambda m, x: _bdot(m, _stack_heads(x)), minv, t1)
    q = each(lambda rd, x, y: rd + _bdot(x, _stack_heads(y)), r_d, a_rb, w)
    yc = each(lambda x, y, z, t: _bdot(x, _stack_heads(y)) + _bdot(z, t), a_rb, uv, a_rk, v_s)
    g = each(lambda e, x, y: jnp.where(eye, jnp.exp(e), 0.0) + jnp.where(same_head, _bdot_tn(x, y), 0.0), c_end, w, b_e)
    z = each(lambda u_, v_, be, ke: jnp.where(same_head, _bdot_tn(jnp.concatenate([u_, v_], axis=0),
                                                                    jnp.concatenate([be, ke], axis=0)), 0.0),
             uv, v, b_e, k_e)
    return {unit: terms for unit, *terms in zip(units, q, yc, g, z)}


def _rwkv_init(carry, carry_l, state, ybuf, wl_bf16, wl_ref):
    carry[...] = jnp.zeros_like(carry)
    carry_l[...] = jnp.zeros_like(carry_l)
    state[...] = jnp.zeros_like(state)
    wl_bf16[...] = wl_ref[...].astype(BF16)


def _rwkv_body(r_ref, k_ref, v_ref, xb_ref, wl_ref, bl_ref, gate_ref, mu_r, mu_k, mu_v, mu_l, w0_ref, wup_ref, a0_ref,
               aup_ref, kk_ref, ka_ref, rk_ref, gn_g, gn_b, o_ref, carry, carry_l, state, ybuf, wl_bf16):
    c = RWKV_CHUNK
    chunks = r_ref.shape[0] // c
    *scan_inputs, bonus = _rwkv_prepare(r_ref, k_ref, v_ref, xb_ref, wl_bf16, bl_ref, mu_r, mu_k, mu_v, mu_l, w0_ref,
                                        wup_ref, a0_ref, aup_ref, kk_ref, ka_ref, rk_ref, carry, carry_l)
    terms = _rwkv_chunk_transforms(*scan_inputs)

    pairs = range(N_PAIRS)
    sts = [state[:, p * PAIR:(p + 1) * PAIR] for p in pairs]
    starts = []
    for ch in range(chunks):
        starts.append(sts)
        sts = [_split_dot(sts[p], terms[ch, p][2], 2, 2) + terms[ch, p][3] for p in pairs]
    for p in pairs:
        state[:, p * PAIR:(p + 1) * PAIR] = sts[p]
    for ch in range(chunks):
        for p in pairs:
            q, yc = terms[ch, p][0], terms[ch, p][1]
            ybuf[ch * c:(ch + 1) * c, p * PAIR:(p + 1) * PAIR] = _split_dot(q, starts[ch][p], 2, 2, NT_DIMS) + yc

    wy = ybuf[...]
    inv_n = 1.0 / RWKV_HEAD
    mu = _head_sums(wy) * inv_n
    d = wy - mu
    var = _head_sums(d * d) * inv_n
    wy = d * lax.rsqrt(var + RWKV_GN_EPS) * gn_g[...] + gn_b[...]
    o_ref[...] = ((wy + bonus) * _silu(gate_ref[...])).astype(BF16)


def _rwkv_specs(tile, layer):
    vec, lora_w = _layer_param(layer, (1, BR_WIDTH)), _layer_param(layer, (LANES, BR_WIDTH))
    lora_in = pl.BlockSpec((pl.Squeezed(), pl.Element(D_MODEL), pl.Element(DECAY_RANK + ICLR_RANK)),
                           lambda i: (layer, 0, LORA_START))
    in_specs = [_row_block(tile, COL_C_R), _row_block(tile, COL_C_K), _row_block(tile, COL_C_V),
                pl.BlockSpec((tile, D_MODEL), lambda i: (i, 0)), lora_in, _layer_param(layer, (1, LANES)),
                _row_block(tile, COL_C_GATE),
                vec, vec, vec, _layer_param(layer, (1, LANES)), vec, lora_w, vec, lora_w, vec, vec, vec, vec, vec]
    return in_specs, [pltpu.VMEM((SUBLANES, BR_WIDTH), F32), pltpu.VMEM((SUBLANES, LANES), F32),
                      pltpu.VMEM((PAIR, BR_WIDTH), F32), pltpu.VMEM((tile, BR_WIDTH), F32),
                      pltpu.VMEM((D_MODEL, DECAY_RANK + ICLR_RANK), BF16)]


def _recurrent_mixers_kernel(*refs):
    n_in = LRU_IN + CONF_IN + RWKV_IN
    ins, (o_a, o_d, o_c), scratch = refs[:n_in], refs[n_in:n_in + 3], refs[n_in + 3:]
    lru_in, conf_in, rwkv_in = ins[:LRU_IN], ins[LRU_IN:LRU_IN + CONF_IN], ins[LRU_IN + CONF_IN:]
    lru_s = scratch[:LRU_SCRATCH]
    conf_s = scratch[LRU_SCRATCH:LRU_SCRATCH + CONF_SCRATCH]
    rwkv_s = scratch[LRU_SCRATCH + CONF_SCRATCH:]

    @pl.when(pl.program_id(0) == 0)
    def _():
        _lru_init(*lru_s)
        _conf_init(*conf_s)
        _rwkv_init(*rwkv_s, rwkv_in[RWKV_LORA_W])

    _rwkv_body(*rwkv_in, o_c, *rwkv_s)
    _conf_body(*conf_in, o_d, *conf_s)
    _lru_body(*lru_in, o_a, *lru_s)


def _recurrent_mixers(h, xb, w_in, layer, lru_args, conf_args, lora_bias, rwkv_args, tile=4 * RWKV_CHUNK):
    s = h.shape[0]
    (lru_specs, lru_scr), (conf_specs, conf_scr), (rwkv_specs, rwkv_scr) = (
        _lru_specs(tile, layer), _conf_specs(tile, layer), _rwkv_specs(tile, layer))
    assert (len(lru_specs), len(conf_specs), len(rwkv_specs)) == (LRU_IN, CONF_IN, RWKV_IN)
    out = pl.BlockSpec((tile, BR_WIDTH), lambda i: (i, 0))
    return pl.pallas_call(
        _recurrent_mixers_kernel,
        grid=(s // tile,),
        in_specs=lru_specs + conf_specs + rwkv_specs,
        out_specs=[out] * 3,
        out_shape=[jax.ShapeDtypeStruct((s, BR_WIDTH), BF16)] * 3,
        scratch_shapes=lru_scr + conf_scr + rwkv_scr,
        compiler_params=_params(("arbitrary",), 40),
        name="recurrent_mixers",
    )(h, h, *lru_args, h, h, h, *conf_args, h, h, h, xb, w_in, lora_bias, h, *rwkv_args)


def _mix_kernel(xb_ref, *refs):
    ygs, wms, bms, wbrs = (refs[k * N_BRANCH:(k + 1) * N_BRANCH] for k in range(4))
    o_ref = refs[4 * N_BRANCH]
    xb = xb_ref[...]
    acc = None
    for n in range(N_BRANCH):
        gate = jax.nn.sigmoid(_bdot(xb, wms[n][...]) + bms[n][...])
        val = gate * _bdot(ygs[n][...], wbrs[n][...])
        acc = val if acc is None else acc + val
    o_ref[...] = acc.astype(BF16)


def _mix(xb, ygs, w_all, layer, bm, wbr_all, tm=1024, tn=256):
    s = xb.shape[0]
    nj = D_MODEL // tn
    per_branch = lambda make: [make(n) for n in range(N_BRANCH)]
    return pl.pallas_call(
        _mix_kernel,
        grid=(s // tm, nj),
        in_specs=[pl.BlockSpec((tm, D_MODEL), lambda i, j: (i, 0))]
        + per_branch(lambda n: pl.BlockSpec((tm, BR_WIDTH), lambda i, j: (i, 0)))
        + per_branch(lambda n: pl.BlockSpec((pl.Squeezed(), pl.Element(D_MODEL), pl.Element(tn)),
                                            lambda i, j: (layer, 0, ((BRANCH_IN + n * D_MODEL) // LANES
                                                                     + j * (tn // LANES)) * LANES)))
        + per_branch(lambda n: pl.BlockSpec((None, 1, tn), lambda i, j: (layer, 0, n * nj + j)))
        + per_branch(lambda n: pl.BlockSpec((None, None, BR_WIDTH, tn), lambda i, j: (layer, n, 0, j))),
        out_specs=pl.BlockSpec((tm, tn), lambda i, j: (i, j)),
        out_shape=jax.ShapeDtypeStruct((s, D_MODEL), BF16),
        compiler_params=_params(("parallel", "arbitrary"), 48),
        name="branch_mix",
    )(xb, *ygs, *([w_all] * N_BRANCH), *([bm] * N_BRANCH), *([wbr_all] * N_BRANCH))


def _out_kernel(mixed_ref, x_ref, w_ref, g_ref, b_ref, o_ref, ob_ref):
    y = ALPHA * x_ref[...] + jnp.dot(mixed_ref[...], w_ref[...], preferred_element_type=F32)
    mu = jnp.mean(y, axis=-1, keepdims=True)
    var = jnp.mean(jnp.square(y - mu), axis=-1, keepdims=True)
    out = (y - mu) * lax.rsqrt(var + LN_EPS) * g_ref[...] + b_ref[...]
    o_ref[...] = out
    ob_ref[...] = out.astype(BF16)


def _out_proj(mixed, x, layer, w, g, b, tm=512):
    s = x.shape[0]
    row = pl.BlockSpec((tm, D_MODEL), lambda i: (i, 0))
    vec = _layer_param(layer, (1, D_MODEL))
    return pl.pallas_call(
        _out_kernel,
        grid=(s // tm,),
        in_specs=[row, row, _layer_param(layer, (D_MODEL, D_MODEL)), vec, vec],
        out_specs=[row, row],
        out_shape=[jax.ShapeDtypeStruct((s, D_MODEL), F32), jax.ShapeDtypeStruct((s, D_MODEL), BF16)],
        compiler_params=_params(("parallel",), 52),
        name="out_proj_ln",
    )(mixed, x, w, g, b)


def _block_diag(w):
    depth, blocks, n, _ = w.shape
    eye = jnp.eye(blocks, dtype=w.dtype)
    return (eye[None, :, None, :, None] * w[:, :, :, None, :]).reshape(depth, blocks * n, blocks * n)


def kernel(x, att_rel_bias, w_in, b_in, lru_conv_w, lru_conv_b, lru_gate_a_w, lru_gate_a_b, lru_gate_x_w, lru_gate_x_b, lru_lambda, rwkv_mu, rwkv_w0, rwkv_w_up, rwkv_a0, rwkv_a_up, rwkv_k_k, rwkv_k_a, rwkv_r_k, rwkv_gn_g, rwkv_gn_b, conf_dw_w, conf_dw_b, conf_ln_g, conf_ln_b, w_br, w_out, ln_g, ln_b):
    bsz, s, d = x.shape
    assert bsz == 1 and d == D_MODEL and s % ATT_ROWS == 0
    vec = lambda t: t.reshape(DEPTH, 1, -1)
    b_h = vec(jnp.concatenate([b_in[:, :H_SPLIT * BR_WIDTH], b_in[:, C_GATE_START:BRANCH_IN]], axis=1))
    b_merge = vec(b_in[:, BRANCH_IN:])
    b_lora = vec(b_in[:, LORA_START:C_GATE_START])
    mu = rwkv_mu
    zpad = jnp.zeros((DEPTH, DECAY_RANK, BR_WIDTH), F32)
    wup = jnp.concatenate([rwkv_w_up, zpad], axis=1).astype(BF16)
    aup = jnp.concatenate([zpad, rwkv_a_up], axis=1).astype(BF16)
    lru_args = (lru_conv_w, vec(lru_conv_b), _block_diag(lru_gate_a_w).astype(BF16), vec(lru_gate_a_b),
                _block_diag(lru_gate_x_w).astype(BF16), vec(lru_gate_x_b), vec(lru_lambda))
    conf_args = (conf_dw_w, vec(conf_dw_b), vec(conf_ln_g), vec(conf_ln_b))
    rwkv_args = (vec(mu[:, :BR_WIDTH]), vec(mu[:, BR_WIDTH:2 * BR_WIDTH]), vec(mu[:, 2 * BR_WIDTH:3 * BR_WIDTH]),
                 vec(mu[:, 3 * BR_WIDTH:]), vec(rwkv_w0), wup, vec(rwkv_a0), aup, vec(rwkv_k_k), vec(rwkv_k_a),
                 vec(rwkv_r_k), vec(rwkv_gn_g), vec(rwkv_gn_b))
    w_out_bf16, ln_g, ln_b = w_out.astype(BF16), vec(ln_g), vec(ln_b)
    att_bias = _attn_bias(att_rel_bias)

    y = x.reshape(s, d)
    yb = y.astype(BF16)
    for layer in range(DEPTH):
        h = _in_proj(yb, w_in, layer, b_h)
        yg_b = _attention(h, att_bias)
        yg_a, yg_d, yg_c = _recurrent_mixers(h, yb, w_in, layer, lru_args, conf_args, b_lora, rwkv_args)
        mixed = _mix(yb, (yg_a, yg_b, yg_c, yg_d), w_in, layer, b_merge, w_br)
        y, yb = _out_proj(mixed, y, layer, w_out_bf16, ln_g, ln_b)
    return y.reshape(bsz, s, d)
```

```python
import functools
import math

import numpy as np
import jax
import jax.numpy as jnp
from jax import lax
from jax.experimental import pallas as pl
from jax.experimental.pallas import tpu as pltpu

D_MODEL = 2048
DEPTH = 2
N_BRANCH = 4
BR_WIDTH = 512
LRU_CONV = 4
LRU_C = 8.0
ATT_GROUPS = ((128, 1), (512, 4), (2048, 16))
ATT_HEADS_PER_GROUP = 4
ATT_HEAD_DIM = BR_WIDTH // ATT_HEADS_PER_GROUP
ATT_HEADS = len(ATT_GROUPS) * ATT_HEADS_PER_GROUP
ATT_QKV = ATT_HEADS * ATT_HEAD_DIM
ATT_SPAN = 128
N_BUCKETS = 32
MAX_DISTANCE = 2048
NEG_INF = -1e30
RWKV_HEAD = 64
DECAY_RANK = 64
ICLR_RANK = 64
RWKV_GN_EPS = 64e-5
CONF_KERNEL = 31
LN_EPS = 1e-5
ALPHA = (2.0 * DEPTH) ** 0.25

LANES = 128
SUBLANES = 8
MIB = 1024 * 1024

BRANCH_IN = 2 * BR_WIDTH + 3 * ATT_QKV + BR_WIDTH + (4 * BR_WIDTH + DECAY_RANK + ICLR_RANK) + 3 * BR_WIDTH
C_GATE_START = BRANCH_IN - 4 * BR_WIDTH
LORA_START = C_GATE_START - (DECAY_RANK + ICLR_RANK)
H_SPLIT = LORA_START // BR_WIDTH
H_BLOCKS = H_SPLIT + 4
H_WIDTH = H_BLOCKS * BR_WIDTH
COL_A_X, COL_A_GATE = 0, 1
COL_Q, COL_K, COL_V, COL_B_GATE = 2, 5, 8, 11
COL_C_R, COL_C_K, COL_C_V = 12, 13, 14
COL_C_GATE, COL_D_VAL, COL_D_GLU, COL_D_GATE = 15, 16, 17, 18

CONF_HALO = 32
RWKV_CHUNK = 64
PAIR = 2 * RWKV_HEAD
N_PAIRS = BR_WIDTH // PAIR

F32 = jnp.float32
BF16 = jnp.bfloat16


def _params(semantics, vmem_mib):
    return pltpu.CompilerParams(dimension_semantics=semantics, vmem_limit_bytes=vmem_mib * MIB)


def _bdot(a, b):
    return jnp.dot(a.astype(BF16), b.astype(BF16), preferred_element_type=F32)


def _bdot_nt(a, b):
    return lax.dot_general(a.astype(BF16), b.astype(BF16), (((1,), (1,)), ((), ())), preferred_element_type=F32)


def _bdot_tn(a, b):
    return lax.dot_general(a.astype(BF16), b.astype(BF16), (((0,), (0,)), ((), ())), preferred_element_type=F32)


NN_DIMS = (((1,), (0,)), ((), ()))
NT_DIMS = (((1,), (1,)), ((), ()))


def _bf16_parts(x, parts):
    out = []
    for _ in range(parts):
        hi = x.astype(BF16)
        out.append(hi)
        x = x - hi.astype(F32)
    return out


def _split_dot(a, b, a_parts, b_parts, dims=NN_DIMS):
    acc = None
    b_terms = _bf16_parts(b, b_parts)
    for i, ai in enumerate(_bf16_parts(a, a_parts)):
        for j, bj in enumerate(b_terms):
            if i + j < max(a_parts, b_parts):
                term = lax.dot_general(ai, bj, dims, preferred_element_type=F32)
                acc = term if acc is None else acc + term
    return acc


def _softplus(z):
    return jnp.maximum(z, 0.0) + jnp.log1p(jnp.exp(-jnp.abs(z)))


def _expm1_nonpos(z):
    u = jnp.exp(z)
    safe = jnp.where(u == 1.0, 0.5, u)
    return jnp.where(u == 1.0, z, jnp.where(u == 0.0, -1.0, (safe - 1.0) * z / jnp.log(safe)))


def _silu(z):
    return z * jax.nn.sigmoid(z)


def _in_proj_kernel(xb_ref, w_ref, b_ref, h_ref):
    h_ref[...] = _bdot(xb_ref[...], w_ref[...]) + b_ref[...]


def _h_source_column(block):
    return block * BR_WIDTH if block < H_SPLIT else C_GATE_START + (block - H_SPLIT) * BR_WIDTH


def _in_proj(xb, w_all, layer, b, tm=2048, tn=512):
    s, k = xb.shape
    assert (H_SPLIT * BR_WIDTH) % tn == 0 and tn % BR_WIDTH == 0
    per_tile = tn // BR_WIDTH
    starts = np.array([_h_source_column(j * per_tile) // LANES for j in range(H_WIDTH // tn)], np.int32)
    return pl.pallas_call(
        lambda starts_ref, *refs: _in_proj_kernel(*refs),
        grid_spec=pltpu.PrefetchScalarGridSpec(
            num_scalar_prefetch=1,
            grid=(s // tm, H_WIDTH // tn),
            in_specs=[
                pl.BlockSpec((tm, k), lambda i, j, st: (i, 0)),
                pl.BlockSpec((pl.Squeezed(), pl.Element(k), pl.Element(tn)),
                             lambda i, j, st: (layer, 0, st[j] * LANES)),
                pl.BlockSpec((None, 1, tn), lambda i, j, st: (layer, 0, j)),
            ],
            out_specs=pl.BlockSpec((tm, tn), lambda i, j, st: (i, j)),
        ),
        out_shape=jax.ShapeDtypeStruct((s, H_WIDTH), F32),
        compiler_params=_params(("parallel", "arbitrary"), 56),
        name="in_proj",
    )(jnp.asarray(starts), xb, w_all, b)


LRU_IN, CONF_IN, RWKV_IN = 9, 7, 20
LRU_SCRATCH, CONF_SCRATCH, RWKV_SCRATCH = 2, 2, 5
RWKV_LORA_W = 4


def _lru_init(ebuf, hc):
    ebuf[0:SUBLANES, :] = jnp.zeros((SUBLANES, BR_WIDTH), F32)
    hc[...] = jnp.zeros_like(hc)


def _lru_body(ax_ref, ag_ref, cw_ref, cb_ref, wa_ref, ba_ref, wx_ref, bx_ref, lam_ref, o_ref, ebuf, hc):
    t = ax_ref.shape[0]
    halo = SUBLANES
    x = ax_ref[...]
    ebuf[halo:halo + t, :] = x
    u = cb_ref[...] + jnp.zeros((t, BR_WIDTH), F32)
    for j in range(LRU_CONV):
        u = u + cw_ref[j:j + 1, :] * ebuf[pl.ds(halo - (LRU_CONV - 1) + j, t), :]
    ebuf[0:halo, :] = x[t - halo:t, :]

    gate_r = jax.nn.sigmoid(_bdot(u, wa_ref[...]) + ba_ref[...])
    gate_i = jax.nn.sigmoid(_bdot(u, wx_ref[...]) + bx_ref[...])
    log_a = -LRU_C * gate_r * _softplus(-lam_ref[...])
    a = jnp.exp(log_a)
    b = jnp.sqrt(-_expm1_nonpos(2.0 * log_a)) * (gate_i * u)

    row = lax.broadcasted_iota(jnp.int32, (t, BR_WIDTH), 0)
    shift = 1
    while shift < t:
        valid = row >= shift
        b = jnp.where(valid, a * pltpu.roll(b, shift, 0), 0.0) + b
        a = jnp.where(valid, a * pltpu.roll(a, shift, 0), a)
        shift *= 2
    h = a * hc[0:1, :] + b
    hc[0:1, :] = h[t - 1:t, :]
    o_ref[...] = (h * _silu(ag_ref[...])).astype(BF16)


def _row_block(tile, col):
    return pl.BlockSpec((tile, BR_WIDTH), lambda i: (i, col))


def _layer_param(layer, shape):
    return pl.BlockSpec((None,) + shape, lambda *_: (layer,) + (0,) * len(shape))


def _lru_specs(tile, layer):
    vec, mat = _layer_param(layer, (1, BR_WIDTH)), _layer_param(layer, (BR_WIDTH, BR_WIDTH))
    in_specs = [_row_block(tile, COL_A_X), _row_block(tile, COL_A_GATE), _layer_param(layer, (LRU_CONV, BR_WIDTH)),
                vec, mat, vec, mat, vec, vec]
    return in_specs, [pltpu.VMEM((tile + SUBLANES, BR_WIDTH), F32), pltpu.VMEM((SUBLANES, BR_WIDTH), F32)]


def _conf_init(ebuf, shifted):
    ebuf[0:CONF_HALO, :] = jnp.zeros((CONF_HALO, BR_WIDTH), F32)


def _conf_body(val_ref, glu_ref, gate_ref, w_ref, b_ref, g_ref, beta_ref, o_ref, ebuf, shifted):
    t = val_ref.shape[0]
    halo = CONF_HALO
    cu = val_ref[...] * jax.nn.sigmoid(glu_ref[...])
    ebuf[halo:halo + t, :] = cu
    for b in range(SUBLANES):
        span = t + (CONF_KERNEL - 1 - b) // SUBLANES * SUBLANES
        shifted[b, 0:span, :] = ebuf[pl.ds(halo - (CONF_KERNEL - 1) + b, span), :]
    acc = b_ref[...] + jnp.zeros((t, BR_WIDTH), F32)
    for j in range(CONF_KERNEL):
        b = j % SUBLANES
        acc = acc + w_ref[j:j + 1, :] * shifted[b, j - b:j - b + t, :]
    ebuf[0:halo, :] = cu[t - halo:t, :]

    mu = jnp.mean(acc, axis=-1, keepdims=True)
    var = jnp.mean(jnp.square(acc - mu), axis=-1, keepdims=True)
    ln = (acc - mu) * lax.rsqrt(var + LN_EPS) * g_ref[...] + beta_ref[...]
    o_ref[...] = (_silu(ln) * _silu(gate_ref[...])).astype(BF16)


def _conf_specs(tile, layer):
    vec = _layer_param(layer, (1, BR_WIDTH))
    in_specs = [_row_block(tile, COL_D_VAL), _row_block(tile, COL_D_GLU), _row_block(tile, COL_D_GATE),
                _layer_param(layer, (CONF_KERNEL, BR_WIDTH)), vec, vec, vec]
    return in_specs, [pltpu.VMEM((tile + CONF_HALO, BR_WIDTH), F32),
                      pltpu.VMEM((SUBLANES, tile + CONF_HALO - SUBLANES, BR_WIDTH), F32)]


def _t5_bucket(dist):
    max_exact = N_BUCKETS // 2
    large = max_exact + (np.log(np.maximum(dist, 1) / max_exact) / math.log(MAX_DISTANCE / max_exact)
                         * (N_BUCKETS - max_exact)).astype(np.int32)
    large = np.minimum(large, N_BUCKETS - 1)
    return np.where(dist < max_exact, dist, large).astype(np.int32)


def _bucket_index():
    qi = np.arange(ATT_SPAN)[:, None]
    kj = np.arange(2 * ATT_SPAN)[None, :]
    dist = qi + ATT_SPAN - kj
    valid = (dist >= 0) & (dist <= ATT_SPAN)
    per_group = [np.where(valid, _t5_bucket(np.clip(dist, 0, ATT_SPAN) * dil), -1) for _, dil in ATT_GROUPS]
    return np.stack(per_group).astype(np.int32)


def _bias_kernel(table_ref, bucket_ref, o_ref):
    head = pl.program_id(0)
    bucket = bucket_ref[...]
    acc = jnp.full(bucket.shape, NEG_INF, F32)
    for bkt in range(N_BUCKETS):
        acc = jnp.where(bucket == bkt, table_ref[bkt, head], acc)
    o_ref[...] = acc


def _attn_bias(table):
    blk = (None, ATT_SPAN, 2 * ATT_SPAN)
    return pl.pallas_call(
        _bias_kernel,
        grid=(ATT_HEADS,),
        in_specs=[pl.BlockSpec(memory_space=pltpu.SMEM),
                  pl.BlockSpec(blk, lambda hd: (hd // ATT_HEADS_PER_GROUP, 0, 0))],
        out_specs=pl.BlockSpec(blk, lambda hd: (hd, 0, 0)),
        out_shape=jax.ShapeDtypeStruct((ATT_HEADS, ATT_SPAN, 2 * ATT_SPAN), F32),
        compiler_params=_params(("parallel",), 32),
        name="attn_bias",
    )(table, jnp.asarray(_bucket_index()))


ATT_DIRECT_STRIDE = 4


def _residue_reader(ref, slab, dilation):
    if dilation == 1:
        return lambda b, r: ref[b * ATT_SPAN:(b + 1) * ATT_SPAN, :]
    if dilation <= ATT_DIRECT_STRIDE:
        return lambda b, r: ref[pl.ds(b * ATT_SPAN * dilation + r, ATT_SPAN, stride=dilation), :]
    inner, outer = ATT_DIRECT_STRIDE, dilation // ATT_DIRECT_STRIDE
    per = ref.shape[0] // inner
    for r0 in range(inner):
        slab[r0] = ref[pl.ds(r0, per, stride=inner), :]
    return lambda b, r: slab[r % inner, pl.ds(b * ATT_SPAN * outer + r // inner, ATT_SPAN, stride=outer), :]


def _residue_writer(ref, slab, dilation):
    if dilation == 1:
        def write(b, r, val):
            ref[b * ATT_SPAN:(b + 1) * ATT_SPAN, :] = val
        return write, lambda: None
    if dilation <= ATT_DIRECT_STRIDE:
        def write(b, r, val):
            ref[pl.ds(b * ATT_SPAN * dilation + r, ATT_SPAN, stride=dilation), :] = val
        return write, lambda: None
    inner, outer = ATT_DIRECT_STRIDE, dilation // ATT_DIRECT_STRIDE
    per = ref.shape[0] // inner

    def write(b, r, val):
        slab[r % inner, pl.ds(b * ATT_SPAN * outer + r // inner, ATT_SPAN, stride=outer), :] = val

    def flush():
        for r0 in range(inner):
            ref[pl.ds(r0, per, stride=inner), :] = slab[r0]

    return write, flush


ATT_ROWS = ATT_SPAN * max(dil for _, dil in ATT_GROUPS)
ATT_GROUP_IN = 6
ATT_SLABS = 8
ATT_UNITS_PER_STAGE = 4


def _attn_group_outputs(q_ref, kc_ref, kp_ref, vc_ref, vp_ref, bias_ref, o_nat, m_nat, d_nat, slabs, first, dilation):
    blocks = ATT_ROWS // (ATT_SPAN * dilation)
    scale = ATT_HEAD_DIM ** -0.5
    read_q, read_kc, read_kp, read_vc, read_vp = (
        _residue_reader(ref, slab, dilation) for ref, slab in zip((q_ref, kc_ref, kp_ref, vc_ref, vp_ref), slabs[:5]))
    (write_o, flush_o), (write_m, flush_m), (write_d, flush_d) = (
        _residue_writer(ref, slab, dilation) for ref, slab in zip((o_nat, m_nat, d_nat), slabs[5:]))
    def key(b, r):
        return (read_kp(0, r) if b < 0 else read_kc(b, r)).astype(BF16)

    def value(b, r):
        return (read_vp(0, r) if b < 0 else read_vc(b, r)).astype(BF16)

    bias = bias_ref[...]
    before_start = first & (lax.broadcasted_iota(jnp.int32, bias.shape, 1) < ATT_SPAN)
    full = (ATT_SPAN, LANES)

    def logits(u):
        b, r = u["b"], u["r"]
        q = read_q(b, r).astype(BF16)
        lg = _bdot_nt(q, jnp.concatenate([key(b - 1, r), key(b, r)], axis=0)) * scale + bias
        u["lg"] = jnp.where(before_start, NEG_INF, lg) if b == 0 else lg

    def row_max(u):
        u["m"] = jnp.max(u["lg"], axis=-1, keepdims=True)

    def weights(u):
        u["p"] = jnp.exp(u.pop("lg") - u["m"])

    def denominator(u):
        u["den"] = jnp.sum(u["p"], axis=-1, keepdims=True)

    def outputs(u):
        b, r = u["b"], u["r"]
        write_o(b, r, _bdot(u.pop("p"), jnp.concatenate([value(b - 1, r), value(b, r)], axis=0)) / u["den"])
        write_m(b, r, jnp.broadcast_to(u["m"], full))
        write_d(b, r, jnp.broadcast_to(u["den"], full))

    stages = (logits, row_max, weights, denominator, outputs)
    units = [dict(b=b, r=r) for b in range(blocks) for r in range(dilation)]
    groups = [units[i:i + ATT_UNITS_PER_STAGE] for i in range(0, len(units), ATT_UNITS_PER_STAGE)]
    for tick in range(len(groups) + len(stages) - 1):
        for s, stage in reversed(list(enumerate(stages))):
            if 0 <= tick - s < len(groups):
                for u in groups[tick - s]:
                    stage(u)
    flush_o()
    flush_m()
    flush_d()


def _attention_kernel(*refs):
    n_in = ATT_GROUP_IN * len(ATT_GROUPS)
    gate_ref, y_ref = refs[n_in], refs[n_in + 1]
    o_nat, m_nat, d_nat = refs[n_in + 2:n_in + 5]
    slabs = refs[n_in + 5:]
    first = pl.program_id(0) == 0
    for g, (_, dil) in enumerate(ATT_GROUPS):
        _attn_group_outputs(*refs[ATT_GROUP_IN * g:ATT_GROUP_IN * (g + 1)], o_nat.at[g], m_nat.at[g], d_nat.at[g],
                            slabs, first, dil)
    ms = [m_nat[g] for g in range(len(ATT_GROUPS))]
    m_all = functools.reduce(jnp.maximum, ms)
    wts = [jnp.exp(m - m_all) * d_nat[g] for g, m in enumerate(ms)]
    num = sum(w * o_nat[g] for g, w in enumerate(wts))
    y_ref[...] = (num / sum(wts) * _silu(gate_ref[...])).astype(BF16)


def _attention(h, bias):
    s = h.shape[0]
    heads = ATT_HEADS_PER_GROUP
    in_specs, operands = [], []
    for g, (_, dil) in enumerate(ATT_GROUPS):
        blk = ATT_SPAN * dil
        per_step = ATT_ROWS // blk

        def spec(col, prev, g=g, blk=blk, per_step=per_step):
            base = (col + g) * heads
            if prev:
                return pl.BlockSpec((blk, ATT_HEAD_DIM), lambda n, hd: (jnp.maximum(n * per_step - 1, 0), base + hd))
            return pl.BlockSpec((ATT_ROWS, ATT_HEAD_DIM), lambda n, hd: (n, base + hd))

        in_specs += [spec(COL_Q, False), spec(COL_K, False), spec(COL_K, True), spec(COL_V, False), spec(COL_V, True),
                     pl.BlockSpec((None, ATT_SPAN, 2 * ATT_SPAN), lambda n, hd, g=g: (g * heads + hd, 0, 0))]
        operands += [h, h, h, h, h, bias]
    in_specs.append(pl.BlockSpec((ATT_ROWS, ATT_HEAD_DIM), lambda n, hd: (n, COL_B_GATE * heads + hd)))
    token_order = pltpu.VMEM((len(ATT_GROUPS), ATT_ROWS, LANES), F32)
    slab = pltpu.VMEM((ATT_DIRECT_STRIDE, ATT_ROWS // ATT_DIRECT_STRIDE, LANES), F32)
    return pl.pallas_call(
        _attention_kernel,
        grid=(s // ATT_ROWS, heads),
        in_specs=in_specs,
        out_specs=pl.BlockSpec((ATT_ROWS, ATT_HEAD_DIM), lambda n, hd: (n, hd)),
        out_shape=jax.ShapeDtypeStruct((s, BR_WIDTH), BF16),
        scratch_shapes=[token_order] * 3 + [slab] * ATT_SLABS,
        compiler_params=_params(("parallel", "arbitrary"), 56),
        name="dilated_attention",
    )(*operands, h)


def _head_sums(x):
    ri = lax.broadcasted_iota(jnp.int32, (PAIR, PAIR), 0)
    ci = lax.broadcasted_iota(jnp.int32, (PAIR, PAIR), 1)
    same_head = jnp.where((ri < RWKV_HEAD) == (ci < RWKV_HEAD), 1.0, 0.0).astype(BF16)
    rows = x.shape[0]
    tall = jnp.concatenate([part[:, p * PAIR:(p + 1) * PAIR] for part in _bf16_parts(x, 2) for p in range(N_PAIRS)], axis=0)
    sums = jnp.dot(tall, same_head, preferred_element_type=F32)
    block = lambda k: sums[k * rows:(k + 1) * rows, :]
    return jnp.concatenate([block(p) + block(N_PAIRS + p) for p in range(N_PAIRS)], axis=1)


def _rwkv_prepare(r_ref, k_ref, v_ref, xb_ref, wl_ref, bl_ref, mu_r, mu_k, mu_v, mu_l, w0_ref, wup_ref, a0_ref, aup_ref,
                  kk_ref, ka_ref, rk_ref, carry, carry_l):
    t = r_ref.shape[0]

    def shift_mix(x, mu, prev_row):
        row = lax.broadcasted_iota(jnp.int32, x.shape, 0)
        x_prev = jnp.where(row == 0, prev_row, pltpu.roll(x, 1, 0))
        return x + mu * (x_prev - x)

    r_in, k_in, v_in = r_ref[...], k_ref[...], v_ref[...]
    l_in = _bdot(xb_ref[...], wl_ref[...]) + bl_ref[...]
    r = shift_mix(r_in, mu_r[...], carry[0:1, :])
    kx = shift_mix(k_in, mu_k[...], carry[1:2, :])
    vv = shift_mix(v_in, mu_v[...], carry[2:3, :])
    lo = shift_mix(l_in, mu_l[...], carry_l[0:1, :])
    carry[0:1, :] = r_in[t - 1:t, :]
    carry[1:2, :] = k_in[t - 1:t, :]
    carry[2:3, :] = v_in[t - 1:t, :]
    carry_l[0:1, :] = l_in[t - 1:t, :]

    w_log = -_softplus(-(w0_ref[...] + _bdot(jnp.tanh(lo), wup_ref[...]))) - 0.5
    log_decay = -jnp.exp(w_log)
    a_icl = jax.nn.sigmoid(a0_ref[...] + _bdot(lo, aup_ref[...]))

    kk = kx * kk_ref[...]
    kk = kk / jnp.maximum(jnp.sqrt(_head_sums(kk * kk)), 1e-12)
    kc = kx * (1.0 + (a_icl - 1.0) * ka_ref[...])
    bonus = _head_sums(r * kc * rk_ref[...]) * vv
    return log_decay, r, kc, vv, -kk, kk * a_icl, bonus


def _stack_heads(x):
    lane = lax.broadcasted_iota(jnp.int32, x.shape, 1)
    return jnp.concatenate([jnp.where(lane < RWKV_HEAD, x, 0.0), jnp.where(lane >= RWKV_HEAD, x, 0.0)], axis=0)


def _time_indices():
    t = lax.broadcasted_iota(jnp.int32, (RWKV_CHUNK, PAIR), 0)
    s = lax.broadcasted_iota(jnp.int32, (RWKV_CHUNK, PAIR), 1) & (RWKV_CHUNK - 1)
    return t, s


def _unit_lower_inverse(a_strict):
    ti, si = _time_indices()

    def same_block(bits):
        return (ti >> bits) == (si >> bits)

    pw = [jnp.where(same_block(4), a, 0.0) for a in a_strict]
    x = [jnp.where(ti == si, 1.0, 0.0) + p for p in pw]
    c = RWKV_CHUNK
    squares = [_bdot(p, _stack_heads(p)) for p in pw]
    for level in range(3):
        pw = squares
        if level < 2:
            both = [_bdot(jnp.concatenate([xi, p], axis=0), _stack_heads(p)) for xi, p in zip(x, pw)]
            x = [xi + b[0:c] for xi, b in zip(x, both)]
            squares = [b[c:2 * c] for b in both]
        else:
            x = [xi + _bdot(xi, _stack_heads(p)) for xi, p in zip(x, pw)]
    for bits in (5, 6):
        join = same_block(bits) & jnp.logical_not(same_block(bits - 1))
        xe = [_bdot(xi, _stack_heads(jnp.where(join, a, 0.0))) for xi, a in zip(x, a_strict)]
        x = [xi + _bdot(t, _stack_heads(xi)) for xi, t in zip(x, xe)]
    return x


def _rwkv_chunk_transforms(lw_all, r_all, k_all, v_all, a_all, b_all):
    c = RWKV_CHUNK
    n = 2 * c
    ti = lax.broadcasted_iota(jnp.int32, (c, c), 0)
    si = lax.broadcasted_iota(jnp.int32, (c, c), 1)
    lower_ones = jnp.where(si <= ti, 1.0, 0.0)
    tt, ss = _time_indices()
    strict = tt > ss
    incl = tt >= ss
    ri = lax.broadcasted_iota(jnp.int32, (n, n), 0)
    ci = lax.broadcasted_iota(jnp.int32, (n, n), 1)
    same_head = (ri < RWKV_HEAD) == (ci < RWKV_HEAD)
    eye = ri == ci

    units = [(ch, p) for ch in range(lw_all.shape[0] // c) for p in range(N_PAIRS)]
    each = lambda f, *cols: [f(*args) for args in zip(*cols)]

    def split(x):
        return [x[ch * c:(ch + 1) * c, p * PAIR:(p + 1) * PAIR] for ch, p in units]

    lw, r, k, v, a, b = (split(x) for x in (lw_all, r_all, k_all, v_all, a_all, b_all))
    cs = each(lambda x: _split_dot(lower_ones, x, 1, 3), lw)
    c_end = each(lambda x: x[c - 1:c, :], cs)
    r_d = each(lambda x, y: x * jnp.exp(y), r, cs)
    a_d = each(lambda x, y, z: x * jnp.exp(y - z), a, cs, lw)
    b_i = each(lambda x, y: x * jnp.exp(-y), b, cs)
    k_i = each(lambda x, y: x * jnp.exp(-y), k, cs)
    b_e = each(lambda x, y, e: x * jnp.exp(e - y), b, cs, c_end)
    k_e = each(lambda x, y, e: x * jnp.exp(e - y), k, cs, c_end)
    v_s = each(_stack_heads, v)

    aa = each(lambda ad, rd, bi, ki: _bdot_nt(jnp.concatenate([ad, rd], axis=0),
                                              jnp.concatenate([_stack_heads(bi), _stack_heads(ki)], axis=0)),
              a_d, r_d, b_i, k_i)
    a_ab = each(lambda x: jnp.where(strict, x[0:c, 0:n], 0.0), aa)
    a_ak = each(lambda x: jnp.where(strict, x[0:c, n:2 * n], 0.0), aa)
    a_rb = each(lambda x: jnp.where(incl, x[c:n, 0:n], 0.0), aa)
    a_rk = each(lambda x: jnp.where(incl, x[c:n, n:2 * n], 0.0), aa)

    minv = _unit_lower_inverse(a_ab)
    t1 = each(_bdot, a_ak, v_s)
    side_by_side = lambda x, y: jnp.concatenate([_stack_heads(x), _stack_heads(y)], axis=1)
    wu = each(lambda m, ad, x: _bdot(m, side_by_side(ad, x)), minv, a_d, t1)
    w = each(lambda x: x[:, 0:n], wu)
    uv = each(lambda x: x[:, n:2 * n], wu)
    qy = each(lambda x, y, z: _bdot(x, side_by_side(y, z)), a_rb, w, uv)
    q = each(lambda rd, x: rd + x[:, 0:n], r_d, qy)
    yc = each(lambda x, z, t: x[:, n:2 * n] + _bdot(z, t), qy, a_rk, v_s)
    g = each(lambda e, x, y: jnp.where(eye, jnp.exp(e), 0.0) + jnp.where(same_head, _bdot_tn(x, y), 0.0), c_end, w, b_e)
    z = each(lambda u_, v_, be, ke: jnp.where(same_head, _bdot_tn(jnp.concatenate([u_, v_], axis=0),
                                                                    jnp.concatenate([be, ke], axis=0)), 0.0),
             uv, v, b_e, k_e)
    return {unit: terms for unit, *terms in zip(units, q, yc, g, z)}


def _rwkv_init(carry, carry_l, state, ybuf, wl_bf16, wl_ref):
    carry[...] = jnp.zeros_like(carry)
    carry_l[...] = jnp.zeros_like(carry_l)
    state[...] = jnp.zeros_like(state)
    wl_bf16[...] = wl_ref[...].astype(BF16)


def _rwkv_body(r_ref, k_ref, v_ref, xb_ref, wl_ref, bl_ref, gate_ref, mu_r, mu_k, mu_v, mu_l, w0_ref, wup_ref, a0_ref,
               aup_ref, kk_ref, ka_ref, rk_ref, gn_g, gn_b, o_ref, carry, carry_l, state, ybuf, wl_bf16):
    c = RWKV_CHUNK
    chunks = r_ref.shape[0] // c
    *scan_inputs, bonus = _rwkv_prepare(r_ref, k_ref, v_ref, xb_ref, wl_bf16, bl_ref, mu_r, mu_k, mu_v, mu_l, w0_ref,
                                        wup_ref, a0_ref, aup_ref, kk_ref, ka_ref, rk_ref, carry, carry_l)
    terms = _rwkv_chunk_transforms(*scan_inputs)

    pairs = range(N_PAIRS)
    sts = [state[:, p * PAIR:(p + 1) * PAIR] for p in pairs]
    starts = []
    for ch in range(chunks):
        starts.append(sts)
        sts = [_split_dot(sts[p], terms[ch, p][2], 2, 2) + terms[ch, p][3] for p in pairs]
    for p in pairs:
        state[:, p * PAIR:(p + 1) * PAIR] = sts[p]
    for ch in range(chunks):
        for p in pairs:
            q, yc = terms[ch, p][0], terms[ch, p][1]
            ybuf[ch * c:(ch + 1) * c, p * PAIR:(p + 1) * PAIR] = _split_dot(q, starts[ch][p], 2, 2, NT_DIMS) + yc

    wy = ybuf[...]
    inv_n = 1.0 / RWKV_HEAD
    mu = _head_sums(wy) * inv_n
    d = wy - mu
    var = _head_sums(d * d) * inv_n
    wy = d * lax.rsqrt(var + RWKV_GN_EPS) * gn_g[...] + gn_b[...]
    o_ref[...] = ((wy + bonus) * _silu(gate_ref[...])).astype(BF16)


def _rwkv_specs(tile, layer):
    vec, lora_w = _layer_param(layer, (1, BR_WIDTH)), _layer_param(layer, (LANES, BR_WIDTH))
    lora_in = pl.BlockSpec((pl.Squeezed(), pl.Element(D_MODEL), pl.Element(DECAY_RANK + ICLR_RANK)),
                           lambda i: (layer, 0, LORA_START))
    in_specs = [_row_block(tile, COL_C_R), _row_block(tile, COL_C_K), _row_block(tile, COL_C_V),
                pl.BlockSpec((tile, D_MODEL), lambda i: (i, 0)), lora_in, _layer_param(layer, (1, LANES)),
                _row_block(tile, COL_C_GATE),
                vec, vec, vec, _layer_param(layer, (1, LANES)), vec, lora_w, vec, lora_w, vec, vec, vec, vec, vec]
    return in_specs, [pltpu.VMEM((SUBLANES, BR_WIDTH), F32), pltpu.VMEM((SUBLANES, LANES), F32),
                      pltpu.VMEM((PAIR, BR_WIDTH), F32), pltpu.VMEM((tile, BR_WIDTH), F32),
                      pltpu.VMEM((D_MODEL, DECAY_RANK + ICLR_RANK), BF16)]


def _recurrent_mixers_kernel(*refs):
    n_in = LRU_IN + CONF_IN + RWKV_IN
    ins, (o_a, o_d, o_c), scratch = refs[:n_in], refs[n_in:n_in + 3], refs[n_in + 3:]
    lru_in, conf_in, rwkv_in = ins[:LRU_IN], ins[LRU_IN:LRU_IN + CONF_IN], ins[LRU_IN + CONF_IN:]
    lru_s = scratch[:LRU_SCRATCH]
    conf_s = scratch[LRU_SCRATCH:LRU_SCRATCH + CONF_SCRATCH]
    rwkv_s = scratch[LRU_SCRATCH + CONF_SCRATCH:]

    @pl.when(pl.program_id(0) == 0)
    def _():
        _lru_init(*lru_s)
        _conf_init(*conf_s)
        _rwkv_init(*rwkv_s, rwkv_in[RWKV_LORA_W])

    _rwkv_body(*rwkv_in, o_c, *rwkv_s)
    _conf_body(*conf_in, o_d, *conf_s)
    _lru_body(*lru_in, o_a, *lru_s)


def _recurrent_mixers(h, xb, w_in, layer, lru_args, conf_args, lora_bias, rwkv_args, tile=4 * RWKV_CHUNK):
    s = h.shape[0]
    (lru_specs, lru_scr), (conf_specs, conf_scr), (rwkv_specs, rwkv_scr) = (
        _lru_specs(tile, layer), _conf_specs(tile, layer), _rwkv_specs(tile, layer))
    assert (len(lru_specs), len(conf_specs), len(rwkv_specs)) == (LRU_IN, CONF_IN, RWKV_IN)
    out = pl.BlockSpec((tile, BR_WIDTH), lambda i: (i, 0))
    return pl.pallas_call(
        _recurrent_mixers_kernel,
        grid=(s // tile,),
        in_specs=lru_specs + conf_specs + rwkv_specs,
        out_specs=[out] * 3,
        out_shape=[jax.ShapeDtypeStruct((s, BR_WIDTH), BF16)] * 3,
        scratch_shapes=lru_scr + conf_scr + rwkv_scr,
        compiler_params=_params(("arbitrary",), 40),
        name="recurrent_mixers",
    )(h, h, *lru_args, h, h, h, *conf_args, h, h, h, xb, w_in, lora_bias, h, *rwkv_args)


def _mix_kernel(xb_ref, *refs):
    ygs, wms, bms, wbrs = (refs[k * N_BRANCH:(k + 1) * N_BRANCH] for k in range(4))
    o_ref = refs[4 * N_BRANCH]
    xb = xb_ref[...]
    acc = None
    for n in range(N_BRANCH):
        gate = jax.nn.sigmoid(_bdot(xb, wms[n][...]) + bms[n][...])
        val = gate * _bdot(ygs[n][...], wbrs[n][...])
        acc = val if acc is None else acc + val
    o_ref[...] = acc.astype(BF16)


def _mix(xb, ygs, w_all, layer, bm, wbr_all, tm=1024, tn=256):
    s = xb.shape[0]
    nj = D_MODEL // tn
    per_branch = lambda make: [make(n) for n in range(N_BRANCH)]
    return pl.pallas_call(
        _mix_kernel,
        grid=(s // tm, nj),
        in_specs=[pl.BlockSpec((tm, D_MODEL), lambda i, j: (i, 0))]
        + per_branch(lambda n: pl.BlockSpec((tm, BR_WIDTH), lambda i, j: (i, 0)))
        + per_branch(lambda n: pl.BlockSpec((pl.Squeezed(), pl.Element(D_MODEL), pl.Element(tn)),
                                            lambda i, j: (layer, 0, ((BRANCH_IN + n * D_MODEL) // LANES
                                                                     + j * (tn // LANES)) * LANES)))
        + per_branch(lambda n: pl.BlockSpec((None, 1, tn), lambda i, j: (layer, 0, n * nj + j)))
        + per_branch(lambda n: pl.BlockSpec((None, None, BR_WIDTH, tn), lambda i, j: (layer, n, 0, j))),
        out_specs=pl.BlockSpec((tm, tn), lambda i, j: (i, j)),
        out_shape=jax.ShapeDtypeStruct((s, D_MODEL), BF16),
        compiler_params=_params(("parallel", "arbitrary"), 48),
        name="branch_mix",
    )(xb, *ygs, *([w_all] * N_BRANCH), *([bm] * N_BRANCH), *([wbr_all] * N_BRANCH))


def _out_kernel(mixed_ref, x_ref, w_ref, g_ref, b_ref, o_ref, ob_ref):
    y = ALPHA * x_ref[...] + jnp.dot(mixed_ref[...], w_ref[...], preferred_element_type=F32)
    mu = jnp.mean(y, axis=-1, keepdims=True)
    var = jnp.mean(jnp.square(y - mu), axis=-1, keepdims=True)
    out = (y - mu) * lax.rsqrt(var + LN_EPS) * g_ref[...] + b_ref[...]
    o_ref[...] = out
    ob_ref[...] = out.astype(BF16)


def _out_proj(mixed, x, layer, w, g, b, tm=512):
    s = x.shape[0]
    row = pl.BlockSpec((tm, D_MODEL), lambda i: (i, 0))
    vec = _layer_param(layer, (1, D_MODEL))
    return pl.pallas_call(
        _out_kernel,
        grid=(s // tm,),
        in_specs=[row, row, _layer_param(layer, (D_MODEL, D_MODEL)), vec, vec],
        out_specs=[row, row],
        out_shape=[jax.ShapeDtypeStruct((s, D_MODEL), F32), jax.ShapeDtypeStruct((s, D_MODEL), BF16)],
        compiler_params=_params(("parallel",), 52),
        name="out_proj_ln",
    )(mixed, x, w, g, b)


def _block_diag(w):
    depth, blocks, n, _ = w.shape
    eye = jnp.eye(blocks, dtype=w.dtype)
    return (eye[None, :, None, :, None] * w[:, :, :, None, :]).reshape(depth, blocks * n, blocks * n)


def kernel(x, att_rel_bias, w_in, b_in, lru_conv_w, lru_conv_b, lru_gate_a_w, lru_gate_a_b, lru_gate_x_w, lru_gate_x_b, lru_lambda, rwkv_mu, rwkv_w0, rwkv_w_up, rwkv_a0, rwkv_a_up, rwkv_k_k, rwkv_k_a, rwkv_r_k, rwkv_gn_g, rwkv_gn_b, conf_dw_w, conf_dw_b, conf_ln_g, conf_ln_b, w_br, w_out, ln_g, ln_b):
    bsz, s, d = x.shape
    assert bsz == 1 and d == D_MODEL and s % ATT_ROWS == 0
    vec = lambda t: t.reshape(DEPTH, 1, -1)
    b_h = vec(jnp.concatenate([b_in[:, :H_SPLIT * BR_WIDTH], b_in[:, C_GATE_START:BRANCH_IN]], axis=1))
    b_merge = vec(b_in[:, BRANCH_IN:])
    b_lora = vec(b_in[:, LORA_START:C_GATE_START])
    mu = rwkv_mu
    zpad = jnp.zeros((DEPTH, DECAY_RANK, BR_WIDTH), F32)
    wup = jnp.concatenate([rwkv_w_up, zpad], axis=1).astype(BF16)
    aup = jnp.concatenate([zpad, rwkv_a_up], axis=1).astype(BF16)
    lru_args = (lru_conv_w, vec(lru_conv_b), _block_diag(lru_gate_a_w).astype(BF16), vec(lru_gate_a_b),
                _block_diag(lru_gate_x_w).astype(BF16), vec(lru_gate_x_b), vec(lru_lambda))
    conf_args = (conf_dw_w, vec(conf_dw_b), vec(conf_ln_g), vec(conf_ln_b))
    rwkv_args = (vec(mu[:, :BR_WIDTH]), vec(mu[:, BR_WIDTH:2 * BR_WIDTH]), vec(mu[:, 2 * BR_WIDTH:3 * BR_WIDTH]),
                 vec(mu[:, 3 * BR_WIDTH:]), vec(rwkv_w0), wup, vec(rwkv_a0), aup, vec(rwkv_k_k), vec(rwkv_k_a),
                 vec(rwkv_r_k), vec(rwkv_gn_g), vec(rwkv_gn_b))
    w_out_bf16, ln_g, ln_b = w_out.astype(BF16), vec(ln_g), vec(ln_b)
    att_bias = _attn_bias(att_rel_bias)

    y = x.reshape(s, d)
    yb = y.astype(BF16)
    for layer in range(DEPTH):
        h = _in_proj(yb, w_in, layer, b_h)
        yg_b = _attention(h, att_bias)
        yg_a, yg_d, yg_c = _recurrent_mixers(h, yb, w_in, layer, lru_args, conf_args, b_lora, rwkv_args)
        mixed = _mix(yb, (yg_a, yg_b, yg_c, yg_d), w_in, layer, b_merge, w_br)
        y, yb = _out_proj(mixed, y, layer, w_out_bf16, ln_g, ln_b)
    return y.reshape(bsz, s, d)
```

```python
import functools
import math

import numpy as np
import jax
import jax.numpy as jnp
from jax import lax
from jax.experimental import pallas as pl
from jax.experimental.pallas import tpu as pltpu

D_MODEL = 2048
DEPTH = 2
N_BRANCH = 4
BR_WIDTH = 512
LRU_CONV = 4
LRU_C = 8.0
ATT_GROUPS = ((128, 1), (512, 4), (2048, 16))
ATT_HEADS_PER_GROUP = 4
ATT_HEAD_DIM = BR_WIDTH // ATT_HEADS_PER_GROUP
ATT_HEADS = len(ATT_GROUPS) * ATT_HEADS_PER_GROUP
ATT_QKV = ATT_HEADS * ATT_HEAD_DIM
ATT_SPAN = 128
N_BUCKETS = 32
MAX_DISTANCE = 2048
NEG_INF = -1e30
RWKV_HEAD = 64
DECAY_RANK = 64
ICLR_RANK = 64
RWKV_GN_EPS = 64e-5
CONF_KERNEL = 31
LN_EPS = 1e-5
ALPHA = (2.0 * DEPTH) ** 0.25

LANES = 128
SUBLANES = 8
MIB = 1024 * 1024

BRANCH_IN = 2 * BR_WIDTH + 3 * ATT_QKV + BR_WIDTH + (4 * BR_WIDTH + DECAY_RANK + ICLR_RANK) + 3 * BR_WIDTH
C_GATE_START = BRANCH_IN - 4 * BR_WIDTH
LORA_START = C_GATE_START - (DECAY_RANK + ICLR_RANK)
H_SPLIT = LORA_START // BR_WIDTH
H_BLOCKS = H_SPLIT + 4
H_WIDTH = H_BLOCKS * BR_WIDTH
COL_A_X, COL_A_GATE = 0, 1
COL_Q, COL_K, COL_V, COL_B_GATE = 2, 5, 8, 11
COL_C_R, COL_C_K, COL_C_V = 12, 13, 14
COL_C_GATE, COL_D_VAL, COL_D_GLU, COL_D_GATE = 15, 16, 17, 18

CONF_HALO = 32
RWKV_CHUNK = 64
PAIR = 2 * RWKV_HEAD
N_PAIRS = BR_WIDTH // PAIR

F32 = jnp.float32
BF16 = jnp.bfloat16


def _params(semantics, vmem_mib):
    return pltpu.CompilerParams(dimension_semantics=semantics, vmem_limit_bytes=vmem_mib * MIB)


def _bdot(a, b):
    return jnp.dot(a.astype(BF16), b.astype(BF16), preferred_element_type=F32)


def _bdot_nt(a, b):
    return lax.dot_general(a.astype(BF16), b.astype(BF16), (((1,), (1,)), ((), ())), preferred_element_type=F32)


def _bdot_tn(a, b):
    return lax.dot_general(a.astype(BF16), b.astype(BF16), (((0,), (0,)), ((), ())), preferred_element_type=F32)


NN_DIMS = (((1,), (0,)), ((), ()))
NT_DIMS = (((1,), (1,)), ((), ()))


def _bf16_parts(x, parts):
    out = []
    for _ in range(parts):
        hi = x.astype(BF16)
        out.append(hi)
        x = x - hi.astype(F32)
    return out


def _split_dot(a, b, a_parts, b_parts, dims=NN_DIMS):
    acc = None
    rows = a.shape[0]
    a_terms = _bf16_parts(a, a_parts)
    for j, bj in enumerate(_bf16_parts(b, b_parts)):
        with_bj = a_terms[:max(a_parts, b_parts) - j]
        if with_bj:
            prod = lax.dot_general(jnp.concatenate(with_bj, axis=0), bj, dims, preferred_element_type=F32)
            for i in range(len(with_bj)):
                term = prod[i * rows:(i + 1) * rows]
                acc = term if acc is None else acc + term
    return acc


def _softplus(z):
    return jnp.maximum(z, 0.0) + jnp.log1p(jnp.exp(-jnp.abs(z)))


def _expm1_nonpos(z):
    u = jnp.exp(z)
    safe = jnp.where(u == 1.0, 0.5, u)
    return jnp.where(u == 1.0, z, jnp.where(u == 0.0, -1.0, (safe - 1.0) * z / jnp.log(safe)))


def _silu(z):
    return z * jax.nn.sigmoid(z)


def _in_proj_kernel(xb_ref, w_ref, b_ref, h_ref):
    h_ref[...] = _bdot(xb_ref[...], w_ref[...]) + b_ref[...]


def _h_source_column(block):
    return block * BR_WIDTH if block < H_SPLIT else C_GATE_START + (block - H_SPLIT) * BR_WIDTH


def _in_proj(xb, w_all, layer, b, tm=2048, tn=512):
    s, k = xb.shape
    assert (H_SPLIT * BR_WIDTH) % tn == 0 and tn % BR_WIDTH == 0
    per_tile = tn // BR_WIDTH
    starts = np.array([_h_source_column(j * per_tile) // LANES for j in range(H_WIDTH // tn)], np.int32)
    return pl.pallas_call(
        lambda starts_ref, *refs: _in_proj_kernel(*refs),
        grid_spec=pltpu.PrefetchScalarGridSpec(
            num_scalar_prefetch=1,
            grid=(s // tm, H_WIDTH // tn),
            in_specs=[
                pl.BlockSpec((tm, k), lambda i, j, st: (i, 0)),
                pl.BlockSpec((pl.Squeezed(), pl.Element(k), pl.Element(tn)),
                             lambda i, j, st: (layer, 0, st[j] * LANES)),
                pl.BlockSpec((None, 1, tn), lambda i, j, st: (layer, 0, j)),
            ],
            out_specs=pl.BlockSpec((tm, tn), lambda i, j, st: (i, j)),
        ),
        out_shape=jax.ShapeDtypeStruct((s, H_WIDTH), F32),
        compiler_params=_params(("parallel", "arbitrary"), 56),
        name="in_proj",
    )(jnp.asarray(starts), xb, w_all, b)


LRU_IN, CONF_IN, RWKV_IN = 9, 7, 20
LRU_SCRATCH, CONF_SCRATCH, RWKV_SCRATCH = 2, 2, 5
RWKV_LORA_W = 4


def _lru_init(ebuf, hc):
    ebuf[0:SUBLANES, :] = jnp.zeros((SUBLANES, BR_WIDTH), F32)
    hc[...] = jnp.zeros_like(hc)


def _lru_body(ax_ref, ag_ref, cw_ref, cb_ref, wa_ref, ba_ref, wx_ref, bx_ref, lam_ref, o_ref, ebuf, hc):
    t = ax_ref.shape[0]
    halo = SUBLANES
    x = ax_ref[...]
    ebuf[halo:halo + t, :] = x
    u = cb_ref[...] + jnp.zeros((t, BR_WIDTH), F32)
    for j in range(LRU_CONV):
        u = u + cw_ref[j:j + 1, :] * ebuf[pl.ds(halo - (LRU_CONV - 1) + j, t), :]
    ebuf[0:halo, :] = x[t - halo:t, :]

    gate_r = jax.nn.sigmoid(_bdot(u, wa_ref[...]) + ba_ref[...])
    gate_i = jax.nn.sigmoid(_bdot(u, wx_ref[...]) + bx_ref[...])
    log_a = -LRU_C * gate_r * _softplus(-lam_ref[...])
    a = jnp.exp(log_a)
    b = jnp.sqrt(-_expm1_nonpos(2.0 * log_a)) * (gate_i * u)

    row = lax.broadcasted_iota(jnp.int32, (t, BR_WIDTH), 0)
    shift = 1
    while shift < t:
        valid = row >= shift
        b = jnp.where(valid, a * pltpu.roll(b, shift, 0), 0.0) + b
        a = jnp.where(valid, a * pltpu.roll(a, shift, 0), a)
        shift *= 2
    h = a * hc[0:1, :] + b
    hc[0:1, :] = h[t - 1:t, :]
    o_ref[...] = (h * _silu(ag_ref[...])).astype(BF16)


def _row_block(tile, col):
    return pl.BlockSpec((tile, BR_WIDTH), lambda i: (i, col))


def _layer_param(layer, shape):
    return pl.BlockSpec((None,) + shape, lambda *_: (layer,) + (0,) * len(shape))


def _lru_specs(tile, layer):
    vec, mat = _layer_param(layer, (1, BR_WIDTH)), _layer_param(layer, (BR_WIDTH, BR_WIDTH))
    in_specs = [_row_block(tile, COL_A_X), _row_block(tile, COL_A_GATE), _layer_param(layer, (LRU_CONV, BR_WIDTH)),
                vec, mat, vec, mat, vec, vec]
    return in_specs, [pltpu.VMEM((tile + SUBLANES, BR_WIDTH), F32), pltpu.VMEM((SUBLANES, BR_WIDTH), F32)]


def _conf_init(ebuf, shifted):
    ebuf[0:CONF_HALO, :] = jnp.zeros((CONF_HALO, BR_WIDTH), F32)


def _conf_body(val_ref, glu_ref, gate_ref, w_ref, b_ref, g_ref, beta_ref, o_ref, ebuf, shifted):
    t = val_ref.shape[0]
    halo = CONF_HALO
    cu = val_ref[...] * jax.nn.sigmoid(glu_ref[...])
    ebuf[halo:halo + t, :] = cu
    for b in range(SUBLANES):
        span = t + (CONF_KERNEL - 1 - b) // SUBLANES * SUBLANES
        shifted[b, 0:span, :] = ebuf[pl.ds(halo - (CONF_KERNEL - 1) + b, span), :]
    acc = b_ref[...] + jnp.zeros((t, BR_WIDTH), F32)
    for j in range(CONF_KERNEL):
        b = j % SUBLANES
        acc = acc + w_ref[j:j + 1, :] * shifted[b, j - b:j - b + t, :]
    ebuf[0:halo, :] = cu[t - halo:t, :]

    mu = jnp.mean(acc, axis=-1, keepdims=True)
    var = jnp.mean(jnp.square(acc - mu), axis=-1, keepdims=True)
    ln = (acc - mu) * lax.rsqrt(var + LN_EPS) * g_ref[...] + beta_ref[...]
    o_ref[...] = (_silu(ln) * _silu(gate_ref[...])).astype(BF16)


def _conf_specs(tile, layer):
    vec = _layer_param(layer, (1, BR_WIDTH))
    in_specs = [_row_block(tile, COL_D_VAL), _row_block(tile, COL_D_GLU), _row_block(tile, COL_D_GATE),
                _layer_param(layer, (CONF_KERNEL, BR_WIDTH)), vec, vec, vec]
    return in_specs, [pltpu.VMEM((tile + CONF_HALO, BR_WIDTH), F32),
                      pltpu.VMEM((SUBLANES, tile + CONF_HALO - SUBLANES, BR_WIDTH), F32)]


def _t5_bucket(dist):
    max_exact = N_BUCKETS // 2
    large = max_exact + (np.log(np.maximum(dist, 1) / max_exact) / math.log(MAX_DISTANCE / max_exact)
                         * (N_BUCKETS - max_exact)).astype(np.int32)
    large = np.minimum(large, N_BUCKETS - 1)
    return np.where(dist < max_exact, dist, large).astype(np.int32)


def _bucket_index():
    qi = np.arange(ATT_SPAN)[:, None]
    kj = np.arange(2 * ATT_SPAN)[None, :]
    dist = qi + ATT_SPAN - kj
    valid = (dist >= 0) & (dist <= ATT_SPAN)
    per_group = [np.where(valid, _t5_bucket(np.clip(dist, 0, ATT_SPAN) * dil), -1) for _, dil in ATT_GROUPS]
    return np.stack(per_group).astype(np.int32)


def _bias_kernel(table_ref, bucket_ref, o_ref):
    head = pl.program_id(0)
    bucket = bucket_ref[...]
    acc = jnp.full(bucket.shape, NEG_INF, F32)
    for bkt in range(N_BUCKETS):
        acc = jnp.where(bucket == bkt, table_ref[bkt, head], acc)
    o_ref[...] = acc


def _attn_bias(table):
    blk = (None, ATT_SPAN, 2 * ATT_SPAN)
    return pl.pallas_call(
        _bias_kernel,
        grid=(ATT_HEADS,),
        in_specs=[pl.BlockSpec(memory_space=pltpu.SMEM),
                  pl.BlockSpec(blk, lambda hd: (hd // ATT_HEADS_PER_GROUP, 0, 0))],
        out_specs=pl.BlockSpec(blk, lambda hd: (hd, 0, 0)),
        out_shape=jax.ShapeDtypeStruct((ATT_HEADS, ATT_SPAN, 2 * ATT_SPAN), F32),
        compiler_params=_params(("parallel",), 32),
        name="attn_bias",
    )(table, jnp.asarray(_bucket_index()))


ATT_DIRECT_STRIDE = 4


def _residue_reader(ref, slab, dilation):
    if dilation == 1:
        return lambda b, r: ref[b * ATT_SPAN:(b + 1) * ATT_SPAN, :]
    if dilation <= ATT_DIRECT_STRIDE:
        return lambda b, r: ref[pl.ds(b * ATT_SPAN * dilation + r, ATT_SPAN, stride=dilation), :]
    inner, outer = ATT_DIRECT_STRIDE, dilation // ATT_DIRECT_STRIDE
    per = ref.shape[0] // inner
    for r0 in range(inner):
        slab[r0] = ref[pl.ds(r0, per, stride=inner), :]
    return lambda b, r: slab[r % inner, pl.ds(b * ATT_SPAN * outer + r // inner, ATT_SPAN, stride=outer), :]


def _residue_writer(ref, slab, dilation):
    if dilation == 1:
        def write(b, r, val):
            ref[b * ATT_SPAN:(b + 1) * ATT_SPAN, :] = val
        return write, lambda: None
    if dilation <= ATT_DIRECT_STRIDE:
        def write(b, r, val):
            ref[pl.ds(b * ATT_SPAN * dilation + r, ATT_SPAN, stride=dilation), :] = val
        return write, lambda: None
    inner, outer = ATT_DIRECT_STRIDE, dilation // ATT_DIRECT_STRIDE
    per = ref.shape[0] // inner

    def write(b, r, val):
        slab[r % inner, pl.ds(b * ATT_SPAN * outer + r // inner, ATT_SPAN, stride=outer), :] = val

    def flush():
        for r0 in range(inner):
            ref[pl.ds(r0, per, stride=inner), :] = slab[r0]

    return write, flush


ATT_ROWS = ATT_SPAN * max(dil for _, dil in ATT_GROUPS)
ATT_GROUP_IN = 6
ATT_SLABS = 8
ATT_UNITS_PER_STAGE = 4


def _attn_group_outputs(q_ref, kc_ref, kp_ref, vc_ref, vp_ref, bias_ref, o_nat, m_nat, d_nat, slabs, first, dilation):
    blocks = ATT_ROWS // (ATT_SPAN * dilation)
    scale = ATT_HEAD_DIM ** -0.5
    read_q, read_kc, read_kp, read_vc, read_vp = (
        _residue_reader(ref, slab, dilation) for ref, slab in zip((q_ref, kc_ref, kp_ref, vc_ref, vp_ref), slabs[:5]))
    (write_o, flush_o), (write_m, flush_m), (write_d, flush_d) = (
        _residue_writer(ref, slab, dilation) for ref, slab in zip((o_nat, m_nat, d_nat), slabs[5:]))
    def key(b, r):
        return (read_kp(0, r) if b < 0 else read_kc(b, r)).astype(BF16)

    def value(b, r):
        return (read_vp(0, r) if b < 0 else read_vc(b, r)).astype(BF16)

    bias = bias_ref[...]
    before_start = first & (lax.broadcasted_iota(jnp.int32, bias.shape, 1) < ATT_SPAN)
    full = (ATT_SPAN, LANES)

    def logits(u):
        b, r = u["b"], u["r"]
        q = read_q(b, r).astype(BF16)
        lg = _bdot_nt(q, jnp.concatenate([key(b - 1, r), key(b, r)], axis=0)) * scale + bias
        u["lg"] = jnp.where(before_start, NEG_INF, lg) if b == 0 else lg

    def row_max(u):
        u["m"] = jnp.max(u["lg"], axis=-1, keepdims=True)

    def weights(u):
        u["p"] = jnp.exp(u.pop("lg") - u["m"])

    def denominator(u):
        u["den"] = jnp.sum(u["p"], axis=-1, keepdims=True)

    def outputs(u):
        b, r = u["b"], u["r"]
        write_o(b, r, _bdot(u.pop("p"), jnp.concatenate([value(b - 1, r), value(b, r)], axis=0)) / u["den"])
        write_m(b, r, jnp.broadcast_to(u["m"], full))
        write_d(b, r, jnp.broadcast_to(u["den"], full))

    stages = (logits, row_max, weights, denominator, outputs)
    units = [dict(b=b, r=r) for b in range(blocks) for r in range(dilation)]
    groups = [units[i:i + ATT_UNITS_PER_STAGE] for i in range(0, len(units), ATT_UNITS_PER_STAGE)]
    for tick in range(len(groups) + len(stages) - 1):
        for s, stage in reversed(list(enumerate(stages))):
            if 0 <= tick - s < len(groups):
                for u in groups[tick - s]:
                    stage(u)
    flush_o()
    flush_m()
    flush_d()


def _attention_kernel(*refs):
    n_in = ATT_GROUP_IN * len(ATT_GROUPS)
    gate_ref, y_ref = refs[n_in], refs[n_in + 1]
    o_nat, m_nat, d_nat = refs[n_in + 2:n_in + 5]
    slabs = refs[n_in + 5:]
    first = pl.program_id(0) == 0
    for g, (_, dil) in enumerate(ATT_GROUPS):
        _attn_group_outputs(*refs[ATT_GROUP_IN * g:ATT_GROUP_IN * (g + 1)], o_nat.at[g], m_nat.at[g], d_nat.at[g],
                            slabs, first, dil)
    ms = [m_nat[g] for g in range(len(ATT_GROUPS))]
    m_all = functools.reduce(jnp.maximum, ms)
    wts = [jnp.exp(m - m_all) * d_nat[g] for g, m in enumerate(ms)]
    num = sum(w * o_nat[g] for g, w in enumerate(wts))
    y_ref[...] = (num / sum(wts) * _silu(gate_ref[...])).astype(BF16)


def _attention(h, bias):
    s = h.shape[0]
    heads = ATT_HEADS_PER_GROUP
    in_specs, operands = [], []
    for g, (_, dil) in enumerate(ATT_GROUPS):
        blk = ATT_SPAN * dil
        per_step = ATT_ROWS // blk

        def spec(col, prev, g=g, blk=blk, per_step=per_step):
            base = (col + g) * heads
            if prev:
                return pl.BlockSpec((blk, ATT_HEAD_DIM), lambda n, hd: (jnp.maximum(n * per_step - 1, 0), base + hd))
            return pl.BlockSpec((ATT_ROWS, ATT_HEAD_DIM), lambda n, hd: (n, base + hd))

        in_specs += [spec(COL_Q, False), spec(COL_K, False), spec(COL_K, True), spec(COL_V, False), spec(COL_V, True),
                     pl.BlockSpec((None, ATT_SPAN, 2 * ATT_SPAN), lambda n, hd, g=g: (g * heads + hd, 0, 0))]
        operands += [h, h, h, h, h, bias]
    in_specs.append(pl.BlockSpec((ATT_ROWS, ATT_HEAD_DIM), lambda n, hd: (n, COL_B_GATE * heads + hd)))
    token_order = pltpu.VMEM((len(ATT_GROUPS), ATT_ROWS, LANES), F32)
    slab = pltpu.VMEM((ATT_DIRECT_STRIDE, ATT_ROWS // ATT_DIRECT_STRIDE, LANES), F32)
    return pl.pallas_call(
        _attention_kernel,
        grid=(s // ATT_ROWS, heads),
        in_specs=in_specs,
        out_specs=pl.BlockSpec((ATT_ROWS, ATT_HEAD_DIM), lambda n, hd: (n, hd)),
        out_shape=jax.ShapeDtypeStruct((s, BR_WIDTH), BF16),
        scratch_shapes=[token_order] * 3 + [slab] * ATT_SLABS,
        compiler_params=_params(("parallel", "arbitrary"), 56),
        name="dilated_attention",
    )(*operands, h)


def _head_sums(x):
    ri = lax.broadcasted_iota(jnp.int32, (PAIR, PAIR), 0)
    ci = lax.broadcasted_iota(jnp.int32, (PAIR, PAIR), 1)
    same_head = jnp.where((ri < RWKV_HEAD) == (ci < RWKV_HEAD), 1.0, 0.0).astype(BF16)
    rows = x.shape[0]
    tall = jnp.concatenate([part[:, p * PAIR:(p + 1) * PAIR] for part in _bf16_parts(x, 2) for p in range(N_PAIRS)], axis=0)
    sums = jnp.dot(tall, same_head, preferred_element_type=F32)
    block = lambda k: sums[k * rows:(k + 1) * rows, :]
    return jnp.concatenate([block(p) + block(N_PAIRS + p) for p in range(N_PAIRS)], axis=1)


def _rwkv_prepare(r_ref, k_ref, v_ref, xb_ref, wl_ref, bl_ref, mu_r, mu_k, mu_v, mu_l, w0_ref, wup_ref, a0_ref, aup_ref,
                  kk_ref, ka_ref, rk_ref, carry, carry_l):
    t = r_ref.shape[0]

    def shift_mix(x, mu, prev_row):
        row = lax.broadcasted_iota(jnp.int32, x.shape, 0)
        x_prev = jnp.where(row == 0, prev_row, pltpu.roll(x, 1, 0))
        return x + mu * (x_prev - x)

    r_in, k_in, v_in = r_ref[...], k_ref[...], v_ref[...]
    l_in = _bdot(xb_ref[...], wl_ref[...]) + bl_ref[...]
    r = shift_mix(r_in, mu_r[...], carry[0:1, :])
    kx = shift_mix(k_in, mu_k[...], carry[1:2, :])
    vv = shift_mix(v_in, mu_v[...], carry[2:3, :])
    lo = shift_mix(l_in, mu_l[...], carry_l[0:1, :])
    carry[0:1, :] = r_in[t - 1:t, :]
    carry[1:2, :] = k_in[t - 1:t, :]
    carry[2:3, :] = v_in[t - 1:t, :]
    carry_l[0:1, :] = l_in[t - 1:t, :]

    w_log = -_softplus(-(w0_ref[...] + _bdot(jnp.tanh(lo), wup_ref[...]))) - 0.5
    log_decay = -jnp.exp(w_log)
    a_icl = jax.nn.sigmoid(a0_ref[...] + _bdot(lo, aup_ref[...]))

    kk = kx * kk_ref[...]
    kk = kk / jnp.maximum(jnp.sqrt(_head_sums(kk * kk)), 1e-12)
    kc = kx * (1.0 + (a_icl - 1.0) * ka_ref[...])
    bonus = _head_sums(r * kc * rk_ref[...]) * vv
    return log_decay, r, kc, vv, -kk, kk * a_icl, bonus


def _stack_heads(x):
    lane = lax.broadcasted_iota(jnp.int32, x.shape, 1)
    return jnp.concatenate([jnp.where(lane < RWKV_HEAD, x, 0.0), jnp.where(lane >= RWKV_HEAD, x, 0.0)], axis=0)


def _time_indices():
    t = lax.broadcasted_iota(jnp.int32, (RWKV_CHUNK, PAIR), 0)
    s = lax.broadcasted_iota(jnp.int32, (RWKV_CHUNK, PAIR), 1) & (RWKV_CHUNK - 1)
    return t, s


def _unit_lower_inverse(a_strict):
    ti, si = _time_indices()

    def same_block(bits):
        return (ti >> bits) == (si >> bits)

    pw = [jnp.where(same_block(4), a, 0.0) for a in a_strict]
    x = [jnp.where(ti == si, 1.0, 0.0) + p for p in pw]
    c = RWKV_CHUNK
    squares = [_bdot(p, _stack_heads(p)) for p in pw]
    for level in range(3):
        pw = squares
        if level < 2:
            both = [_bdot(jnp.concatenate([xi, p], axis=0), _stack_heads(p)) for xi, p in zip(x, pw)]
            x = [xi + b[0:c] for xi, b in zip(x, both)]
            squares = [b[c:2 * c] for b in both]
        else:
            x = [xi + _bdot(xi, _stack_heads(p)) for xi, p in zip(x, pw)]
    for bits in (5, 6):
        join = same_block(bits) & jnp.logical_not(same_block(bits - 1))
        xe = [_bdot(xi, _stack_heads(jnp.where(join, a, 0.0))) for xi, a in zip(x, a_strict)]
        x = [xi + _bdot(t, _stack_heads(xi)) for xi, t in zip(x, xe)]
    return x


def _rwkv_chunk_transforms(lw_all, r_all, k_all, v_all, a_all, b_all):
    c = RWKV_CHUNK
    n = 2 * c
    ti = lax.broadcasted_iota(jnp.int32, (c, c), 0)
    si = lax.broadcasted_iota(jnp.int32, (c, c), 1)
    lower_ones = jnp.where(si <= ti, 1.0, 0.0)
    tt, ss = _time_indices()
    strict = tt > ss
    incl = tt >= ss
    ri = lax.broadcasted_iota(jnp.int32, (n, n), 0)
    ci = lax.broadcasted_iota(jnp.int32, (n, n), 1)
    same_head = (ri < RWKV_HEAD) == (ci < RWKV_HEAD)
    eye = ri == ci

    units = [(ch, p) for ch in range(lw_all.shape[0] // c) for p in range(N_PAIRS)]
    each = lambda f, *cols: [f(*args) for args in zip(*cols)]

    def split(x):
        return [x[ch * c:(ch + 1) * c, p * PAIR:(p + 1) * PAIR] for ch, p in units]

    lw, r, k, v, a, b = (split(x) for x in (lw_all, r_all, k_all, v_all, a_all, b_all))
    cs = each(lambda x: _split_dot(lower_ones, x, 1, 3), lw)
    c_end = each(lambda x: x[c - 1:c, :], cs)
    r_d = each(lambda x, y: x * jnp.exp(y), r, cs)
    a_d = each(lambda x, y, z: x * jnp.exp(y - z), a, cs, lw)
    b_i = each(lambda x, y: x * jnp.exp(-y), b, cs)
    k_i = each(lambda x, y: x * jnp.exp(-y), k, cs)
    b_e = each(lambda x, y, e: x * jnp.exp(e - y), b, cs, c_end)
    k_e = each(lambda x, y, e: x * jnp.exp(e - y), k, cs, c_end)
    v_s = each(_stack_heads, v)

    aa = each(lambda ad, rd, bi, ki: _bdot_nt(jnp.concatenate([ad, rd], axis=0),
                                              jnp.concatenate([_stack_heads(bi), _stack_heads(ki)], axis=0)),
              a_d, r_d, b_i, k_i)
    a_ab = each(lambda x: jnp.where(strict, x[0:c, 0:n], 0.0), aa)
    a_ak = each(lambda x: jnp.where(strict, x[0:c, n:2 * n], 0.0), aa)
    a_rb = each(lambda x: jnp.where(incl, x[c:n, 0:n], 0.0), aa)
    a_rk = each(lambda x: jnp.where(incl, x[c:n, n:2 * n], 0.0), aa)

    minv = _unit_lower_inverse(a_ab)
    t1 = each(_bdot, a_ak, v_s)
    side_by_side = lambda x, y: jnp.concatenate([_stack_heads(x), _stack_heads(y)], axis=1)
    wu = each(lambda m, ad, x: _bdot(m, side_by_side(ad, x)), minv, a_d, t1)
    w = each(lambda x: x[:, 0:n], wu)
    uv = each(lambda x: x[:, n:2 * n], wu)
    qy = each(lambda x, y, z: _bdot(x, side_by_side(y, z)), a_rb, w, uv)
    q = each(lambda rd, x: rd + x[:, 0:n], r_d, qy)
    yc = each(lambda x, z, t: x[:, n:2 * n] + _bdot(z, t), qy, a_rk, v_s)
    g = each(lambda e, x, y: jnp.where(eye, jnp.exp(e), 0.0) + jnp.where(same_head, _bdot_tn(x, y), 0.0), c_end, w, b_e)
    z = each(lambda u_, v_, be, ke: jnp.where(same_head, _bdot_tn(jnp.concatenate([u_, v_], axis=0),
                                                                    jnp.concatenate([be, ke], axis=0)), 0.0),
             uv, v, b_e, k_e)
    return {unit: terms for unit, *terms in zip(units, q, yc, g, z)}


def _rwkv_init(carry, carry_l, state, ybuf, wl_bf16, wl_ref):
    carry[...] = jnp.zeros_like(carry)
    carry_l[...] = jnp.zeros_like(carry_l)
    state[...] = jnp.zeros_like(state)
    wl_bf16[...] = wl_ref[...].astype(BF16)


def _rwkv_body(r_ref, k_ref, v_ref, xb_ref, wl_ref, bl_ref, gate_ref, mu_r, mu_k, mu_v, mu_l, w0_ref, wup_ref, a0_ref,
               aup_ref, kk_ref, ka_ref, rk_ref, gn_g, gn_b, o_ref, carry, carry_l, state, ybuf, wl_bf16):
    c = RWKV_CHUNK
    chunks = r_ref.shape[0] // c
    *scan_inputs, bonus = _rwkv_prepare(r_ref, k_ref, v_ref, xb_ref, wl_bf16, bl_ref, mu_r, mu_k, mu_v, mu_l, w0_ref,
                                        wup_ref, a0_ref, aup_ref, kk_ref, ka_ref, rk_ref, carry, carry_l)
    terms = _rwkv_chunk_transforms(*scan_inputs)

    pairs = range(N_PAIRS)
    sts = [state[:, p * PAIR:(p + 1) * PAIR] for p in pairs]
    starts = []
    for ch in range(chunks):
        starts.append(sts)
        sts = [_split_dot(sts[p], terms[ch, p][2], 2, 2) + terms[ch, p][3] for p in pairs]
    for p in pairs:
        state[:, p * PAIR:(p + 1) * PAIR] = sts[p]
    for ch in range(chunks):
        for p in pairs:
            q, yc = terms[ch, p][0], terms[ch, p][1]
            ybuf[ch * c:(ch + 1) * c, p * PAIR:(p + 1) * PAIR] = _split_dot(q, starts[ch][p], 2, 2, NT_DIMS) + yc

    wy = ybuf[...]
    inv_n = 1.0 / RWKV_HEAD
    mu = _head_sums(wy) * inv_n
    d = wy - mu
    var = _head_sums(d * d) * inv_n
    wy = d * lax.rsqrt(var + RWKV_GN_EPS) * gn_g[...] + gn_b[...]
    o_ref[...] = ((wy + bonus) * _silu(gate_ref[...])).astype(BF16)


def _rwkv_specs(tile, layer):
    vec, lora_w = _layer_param(layer, (1, BR_WIDTH)), _layer_param(layer, (LANES, BR_WIDTH))
    lora_in = pl.BlockSpec((pl.Squeezed(), pl.Element(D_MODEL), pl.Element(DECAY_RANK + ICLR_RANK)),
                           lambda i: (layer, 0, LORA_START))
    in_specs = [_row_block(tile, COL_C_R), _row_block(tile, COL_C_K), _row_block(tile, COL_C_V),
                pl.BlockSpec((tile, D_MODEL), lambda i: (i, 0)), lora_in, _layer_param(layer, (1, LANES)),
                _row_block(tile, COL_C_GATE),
                vec, vec, vec, _layer_param(layer, (1, LANES)), vec, lora_w, vec, lora_w, vec, vec, vec, vec, vec]
    return in_specs, [pltpu.VMEM((SUBLANES, BR_WIDTH), F32), pltpu.VMEM((SUBLANES, LANES), F32),
                      pltpu.VMEM((PAIR, BR_WIDTH), F32), pltpu.VMEM((tile, BR_WIDTH), F32),
                      pltpu.VMEM((D_MODEL, DECAY_RANK + ICLR_RANK), BF16)]


def _recurrent_mixers_kernel(*refs):
    n_in = LRU_IN + CONF_IN + RWKV_IN
    ins, (o_a, o_d, o_c), scratch = refs[:n_in], refs[n_in:n_in + 3], refs[n_in + 3:]
    lru_in, conf_in, rwkv_in = ins[:LRU_IN], ins[LRU_IN:LRU_IN + CONF_IN], ins[LRU_IN + CONF_IN:]
    lru_s = scratch[:LRU_SCRATCH]
    conf_s = scratch[LRU_SCRATCH:LRU_SCRATCH + CONF_SCRATCH]
    rwkv_s = scratch[LRU_SCRATCH + CONF_SCRATCH:]

    @pl.when(pl.program_id(0) == 0)
    def _():
        _lru_init(*lru_s)
        _conf_init(*conf_s)
        _rwkv_init(*rwkv_s, rwkv_in[RWKV_LORA_W])

    _rwkv_body(*rwkv_in, o_c, *rwkv_s)
    _conf_body(*conf_in, o_d, *conf_s)
    _lru_body(*lru_in, o_a, *lru_s)


def _recurrent_mixers(h, xb, w_in, layer, lru_args, conf_args, lora_bias, rwkv_args, tile=4 * RWKV_CHUNK):
    s = h.shape[0]
    (lru_specs, lru_scr), (conf_specs, conf_scr), (rwkv_specs, rwkv_scr) = (
        _lru_specs(tile, layer), _conf_specs(tile, layer), _rwkv_specs(tile, layer))
    assert (len(lru_specs), len(conf_specs), len(rwkv_specs)) == (LRU_IN, CONF_IN, RWKV_IN)
    out = pl.BlockSpec((tile, BR_WIDTH), lambda i: (i, 0))
    return pl.pallas_call(
        _recurrent_mixers_kernel,
        grid=(s // tile,),
        in_specs=lru_specs + conf_specs + rwkv_specs,
        out_specs=[out] * 3,
        out_shape=[jax.ShapeDtypeStruct((s, BR_WIDTH), BF16)] * 3,
        scratch_shapes=lru_scr + conf_scr + rwkv_scr,
        compiler_params=_params(("arbitrary",), 40),
        name="recurrent_mixers",
    )(h, h, *lru_args, h, h, h, *conf_args, h, h, h, xb, w_in, lora_bias, h, *rwkv_args)


def _mix_kernel(xb_ref, *refs):
    ygs, wms, bms, wbrs = (refs[k * N_BRANCH:(k + 1) * N_BRANCH] for k in range(4))
    o_ref = refs[4 * N_BRANCH]
    xb = xb_ref[...]
    acc = None
    for n in range(N_BRANCH):
        gate = jax.nn.sigmoid(_bdot(xb, wms[n][...]) + bms[n][...])
        val = gate * _bdot(ygs[n][...], wbrs[n][...])
        acc = val if acc is None else acc + val
    o_ref[...] = acc.astype(BF16)


def _mix(xb, ygs, w_all, layer, bm, wbr_all, tm=1024, tn=256):
    s = xb.shape[0]
    nj = D_MODEL // tn
    per_branch = lambda make: [make(n) for n in range(N_BRANCH)]
    return pl.pallas_call(
        _mix_kernel,
        grid=(s // tm, nj),
        in_specs=[pl.BlockSpec((tm, D_MODEL), lambda i, j: (i, 0))]
        + per_branch(lambda n: pl.BlockSpec((tm, BR_WIDTH), lambda i, j: (i, 0)))
        + per_branch(lambda n: pl.BlockSpec((pl.Squeezed(), pl.Element(D_MODEL), pl.Element(tn)),
                                            lambda i, j: (layer, 0, ((BRANCH_IN + n * D_MODEL) // LANES
                                                                     + j * (tn // LANES)) * LANES)))
        + per_branch(lambda n: pl.BlockSpec((None, 1, tn), lambda i, j: (layer, 0, n * nj + j)))
        + per_branch(lambda n: pl.BlockSpec((None, None, BR_WIDTH, tn), lambda i, j: (layer, n, 0, j))),
        out_specs=pl.BlockSpec((tm, tn), lambda i, j: (i, j)),
        out_shape=jax.ShapeDtypeStruct((s, D_MODEL), BF16),
        compiler_params=_params(("parallel", "arbitrary"), 48),
        name="branch_mix",
    )(xb, *ygs, *([w_all] * N_BRANCH), *([bm] * N_BRANCH), *([wbr_all] * N_BRANCH))


def _out_kernel(mixed_ref, x_ref, w_ref, g_ref, b_ref, o_ref, ob_ref):
    y = ALPHA * x_ref[...] + jnp.dot(mixed_ref[...], w_ref[...], preferred_element_type=F32)
    mu = jnp.mean(y, axis=-1, keepdims=True)
    var = jnp.mean(jnp.square(y - mu), axis=-1, keepdims=True)
    out = (y - mu) * lax.rsqrt(var + LN_EPS) * g_ref[...] + b_ref[...]
    o_ref[...] = out
    ob_ref[...] = out.astype(BF16)


def _out_proj(mixed, x, layer, w, g, b, tm=512):
    s = x.shape[0]
    row = pl.BlockSpec((tm, D_MODEL), lambda i: (i, 0))
    vec = _layer_param(layer, (1, D_MODEL))
    return pl.pallas_call(
        _out_kernel,
        grid=(s // tm,),
        in_specs=[row, row, _layer_param(layer, (D_MODEL, D_MODEL)), vec, vec],
        out_specs=[row, row],
        out_shape=[jax.ShapeDtypeStruct((s, D_MODEL), F32), jax.ShapeDtypeStruct((s, D_MODEL), BF16)],
        compiler_params=_params(("parallel",), 52),
        name="out_proj_ln",
    )(mixed, x, w, g, b)


def _block_diag(w):
    depth, blocks, n, _ = w.shape
    eye = jnp.eye(blocks, dtype=w.dtype)
    return (eye[None, :, None, :, None] * w[:, :, :, None, :]).reshape(depth, blocks * n, blocks * n)


def kernel(x, att_rel_bias, w_in, b_in, lru_conv_w, lru_conv_b, lru_gate_a_w, lru_gate_a_b, lru_gate_x_w, lru_gate_x_b, lru_lambda, rwkv_mu, rwkv_w0, rwkv_w_up, rwkv_a0, rwkv_a_up, rwkv_k_k, rwkv_k_a, rwkv_r_k, rwkv_gn_g, rwkv_gn_b, conf_dw_w, conf_dw_b, conf_ln_g, conf_ln_b, w_br, w_out, ln_g, ln_b):
    bsz, s, d = x.shape
    assert bsz == 1 and d == D_MODEL and s % ATT_ROWS == 0
    vec = lambda t: t.reshape(DEPTH, 1, -1)
    b_h = vec(jnp.concatenate([b_in[:, :H_SPLIT * BR_WIDTH], b_in[:, C_GATE_START:BRANCH_IN]], axis=1))
    b_merge = vec(b_in[:, BRANCH_IN:])
    b_lora = vec(b_in[:, LORA_START:C_GATE_START])
    mu = rwkv_mu
    zpad = jnp.zeros((DEPTH, DECAY_RANK, BR_WIDTH), F32)
    wup = jnp.concatenate([rwkv_w_up, zpad], axis=1).astype(BF16)
    aup = jnp.concatenate([zpad, rwkv_a_up], axis=1).astype(BF16)
    lru_args = (lru_conv_w, vec(lru_conv_b), _block_diag(lru_gate_a_w).astype(BF16), vec(lru_gate_a_b),
                _block_diag(lru_gate_x_w).astype(BF16), vec(lru_gate_x_b), vec(lru_lambda))
    conf_args = (conf_dw_w, vec(conf_dw_b), vec(conf_ln_g), vec(conf_ln_b))
    rwkv_args = (vec(mu[:, :BR_WIDTH]), vec(mu[:, BR_WIDTH:2 * BR_WIDTH]), vec(mu[:, 2 * BR_WIDTH:3 * BR_WIDTH]),
                 vec(mu[:, 3 * BR_WIDTH:]), vec(rwkv_w0), wup, vec(rwkv_a0), aup, vec(rwkv_k_k), vec(rwkv_k_a),
                 vec(rwkv_r_k), vec(rwkv_gn_g), vec(rwkv_gn_b))
    w_out_bf16, ln_g, ln_b = w_out.astype(BF16), vec(ln_g), vec(ln_b)
    att_bias = _attn_bias(att_rel_bias)

    y = x.reshape(s, d)
    yb = y.astype(BF16)
    for layer in range(DEPTH):
        h = _in_proj(yb, w_in, layer, b_h)
        yg_b = _attention(h, att_bias)
        yg_a, yg_d, yg_c = _recurrent_mixers(h, yb, w_in, layer, lru_args, conf_args, b_lora, rwkv_args)
        mixed = _mix(yb, (yg_a, yg_b, yg_c, yg_d), w_in, layer, b_merge, w_br)
        y, yb = _out_proj(mixed, y, layer, w_out_bf16, ln_g, ln_b)
    return y.reshape(bsz, s, d)
```

```python
import functools
import math

import numpy as np
import jax
import jax.numpy as jnp
from jax import lax
from jax.experimental import pallas as pl
from jax.experimental.pallas import tpu as pltpu

D_MODEL = 2048
DEPTH = 2
N_BRANCH = 4
BR_WIDTH = 512
LRU_CONV = 4
LRU_C = 8.0
ATT_GROUPS = ((128, 1), (512, 4), (2048, 16))
ATT_HEADS_PER_GROUP = 4
ATT_HEAD_DIM = BR_WIDTH // ATT_HEADS_PER_GROUP
ATT_HEADS = len(ATT_GROUPS) * ATT_HEADS_PER_GROUP
ATT_QKV = ATT_HEADS * ATT_HEAD_DIM
ATT_SPAN = 128
N_BUCKETS = 32
MAX_DISTANCE = 2048
NEG_INF = -1e30
RWKV_HEAD = 64
DECAY_RANK = 64
ICLR_RANK = 64
RWKV_GN_EPS = 64e-5
CONF_KERNEL = 31
LN_EPS = 1e-5
ALPHA = (2.0 * DEPTH) ** 0.25

LANES = 128
SUBLANES = 8
MIB = 1024 * 1024

BRANCH_IN = 2 * BR_WIDTH + 3 * ATT_QKV + BR_WIDTH + (4 * BR_WIDTH + DECAY_RANK + ICLR_RANK) + 3 * BR_WIDTH
C_GATE_START = BRANCH_IN - 4 * BR_WIDTH
LORA_START = C_GATE_START - (DECAY_RANK + ICLR_RANK)
H_SPLIT = LORA_START // BR_WIDTH
H_BLOCKS = H_SPLIT + 4
H_WIDTH = H_BLOCKS * BR_WIDTH
COL_A_X, COL_A_GATE = 0, 1
COL_Q, COL_K, COL_V, COL_B_GATE = 2, 5, 8, 11
COL_C_R, COL_C_K, COL_C_V = 12, 13, 14
COL_C_GATE, COL_D_VAL, COL_D_GLU, COL_D_GATE = 15, 16, 17, 18

CONF_HALO = 32
RWKV_CHUNK = 64
PAIR = 2 * RWKV_HEAD
N_PAIRS = BR_WIDTH // PAIR

F32 = jnp.float32
BF16 = jnp.bfloat16


def _params(semantics, vmem_mib):
    return pltpu.CompilerParams(dimension_semantics=semantics, vmem_limit_bytes=vmem_mib * MIB)


def _bdot(a, b):
    return jnp.dot(a.astype(BF16), b.astype(BF16), preferred_element_type=F32)


def _bdot_nt(a, b):
    return lax.dot_general(a.astype(BF16), b.astype(BF16), (((1,), (1,)), ((), ())), preferred_element_type=F32)


def _bdot_tn(a, b):
    return lax.dot_general(a.astype(BF16), b.astype(BF16), (((0,), (0,)), ((), ())), preferred_element_type=F32)


NN_DIMS = (((1,), (0,)), ((), ()))
NT_DIMS = (((1,), (1,)), ((), ()))


def _bf16_parts(x, parts):
    out = []
    for _ in range(parts):
        hi = x.astype(BF16)
        out.append(hi)
        x = x - hi.astype(F32)
    return out


def _split_dot(a, b, a_parts, b_parts, dims=NN_DIMS):
    acc = None
    rows = a.shape[0]
    a_terms = _bf16_parts(a, a_parts)
    for j, bj in enumerate(_bf16_parts(b, b_parts)):
        with_bj = a_terms[:max(a_parts, b_parts) - j]
        if with_bj:
            prod = lax.dot_general(jnp.concatenate(with_bj, axis=0), bj, dims, preferred_element_type=F32)
            for i in range(len(with_bj)):
                term = prod[i * rows:(i + 1) * rows]
                acc = term if acc is None else acc + term
    return acc


def _softplus(z):
    return jnp.maximum(z, 0.0) + jnp.log1p(jnp.exp(-jnp.abs(z)))


def _expm1_nonpos(z):
    u = jnp.exp(z)
    safe = jnp.where(u == 1.0, 0.5, u)
    return jnp.where(u == 1.0, z, jnp.where(u == 0.0, -1.0, (safe - 1.0) * z / jnp.log(safe)))


def _silu(z):
    return z * jax.nn.sigmoid(z)


def _in_proj_kernel(xb_ref, w_ref, b_ref, h_ref):
    h_ref[...] = _bdot(xb_ref[...], w_ref[...]) + b_ref[...]


def _h_source_column(block):
    return block * BR_WIDTH if block < H_SPLIT else C_GATE_START + (block - H_SPLIT) * BR_WIDTH


def _in_proj(xb, w_all, layer, b, tm=2048, tn=512):
    s, k = xb.shape
    assert (H_SPLIT * BR_WIDTH) % tn == 0 and tn % BR_WIDTH == 0
    per_tile = tn // BR_WIDTH
    starts = np.array([_h_source_column(j * per_tile) // LANES for j in range(H_WIDTH // tn)], np.int32)
    return pl.pallas_call(
        lambda starts_ref, *refs: _in_proj_kernel(*refs),
        grid_spec=pltpu.PrefetchScalarGridSpec(
            num_scalar_prefetch=1,
            grid=(s // tm, H_WIDTH // tn),
            in_specs=[
                pl.BlockSpec((tm, k), lambda i, j, st: (i, 0)),
                pl.BlockSpec((pl.Squeezed(), pl.Element(k), pl.Element(tn)),
                             lambda i, j, st: (layer, 0, st[j] * LANES)),
                pl.BlockSpec((None, 1, tn), lambda i, j, st: (layer, 0, j)),
            ],
            out_specs=pl.BlockSpec((tm, tn), lambda i, j, st: (i, j)),
        ),
        out_shape=jax.ShapeDtypeStruct((s, H_WIDTH), F32),
        compiler_params=_params(("parallel", "arbitrary"), 56),
        name="in_proj",
    )(jnp.asarray(starts), xb, w_all, b)


LRU_IN, CONF_IN, RWKV_IN = 9, 7, 20
LRU_SCRATCH, CONF_SCRATCH, RWKV_SCRATCH = 2, 2, 5
RWKV_LORA_W = 4


def _lru_init(ebuf, hc):
    ebuf[0:SUBLANES, :] = jnp.zeros((SUBLANES, BR_WIDTH), F32)
    hc[...] = jnp.zeros_like(hc)


def _lru_body(ax_ref, ag_ref, cw_ref, cb_ref, wa_ref, ba_ref, wx_ref, bx_ref, lam_ref, o_ref, ebuf, hc):
    t = ax_ref.shape[0]
    halo = SUBLANES
    x = ax_ref[...]
    ebuf[halo:halo + t, :] = x
    u = cb_ref[...] + jnp.zeros((t, BR_WIDTH), F32)
    for j in range(LRU_CONV):
        u = u + cw_ref[j:j + 1, :] * ebuf[pl.ds(halo - (LRU_CONV - 1) + j, t), :]
    ebuf[0:halo, :] = x[t - halo:t, :]

    gate_r = jax.nn.sigmoid(_bdot(u, wa_ref[...]) + ba_ref[...])
    gate_i = jax.nn.sigmoid(_bdot(u, wx_ref[...]) + bx_ref[...])
    log_a = -LRU_C * gate_r * _softplus(-lam_ref[...])
    a = jnp.exp(log_a)
    b = jnp.sqrt(-_expm1_nonpos(2.0 * log_a)) * (gate_i * u)

    row = lax.broadcasted_iota(jnp.int32, (t, BR_WIDTH), 0)
    shift = 1
    while shift < t:
        valid = row >= shift
        b = jnp.where(valid, a * pltpu.roll(b, shift, 0), 0.0) + b
        a = jnp.where(valid, a * pltpu.roll(a, shift, 0), a)
        shift *= 2
    h = a * hc[0:1, :] + b
    hc[0:1, :] = h[t - 1:t, :]
    o_ref[...] = (h * _silu(ag_ref[...])).astype(BF16)


def _row_block(tile, col):
    return pl.BlockSpec((tile, BR_WIDTH), lambda i: (i, col))


def _layer_param(layer, shape):
    return pl.BlockSpec((None,) + shape, lambda *_: (layer,) + (0,) * len(shape))


def _lru_specs(tile, layer):
    vec, mat = _layer_param(layer, (1, BR_WIDTH)), _layer_param(layer, (BR_WIDTH, BR_WIDTH))
    in_specs = [_row_block(tile, COL_A_X), _row_block(tile, COL_A_GATE), _layer_param(layer, (LRU_CONV, BR_WIDTH)),
                vec, mat, vec, mat, vec, vec]
    return in_specs, [pltpu.VMEM((tile + SUBLANES, BR_WIDTH), F32), pltpu.VMEM((SUBLANES, BR_WIDTH), F32)]


def _conf_init(ebuf, shifted):
    ebuf[0:CONF_HALO, :] = jnp.zeros((CONF_HALO, BR_WIDTH), F32)


def _conf_body(val_ref, glu_ref, gate_ref, w_ref, b_ref, g_ref, beta_ref, o_ref, ebuf, shifted):
    t = val_ref.shape[0]
    halo = CONF_HALO
    cu = val_ref[...] * jax.nn.sigmoid(glu_ref[...])
    ebuf[halo:halo + t, :] = cu
    for b in range(SUBLANES):
        span = t + (CONF_KERNEL - 1 - b) // SUBLANES * SUBLANES
        shifted[b, 0:span, :] = ebuf[pl.ds(halo - (CONF_KERNEL - 1) + b, span), :]
    acc = b_ref[...] + jnp.zeros((t, BR_WIDTH), F32)
    for j in range(CONF_KERNEL):
        b = j % SUBLANES
        acc = acc + w_ref[j:j + 1, :] * shifted[b, j - b:j - b + t, :]
    ebuf[0:halo, :] = cu[t - halo:t, :]

    mu = jnp.mean(acc, axis=-1, keepdims=True)
    var = jnp.mean(jnp.square(acc - mu), axis=-1, keepdims=True)
    ln = (acc - mu) * lax.rsqrt(var + LN_EPS) * g_ref[...] + beta_ref[...]
    o_ref[...] = (_silu(ln) * _silu(gate_ref[...])).astype(BF16)


def _conf_specs(tile, layer):
    vec = _layer_param(layer, (1, BR_WIDTH))
    in_specs = [_row_block(tile, COL_D_VAL), _row_block(tile, COL_D_GLU), _row_block(tile, COL_D_GATE),
                _layer_param(layer, (CONF_KERNEL, BR_WIDTH)), vec, vec, vec]
    return in_specs, [pltpu.VMEM((tile + CONF_HALO, BR_WIDTH), F32),
                      pltpu.VMEM((SUBLANES, tile + CONF_HALO - SUBLANES, BR_WIDTH), F32)]


def _t5_bucket(dist):
    max_exact = N_BUCKETS // 2
    large = max_exact + (np.log(np.maximum(dist, 1) / max_exact) / math.log(MAX_DISTANCE / max_exact)
                         * (N_BUCKETS - max_exact)).astype(np.int32)
    large = np.minimum(large, N_BUCKETS - 1)
    return np.where(dist < max_exact, dist, large).astype(np.int32)


def _bucket_index():
    qi = np.arange(ATT_SPAN)[:, None]
    kj = np.arange(2 * ATT_SPAN)[None, :]
    dist = qi + ATT_SPAN - kj
    valid = (dist >= 0) & (dist <= ATT_SPAN)
    per_group = [np.where(valid, _t5_bucket(np.clip(dist, 0, ATT_SPAN) * dil), -1) for _, dil in ATT_GROUPS]
    return np.stack(per_group).astype(np.int32)


def _bias_kernel(table_ref, bucket_ref, o_ref):
    head = pl.program_id(0)
    bucket = bucket_ref[...]
    acc = jnp.full(bucket.shape, NEG_INF, F32)
    for bkt in range(N_BUCKETS):
        acc = jnp.where(bucket == bkt, table_ref[bkt, head], acc)
    o_ref[...] = acc


def _attn_bias(table):
    blk = (None, ATT_SPAN, 2 * ATT_SPAN)
    return pl.pallas_call(
        _bias_kernel,
        grid=(ATT_HEADS,),
        in_specs=[pl.BlockSpec(memory_space=pltpu.SMEM),
                  pl.BlockSpec(blk, lambda hd: (hd // ATT_HEADS_PER_GROUP, 0, 0))],
        out_specs=pl.BlockSpec(blk, lambda hd: (hd, 0, 0)),
        out_shape=jax.ShapeDtypeStruct((ATT_HEADS, ATT_SPAN, 2 * ATT_SPAN), F32),
        compiler_params=_params(("parallel",), 32),
        name="attn_bias",
    )(table, jnp.asarray(_bucket_index()))


ATT_DIRECT_STRIDE = 4


def _residue_reader(ref, slab, dilation):
    if dilation == 1:
        return lambda b, r: ref[b * ATT_SPAN:(b + 1) * ATT_SPAN, :]
    if dilation <= ATT_DIRECT_STRIDE:
        return lambda b, r: ref[pl.ds(b * ATT_SPAN * dilation + r, ATT_SPAN, stride=dilation), :]
    inner, outer = ATT_DIRECT_STRIDE, dilation // ATT_DIRECT_STRIDE
    per = ref.shape[0] // inner
    for r0 in range(inner):
        slab[r0] = ref[pl.ds(r0, per, stride=inner), :]
    return lambda b, r: slab[r % inner, pl.ds(b * ATT_SPAN * outer + r // inner, ATT_SPAN, stride=outer), :]


def _residue_writer(ref, slab, dilation):
    if dilation == 1:
        def write(b, r, val):
            ref[b * ATT_SPAN:(b + 1) * ATT_SPAN, :] = val
        return write, lambda: None
    if dilation <= ATT_DIRECT_STRIDE:
        def write(b, r, val):
            ref[pl.ds(b * ATT_SPAN * dilation + r, ATT_SPAN, stride=dilation), :] = val
        return write, lambda: None
    inner, outer = ATT_DIRECT_STRIDE, dilation // ATT_DIRECT_STRIDE
    per = ref.shape[0] // inner

    def write(b, r, val):
        slab[r % inner, pl.ds(b * ATT_SPAN * outer + r // inner, ATT_SPAN, stride=outer), :] = val

    def flush():
        for r0 in range(inner):
            ref[pl.ds(r0, per, stride=inner), :] = slab[r0]

    return write, flush


ATT_ROWS = ATT_SPAN * max(dil for _, dil in ATT_GROUPS)
ATT_GROUP_IN = 6
ATT_SLABS = 8
ATT_UNITS_PER_STAGE = 4


def _attn_group_outputs(q_ref, kc_ref, kp_ref, vc_ref, vp_ref, bias_ref, o_nat, m_nat, d_nat, slabs, first, dilation):
    blocks = ATT_ROWS // (ATT_SPAN * dilation)
    scale = ATT_HEAD_DIM ** -0.5
    read_q, read_kc, read_kp, read_vc, read_vp = (
        _residue_reader(ref, slab, dilation) for ref, slab in zip((q_ref, kc_ref, kp_ref, vc_ref, vp_ref), slabs[:5]))
    (write_o, flush_o), (write_m, flush_m), (write_d, flush_d) = (
        _residue_writer(ref, slab, dilation) for ref, slab in zip((o_nat, m_nat, d_nat), slabs[5:]))
    def key(b, r):
        return (read_kp(0, r) if b < 0 else read_kc(b, r)).astype(BF16)

    def value(b, r):
        return (read_vp(0, r) if b < 0 else read_vc(b, r)).astype(BF16)

    bias = bias_ref[...]
    before_start = first & (lax.broadcasted_iota(jnp.int32, bias.shape, 1) < ATT_SPAN)
    full = (ATT_SPAN, LANES)

    def logits(u):
        b, r = u["b"], u["r"]
        q = read_q(b, r).astype(BF16)
        lg = _bdot_nt(q, jnp.concatenate([key(b - 1, r), key(b, r)], axis=0)) * scale + bias
        u["lg"] = jnp.where(before_start, NEG_INF, lg) if b == 0 else lg

    def row_max(u):
        u["m"] = jnp.max(u["lg"], axis=-1, keepdims=True)

    def weights(u):
        u["p"] = jnp.exp(u.pop("lg") - u["m"])

    def denominator(u):
        u["den"] = jnp.sum(u["p"], axis=-1, keepdims=True)

    def outputs(u):
        b, r = u["b"], u["r"]
        write_o(b, r, _bdot(u.pop("p"), jnp.concatenate([value(b - 1, r), value(b, r)], axis=0)) / u["den"])
        write_m(b, r, jnp.broadcast_to(u["m"], full))
        write_d(b, r, jnp.broadcast_to(u["den"], full))

    stages = (logits, row_max, weights, denominator, outputs)
    units = [dict(b=b, r=r) for b in range(blocks) for r in range(dilation)]
    groups = [units[i:i + ATT_UNITS_PER_STAGE] for i in range(0, len(units), ATT_UNITS_PER_STAGE)]
    for tick in range(len(groups) + len(stages) - 1):
        for s, stage in reversed(list(enumerate(stages))):
            if 0 <= tick - s < len(groups):
                for u in groups[tick - s]:
                    stage(u)
    flush_o()
    flush_m()
    flush_d()


def _attention_kernel(*refs):
    n_in = ATT_GROUP_IN * len(ATT_GROUPS)
    gate_ref, y_ref = refs[n_in], refs[n_in + 1]
    o_nat, m_nat, d_nat = refs[n_in + 2:n_in + 5]
    slabs = refs[n_in + 5:]
    first = pl.program_id(0) == 0
    for g, (_, dil) in enumerate(ATT_GROUPS):
        _attn_group_outputs(*refs[ATT_GROUP_IN * g:ATT_GROUP_IN * (g + 1)], o_nat.at[g], m_nat.at[g], d_nat.at[g],
                            slabs, first, dil)
    ms = [m_nat[g] for g in range(len(ATT_GROUPS))]
    m_all = functools.reduce(jnp.maximum, ms)
    wts = [jnp.exp(m - m_all) * d_nat[g] for g, m in enumerate(ms)]
    num = sum(w * o_nat[g] for g, w in enumerate(wts))
    y_ref[...] = (num / sum(wts) * _silu(gate_ref[...])).astype(BF16)


def _attention(h, bias):
    s = h.shape[0]
    heads = ATT_HEADS_PER_GROUP
    in_specs, operands = [], []
    for g, (_, dil) in enumerate(ATT_GROUPS):
        blk = ATT_SPAN * dil
        per_step = ATT_ROWS // blk

        def spec(col, prev, g=g, blk=blk, per_step=per_step):
            base = (col + g) * heads
            if prev:
                return pl.BlockSpec((blk, ATT_HEAD_DIM), lambda n, hd: (jnp.maximum(n * per_step - 1, 0), base + hd))
            return pl.BlockSpec((ATT_ROWS, ATT_HEAD_DIM), lambda n, hd: (n, base + hd))

        in_specs += [spec(COL_Q, False), spec(COL_K, False), spec(COL_K, True), spec(COL_V, False), spec(COL_V, True),
                     pl.BlockSpec((None, ATT_SPAN, 2 * ATT_SPAN), lambda n, hd, g=g: (g * heads + hd, 0, 0))]
        operands += [h, h, h, h, h, bias]
    in_specs.append(pl.BlockSpec((ATT_ROWS, ATT_HEAD_DIM), lambda n, hd: (n, COL_B_GATE * heads + hd)))
    token_order = pltpu.VMEM((len(ATT_GROUPS), ATT_ROWS, LANES), F32)
    slab = pltpu.VMEM((ATT_DIRECT_STRIDE, ATT_ROWS // ATT_DIRECT_STRIDE, LANES), F32)
    return pl.pallas_call(
        _attention_kernel,
        grid=(s // ATT_ROWS, heads),
        in_specs=in_specs,
        out_specs=pl.BlockSpec((ATT_ROWS, ATT_HEAD_DIM), lambda n, hd: (n, hd)),
        out_shape=jax.ShapeDtypeStruct((s, BR_WIDTH), BF16),
        scratch_shapes=[token_order] * 3 + [slab] * ATT_SLABS,
        compiler_params=_params(("parallel", "arbitrary"), 56),
        name="dilated_attention",
    )(*operands, h)


def _head_sums(x):
    ri = lax.broadcasted_iota(jnp.int32, (PAIR, PAIR), 0)
    ci = lax.broadcasted_iota(jnp.int32, (PAIR, PAIR), 1)
    same_head = jnp.where((ri < RWKV_HEAD) == (ci < RWKV_HEAD), 1.0, 0.0).astype(BF16)
    rows = x.shape[0]
    tall = jnp.concatenate([part[:, p * PAIR:(p + 1) * PAIR] for part in _bf16_parts(x, 2) for p in range(N_PAIRS)], axis=0)
    sums = jnp.dot(tall, same_head, preferred_element_type=F32)
    block = lambda k: sums[k * rows:(k + 1) * rows, :]
    return jnp.concatenate([block(p) + block(N_PAIRS + p) for p in range(N_PAIRS)], axis=1)


def _rwkv_prepare(r_ref, k_ref, v_ref, xb_ref, wl_ref, bl_ref, mu_r, mu_k, mu_v, mu_l, w0_ref, wup_ref, a0_ref, aup_ref,
                  kk_ref, ka_ref, rk_ref, carry, carry_l):
    t = r_ref.shape[0]

    def shift_mix(x, mu, prev_row):
        row = lax.broadcasted_iota(jnp.int32, x.shape, 0)
        x_prev = jnp.where(row == 0, prev_row, pltpu.roll(x, 1, 0))
        return x + mu * (x_prev - x)

    r_in, k_in, v_in = r_ref[...], k_ref[...], v_ref[...]
    l_in = _bdot(xb_ref[...], wl_ref[...]) + bl_ref[...]
    r = shift_mix(r_in, mu_r[...], carry[0:1, :])
    kx = shift_mix(k_in, mu_k[...], carry[1:2, :])
    vv = shift_mix(v_in, mu_v[...], carry[2:3, :])
    lo = shift_mix(l_in, mu_l[...], carry_l[0:1, :])
    carry[0:1, :] = r_in[t - 1:t, :]
    carry[1:2, :] = k_in[t - 1:t, :]
    carry[2:3, :] = v_in[t - 1:t, :]
    carry_l[0:1, :] = l_in[t - 1:t, :]

    w_log = -_softplus(-(w0_ref[...] + _bdot(jnp.tanh(lo), wup_ref[...]))) - 0.5
    log_decay = -jnp.exp(w_log)
    a_icl = jax.nn.sigmoid(a0_ref[...] + _bdot(lo, aup_ref[...]))

    kk = kx * kk_ref[...]
    kk = kk / jnp.maximum(jnp.sqrt(_head_sums(kk * kk)), 1e-12)
    kc = kx * (1.0 + (a_icl - 1.0) * ka_ref[...])
    bonus = _head_sums(r * kc * rk_ref[...]) * vv
    return log_decay, r, kc, vv, -kk, kk * a_icl, bonus


def _stack_heads(x):
    lane = lax.broadcasted_iota(jnp.int32, x.shape, 1)
    return jnp.concatenate([jnp.where(lane < RWKV_HEAD, x, 0.0), jnp.where(lane >= RWKV_HEAD, x, 0.0)], axis=0)


def _time_indices():
    t = lax.broadcasted_iota(jnp.int32, (RWKV_CHUNK, PAIR), 0)
    s = lax.broadcasted_iota(jnp.int32, (RWKV_CHUNK, PAIR), 1) & (RWKV_CHUNK - 1)
    return t, s


def _unit_lower_inverse(a_strict):
    ti, si = _time_indices()

    def same_block(bits):
        return (ti >> bits) == (si >> bits)

    pw = [jnp.where(same_block(4), a, 0.0) for a in a_strict]
    x = [jnp.where(ti == si, 1.0, 0.0) + p for p in pw]
    c = RWKV_CHUNK
    squares = [_bdot(p, _stack_heads(p)) for p in pw]
    for level in range(3):
        pw = squares
        if level < 2:
            both = [_bdot(jnp.concatenate([xi, p], axis=0), _stack_heads(p)) for xi, p in zip(x, pw)]
            x = [xi + b[0:c] for xi, b in zip(x, both)]
            squares = [b[c:2 * c] for b in both]
        else:
            x = [xi + _bdot(xi, _stack_heads(p)) for xi, p in zip(x, pw)]
    for bits in (5, 6):
        join = same_block(bits) & jnp.logical_not(same_block(bits - 1))
        xe = [_bdot(xi, _stack_heads(jnp.where(join, a, 0.0))) for xi, a in zip(x, a_strict)]
        x = [xi + _bdot(t, _stack_heads(xi)) for xi, t in zip(x, xe)]
    return x


def _rwkv_chunk_transforms(lw_all, r_all, k_all, v_all, a_all, b_all):
    c = RWKV_CHUNK
    n = 2 * c
    ti = lax.broadcasted_iota(jnp.int32, (c, c), 0)
    si = lax.broadcasted_iota(jnp.int32, (c, c), 1)
    lower_ones = jnp.where(si <= ti, 1.0, 0.0)
    tt, ss = _time_indices()
    strict = tt > ss
    incl = tt >= ss
    ri = lax.broadcasted_iota(jnp.int32, (n, n), 0)
    ci = lax.broadcasted_iota(jnp.int32, (n, n), 1)
    same_head = (ri < RWKV_HEAD) == (ci < RWKV_HEAD)
    eye = ri == ci

    units = [(ch, p) for ch in range(lw_all.shape[0] // c) for p in range(N_PAIRS)]
    each = lambda f, *cols: [f(*args) for args in zip(*cols)]

    def split(x):
        return [x[ch * c:(ch + 1) * c, p * PAIR:(p + 1) * PAIR] for ch, p in units]

    lw, r, k, v, a, b = (split(x) for x in (lw_all, r_all, k_all, v_all, a_all, b_all))
    cs = each(lambda x: _split_dot(lower_ones, x, 1, 3), lw)
    c_end = each(lambda x: x[c - 1:c, :], cs)
    r_d = each(lambda x, y: x * jnp.exp(y), r, cs)
    a_d = each(lambda x, y, z: x * jnp.exp(y - z), a, cs, lw)
    b_i = each(lambda x, y: x * jnp.exp(-y), b, cs)
    k_i = each(lambda x, y: x * jnp.exp(-y), k, cs)
    b_e = each(lambda x, y, e: x * jnp.exp(e - y), b, cs, c_end)
    k_e = each(lambda x, y, e: x * jnp.exp(e - y), k, cs, c_end)
    v_s = each(_stack_heads, v)

    aa = each(lambda ad, rd, bi, ki: _bdot_nt(jnp.concatenate([ad, rd], axis=0),
                                              jnp.concatenate([_stack_heads(bi), _stack_heads(ki)], axis=0)),
              a_d, r_d, b_i, k_i)
    a_ab = each(lambda x: jnp.where(strict, x[0:c, 0:n], 0.0), aa)
    a_ak = each(lambda x: jnp.where(strict, x[0:c, n:2 * n], 0.0), aa)
    a_rb = each(lambda x: jnp.where(incl, x[c:n, 0:n], 0.0), aa)
    a_rk = each(lambda x: jnp.where(incl, x[c:n, n:2 * n], 0.0), aa)

    minv = _unit_lower_inverse(a_ab)
    av = each(lambda x, y, vs: _bdot(jnp.concatenate([x, y], axis=0), vs), a_ak, a_rk, v_s)
    t1 = each(lambda x: x[0:c], av)
    side_by_side = lambda x, y: jnp.concatenate([_stack_heads(x), _stack_heads(y)], axis=1)
    wu = each(lambda m, ad, x: _bdot(m, side_by_side(ad, x)), minv, a_d, t1)
    w = each(lambda x: x[:, 0:n], wu)
    uv = each(lambda x: x[:, n:2 * n], wu)
    qy = each(lambda x, y, z: _bdot(x, side_by_side(y, z)), a_rb, w, uv)
    q = each(lambda rd, x: rd + x[:, 0:n], r_d, qy)
    yc = each(lambda x, y: x[:, n:2 * n] + y[c:n], qy, av)
    g = each(lambda e, x, y: jnp.where(eye, jnp.exp(e), 0.0) + jnp.where(same_head, _bdot_tn(x, y), 0.0), c_end, w, b_e)
    z = each(lambda u_, v_, be, ke: jnp.where(same_head, _bdot_tn(jnp.concatenate([u_, v_], axis=0),
                                                                    jnp.concatenate([be, ke], axis=0)), 0.0),
             uv, v, b_e, k_e)
    return {unit: terms for unit, *terms in zip(units, q, yc, g, z)}


def _rwkv_init(carry, carry_l, state, ybuf, wl_bf16, wl_ref):
    carry[...] = jnp.zeros_like(carry)
    carry_l[...] = jnp.zeros_like(carry_l)
    state[...] = jnp.zeros_like(state)
    wl_bf16[...] = wl_ref[...].astype(BF16)


def _rwkv_body(r_ref, k_ref, v_ref, xb_ref, wl_ref, bl_ref, gate_ref, mu_r, mu_k, mu_v, mu_l, w0_ref, wup_ref, a0_ref,
               aup_ref, kk_ref, ka_ref, rk_ref, gn_g, gn_b, o_ref, carry, carry_l, state, ybuf, wl_bf16):
    c = RWKV_CHUNK
    chunks = r_ref.shape[0] // c
    *scan_inputs, bonus = _rwkv_prepare(r_ref, k_ref, v_ref, xb_ref, wl_bf16, bl_ref, mu_r, mu_k, mu_v, mu_l, w0_ref,
                                        wup_ref, a0_ref, aup_ref, kk_ref, ka_ref, rk_ref, carry, carry_l)
    terms = _rwkv_chunk_transforms(*scan_inputs)

    pairs = range(N_PAIRS)
    sts = [state[:, p * PAIR:(p + 1) * PAIR] for p in pairs]
    starts = []
    for ch in range(chunks):
        starts.append(sts)
        sts = [_split_dot(sts[p], terms[ch, p][2], 2, 2) + terms[ch, p][3] for p in pairs]
    for p in pairs:
        state[:, p * PAIR:(p + 1) * PAIR] = sts[p]
    for ch in range(chunks):
        for p in pairs:
            q, yc = terms[ch, p][0], terms[ch, p][1]
            ybuf[ch * c:(ch + 1) * c, p * PAIR:(p + 1) * PAIR] = _split_dot(q, starts[ch][p], 2, 2, NT_DIMS) + yc

    wy = ybuf[...]
    inv_n = 1.0 / RWKV_HEAD
    mu = _head_sums(wy) * inv_n
    d = wy - mu
    var = _head_sums(d * d) * inv_n
    wy = d * lax.rsqrt(var + RWKV_GN_EPS) * gn_g[...] + gn_b[...]
    o_ref[...] = ((wy + bonus) * _silu(gate_ref[...])).astype(BF16)


def _rwkv_specs(tile, layer):
    vec, lora_w = _layer_param(layer, (1, BR_WIDTH)), _layer_param(layer, (LANES, BR_WIDTH))
    lora_in = pl.BlockSpec((pl.Squeezed(), pl.Element(D_MODEL), pl.Element(DECAY_RANK + ICLR_RANK)),
                           lambda i: (layer, 0, LORA_START))
    in_specs = [_row_block(tile, COL_C_R), _row_block(tile, COL_C_K), _row_block(tile, COL_C_V),
                pl.BlockSpec((tile, D_MODEL), lambda i: (i, 0)), lora_in, _layer_param(layer, (1, LANES)),
                _row_block(tile, COL_C_GATE),
                vec, vec, vec, _layer_param(layer, (1, LANES)), vec, lora_w, vec, lora_w, vec, vec, vec, vec, vec]
    return in_specs, [pltpu.VMEM((SUBLANES, BR_WIDTH), F32), pltpu.VMEM((SUBLANES, LANES), F32),
                      pltpu.VMEM((PAIR, BR_WIDTH), F32), pltpu.VMEM((tile, BR_WIDTH), F32),
                      pltpu.VMEM((D_MODEL, DECAY_RANK + ICLR_RANK), BF16)]


def _recurrent_mixers_kernel(*refs):
    n_in = LRU_IN + CONF_IN + RWKV_IN
    ins, (o_a, o_d, o_c), scratch = refs[:n_in], refs[n_in:n_in + 3], refs[n_in + 3:]
    lru_in, conf_in, rwkv_in = ins[:LRU_IN], ins[LRU_IN:LRU_IN + CONF_IN], ins[LRU_IN + CONF_IN:]
    lru_s = scratch[:LRU_SCRATCH]
    conf_s = scratch[LRU_SCRATCH:LRU_SCRATCH + CONF_SCRATCH]
    rwkv_s = scratch[LRU_SCRATCH + CONF_SCRATCH:]

    @pl.when(pl.program_id(0) == 0)
    def _():
        _lru_init(*lru_s)
        _conf_init(*conf_s)
        _rwkv_init(*rwkv_s, rwkv_in[RWKV_LORA_W])

    _rwkv_body(*rwkv_in, o_c, *rwkv_s)
    _conf_body(*conf_in, o_d, *conf_s)
    _lru_body(*lru_in, o_a, *lru_s)


def _recurrent_mixers(h, xb, w_in, layer, lru_args, conf_args, lora_bias, rwkv_args, tile=4 * RWKV_CHUNK):
    s = h.shape[0]
    (lru_specs, lru_scr), (conf_specs, conf_scr), (rwkv_specs, rwkv_scr) = (
        _lru_specs(tile, layer), _conf_specs(tile, layer), _rwkv_specs(tile, layer))
    assert (len(lru_specs), len(conf_specs), len(rwkv_specs)) == (LRU_IN, CONF_IN, RWKV_IN)
    out = pl.BlockSpec((tile, BR_WIDTH), lambda i: (i, 0))
    return pl.pallas_call(
        _recurrent_mixers_kernel,
        grid=(s // tile,),
        in_specs=lru_specs + conf_specs + rwkv_specs,
        out_specs=[out] * 3,
        out_shape=[jax.ShapeDtypeStruct((s, BR_WIDTH), BF16)] * 3,
        scratch_shapes=lru_scr + conf_scr + rwkv_scr,
        compiler_params=_params(("arbitrary",), 40),
        name="recurrent_mixers",
    )(h, h, *lru_args, h, h, h, *conf_args, h, h, h, xb, w_in, lora_bias, h, *rwkv_args)


def _mix_kernel(xb_ref, *refs):
    ygs, wms, bms, wbrs = (refs[k * N_BRANCH:(k + 1) * N_BRANCH] for k in range(4))
    o_ref = refs[4 * N_BRANCH]
    xb = xb_ref[...]
    acc = None
    for n in range(N_BRANCH):
        gate = jax.nn.sigmoid(_bdot(xb, wms[n][...]) + bms[n][...])
        val = gate * _bdot(ygs[n][...], wbrs[n][...])
        acc = val if acc is None else acc + val
    o_ref[...] = acc.astype(BF16)


def _mix(xb, ygs, w_all, layer, bm, wbr_all, tm=1024, tn=256):
    s = xb.shape[0]
    nj = D_MODEL // tn
    per_branch = lambda make: [make(n) for n in range(N_BRANCH)]
    return pl.pallas_call(
        _mix_kernel,
        grid=(s // tm, nj),
        in_specs=[pl.BlockSpec((tm, D_MODEL), lambda i, j: (i, 0))]
        + per_branch(lambda n: pl.BlockSpec((tm, BR_WIDTH), lambda i, j: (i, 0)))
        + per_branch(lambda n: pl.BlockSpec((pl.Squeezed(), pl.Element(D_MODEL), pl.Element(tn)),
                                            lambda i, j: (layer, 0, ((BRANCH_IN + n * D_MODEL) // LANES
                                                                     + j * (tn // LANES)) * LANES)))
        + per_branch(lambda n: pl.BlockSpec((None, 1, tn), lambda i, j: (layer, 0, n * nj + j)))
        + per_branch(lambda n: pl.BlockSpec((None, None, BR_WIDTH, tn), lambda i, j: (layer, n, 0, j))),
        out_specs=pl.BlockSpec((tm, tn), lambda i, j: (i, j)),
        out_shape=jax.ShapeDtypeStruct((s, D_MODEL), BF16),
        compiler_params=_params(("parallel", "arbitrary"), 48),
        name="branch_mix",
    )(xb, *ygs, *([w_all] * N_BRANCH), *([bm] * N_BRANCH), *([wbr_all] * N_BRANCH))


def _out_kernel(mixed_ref, x_ref, w_ref, g_ref, b_ref, o_ref, ob_ref):
    y = ALPHA * x_ref[...] + jnp.dot(mixed_ref[...], w_ref[...], preferred_element_type=F32)
    mu = jnp.mean(y, axis=-1, keepdims=True)
    var = jnp.mean(jnp.square(y - mu), axis=-1, keepdims=True)
    out = (y - mu) * lax.rsqrt(var + LN_EPS) * g_ref[...] + b_ref[...]
    o_ref[...] = out
    ob_ref[...] = out.astype(BF16)


def _out_proj(mixed, x, layer, w, g, b, tm=512):
    s = x.shape[0]
    row = pl.BlockSpec((tm, D_MODEL), lambda i: (i, 0))
    vec = _layer_param(layer, (1, D_MODEL))
    return pl.pallas_call(
        _out_kernel,
        grid=(s // tm,),
        in_specs=[row, row, _layer_param(layer, (D_MODEL, D_MODEL)), vec, vec],
        out_specs=[row, row],
        out_shape=[jax.ShapeDtypeStruct((s, D_MODEL), F32), jax.ShapeDtypeStruct((s, D_MODEL), BF16)],
        compiler_params=_params(("parallel",), 52),
        name="out_proj_ln",
    )(mixed, x, w, g, b)


def _block_diag(w):
    depth, blocks, n, _ = w.shape
    eye = jnp.eye(blocks, dtype=w.dtype)
    return (eye[None, :, None, :, None] * w[:, :, :, None, :]).reshape(depth, blocks * n, blocks * n)


def kernel(x, att_rel_bias, w_in, b_in, lru_conv_w, lru_conv_b, lru_gate_a_w, lru_gate_a_b, lru_gate_x_w, lru_gate_x_b, lru_lambda, rwkv_mu, rwkv_w0, rwkv_w_up, rwkv_a0, rwkv_a_up, rwkv_k_k, rwkv_k_a, rwkv_r_k, rwkv_gn_g, rwkv_gn_b, conf_dw_w, conf_dw_b, conf_ln_g, conf_ln_b, w_br, w_out, ln_g, ln_b):
    bsz, s, d = x.shape
    assert bsz == 1 and d == D_MODEL and s % ATT_ROWS == 0
    vec = lambda t: t.reshape(DEPTH, 1, -1)
    b_h = vec(jnp.concatenate([b_in[:, :H_SPLIT * BR_WIDTH], b_in[:, C_GATE_START:BRANCH_IN]], axis=1))
    b_merge = vec(b_in[:, BRANCH_IN:])
    b_lora = vec(b_in[:, LORA_START:C_GATE_START])
    mu = rwkv_mu
    zpad = jnp.zeros((DEPTH, DECAY_RANK, BR_WIDTH), F32)
    wup = jnp.concatenate([rwkv_w_up, zpad], axis=1).astype(BF16)
    aup = jnp.concatenate([zpad, rwkv_a_up], axis=1).astype(BF16)
    lru_args = (lru_conv_w, vec(lru_conv_b), _block_diag(lru_gate_a_w).astype(BF16), vec(lru_gate_a_b),
                _block_diag(lru_gate_x_w).astype(BF16), vec(lru_gate_x_b), vec(lru_lambda))
    conf_args = (conf_dw_w, vec(conf_dw_b), vec(conf_ln_g), vec(conf_ln_b))
    rwkv_args = (vec(mu[:, :BR_WIDTH]), vec(mu[:, BR_WIDTH:2 * BR_WIDTH]), vec(mu[:, 2 * BR_WIDTH:3 * BR_WIDTH]),
                 vec(mu[:, 3 * BR_WIDTH:]), vec(rwkv_w0), wup, vec(rwkv_a0), aup, vec(rwkv_k_k), vec(rwkv_k_a),
                 vec(rwkv_r_k), vec(rwkv_gn_g), vec(rwkv_gn_b))
    w_out_bf16, ln_g, ln_b = w_out.astype(BF16), vec(ln_g), vec(ln_b)
    att_bias = _attn_bias(att_rel_bias)

    y = x.reshape(s, d)
    yb = y.astype(BF16)
    for layer in range(DEPTH):
        h = _in_proj(yb, w_in, layer, b_h)
        yg_b = _attention(h, att_bias)
        yg_a, yg_d, yg_c = _recurrent_mixers(h, yb, w_in, layer, lru_args, conf_args, b_lora, rwkv_args)
        mixed = _mix(yb, (yg_a, yg_b, yg_c, yg_d), w_in, layer, b_merge, w_br)
        y, yb = _out_proj(mixed, y, layer, w_out_bf16, ln_g, ln_b)
    return y.reshape(bsz, s, d)
```
